```python
import math
import jax, jax.numpy as jnp
from jax import lax
import numpy as np

D_MODEL = 1024
BATCH = 16
SEQ = 256
DEPTH = 4
DEC_BATCH = 2
DEC_SEQ = 2048
PAST_LEN = 512

GRID_W = 64
N_AB = (DEPTH + 1) // 2
N_C = DEPTH // 2
DIFF_HEADS = 8
DIFF_QK = 32
DIFF_V = 2 * DIFF_QK
DIFF_WIDTH = DIFF_HEADS * DIFF_V
DIFF_QW = DIFF_HEADS * 2 * DIFF_QK
ROPE_PAIRS = DIFF_QK // 4
ROPE_BASE = 10000.0
Q_BLOCK = 128
RWKV_HEADS = 8
RWKV_HS = 64
RWKV_WIDTH = RWKV_HEADS * RWKV_HS
DECAY_LORA = 64
ICL_LORA = 64
GATE_LORA = 128
RWKV_IN = 3 * RWKV_WIDTH + DECAY_LORA + ICL_LORA + GATE_LORA
AB_IN = 2 * DIFF_QW + DIFF_WIDTH + RWKV_IN
AB_OUT = DIFF_WIDTH + RWKV_WIDTH
HYENA_ORDER = 2
BANDS = 16
FILTER_EMB = 1 + 2 * BANDS
FILTER_WIDTH = 64
FILTER_OUT = HYENA_ORDER * 2 * D_MODEL
MAX_DECAY = math.log(1e-2) / 0.3
MIN_DECAY = math.log(1e-2) / 1.5
D_FF = ((8 * D_MODEL // 3 + 255) // 256) * 256
RMS_EPS = 1e-6
GN_EPS = 64e-5

kernel_name = 'hybrid_diffattn_rwkv7_hyena_prefix_dit_step'


def _lambda_init(l):
    return 0.8 - 0.6 * math.exp(-0.3 * l)


def _layer_slice(stack, i):
    return {n: a[i] for n, a in stack.items()}


def _rmsnorm(x, g):
    xf = x.astype(jnp.float32)
    y = xf * lax.rsqrt(jnp.mean(xf * xf, axis=-1, keepdims=True) + RMS_EPS)
    return y.astype(x.dtype) * g


def _ada(cond, w, b):
    m = jax.nn.silu(cond) @ w + b
    return [t[:, None, :] for t in jnp.split(m, 6, axis=-1)]


def _ffn_sub(x, shift, scale, gate, g, w1, w3, w2):
    h = _rmsnorm(x, g) * (1 + scale) + shift
    return x + gate * ((jax.nn.silu(h @ w1) * (h @ w3)) @ w2)


def _axial_rope(L):
    n_rows = L // GRID_W
    row = jnp.repeat(jnp.arange(n_rows, dtype=jnp.float32), GRID_W)
    col = jnp.tile(jnp.arange(GRID_W, dtype=jnp.float32), n_rows)
    inv = ROPE_BASE ** (-jnp.arange(ROPE_PAIRS, dtype=jnp.float32) / ROPE_PAIRS)
    ar = row[:, None] * inv[None]
    ac = col[:, None] * inv[None]
    ang = jnp.concatenate([ar, ar, ac, ac], axis=-1)
    return jnp.cos(ang), jnp.sin(ang)


def _apply_rope(x, cos, sin):
    xs = x.reshape(x.shape[:-1] + (2, 2, ROPE_PAIRS))
    rot = jnp.stack([-xs[..., 1, :], xs[..., 0, :]], axis=-2).reshape(x.shape)
    c = cos.astype(x.dtype)[None, :, None, None, :]
    s = sin.astype(x.dtype)[None, :, None, None, :]
    return x * c + rot * s


def _diff_attend(q, k, v, lam):
    B, H, _, Lq, dq = q.shape
    nblk = Lq // Q_BLOCK
    qb = jnp.moveaxis(q.reshape(B, H, 2, nblk, Q_BLOCK, dq), 3, 0)
    scale = dq ** -0.5

    def one(qi):
        s = jnp.einsum('bhmqd,bhmkd->bhmqk', qi, k).astype(jnp.float32) * scale
        p = jax.nn.softmax(s, axis=-1)
        p = p[:, :, 0] - lam * p[:, :, 1]
        return jnp.einsum('bhqk,bhkd->bhqd', p.astype(v.dtype), v)

    o = lax.map(one, qb)
    return jnp.moveaxis(o, 0, 2).reshape(B, H, Lq, v.shape[-1])


def _head_rms(o, g):
    of = o.astype(jnp.float32)
    y = of * lax.rsqrt(jnp.mean(of * of, axis=-1, keepdims=True) + 1e-5)
    return y.astype(o.dtype) * g


def _center_shift(p, mu):
    prev = jnp.pad(p, ((0, 0), (1, 0), (0, 0)))[:, :-1]
    nxt = jnp.pad(p, ((0, 0), (0, 1), (0, 0)))[:, 1:]
    return p + mu[0] * (prev - p) + mu[1] * (nxt - p)


def _rwkv_scan(r, w, k, v, kk, b, s0, reverse):
    xs = tuple(jnp.moveaxis(t.astype(jnp.float32), 1, 0) for t in (r, w, k, v, kk, b))

    def step(S, inp):
        r_t, w_t, k_t, v_t, kk_t, b_t = inp
        sa = jnp.einsum('bhvk,bhk->bhv', S, -kk_t)
        S = S * w_t[:, :, None, :] + sa[..., None] * b_t[:, :, None, :] + v_t[..., None] * k_t[:, :, None, :]
        return S, jnp.einsum('bhvk,bhk->bhv', S, r_t)

    S, y = lax.scan(step, s0.astype(jnp.float32), xs, reverse=reverse)
    return jnp.moveaxis(y, 0, 1), S


def _rwkv_mix(rw, lp, s0f, s0b):
    B, L, _ = rw.shape
    W = RWKV_WIDTH
    rw = _center_shift(rw, lp['mu'])
    r, k, v = rw[..., :W], rw[..., W:2 * W], rw[..., 2 * W:3 * W]
    o = 3 * W
    wd = rw[..., o:o + DECAY_LORA]
    ad = rw[..., o + DECAY_LORA:o + DECAY_LORA + ICL_LORA]
    gd = rw[..., o + DECAY_LORA + ICL_LORA:]
    g = jax.nn.sigmoid(gd) @ lp['g_up']
    hd = lambda t: t.reshape(B, L, RWKV_HEADS, RWKV_HS)
    kk = hd((k * lp['k_k']).astype(jnp.float32))
    kk = kk / jnp.maximum(jnp.sqrt(jnp.sum(kk * kk, axis=-1, keepdims=True)), 1e-12)
    outs, states = [], []
    for d, s0 in enumerate((s0f, s0b)):
        w = -jax.nn.softplus(-(lp['w0'][d] + jnp.tanh(wd) @ lp['w_up'][d])) - 0.5
        decay = jnp.exp(-jnp.exp(w.astype(jnp.float32)))
        a = jax.nn.sigmoid(lp['a0'][d] + ad @ lp['a_up'][d])
        kd = k * (1 + (a - 1) * lp['k_a'])
        yd, sd = _rwkv_scan(hd(r), hd(decay), hd(kd), hd(v), kk, kk * hd(a).astype(jnp.float32), s0, d == 1)
        bonus = jnp.sum(hd(r) * hd(kd) * lp['r_k'], axis=-1, keepdims=True) * hd(v)
        outs.append(yd + bonus.astype(jnp.float32))
        states.append(sd)
    y = outs[0] + outs[1]
    mu = jnp.mean(y, axis=-1, keepdims=True)
    var = jnp.mean(jnp.square(y - mu), axis=-1, keepdims=True)
    y = ((y - mu) * lax.rsqrt(var + GN_EPS)).reshape(B, L, W).astype(rw.dtype)
    y = (y * lp['ln_g'] + lp['ln_b']) * g
    return y, states[0].astype(rw.dtype), states[1].astype(rw.dtype)


def _ab_mixer(h, lp, lam_init, rope, k_ctx, v_ctx, s0f, s0b):
    B, L, _ = h.shape
    p = h @ lp['w_in']
    q = p[..., :DIFF_QW].reshape(B, L, DIFF_HEADS, 2, DIFF_QK)
    k = p[..., DIFF_QW:2 * DIFF_QW].reshape(B, L, DIFF_HEADS, 2, DIFF_QK)
    v = p[..., 2 * DIFF_QW:2 * DIFF_QW + DIFF_WIDTH].reshape(B, L, DIFF_HEADS, DIFF_V)
    rw = p[..., 2 * DIFF_QW + DIFF_WIDTH:]
    if rope is not None:
        q = _apply_rope(q, rope[0], rope[1])
        k = _apply_rope(k, rope[0], rope[1])
    q = q.transpose(0, 2, 3, 1, 4)
    k = k.transpose(0, 2, 3, 1, 4)
    v = v.transpose(0, 2, 1, 3)
    if k_ctx is None:
        keys, vals = k, v
    else:
        keys = jnp.concatenate([k_ctx, k], axis=3)
        vals = jnp.concatenate([v_ctx, v], axis=2)
    lv = lp['lam'].astype(jnp.float32)
    lam = jnp.exp(jnp.sum(lv[0] * lv[1])) - jnp.exp(jnp.sum(lv[2] * lv[3])) + lam_init
    o = _diff_attend(q, keys, vals, lam)
    o = _head_rms(o, lp['subln_g']) * (1 - lam_init)
    o = o.transpose(0, 2, 1, 3).reshape(B, L, DIFF_WIDTH)
    if s0f is None:
        s0f = jnp.zeros((B, RWKV_HEADS, RWKV_HS, RWKV_HS), jnp.float32)
        s0b = s0f
    r_out, sf, sb = _rwkv_mix(rw, lp, s0f, s0b)
    out = jnp.concatenate([o, r_out], axis=-1) @ lp['w_out']
    return out, k, v, sf, sb


def _short_conv(z, w, b):
    L = z.shape[1]
    zp = jnp.pad(z, ((0, 0), (1, 1), (0, 0)))
    return zp[:, :L] * w[0] + zp[:, 1:L + 1] * w[1] + zp[:, 2:] * w[2] + b


def _hyena_filters(L, lp):
    f32 = jnp.float32
    t = jnp.linspace(0.0, 1.0, L, dtype=f32)[:, None]
    wpos = 2.0 * math.pi * jnp.arange(L, dtype=f32)[:, None] / L
    f = jnp.linspace(1e-4, BANDS - 1, BANDS, dtype=f32)[None]
    z = jnp.concatenate([t, jnp.cos(f * wpos), -jnp.sin(f * wpos)], axis=-1)
    fr = lp['f_freq'].astype(f32)
    hdn = jnp.sin(fr[0] * (z @ lp['f_w0'].astype(f32) + lp['f_b0'].astype(f32)))
    hdn = jnp.sin(fr[1] * (hdn @ lp['f_w1'].astype(f32) + lp['f_b1'].astype(f32)))
    hdn = jnp.sin(fr[2] * (hdn @ lp['f_w2'].astype(f32) + lp['f_b2'].astype(f32)))
    filt = (hdn @ lp['f_w3'].astype(f32)).reshape(L, HYENA_ORDER, 2, D_MODEL)
    deltas = jnp.abs(jnp.linspace(MIN_DECAY, MAX_DECAY, D_MODEL, dtype=f32))
    window = jnp.exp(-t * deltas[None])
    return filt * window[:, None, None, :]


def _bidir_long_conv(u, h_fwd, h_bwd, bias):
    L = u.shape[1]
    kern = jnp.concatenate([h_fwd, jnp.zeros((1, h_fwd.shape[1]), h_fwd.dtype), h_bwd[:0:-1]], axis=0)
    uf32 = u.astype(jnp.float32)
    uf = jnp.fft.rfft(uf32, n=2 * L, axis=1)
    kf = jnp.fft.rfft(kern, axis=0)
    y = jnp.fft.irfft(uf * kf[None], n=2 * L, axis=1)[:, :L]
    return (y + uf32 * bias.astype(jnp.float32)).astype(u.dtype)


def _hyena_mixer(h, lp):
    L = h.shape[1]
    z = _short_conv(h @ lp['w_in'] + lp['b_in'], lp['conv_w'], lp['conv_b'])
    x1, x2, v = jnp.split(z, 3, axis=-1)
    filt = _hyena_filters(L, lp)
    u = x1 * _bidir_long_conv(v, filt[:, 0, 0], filt[:, 0, 1], lp['bias'][0])
    u = x2 * _bidir_long_conv(u, filt[:, 1, 0], filt[:, 1, 1], lp['bias'][1])
    return u @ lp['w_out'] + lp['b_out']


def setup_inputs(seed: int = 0) -> dict:
    key = jax.random.key(seed)
    ks = iter(jax.random.split(key, 64))
    D = D_MODEL

    def nrm(shape, scale):
        return jax.random.normal(next(ks), shape, jnp.float32) * scale

    def unif(shape, lo, hi):
        return jax.random.uniform(next(ks), shape, jnp.float32, lo, hi)

    return {
        'x_prompt': nrm((BATCH, SEQ, D), 1.0),
        'x_sample': nrm((DEC_BATCH, DEC_SEQ, D), 1.0),
        'cache_diff_k': nrm((DEC_BATCH, N_AB, DIFF_HEADS, 2, PAST_LEN, DIFF_QK), 1.0),
        'cache_diff_v': nrm((DEC_BATCH, N_AB, DIFF_HEADS, PAST_LEN, DIFF_V), 1.0),
        'state_rwkv_fwd': nrm((DEC_BATCH, N_AB, RWKV_HEADS, RWKV_HS, RWKV_HS), 0.5),
        'state_rwkv_bwd': nrm((DEC_BATCH, N_AB, RWKV_HEADS, RWKV_HS, RWKV_HS), 0.5),
        'c': nrm((DEC_BATCH, D), 1.0),
        'c_ctx': nrm((D,), 1.0),
        'ada_w': nrm((DEPTH, D, 6 * D), 0.5 * D ** -0.5),
        'ada_b': nrm((DEPTH, 6 * D), 0.02),
        'norm1_g': 1.0 + nrm((DEPTH, D), 0.05),
        'norm2_g': 1.0 + nrm((DEPTH, D), 0.05),
        'ffn_w1': nrm((DEPTH, D, D_FF), D ** -0.5),
        'ffn_w3': nrm((DEPTH, D, D_FF), D ** -0.5),
        'ffn_w2': nrm((DEPTH, D_FF, D), D_FF ** -0.5),
        'final_g': 1.0 + nrm((D,), 0.05),
        'ab_w_in': nrm((N_AB, D, AB_IN), D ** -0.5),
        'ab_w_out': nrm((N_AB, AB_OUT, D), AB_OUT ** -0.5),
        'diff_lambda': nrm((N_AB, 4, DIFF_QK), 0.1),
        'diff_subln_g': 1.0 + nrm((N_AB, DIFF_V), 0.05),
        'rwkv_mu': unif((N_AB, 2, RWKV_IN), 0.0, 0.5),
        'rwkv_w0': unif((N_AB, 2, RWKV_WIDTH), -6.0, 1.0),
        'rwkv_w_up': nrm((N_AB, 2, DECAY_LORA, RWKV_WIDTH), 0.1),
        'rwkv_a0': nrm((N_AB, 2, RWKV_WIDTH), 0.5),
        'rwkv_a_up': nrm((N_AB, 2, ICL_LORA, RWKV_WIDTH), 0.1),
        'rwkv_g_up': nrm((N_AB, GATE_LORA, RWKV_WIDTH), GATE_LORA ** -0.5),
        'rwkv_k_k': 0.85 + nrm((N_AB, RWKV_WIDTH), 0.05),
        'rwkv_k_a': 1.0 + nrm((N_AB, RWKV_WIDTH), 0.05),
        'rwkv_r_k': nrm((N_AB, RWKV_HEADS, RWKV_HS), 0.1),
        'rwkv_ln_g': 1.0 + nrm((N_AB, RWKV_WIDTH), 0.05),
        'rwkv_ln_b': nrm((N_AB, RWKV_WIDTH), 0.02),
        'hy_w_in': nrm((N_C, D, 3 * D), D ** -0.5),
        'hy_b_in': nrm((N_C, 3 * D), 0.02),
        'hy_conv_w': nrm((N_C, 3, 3 * D), 0.5),
        'hy_conv_b': nrm((N_C, 3 * D), 0.02),
        'hy_f_w0': nrm((N_C, FILTER_EMB, FILTER_WIDTH), FILTER_EMB ** -0.5),
        'hy_f_b0': nrm((N_C, FILTER_WIDTH), 0.1),
        'hy_f_w1': nrm((N_C, FILTER_WIDTH, FILTER_WIDTH), FILTER_WIDTH ** -0.5),
        'hy_f_b1': nrm((N_C, FILTER_WIDTH), 0.1),
        'hy_f_w2': nrm((N_C, FILTER_WIDTH, FILTER_WIDTH), FILTER_WIDTH ** -0.5),
        'hy_f_b2': nrm((N_C, FILTER_WIDTH), 0.1),
        'hy_f_w3': nrm((N_C, FILTER_WIDTH, FILTER_OUT), 0.01),
        'hy_f_freq': 1.0 + nrm((N_C, 3, FILTER_WIDTH), 0.1),
        'hy_bias': nrm((N_C, HYENA_ORDER, D), 0.5),
        'hy_w_out': nrm((N_C, D, D), D ** -0.5),
        'hy_b_out': nrm((N_C, D), 0.02),
    }


def reference(x_prompt, x_sample, cache_diff_k, cache_diff_v, state_rwkv_fwd, state_rwkv_bwd, c, c_ctx,
              ada_w, ada_b, norm1_g, norm2_g, ffn_w1, ffn_w3, ffn_w2, final_g,
              ab_w_in, ab_w_out, diff_lambda, diff_subln_g, rwkv_mu, rwkv_w0, rwkv_w_up, rwkv_a0, rwkv_a_up,
              rwkv_g_up, rwkv_k_k, rwkv_k_a, rwkv_r_k, rwkv_ln_g, rwkv_ln_b,
              hy_w_in, hy_b_in, hy_conv_w, hy_conv_b, hy_f_w0, hy_f_b0, hy_f_w1, hy_f_b1, hy_f_w2, hy_f_b2,
              hy_f_w3, hy_f_freq, hy_bias, hy_w_out, hy_b_out):
    ab = dict(w_in=ab_w_in, w_out=ab_w_out, lam=diff_lambda, subln_g=diff_subln_g, mu=rwkv_mu,
              w0=rwkv_w0, w_up=rwkv_w_up, a0=rwkv_a0, a_up=rwkv_a_up, g_up=rwkv_g_up, k_k=rwkv_k_k,
              k_a=rwkv_k_a, r_k=rwkv_r_k, ln_g=rwkv_ln_g, ln_b=rwkv_ln_b)
    hy = dict(w_in=hy_w_in, b_in=hy_b_in, conv_w=hy_conv_w, conv_b=hy_conv_b, f_w0=hy_f_w0, f_b0=hy_f_b0,
              f_w1=hy_f_w1, f_b1=hy_f_b1, f_w2=hy_f_w2, f_b2=hy_f_b2, f_w3=hy_f_w3, f_freq=hy_f_freq,
              bias=hy_bias, w_out=hy_w_out, b_out=hy_b_out)

    x = x_prompt
    new_k, new_v, new_sf, new_sb = [], [], [], []
    for l in range(DEPTH):
        sh1, sc1, gt1, sh2, sc2, gt2 = _ada(c_ctx[None], ada_w[l], ada_b[l])
        h = _rmsnorm(x, norm1_g[l]) * (1 + sc1) + sh1
        if l % 2 == 0:
            out, kc, vc, sf, sb = _ab_mixer(h, _layer_slice(ab, l // 2), _lambda_init(l),
                                            None, None, None, None, None)
            new_k.append(kc)
            new_v.append(vc)
            new_sf.append(sf)
            new_sb.append(sb)
        else:
            out = _hyena_mixer(h, _layer_slice(hy, l // 2))
        x = x + gt1 * out
        x = _ffn_sub(x, sh2, sc2, gt2, norm2_g[l], ffn_w1[l], ffn_w3[l], ffn_w2[l])
    y_prompt = _rmsnorm(x, final_g)
    new_diff_k = jnp.stack(new_k, axis=1)
    new_diff_v = jnp.stack(new_v, axis=1)
    new_rwkv_fwd = jnp.stack(new_sf, axis=1)
    new_rwkv_bwd = jnp.stack(new_sb, axis=1)

    rope = _axial_rope(x_sample.shape[1])
    x = x_sample
    for l in range(DEPTH):
        sh1, sc1, gt1, sh2, sc2, gt2 = _ada(c, ada_w[l], ada_b[l])
        h = _rmsnorm(x, norm1_g[l]) * (1 + sc1) + sh1
        if l % 2 == 0:
            i = l // 2
            out, _, _, _, _ = _ab_mixer(h, _layer_slice(ab, i), _lambda_init(l), rope,
                                        cache_diff_k[:, i], cache_diff_v[:, i],
                                        state_rwkv_fwd[:, i], state_rwkv_bwd[:, i])
        else:
            out = _hyena_mixer(h, _layer_slice(hy, l // 2))
        x = x + gt1 * out
        x = _ffn_sub(x, sh2, sc2, gt2, norm2_g[l], ffn_w1[l], ffn_w3[l], ffn_w2[l])
    y_sample = _rmsnorm(x, final_g)

    return (y_prompt, y_sample, new_diff_k, new_diff_v, new_rwkv_fwd, new_rwkv_bwd)
```

```python
import functools
import math

import jax
import jax.numpy as jnp
from jax import lax
from jax.experimental import pallas as pl
from jax.experimental.pallas import tpu as pltpu

F32 = jnp.float32
BF16 = jnp.bfloat16

D_MODEL = 1024
DEPTH = 4
GRID_W = 64
DIFF_HEADS = 8
DIFF_QK = 32
DIFF_V = 64
DIFF_QW = DIFF_HEADS * 2 * DIFF_QK
DIFF_WIDTH = DIFF_HEADS * DIFF_V
ROPE_PAIRS = DIFF_QK // 4
ROPE_BASE = 10000.0
RWKV_HEADS = 8
RWKV_HS = 64
RWKV_WIDTH = RWKV_HEADS * RWKV_HS
RWKV_IN = 3 * RWKV_WIDTH + 64 + 64 + 128
AB_IN = 2 * DIFF_QW + DIFF_WIDTH + RWKV_IN
BANDS = 16
FILTER_WIDTH = 64
MAX_DECAY = math.log(1e-2) / 0.3
MIN_DECAY = math.log(1e-2) / 1.5
D_FF = 2816
RMS_EPS = 1e-6
GN_EPS = 64e-5
HEAD_RMS_EPS = 1e-5

LANES = 128
CHUNK = 64
FREQ_TILE = 256
VMEM_LIMIT = 56 * 1024 * 1024

N_MOD_ROWS = 8


def _lambda_init(l):
    return 0.8 - 0.6 * math.exp(-0.3 * l)


def _params(*sem):
    return pltpu.CompilerParams(dimension_semantics=sem, vmem_limit_bytes=VMEM_LIMIT)


def _bdot(a, b):
    return jnp.dot(a.astype(BF16), b.astype(BF16), preferred_element_type=F32)


def _bdot_nt(a, b):
    return lax.dot_general(a.astype(BF16), b.astype(BF16), (((1,), (1,)), ((), ())),
                           preferred_element_type=F32)


def _split3(x):
    h1 = x.astype(BF16)
    r1 = x - h1.astype(F32)
    h2 = r1.astype(BF16)
    h3 = (r1 - h2.astype(F32)).astype(BF16)
    return h1, h2, h3


def _dot_f32(a, b):
    a1, a2, a3 = _split3(a)
    b1, b2, b3 = _split3(b)
    d = lambda x, y: jnp.dot(x, y, preferred_element_type=F32)
    return (d(a1, b1) + (d(a1, b2) + d(a2, b1))) + ((d(a1, b3) + d(a3, b1)) + d(a2, b2))


def _dot_exact_lhs(a_bf16, b):
    b1, b2, b3 = _split3(b)
    d = lambda y: jnp.dot(a_bf16, y, preferred_element_type=F32)
    return d(b1) + (d(b2) + d(b3))


def _sigmoid(x):
    return 1.0 / (1.0 + jnp.exp(-x))


def _iota(shape, dim):
    return lax.broadcasted_iota(jnp.int32, shape, dim)


def _shift_rows(x):
    n = x.shape[0]
    row = _iota(x.shape, 0)
    prev = jnp.where(row == 0, 0.0, pltpu.roll(x, 1, 0))
    nxt = jnp.where(row == n - 1, 0.0, pltpu.roll(x, n - 1, 0))
    return prev, nxt


def _half_sum(x, lo_mask):
    s_lo = jnp.sum(jnp.where(lo_mask, x, 0.0), axis=-1, keepdims=True)
    s_hi = jnp.sum(jnp.where(lo_mask, 0.0, x), axis=-1, keepdims=True)
    return jnp.where(lo_mask, s_lo, s_hi)


def _ada_kernel(c_ref, w_ref, b_ref, o_ref):
    c = c_ref[...]
    o_ref[0] = _bdot(c * _sigmoid(c), w_ref[0]) + b_ref[0]


def _ada_all(cond, ada_w, ada_b):
    tn = 1536
    n_out = 6 * D_MODEL
    return pl.pallas_call(
        _ada_kernel,
        out_shape=jax.ShapeDtypeStruct((DEPTH, N_MOD_ROWS, n_out), F32),
        grid=(DEPTH, n_out // tn),
        in_specs=[
            pl.BlockSpec((N_MOD_ROWS, D_MODEL), lambda l, j: (0, 0)),
            pl.BlockSpec((1, D_MODEL, tn), lambda l, j: (l, 0, j)),
            pl.BlockSpec((1, 1, tn), lambda l, j: (l, 0, j)),
        ],
        out_specs=pl.BlockSpec((1, N_MOD_ROWS, tn), lambda l, j: (l, 0, j)),
        compiler_params=_params("parallel", "parallel"),
        name="ada",
    )(cond, ada_w, ada_b.reshape(DEPTH, 1, n_out))


def _mod_spec(layer, chunk, row_of_block):
    def index_map(i, *_):
        return ((layer * N_MOD_ROWS + row_of_block(i)) * 6 + chunk, 0, 0)
    return pl.BlockSpec((1, 1, D_MODEL), index_map)


def _normed(x, g, sc, sh):
    ms = jnp.mean(x * x, axis=-1, keepdims=True)
    return (x * lax.rsqrt(ms + RMS_EPS)) * g * (1.0 + sc) + sh


def _proj_in_kernel(x_ref, g_ref, sc_ref, sh_ref, w_ref, b_ref, o_ref, h_scr):
    @pl.when(pl.program_id(1) == 0)
    def _():
        h_scr[...] = _normed(x_ref[...], g_ref[...], sc_ref[0], sh_ref[0]).astype(BF16)

    o_ref[...] = jnp.dot(h_scr[...], w_ref[...].astype(BF16), preferred_element_type=F32) + b_ref[...]


def _proj_in(x, g, mods, layer, row_of_block, w, b, tm, tn):
    t, n_out = x.shape[0], w.shape[1]
    return pl.pallas_call(
        _proj_in_kernel,
        out_shape=jax.ShapeDtypeStruct((t, n_out), F32),
        grid=(t // tm, n_out // tn),
        in_specs=[
            pl.BlockSpec((tm, D_MODEL), lambda i, j: (i, 0)),
            pl.BlockSpec((1, D_MODEL), lambda i, j: (0, 0)),
            _mod_spec(layer, 1, row_of_block),
            _mod_spec(layer, 0, row_of_block),
            pl.BlockSpec((D_MODEL, tn), lambda i, j: (0, j)),
            pl.BlockSpec((1, tn), lambda i, j: (0, j)),
        ],
        out_specs=pl.BlockSpec((tm, tn), lambda i, j: (i, j)),
        scratch_shapes=[pltpu.VMEM((tm, D_MODEL), BF16)],
        compiler_params=_params("parallel", "arbitrary"),
        name="proj_in",
    )(x, g.reshape(1, D_MODEL), mods, mods, w, b.reshape(1, n_out))


def _proj_out_kernel(n_u, x_ref, gt_ref, *refs):
    u_refs, (w_ref, b_ref, o_ref) = refs[:n_u], refs[n_u:]
    acc = b_ref[...]
    off = 0
    for u_ref in u_refs:
        k = u_ref.shape[1]
        acc = acc + _bdot(u_ref[...], w_ref[off:off + k, :])
        off += k
    o_ref[...] = x_ref[...] + gt_ref[0] * acc


def _proj_out(x, mods, layer, row_of_block, us, w, b, tm):
    t = x.shape[0]
    return pl.pallas_call(
        functools.partial(_proj_out_kernel, len(us)),
        out_shape=jax.ShapeDtypeStruct((t, D_MODEL), F32),
        grid=(t // tm,),
        in_specs=[
            pl.BlockSpec((tm, D_MODEL), lambda i: (i, 0)),
            _mod_spec(layer, 2, row_of_block),
            *[pl.BlockSpec((tm, u.shape[1]), lambda i: (i, 0)) for u in us],
            pl.BlockSpec(w.shape, lambda i: (0, 0)),
            pl.BlockSpec((1, D_MODEL), lambda i: (0, 0)),
        ],
        out_specs=pl.BlockSpec((tm, D_MODEL), lambda i: (i, 0)),
        compiler_params=_params("parallel"),
        name="proj_out",
    )(x, mods, *us, w, b.reshape(1, D_MODEL))


def _ffn_kernel(final, x_ref, g_ref, sc_ref, sh_ref, gt_ref, w1_ref, w3_ref, w2_ref, fg_ref, o_ref,
                h_scr, acc_scr):
    k = pl.program_id(1)

    @pl.when(k == 0)
    def _():
        h_scr[...] = _normed(x_ref[...], g_ref[...], sc_ref[0], sh_ref[0]).astype(BF16)
        acc_scr[...] = jnp.zeros_like(acc_scr)

    h = h_scr[...]
    a1 = jnp.dot(h, w1_ref[...].astype(BF16), preferred_element_type=F32)
    a3 = jnp.dot(h, w3_ref[...].astype(BF16), preferred_element_type=F32)
    acc_scr[...] += _bdot((a1 * _sigmoid(a1)) * a3, w2_ref[...])

    @pl.when(k == pl.num_programs(1) - 1)
    def _():
        y = x_ref[...] + gt_ref[0] * acc_scr[...]
        if final:
            ms = jnp.mean(y * y, axis=-1, keepdims=True)
            y = (y * lax.rsqrt(ms + RMS_EPS)) * fg_ref[...]
        o_ref[...] = y


def _ffn(x, g, mods, layer, row_of_block, w1, w3, w2, final_g, final, tm, tk):
    t = x.shape[0]
    return pl.pallas_call(
        functools.partial(_ffn_kernel, final),
        out_shape=jax.ShapeDtypeStruct((t, D_MODEL), F32),
        grid=(t // tm, D_FF // tk),
        in_specs=[
            pl.BlockSpec((tm, D_MODEL), lambda i, k: (i, 0)),
            pl.BlockSpec((1, D_MODEL), lambda i, k: (0, 0)),
            _mod_spec(layer, 4, row_of_block),
            _mod_spec(layer, 3, row_of_block),
            _mod_spec(layer, 5, row_of_block),
            pl.BlockSpec((D_MODEL, tk), lambda i, k: (0, k)),
            pl.BlockSpec((D_MODEL, tk), lambda i, k: (0, k)),
            pl.BlockSpec((tk, D_MODEL), lambda i, k: (k, 0)),
            pl.BlockSpec((1, D_MODEL), lambda i, k: (0, 0)),
        ],
        out_specs=pl.BlockSpec((tm, D_MODEL), lambda i, k: (i, 0)),
        scratch_shapes=[pltpu.VMEM((tm, D_MODEL), BF16), pltpu.VMEM((tm, D_MODEL), F32)],
        compiler_params=_params("parallel", "arbitrary"),
        name="ffn",
    )(x, g.reshape(1, D_MODEL), mods, mods, mods, w1, w3, w2, final_g.reshape(1, D_MODEL))


def _rope(x, cos, sin):
    lane = _iota(x.shape, 1)
    rot = jnp.where((lane % 16) < 8, -pltpu.roll(x, LANES - 8, 1), pltpu.roll(x, 8, 1))
    return x * cos + rot * sin


def _attn_kernel(n_ctx, use_rope, lam_init, *refs):
    it = iter(refs)
    q_ref, k_ref, v_ref = next(it), next(it), next(it)
    kc_ref = vc_ref = cq_ref = sq_ref = ck_ref = sk_ref = None
    if n_ctx:
        kc_ref, vc_ref = next(it), next(it)
    if use_rope:
        cq_ref, sq_ref, ck_ref, sk_ref = next(it), next(it), next(it), next(it)
    lam_ref, g_ref, o_ref, kall, vall = next(it), next(it), next(it), next(it), next(it)
    n_own = k_ref.shape[0]

    @pl.when(pl.program_id(2) == 0)
    def _():
        k = k_ref[...]
        if use_rope:
            k = _rope(k, ck_ref[...], sk_ref[...])
        if n_ctx:
            kall[0:n_ctx, :] = kc_ref[0, 0].astype(BF16)
            vall[0:n_ctx, :] = vc_ref[0, 0].astype(BF16)
        kall[n_ctx:n_ctx + n_own, :] = k.astype(BF16)
        vall[n_ctx:n_ctx + n_own, :] = v_ref[...].astype(BF16)

    lv = lam_ref[...]
    lam = (jnp.exp(jnp.sum(lv[0:1] * lv[1:2], axis=-1, keepdims=True))
           - jnp.exp(jnp.sum(lv[2:3] * lv[3:4], axis=-1, keepdims=True)) + lam_init)

    q = q_ref[...]
    if use_rope:
        q = _rope(q, cq_ref[...], sq_ref[...])
    lane = _iota(q.shape, 1)
    scale = DIFF_QK ** -0.5
    ks, vs = kall[...], vall[...]
    outs = []
    for h in range(2):
        probs = []
        for m in range(2):
            j = 2 * h + m
            qm = jnp.where((lane >= DIFF_QK * j) & (lane < DIFF_QK * (j + 1)), q, 0.0)
            s = _bdot_nt(qm, ks) * scale
            e = jnp.exp(s - jnp.max(s, axis=-1, keepdims=True))
            probs.append(e * (1.0 / jnp.sum(e, axis=-1, keepdims=True)))
        outs.append(_bdot(probs[0] - lam * probs[1], vs))
    lo = lane < DIFF_V
    o = jnp.where(lo, outs[0], outs[1])
    ms = _half_sum(o * o, lo) * (1.0 / DIFF_V)
    o_ref[...] = (o * lax.rsqrt(ms + HEAD_RMS_EPS)) * g_ref[...] * (1.0 - lam_init)


def _attention(p, seq, tq, lam_init, lam, subln_g, ctx=None, rope=None):
    t = p.shape[0]
    nb, nq = t // seq, seq // tq
    n_ctx = 0 if ctx is None else ctx[0].shape[2]
    k_blk, v_blk = DIFF_QW // LANES, 2 * DIFF_QW // LANES
    in_specs = [
        pl.BlockSpec((tq, LANES), lambda b, hp, qi: (b * nq + qi, hp)),
        pl.BlockSpec((seq, LANES), lambda b, hp, qi: (b, k_blk + hp)),
        pl.BlockSpec((seq, LANES), lambda b, hp, qi: (b, v_blk + hp)),
    ]
    args = [p, p, p]
    if ctx is not None:
        in_specs += [pl.BlockSpec((1, 1, n_ctx, LANES), lambda b, hp, qi: (b, hp, 0, 0))] * 2
        args += list(ctx)
    if rope is not None:
        in_specs += [pl.BlockSpec((tq, LANES), lambda b, hp, qi: (qi, 0))] * 2
        in_specs += [pl.BlockSpec((seq, LANES), lambda b, hp, qi: (0, 0))] * 2
        args += [rope[0], rope[1], rope[0], rope[1]]
    in_specs += [pl.BlockSpec((4, DIFF_QK), lambda b, hp, qi: (0, 0)),
                 pl.BlockSpec((1, LANES), lambda b, hp, qi: (0, 0))]
    args += [lam, jnp.tile(subln_g, 2).reshape(1, LANES)]
    return pl.pallas_call(
        functools.partial(_attn_kernel, n_ctx, rope is not None, lam_init),
        out_shape=jax.ShapeDtypeStruct((t, DIFF_WIDTH), F32),
        grid=(nb, DIFF_HEADS // 2, nq),
        in_specs=in_specs,
        out_specs=pl.BlockSpec((tq, LANES), lambda b, hp, qi: (b * nq + qi, hp)),
        scratch_shapes=[pltpu.VMEM((n_ctx + seq, LANES), BF16), pltpu.VMEM((n_ctx + seq, LANES), BF16)],
        compiler_params=_params("parallel", "parallel", "arbitrary"),
        name="diff_attn",
    )(*args)


def _tri_inverse(n_mat, eye, blk8, merge_masks):
    nd = jnp.where(blk8, n_mat, 0.0)
    t = eye + nd
    p = _bdot(nd, nd)
    t = t + _bdot(t, p)
    p = _bdot(p, p)
    t = t + _bdot(t, p)
    for msk in merge_masks:
        t = t + _bdot(t, _bdot(jnp.where(msk, n_mat, 0.0), t))
    return t


def _rwkv_kernel(has_s0, *refs):
    it = iter(refs)
    r_ref, k_ref, v_ref, wa_ref, gd_ref = (next(it) for _ in range(5))
    mu_r, mu_k, mu_v, mu_wa, mu_gd = (next(it) for _ in range(5))
    w0_ref, a0_ref, wa_up_ref, g_up_ref, vec_ref = (next(it) for _ in range(5))
    s0f_ref = s0b_ref = None
    if has_s0:
        s0f_ref, s0b_ref = next(it), next(it)
    o_ref, sf_ref, sb_ref = next(it), next(it), next(it)
    r_s, v_s, kk_s = next(it), next(it), next(it)
    lw_s, kd_s, b_s, y_s = next(it), next(it), next(it), next(it)
    st_s = next(it)

    seq = r_ref.shape[0]
    n_chunks = seq // CHUNK

    def shifted(ref, mu_ref):
        x = ref[...]
        prev, nxt = _shift_rows(x)
        mu = mu_ref[...]
        return x + mu[0:1] * (prev - x) + mu[1:2] * (nxt - x)

    r = shifted(r_ref, mu_r)
    k = shifted(k_ref, mu_k)
    v = shifted(v_ref, mu_v)
    wa = shifted(wa_ref, mu_wa)
    gd = shifted(gd_ref, mu_gd)
    vec = vec_ref[...]
    k_k, k_a, r_k, ln_g, ln_b = (vec[i:i + 1] for i in range(5))

    lane = _iota((seq, LANES), 1)
    lo = lane < RWKV_HS
    gate = _bdot(_sigmoid(gd), g_up_ref[...])
    kk = k * k_k
    kk = kk / jnp.maximum(jnp.sqrt(_half_sum(kk * kk, lo)), 1e-12)
    r_s[...] = r
    v_s[...] = v
    kk_s[...] = kk
    wd_in = jnp.where(lo, jnp.tanh(wa), 0.0)
    ad_in = jnp.where(lo, 0.0, wa)
    bonus = jnp.zeros((seq, LANES), F32)
    for d in range(2):
        wpre = w0_ref[d:d + 1, :] + _bdot(wd_in, wa_up_ref[d])
        sp = jnp.maximum(-wpre, 0.0) + jnp.log(1.0 + jnp.exp(-jnp.abs(wpre)))
        lw_s[d] = -jnp.exp(-sp - 0.5)
        a = _sigmoid(a0_ref[d:d + 1, :] + _bdot(ad_in, wa_up_ref[d]))
        kd = k * (1.0 + (a - 1.0) * k_a)
        kd_s[d] = kd
        b_s[d] = kk * a
        bonus = bonus + _half_sum(r * kd * r_k, lo) * v

    ci = _iota((CHUNK, CHUNK), 0)
    cj = _iota((CHUNK, CHUNK), 1)
    eye = jnp.where(ci == cj, 1.0, 0.0)
    blk8 = (ci // 8) == (cj // 8)
    merge_masks = [((ci // (2 * s)) == (cj // (2 * s))) & ((ci // s) != (cj // s)) for s in (8, 16, 32)]
    before = (cj < ci, cj > ci)
    upto = (cj <= ci, cj >= ci)
    tri = tuple(jnp.where(m, 1.0, 0.0).astype(BF16) for m in upto)
    clane = _iota((CHUNK, LANES), 1)
    head_lanes = (clane < RWKV_HS, clane >= RWKV_HS)
    si = _iota((LANES, LANES), 0)
    sj = _iota((LANES, LANES), 1)
    same_head = (si < RWKV_HS) == (sj < RWKV_HS)
    diag = si == sj

    def chunk_step(d, c, st):
        rows = pl.ds(pl.multiple_of(c * CHUNK, CHUNK), CHUNK)
        r_c, v_c, kk_c = r_s[rows, :], v_s[rows, :], kk_s[rows, :]
        lw_c, kd_c, b_c = lw_s[d, rows, :], kd_s[d, rows, :], b_s[d, rows, :]
        cum = _dot_exact_lhs(tri[d], lw_c)
        tot = cum[CHUNK - 1:CHUNK, :] if d == 0 else cum[0:1, :]
        e_neg = jnp.exp(-cum)
        al = jnp.exp(cum - lw_c) * kk_c
        be = b_c * e_neg
        ka = kd_c * e_neg
        rh = r_c * jnp.exp(cum)
        wc = jnp.exp(tot)
        alp_h, uv_h, yv_h, m2_h = [], [], [], []
        for h in range(2):
            x = jnp.concatenate([jnp.where(head_lanes[h], al, 0.0), jnp.where(head_lanes[h], rh, 0.0)], axis=0)
            xb = _bdot_nt(x, be)
            xk = _bdot_nt(x, ka)
            n_mat = jnp.where(before[d], -xb[0:CHUNK], 0.0)
            g_mat = jnp.where(before[d], xk[0:CHUNK], 0.0)
            m2_h.append(jnp.where(upto[d], xb[CHUNK:], 0.0))
            m1 = jnp.where(upto[d], xk[CHUNK:], 0.0)
            t_inv = _tri_inverse(n_mat, eye, blk8, merge_masks)
            gv = _bdot(g_mat, v_c)
            z = _bdot(t_inv, jnp.concatenate([al, gv], axis=1))
            alp_h.append(z[:, :LANES])
            uv_h.append(z[:, LANES:])
            yv_h.append(_bdot(m1, v_c))
        pick = lambda xs: jnp.where(head_lanes[0], xs[0], xs[1])
        alp, uv, yv = pick(alp_h), pick(uv_h), pick(yv_h)
        be_t = (be * wc).T
        ka_t = (ka * wc).T
        pz = _bdot(be_t, jnp.concatenate([alp, uv], axis=1))
        a_mat = jnp.where(same_head, jnp.where(diag, wc, 0.0) - pz[:, :LANES], 0.0)
        b_mat = jnp.where(same_head, _bdot(ka_t, v_c) - pz[:, LANES:], 0.0)
        xs = _bdot(jnp.concatenate([alp, rh], axis=0), st)
        u = uv + xs[0:CHUNK]
        m2u = jnp.where(head_lanes[0], _bdot(m2_h[0], u), _bdot(m2_h[1], u))
        y_s[d, rows, :] = xs[CHUNK:] + yv - m2u
        return _bdot(a_mat, st) + b_mat

    def body(c, carry):
        st_s[0] = chunk_step(0, c, st_s[0])
        st_s[1] = chunk_step(1, n_chunks - 1 - c, st_s[1])
        return carry

    if has_s0:
        st_s[0] = s0f_ref[0, 0]
        st_s[1] = s0b_ref[0, 0]
    else:
        st_s[...] = jnp.zeros_like(st_s)
    lax.fori_loop(0, n_chunks, body, 0)
    sf_ref[0, 0] = st_s[0]
    sb_ref[0, 0] = st_s[1]

    y = (y_s[0] + y_s[1]) + bonus
    mean = _half_sum(y, lo) * (1.0 / RWKV_HS)
    yc = y - mean
    var = _half_sum(yc * yc, lo) * (1.0 / RWKV_HS)
    o_ref[...] = ((yc * lax.rsqrt(var + GN_EPS)) * ln_g + ln_b) * gate


def _rwkv(p, seq, mu, w0, a0, wa_up, g_up, vecs, s0=None):
    t = p.shape[0]
    nb = t // seq
    n_hp = RWKV_HEADS // 2
    base = (2 * DIFF_QW + DIFF_WIDTH) // LANES
    tail = base + 3 * n_hp
    col = lambda blk: pl.BlockSpec((seq, LANES), lambda b, hp: (b, blk(hp)))
    mu_col = lambda blk: pl.BlockSpec((2, LANES), lambda b, hp: (0, blk(hp)))
    in_specs = [
        col(lambda hp: base + hp), col(lambda hp: base + n_hp + hp), col(lambda hp: base + 2 * n_hp + hp),
        col(lambda hp: tail), col(lambda hp: tail + 1),
        mu_col(lambda hp: hp), mu_col(lambda hp: n_hp + hp), mu_col(lambda hp: 2 * n_hp + hp),
        mu_col(lambda hp: 3 * n_hp), mu_col(lambda hp: 3 * n_hp + 1),
        pl.BlockSpec((2, LANES), lambda b, hp: (0, hp)),
        pl.BlockSpec((2, LANES), lambda b, hp: (0, hp)),
        pl.BlockSpec((2, LANES, LANES), lambda b, hp: (0, 0, hp)),
        pl.BlockSpec((LANES, LANES), lambda b, hp: (0, hp)),
        pl.BlockSpec((8, LANES), lambda b, hp: (0, hp)),
    ]
    args = [p] * 5 + [mu] * 5 + [w0, a0, wa_up, g_up, vecs]
    st_spec = pl.BlockSpec((1, 1, LANES, LANES), lambda b, hp: (b, hp, 0, 0))
    if s0 is not None:
        in_specs += [st_spec, st_spec]
        args += list(s0)
    st_shape = jax.ShapeDtypeStruct((nb, n_hp, LANES, LANES), F32)
    return pl.pallas_call(
        functools.partial(_rwkv_kernel, s0 is not None),
        out_shape=(jax.ShapeDtypeStruct((t, RWKV_WIDTH), F32), st_shape, st_shape),
        grid=(nb, n_hp),
        in_specs=in_specs,
        out_specs=(pl.BlockSpec((seq, LANES), lambda b, hp: (b, hp)), st_spec, st_spec),
        scratch_shapes=[pltpu.VMEM((seq, LANES), F32)] * 3 + [pltpu.VMEM((2, seq, LANES), F32)] * 4
        + [pltpu.VMEM((2, LANES, LANES), F32)],
        compiler_params=_params("parallel", "parallel"),
        name="rwkv",
    )(*args)


def _pair_states_in(s):
    nb = s.shape[0]
    st = jnp.swapaxes(s.astype(F32), -1, -2).reshape(nb, RWKV_HEADS // 2, 2, RWKV_HS, RWKV_HS)
    z = jnp.zeros_like(st[:, :, 0])
    top = jnp.concatenate([st[:, :, 0], z], axis=-1)
    bot = jnp.concatenate([z, st[:, :, 1]], axis=-1)
    return jnp.concatenate([top, bot], axis=-2)


def _pair_states_out(st):
    nb = st.shape[0]
    h0 = st[:, :, :RWKV_HS, :RWKV_HS]
    h1 = st[:, :, RWKV_HS:, RWKV_HS:]
    s = jnp.stack([h0, h1], axis=2).reshape(nb, RWKV_HEADS, RWKV_HS, RWKV_HS)
    return jnp.swapaxes(s, -1, -2)


def _dft_kernel(seq, f_ref, ft_ref):
    t = pl.program_id(0)

    def tile(freq_axis):
        shape = (2 * FREQ_TILE, seq) if freq_axis == 0 else (seq, 2 * FREQ_TILE)
        fr = _iota(shape, freq_axis)
        tau = _iota(shape, 1 - freq_axis)
        is_sin = fr >= FREQ_TILE
        kf = t * FREQ_TILE + jnp.where(is_sin, fr - FREQ_TILE, fr)
        m = (kf * tau) & (2 * seq - 1)
        ang = m.astype(F32) * (math.pi / seq)
        val = jnp.where(is_sin, -jnp.sin(ang), jnp.cos(ang))
        nyq = jnp.where((tau & 1) == 0, 1.0, -1.0)
        return jnp.where(is_sin & (kf == 0), nyq, val)

    f_ref[0] = tile(0).astype(BF16)
    ft_ref[0] = tile(1).astype(BF16)


def _dft_mats(seq):
    nt = seq // FREQ_TILE
    return pl.pallas_call(
        functools.partial(_dft_kernel, seq),
        out_shape=(jax.ShapeDtypeStruct((nt, 2 * FREQ_TILE, seq), BF16),
                   jax.ShapeDtypeStruct((nt, seq, 2 * FREQ_TILE), BF16)),
        grid=(nt,),
        out_specs=(pl.BlockSpec((1, 2 * FREQ_TILE, seq), lambda t: (t, 0, 0)),
                   pl.BlockSpec((1, seq, 2 * FREQ_TILE), lambda t: (t, 0, 0))),
        compiler_params=_params("parallel"),
        name="dft_mats",
    )()


def _filter_kernel(seq, w0t_ref, w0c_ref, w0s_ref, b0_ref, w1_ref, b1_ref, w2_ref, b2_ref, fr_ref,
                   w3f_ref, w3b_ref, f_ref, o_ref, hf_scr, hb_scr):
    t = pl.program_id(2)
    cb = pl.program_id(1)
    width = w3f_ref.shape[1]

    @pl.when(t == 0)
    def _():
        tau = _iota((seq, BANDS), 0).astype(F32)
        band = _iota((seq, BANDS), 1).astype(F32)
        freq = 1e-4 + band * ((BANDS - 1 - 1e-4) / (BANDS - 1))
        wpos = tau * (2.0 * math.pi / seq)
        pos = _iota((seq, 1), 0).astype(F32) * (1.0 / (seq - 1))
        fr = fr_ref[...]
        pre = (pos * w0t_ref[...] + _dot_f32(jnp.cos(freq * wpos), w0c_ref[...])
               + _dot_f32(-jnp.sin(freq * wpos), w0s_ref[...]) + b0_ref[...])
        hdn = jnp.sin(fr[0:1] * pre)
        hdn = jnp.sin(fr[1:2] * (_dot_f32(hdn, w1_ref[...]) + b1_ref[...]))
        hdn = jnp.sin(fr[2:3] * (_dot_f32(hdn, w2_ref[...]) + b2_ref[...]))
        chan = (cb * width + _iota((1, width), 1)).astype(F32)
        delta = jnp.abs(MIN_DECAY + chan * ((MAX_DECAY - MIN_DECAY) / (D_MODEL - 1)))
        window = jnp.exp(-pos * delta)
        hf_scr[...] = (_dot_f32(hdn, w3f_ref[...]) * window).astype(BF16)
        hb = _dot_f32(hdn, w3b_ref[...]) * window
        hb_scr[...] = jnp.where(_iota((seq, width), 0) == 0, 0.0, hb).astype(BF16)

    f = f_ref[0]
    kf = jnp.dot(f, hf_scr[...], preferred_element_type=F32)
    kb = jnp.dot(f, hb_scr[...], preferred_element_type=F32)
    row = _iota(kf.shape, 0)
    conj = (row >= FREQ_TILE) & ~((row == FREQ_TILE) & (t == 0))
    o_ref[0] = kf + jnp.where(conj, -kb, kb)


def _hyena_filters(seq, fmat, w0, b0, w1, b1, w2, b2, w3, freq):
    nt = seq // FREQ_TILE
    width = 512
    ncb = D_MODEL // width
    full = lambda a: pl.BlockSpec(a.shape, lambda o, cb, t: (0,) * a.ndim)
    small = [w0[0:1], w0[1:1 + BANDS], w0[1 + BANDS:], b0.reshape(1, -1), w1, b1.reshape(1, -1),
             w2, b2.reshape(1, -1), freq]
    return pl.pallas_call(
        functools.partial(_filter_kernel, seq),
        out_shape=jax.ShapeDtypeStruct((nt, 2 * FREQ_TILE, 2 * D_MODEL), F32),
        grid=(2, ncb, nt),
        in_specs=[full(a) for a in small] + [
            pl.BlockSpec((FILTER_WIDTH, width), lambda o, cb, t: (0, o * 2 * ncb + cb)),
            pl.BlockSpec((FILTER_WIDTH, width), lambda o, cb, t: (0, o * 2 * ncb + ncb + cb)),
            pl.BlockSpec((1, 2 * FREQ_TILE, seq), lambda o, cb, t: (t, 0, 0)),
        ],
        out_specs=pl.BlockSpec((1, 2 * FREQ_TILE, width), lambda o, cb, t: (t, 0, o * ncb + cb)),
        scratch_shapes=[pltpu.VMEM((seq, width), BF16), pltpu.VMEM((seq, width), BF16)],
        compiler_params=_params("parallel", "parallel", "arbitrary"),
        name="hyena_filters",
    )(*small, w3, w3, fmat)


def _short_conv(z, w, b):
    prev, nxt = _shift_rows(z)
    return prev * w[0:1] + z * w[1:2] + nxt * w[2:3] + b


def _conv_kernel(conv_v, seq, v_ref, x_ref, wv_ref, bv_ref, wx_ref, bx_ref, f_ref, ft_ref, kf_ref, bias_ref,
                 o_ref, vb_scr, v_scr, x_scr, acc_scr):
    t = pl.program_id(2)

    @pl.when(t == 0)
    def _():
        v = v_ref[...]
        if conv_v:
            v = _short_conv(v, wv_ref[...], bv_ref[...])
        v_scr[...] = v
        vb_scr[...] = v.astype(BF16)
        x_scr[...] = _short_conv(x_ref[...], wx_ref[...], bx_ref[...])
        acc_scr[...] = jnp.zeros_like(acc_scr)

    uf = jnp.dot(f_ref[0], vb_scr[...], preferred_element_type=F32)
    kf = kf_ref[0]
    ur, ui = uf[0:FREQ_TILE], uf[FREQ_TILE:]
    kr, ki = kf[0:FREQ_TILE], kf[FREQ_TILE:]
    real_pair = (_iota(ur.shape, 0) == 0) & (t == 0)
    scale = jnp.where(real_pair, 0.5 / seq, 1.0 / seq)
    uiki = ui * ki
    yr = (ur * kr - jnp.where(real_pair, 0.0, uiki)) * scale
    yi = jnp.where(real_pair, uiki, ur * ki + ui * kr) * scale
    yf = jnp.concatenate([yr, yi], axis=0).astype(BF16)
    acc_scr[...] += jnp.dot(ft_ref[0], yf, preferred_element_type=F32)

    @pl.when(t == pl.num_programs(2) - 1)
    def _():
        o_ref[...] = x_scr[...] * (acc_scr[...] + v_scr[...] * bias_ref[...])


def _hyena_conv(v_arr, v_blk0, x_arr, x_blk0, conv_v, seq, cw, cwb, fmat, fmat_t, kf, kf_blk0, bias, width):
    t = v_arr.shape[0]
    nb, ncb, nt = t // seq, D_MODEL // width, seq // FREQ_TILE
    wv_blk0 = v_blk0 if conv_v else 0
    return pl.pallas_call(
        functools.partial(_conv_kernel, conv_v, seq),
        out_shape=jax.ShapeDtypeStruct((t, D_MODEL), F32),
        grid=(nb, ncb, nt),
        in_specs=[
            pl.BlockSpec((seq, width), lambda b, c, f: (b, v_blk0 + c)),
            pl.BlockSpec((seq, width), lambda b, c, f: (b, x_blk0 + c)),
            pl.BlockSpec((3, width), lambda b, c, f: (0, wv_blk0 + c)),
            pl.BlockSpec((1, width), lambda b, c, f: (0, wv_blk0 + c)),
            pl.BlockSpec((3, width), lambda b, c, f: (0, x_blk0 + c)),
            pl.BlockSpec((1, width), lambda b, c, f: (0, x_blk0 + c)),
            pl.BlockSpec((1, 2 * FREQ_TILE, seq), lambda b, c, f: (f, 0, 0)),
            pl.BlockSpec((1, seq, 2 * FREQ_TILE), lambda b, c, f: (f, 0, 0)),
            pl.BlockSpec((1, 2 * FREQ_TILE, width), lambda b, c, f: (f, 0, kf_blk0 + c)),
            pl.BlockSpec((1, width), lambda b, c, f: (0, c)),
        ],
        out_specs=pl.BlockSpec((seq, width), lambda b, c, f: (b, c)),
        scratch_shapes=[pltpu.VMEM((seq, width), BF16), pltpu.VMEM((seq, width), F32),
                        pltpu.VMEM((seq, width), F32), pltpu.VMEM((seq, width), F32)],
        compiler_params=_params("parallel", "parallel", "arbitrary"),
        name="hyena_conv",
    )(v_arr, x_arr, cw, cwb, cw, cwb, fmat, fmat_t, kf, bias.reshape(1, D_MODEL))


def _axial_rope_tables(seq):
    n_rows = seq // GRID_W
    row = jnp.repeat(jnp.arange(n_rows, dtype=F32), GRID_W)
    col = jnp.tile(jnp.arange(GRID_W, dtype=F32), n_rows)
    inv = ROPE_BASE ** (-jnp.arange(ROPE_PAIRS, dtype=F32) / ROPE_PAIRS)
    ar = row[:, None] * inv[None]
    ac = col[:, None] * inv[None]
    ang = jnp.concatenate([ar, ar, ac, ac], axis=-1)
    reps = LANES // DIFF_QK
    return jnp.tile(jnp.cos(ang), (1, reps)), jnp.tile(jnp.sin(ang), (1, reps))


def _pair_lanes(a, seq_axis):
    nb = a.shape[0]
    if a.ndim == 5:
        a = a.reshape(nb, DIFF_HEADS // 2, 2, 2, a.shape[3], DIFF_QK).transpose(0, 1, 4, 2, 3, 5)
    else:
        a = a.reshape(nb, DIFF_HEADS // 2, 2, a.shape[2], DIFF_V).transpose(0, 1, 3, 2, 4)
    return a.reshape(nb, DIFF_HEADS // 2, a.shape[2], LANES)


def kernel(x_prompt, x_sample, cache_diff_k, cache_diff_v, state_rwkv_fwd, state_rwkv_bwd, c, c_ctx, ada_w, ada_b, norm1_g, norm2_g, ffn_w1, ffn_w3, ffn_w2, final_g, ab_w_in, ab_w_out, diff_lambda, diff_subln_g, rwkv_mu, rwkv_w0, rwkv_w_up, rwkv_a0, rwkv_a_up, rwkv_g_up, rwkv_k_k, rwkv_k_a, rwkv_r_k, rwkv_ln_g, rwkv_ln_b, hy_w_in, hy_b_in, hy_conv_w, hy_conv_b, hy_f_w0, hy_f_b0, hy_f_w1, hy_f_b1, hy_f_w2, hy_f_b2, hy_f_w3, hy_f_freq, hy_bias, hy_w_out, hy_b_out):
    n_ctx_seqs, ctx_len, _ = x_prompt.shape
    n_lat_seqs, lat_len, _ = x_sample.shape
    tm = 1024

    cond = jnp.zeros((N_MOD_ROWS, D_MODEL), F32).at[0].set(c_ctx).at[1:1 + n_lat_seqs].set(c)
    mods = _ada_all(cond, ada_w, ada_b).reshape(DEPTH * N_MOD_ROWS * 6, 1, D_MODEL)

    groups = [
        dict(x=x_prompt.reshape(-1, D_MODEL), seq=ctx_len, row=lambda i: 0, tq=ctx_len, conv_width=1024),
        dict(x=x_sample.reshape(-1, D_MODEL), seq=lat_len, row=lambda i: 1 + i // (lat_len // tm), tq=256,
             conv_width=256),
    ]
    rope = _axial_rope_tables(lat_len)
    dft = {g["seq"]: _dft_mats(g["seq"]) for g in groups}
    zero_b = jnp.zeros((AB_IN,), F32)
    zero_o = jnp.zeros((D_MODEL,), F32)
    new_k, new_v, new_sf, new_sb = [], [], [], []

    for l in range(DEPTH):
        i = l // 2
        if l % 2 == 0:
            wa_up = jnp.concatenate([rwkv_w_up[i], rwkv_a_up[i]], axis=1)
            vecs = jnp.zeros((8, RWKV_WIDTH), F32).at[0].set(rwkv_k_k[i]).at[1].set(rwkv_k_a[i]) \
                .at[2].set(rwkv_r_k[i].reshape(-1)).at[3].set(rwkv_ln_g[i]).at[4].set(rwkv_ln_b[i])
        for gi, g in enumerate(groups):
            x, seq, row = g["x"], g["seq"], g["row"]
            if l % 2 == 0:
                p = _proj_in(x, norm1_g[l], mods, l, row, ab_w_in[i], zero_b, tm, AB_IN // 2)
                if gi == 0:
                    nb = x.shape[0] // seq
                    kq = p[:, DIFF_QW:2 * DIFF_QW].reshape(nb, seq, DIFF_HEADS, 2, DIFF_QK)
                    new_k.append(kq.transpose(0, 2, 3, 1, 4))
                    vq = p[:, 2 * DIFF_QW:2 * DIFF_QW + DIFF_WIDTH].reshape(nb, seq, DIFF_HEADS, DIFF_V)
                    new_v.append(vq.transpose(0, 2, 1, 3))
                    att = _attention(p, seq, g["tq"], _lambda_init(l), diff_lambda[i], diff_subln_g[i])
                    mix, sf, sb = _rwkv(p, seq, rwkv_mu[i], rwkv_w0[i], rwkv_a0[i], wa_up, rwkv_g_up[i], vecs)
                    new_sf.append(_pair_states_out(sf))
                    new_sb.append(_pair_states_out(sb))
                else:
                    ctx = (_pair_lanes(cache_diff_k[:, i], 3), _pair_lanes(cache_diff_v[:, i], 2))
                    att = _attention(p, seq, g["tq"], _lambda_init(l), diff_lambda[i], diff_subln_g[i],
                                     ctx=ctx, rope=rope)
                    s0 = (_pair_states_in(state_rwkv_fwd[:, i]), _pair_states_in(state_rwkv_bwd[:, i]))
                    mix, _, _ = _rwkv(p, seq, rwkv_mu[i], rwkv_w0[i], rwkv_a0[i], wa_up, rwkv_g_up[i], vecs, s0=s0)
                x = _proj_out(x, mods, l, row, [att, mix], ab_w_out[i], zero_o, tm)
            else:
                fmat, fmat_t = dft[seq]
                z = _proj_in(x, norm1_g[l], mods, l, row, hy_w_in[i], hy_b_in[i], tm, 1536)
                kf = _hyena_filters(seq, fmat, hy_f_w0[i], hy_f_b0[i], hy_f_w1[i], hy_f_b1[i], hy_f_w2[i],
                                    hy_f_b2[i], hy_f_w3[i], hy_f_freq[i])
                width = g["conv_width"]
                ncb = D_MODEL // width
                cw, cwb = hy_conv_w[i], hy_conv_b[i].reshape(1, -1)
                u = _hyena_conv(z, 2 * ncb, z, 0, True, seq, cw, cwb, fmat, fmat_t, kf, 0, hy_bias[i, 0], width)
                u = _hyena_conv(u, 0, z, ncb, False, seq, cw, cwb, fmat, fmat_t, kf, ncb, hy_bias[i, 1], width)
                x = _proj_out(x, mods, l, row, [u], hy_w_out[i], hy_b_out[i], tm)
            g["x"] = _ffn(x, norm2_g[l], mods, l, row, ffn_w1[l], ffn_w3[l], ffn_w2[l], final_g,
                          l == DEPTH - 1, tm, 256)

    y_prompt = groups[0]["x"].reshape(x_prompt.shape)
    y_sample = groups[1]["x"].reshape(x_sample.shape)
    return (y_prompt, y_sample, jnp.stack(new_k, axis=1), jnp.stack(new_v, axis=1),
            jnp.stack(new_sf, axis=1), jnp.stack(new_sb, axis=1))
```

```python
import functools
import math

import jax
import jax.numpy as jnp
from jax import lax
from jax.experimental import pallas as pl
from jax.experimental.pallas import tpu as pltpu

F32 = jnp.float32
BF16 = jnp.bfloat16

D_MODEL = 1024
DEPTH = 4
GRID_W = 64
DIFF_HEADS = 8
DIFF_QK = 32
DIFF_V = 64
DIFF_QW = DIFF_HEADS * 2 * DIFF_QK
DIFF_WIDTH = DIFF_HEADS * DIFF_V
ROPE_PAIRS = DIFF_QK // 4
ROPE_BASE = 10000.0
RWKV_HEADS = 8
RWKV_HS = 64
RWKV_WIDTH = RWKV_HEADS * RWKV_HS
RWKV_IN = 3 * RWKV_WIDTH + 64 + 64 + 128
AB_IN = 2 * DIFF_QW + DIFF_WIDTH + RWKV_IN
BANDS = 16
FILTER_WIDTH = 64
MAX_DECAY = math.log(1e-2) / 0.3
MIN_DECAY = math.log(1e-2) / 1.5
D_FF = 2816
RMS_EPS = 1e-6
GN_EPS = 64e-5
HEAD_RMS_EPS = 1e-5

LANES = 128
CHUNK = 64
SCAN_UNROLL = 4
FREQ_TILE = 256
VMEM_LIMIT = 56 * 1024 * 1024

N_MOD_ROWS = 8


def _lambda_init(l):
    return 0.8 - 0.6 * math.exp(-0.3 * l)


def _params(*sem):
    return pltpu.CompilerParams(dimension_semantics=sem, vmem_limit_bytes=VMEM_LIMIT)


def _bdot(a, b):
    return jnp.dot(a.astype(BF16), b.astype(BF16), preferred_element_type=F32)


def _bdot_nt(a, b):
    return lax.dot_general(a.astype(BF16), b.astype(BF16), (((1,), (1,)), ((), ())),
                           preferred_element_type=F32)


def _split3(x):
    h1 = x.astype(BF16)
    r1 = x - h1.astype(F32)
    h2 = r1.astype(BF16)
    h3 = (r1 - h2.astype(F32)).astype(BF16)
    return h1, h2, h3


def _dot_f32(a, b):
    a1, a2, a3 = _split3(a)
    b1, b2, b3 = _split3(b)
    d = lambda x, y: jnp.dot(x, y, preferred_element_type=F32)
    return (d(a1, b1) + (d(a1, b2) + d(a2, b1))) + ((d(a1, b3) + d(a3, b1)) + d(a2, b2))


def _dot_exact_lhs(a_bf16, b):
    b1, b2, b3 = _split3(b)
    d = lambda y: jnp.dot(a_bf16, y, preferred_element_type=F32)
    return d(b1) + (d(b2) + d(b3))


def _sigmoid(x):
    return 1.0 / (1.0 + jnp.exp(-x))


def _iota(shape, dim):
    return lax.broadcasted_iota(jnp.int32, shape, dim)


def _shift_rows(x):
    n = x.shape[0]
    row = _iota(x.shape, 0)
    prev = jnp.where(row == 0, 0.0, pltpu.roll(x, 1, 0))
    nxt = jnp.where(row == n - 1, 0.0, pltpu.roll(x, n - 1, 0))
    return prev, nxt


def _half_sum(x, lo_mask):
    s_lo = jnp.sum(jnp.where(lo_mask, x, 0.0), axis=-1, keepdims=True)
    s_hi = jnp.sum(jnp.where(lo_mask, 0.0, x), axis=-1, keepdims=True)
    return jnp.where(lo_mask, s_lo, s_hi)


def _ada_kernel(c_ref, w_ref, b_ref, o_ref):
    c = c_ref[...]
    o_ref[0] = _bdot(c * _sigmoid(c), w_ref[0]) + b_ref[0]


def _ada_all(cond, ada_w, ada_b):
    tn = 1536
    n_out = 6 * D_MODEL
    return pl.pallas_call(
        _ada_kernel,
        out_shape=jax.ShapeDtypeStruct((DEPTH, N_MOD_ROWS, n_out), F32),
        grid=(DEPTH, n_out // tn),
        in_specs=[
            pl.BlockSpec((N_MOD_ROWS, D_MODEL), lambda l, j: (0, 0)),
            pl.BlockSpec((1, D_MODEL, tn), lambda l, j: (l, 0, j)),
            pl.BlockSpec((1, 1, tn), lambda l, j: (l, 0, j)),
        ],
        out_specs=pl.BlockSpec((1, N_MOD_ROWS, tn), lambda l, j: (l, 0, j)),
        compiler_params=_params("parallel", "parallel"),
        name="ada",
    )(cond, ada_w, ada_b.reshape(DEPTH, 1, n_out))


def _mod_spec(layer, chunk, row_of_block):
    def index_map(i, *_):
        return ((layer * N_MOD_ROWS + row_of_block(i)) * 6 + chunk, 0, 0)
    return pl.BlockSpec((1, 1, D_MODEL), index_map)


def _normed(x, g, sc, sh):
    ms = jnp.mean(x * x, axis=-1, keepdims=True)
    return (x * lax.rsqrt(ms + RMS_EPS)) * g * (1.0 + sc) + sh


def _proj_in_kernel(x_ref, g_ref, sc_ref, sh_ref, w_ref, b_ref, o_ref, h_scr):
    @pl.when(pl.program_id(1) == 0)
    def _():
        h_scr[...] = _normed(x_ref[...], g_ref[...], sc_ref[0], sh_ref[0]).astype(BF16)

    o_ref[...] = jnp.dot(h_scr[...], w_ref[...].astype(BF16), preferred_element_type=F32) + b_ref[...]


def _proj_in(x, g, mods, layer, row_of_block, w, b, tm, tn):
    t, n_out = x.shape[0], w.shape[1]
    return pl.pallas_call(
        _proj_in_kernel,
        out_shape=jax.ShapeDtypeStruct((t, n_out), F32),
        grid=(t // tm, n_out // tn),
        in_specs=[
            pl.BlockSpec((tm, D_MODEL), lambda i, j: (i, 0)),
            pl.BlockSpec((1, D_MODEL), lambda i, j: (0, 0)),
            _mod_spec(layer, 1, row_of_block),
            _mod_spec(layer, 0, row_of_block),
            pl.BlockSpec((D_MODEL, tn), lambda i, j: (0, j)),
            pl.BlockSpec((1, tn), lambda i, j: (0, j)),
        ],
        out_specs=pl.BlockSpec((tm, tn), lambda i, j: (i, j)),
        scratch_shapes=[pltpu.VMEM((tm, D_MODEL), BF16)],
        compiler_params=_params("parallel", "arbitrary"),
        name="proj_in",
    )(x, g.reshape(1, D_MODEL), mods, mods, w, b.reshape(1, n_out))


def _proj_out_kernel(n_u, x_ref, gt_ref, *refs):
    u_refs, (w_ref, b_ref, o_ref) = refs[:n_u], refs[n_u:]
    acc = b_ref[...]
    off = 0
    for u_ref in u_refs:
        k = u_ref.shape[1]
        acc = acc + _bdot(u_ref[...], w_ref[off:off + k, :])
        off += k
    o_ref[...] = x_ref[...] + gt_ref[0] * acc


def _proj_out(x, mods, layer, row_of_block, us, w, b, tm):
    t = x.shape[0]
    return pl.pallas_call(
        functools.partial(_proj_out_kernel, len(us)),
        out_shape=jax.ShapeDtypeStruct((t, D_MODEL), F32),
        grid=(t // tm,),
        in_specs=[
            pl.BlockSpec((tm, D_MODEL), lambda i: (i, 0)),
            _mod_spec(layer, 2, row_of_block),
            *[pl.BlockSpec((tm, u.shape[1]), lambda i: (i, 0)) for u in us],
            pl.BlockSpec(w.shape, lambda i: (0, 0)),
            pl.BlockSpec((1, D_MODEL), lambda i: (0, 0)),
        ],
        out_specs=pl.BlockSpec((tm, D_MODEL), lambda i: (i, 0)),
        compiler_params=_params("parallel"),
        name="proj_out",
    )(x, mods, *us, w, b.reshape(1, D_MODEL))


def _ffn_kernel(final, x_ref, g_ref, sc_ref, sh_ref, gt_ref, w1_ref, w3_ref, w2_ref, fg_ref, o_ref,
                h_scr, acc_scr):
    k = pl.program_id(1)

    @pl.when(k == 0)
    def _():
        h_scr[...] = _normed(x_ref[...], g_ref[...], sc_ref[0], sh_ref[0]).astype(BF16)
        acc_scr[...] = jnp.zeros_like(acc_scr)

    h = h_scr[...]
    a1 = jnp.dot(h, w1_ref[...].astype(BF16), preferred_element_type=F32)
    a3 = jnp.dot(h, w3_ref[...].astype(BF16), preferred_element_type=F32)
    acc_scr[...] += _bdot((a1 * _sigmoid(a1)) * a3, w2_ref[...])

    @pl.when(k == pl.num_programs(1) - 1)
    def _():
        y = x_ref[...] + gt_ref[0] * acc_scr[...]
        if final:
            ms = jnp.mean(y * y, axis=-1, keepdims=True)
            y = (y * lax.rsqrt(ms + RMS_EPS)) * fg_ref[...]
        o_ref[...] = y


def _ffn(x, g, mods, layer, row_of_block, w1, w3, w2, final_g, final, tm, tk):
    t = x.shape[0]
    return pl.pallas_call(
        functools.partial(_ffn_kernel, final),
        out_shape=jax.ShapeDtypeStruct((t, D_MODEL), F32),
        grid=(t // tm, D_FF // tk),
        in_specs=[
            pl.BlockSpec((tm, D_MODEL), lambda i, k: (i, 0)),
            pl.BlockSpec((1, D_MODEL), lambda i, k: (0, 0)),
            _mod_spec(layer, 4, row_of_block),
            _mod_spec(layer, 3, row_of_block),
            _mod_spec(layer, 5, row_of_block),
            pl.BlockSpec((D_MODEL, tk), lambda i, k: (0, k)),
            pl.BlockSpec((D_MODEL, tk), lambda i, k: (0, k)),
            pl.BlockSpec((tk, D_MODEL), lambda i, k: (k, 0)),
            pl.BlockSpec((1, D_MODEL), lambda i, k: (0, 0)),
        ],
        out_specs=pl.BlockSpec((tm, D_MODEL), lambda i, k: (i, 0)),
        scratch_shapes=[pltpu.VMEM((tm, D_MODEL), BF16), pltpu.VMEM((tm, D_MODEL), F32)],
        compiler_params=_params("parallel", "arbitrary"),
        name="ffn",
    )(x, g.reshape(1, D_MODEL), mods, mods, mods, w1, w3, w2, final_g.reshape(1, D_MODEL))


def _rope(x, cos, sin):
    lane = _iota(x.shape, 1)
    rot = jnp.where((lane % 16) < 8, -pltpu.roll(x, LANES - 8, 1), pltpu.roll(x, 8, 1))
    return x * cos + rot * sin


def _attn_kernel(n_ctx, use_rope, lam_init, *refs):
    it = iter(refs)
    q_ref, k_ref, v_ref = next(it), next(it), next(it)
    kc_ref = vc_ref = cq_ref = sq_ref = ck_ref = sk_ref = None
    if n_ctx:
        kc_ref, vc_ref = next(it), next(it)
    if use_rope:
        cq_ref, sq_ref, ck_ref, sk_ref = next(it), next(it), next(it), next(it)
    lam_ref, g_ref, o_ref, kall, vall = next(it), next(it), next(it), next(it), next(it)
    n_own = k_ref.shape[0]

    @pl.when(pl.program_id(2) == 0)
    def _():
        k = k_ref[...]
        if use_rope:
            k = _rope(k, ck_ref[...], sk_ref[...])
        if n_ctx:
            kall[0:n_ctx, :] = kc_ref[0, 0].astype(BF16)
            vall[0:n_ctx, :] = vc_ref[0, 0].astype(BF16)
        kall[n_ctx:n_ctx + n_own, :] = k.astype(BF16)
        vall[n_ctx:n_ctx + n_own, :] = v_ref[...].astype(BF16)

    lv = lam_ref[...]
    lam = (jnp.exp(jnp.sum(lv[0:1] * lv[1:2], axis=-1, keepdims=True))
           - jnp.exp(jnp.sum(lv[2:3] * lv[3:4], axis=-1, keepdims=True)) + lam_init)

    q = q_ref[...]
    if use_rope:
        q = _rope(q, cq_ref[...], sq_ref[...])
    lane = _iota(q.shape, 1)
    scale = DIFF_QK ** -0.5
    ks, vs = kall[...], vall[...]
    outs = []
    for h in range(2):
        probs = []
        for m in range(2):
            j = 2 * h + m
            qm = jnp.where((lane >= DIFF_QK * j) & (lane < DIFF_QK * (j + 1)), q, 0.0)
            s = _bdot_nt(qm, ks) * scale
            e = jnp.exp(s - jnp.max(s, axis=-1, keepdims=True))
            probs.append(e * (1.0 / jnp.sum(e, axis=-1, keepdims=True)))
        outs.append(_bdot(probs[0] - lam * probs[1], vs))
    lo = lane < DIFF_V
    o = jnp.where(lo, outs[0], outs[1])
    ms = _half_sum(o * o, lo) * (1.0 / DIFF_V)
    o_ref[...] = (o * lax.rsqrt(ms + HEAD_RMS_EPS)) * g_ref[...] * (1.0 - lam_init)


def _attention(p, seq, tq, lam_init, lam, subln_g, ctx=None, rope=None):
    t = p.shape[0]
    nb, nq = t // seq, seq // tq
    n_ctx = 0 if ctx is None else ctx[0].shape[2]
    k_blk, v_blk = DIFF_QW // LANES, 2 * DIFF_QW // LANES
    in_specs = [
        pl.BlockSpec((tq, LANES), lambda b, hp, qi: (b * nq + qi, hp)),
        pl.BlockSpec((seq, LANES), lambda b, hp, qi: (b, k_blk + hp)),
        pl.BlockSpec((seq, LANES), lambda b, hp, qi: (b, v_blk + hp)),
    ]
    args = [p, p, p]
    if ctx is not None:
        in_specs += [pl.BlockSpec((1, 1, n_ctx, LANES), lambda b, hp, qi: (b, hp, 0, 0))] * 2
        args += list(ctx)
    if rope is not None:
        in_specs += [pl.BlockSpec((tq, LANES), lambda b, hp, qi: (qi, 0))] * 2
        in_specs += [pl.BlockSpec((seq, LANES), lambda b, hp, qi: (0, 0))] * 2
        args += [rope[0], rope[1], rope[0], rope[1]]
    in_specs += [pl.BlockSpec((4, DIFF_QK), lambda b, hp, qi: (0, 0)),
                 pl.BlockSpec((1, LANES), lambda b, hp, qi: (0, 0))]
    args += [lam, jnp.tile(subln_g, 2).reshape(1, LANES)]
    return pl.pallas_call(
        functools.partial(_attn_kernel, n_ctx, rope is not None, lam_init),
        out_shape=jax.ShapeDtypeStruct((t, DIFF_WIDTH), F32),
        grid=(nb, DIFF_HEADS // 2, nq),
        in_specs=in_specs,
        out_specs=pl.BlockSpec((tq, LANES), lambda b, hp, qi: (b * nq + qi, hp)),
        scratch_shapes=[pltpu.VMEM((n_ctx + seq, LANES), BF16), pltpu.VMEM((n_ctx + seq, LANES), BF16)],
        compiler_params=_params("parallel", "parallel", "arbitrary"),
        name="diff_attn",
    )(*args)


def _rwkv_kernel(has_s0, *refs):
    it = iter(refs)
    r_ref, k_ref, v_ref, wa_ref, gd_ref = (next(it) for _ in range(5))
    mu_r, mu_k, mu_v, mu_wa, mu_gd = (next(it) for _ in range(5))
    w0_ref, a0_ref, wa_up_ref, g_up_ref, vec_ref = (next(it) for _ in range(5))
    s0f_ref = s0b_ref = None
    if has_s0:
        s0f_ref, s0b_ref = next(it), next(it)
    o_ref, sf_ref, sb_ref = next(it), next(it), next(it)
    r_s, v_s, kk_s = next(it), next(it), next(it)
    lw_s, kd_s, b_s, y_s = next(it), next(it), next(it), next(it)
    st_s = next(it)

    seq = r_ref.shape[0]
    n_chunks = seq // CHUNK

    def shifted(ref, mu_ref):
        x = ref[...]
        prev, nxt = _shift_rows(x)
        mu = mu_ref[...]
        return x + mu[0:1] * (prev - x) + mu[1:2] * (nxt - x)

    r = shifted(r_ref, mu_r)
    k = shifted(k_ref, mu_k)
    v = shifted(v_ref, mu_v)
    wa = shifted(wa_ref, mu_wa)
    gd = shifted(gd_ref, mu_gd)
    vec = vec_ref[...]
    k_k, k_a, r_k, ln_g, ln_b = (vec[i:i + 1] for i in range(5))

    lane = _iota((seq, LANES), 1)
    lo = lane < RWKV_HS
    gate = _bdot(_sigmoid(gd), g_up_ref[...])
    kk = k * k_k
    kk = kk / jnp.maximum(jnp.sqrt(_half_sum(kk * kk, lo)), 1e-12)
    r_s[...] = r
    v_s[...] = v
    kk_s[...] = kk
    wd_in = jnp.where(lo, jnp.tanh(wa), 0.0)
    ad_in = jnp.where(lo, 0.0, wa)
    bonus = jnp.zeros((seq, LANES), F32)
    for d in range(2):
        wpre = w0_ref[d:d + 1, :] + _bdot(wd_in, wa_up_ref[d])
        sp = jnp.maximum(-wpre, 0.0) + jnp.log(1.0 + jnp.exp(-jnp.abs(wpre)))
        lw_s[d] = -jnp.exp(-sp - 0.5)
        a = _sigmoid(a0_ref[d:d + 1, :] + _bdot(ad_in, wa_up_ref[d]))
        kd = k * (1.0 + (a - 1.0) * k_a)
        kd_s[d] = kd
        b_s[d] = kk * a
        bonus = bonus + _half_sum(r * kd * r_k, lo) * v

    ci = _iota((CHUNK, CHUNK), 0)
    cj = _iota((CHUNK, CHUNK), 1)
    eye = jnp.where(ci == cj, 1.0, 0.0)
    blk8 = (ci // 8) == (cj // 8)
    merge_masks = [((ci // (2 * s)) == (cj // (2 * s))) & ((ci // s) != (cj // s)) for s in (8, 16, 32)]
    before = (cj < ci, cj > ci)
    upto = (cj <= ci, cj >= ci)
    tri = tuple(jnp.where(m, 1.0, 0.0).astype(BF16) for m in upto)
    clane = _iota((CHUNK, LANES), 1)
    head_lanes = (clane < RWKV_HS, clane >= RWKV_HS)
    si = _iota((LANES, LANES), 0)
    sj = _iota((LANES, LANES), 1)
    same_head = (si < RWKV_HS) == (sj < RWKV_HS)
    diag = si == sj

    def chunk_group(g):
        items = ([(0, g * SCAN_UNROLL + u) for u in range(SCAN_UNROLL)]
                 + [(1, n_chunks - 1 - (g * SCAN_UNROLL + u)) for u in range(SCAN_UNROLL)])
        n_it = len(items)
        dirs = [d for d, _ in items]
        rows = [pl.ds(pl.multiple_of(c * CHUNK, CHUNK), CHUNK) for _, c in items]
        r_c = [r_s[rw, :] for rw in rows]
        v_c = [v_s[rw, :] for rw in rows]
        kk_c = [kk_s[rw, :] for rw in rows]
        lw_c = [lw_s[d, rw, :] for d, rw in zip(dirs, rows)]
        kd_c = [kd_s[d, rw, :] for d, rw in zip(dirs, rows)]
        b_c = [b_s[d, rw, :] for d, rw in zip(dirs, rows)]
        cum = [_dot_exact_lhs(tri[d], lw) for d, lw in zip(dirs, lw_c)]
        tot = [cm[CHUNK - 1:CHUNK, :] if d == 0 else cm[0:1, :] for d, cm in zip(dirs, cum)]
        e_neg = [jnp.exp(-cm) for cm in cum]
        al = [jnp.exp(cm - lw) * kk for cm, lw, kk in zip(cum, lw_c, kk_c)]
        be = [b * e for b, e in zip(b_c, e_neg)]
        ka = [kd * e for kd, e in zip(kd_c, e_neg)]
        rh = [r * jnp.exp(cm) for r, cm in zip(r_c, cum)]
        wc = [jnp.exp(tt) for tt in tot]
        combos = [(i, h) for i in range(n_it) for h in range(2)]
        x_h = [jnp.concatenate([jnp.where(head_lanes[h], al[i], 0.0), jnp.where(head_lanes[h], rh[i], 0.0)], axis=0)
               for i, h in combos]
        xb = [_bdot_nt(x, be[i]) for x, (i, _) in zip(x_h, combos)]
        xk = [_bdot_nt(x, ka[i]) for x, (i, _) in zip(x_h, combos)]
        n_mat = [jnp.where(before[dirs[i]], -x[0:CHUNK], 0.0) for x, (i, _) in zip(xb, combos)]
        g_mat = [jnp.where(before[dirs[i]], x[0:CHUNK], 0.0) for x, (i, _) in zip(xk, combos)]
        m2 = [jnp.where(upto[dirs[i]], x[CHUNK:], 0.0) for x, (i, _) in zip(xb, combos)]
        m1 = [jnp.where(upto[dirs[i]], x[CHUNK:], 0.0) for x, (i, _) in zip(xk, combos)]
        nd = [jnp.where(blk8, n, 0.0) for n in n_mat]
        t = [eye + x for x in nd]
        p = [_bdot(x, x) for x in nd]
        t = [a + _bdot(a, b) for a, b in zip(t, p)]
        p = [_bdot(x, x) for x in p]
        t = [a + _bdot(a, b) for a, b in zip(t, p)]
        for msk in merge_masks:
            q = [_bdot(jnp.where(msk, n, 0.0), a) for n, a in zip(n_mat, t)]
            t = [a + _bdot(a, b) for a, b in zip(t, q)]
        gv = [_bdot(g_, v_c[i]) for g_, (i, _) in zip(g_mat, combos)]
        z = [_bdot(t_, jnp.concatenate([al[i], g_], axis=1)) for t_, g_, (i, _) in zip(t, gv, combos)]
        yv_h = [_bdot(m, v_c[i]) for m, (i, _) in zip(m1, combos)]
        pick = lambda a, b: jnp.where(head_lanes[0], a, b)
        alp = [pick(z[2 * i][:, :LANES], z[2 * i + 1][:, :LANES]) for i in range(n_it)]
        uv = [pick(z[2 * i][:, LANES:], z[2 * i + 1][:, LANES:]) for i in range(n_it)]
        yv = [pick(yv_h[2 * i], yv_h[2 * i + 1]) for i in range(n_it)]
        be_t = [(b * w).T for b, w in zip(be, wc)]
        ka_t = [(k_ * w).T for k_, w in zip(ka, wc)]
        pz = [_bdot(bt, jnp.concatenate([a, u_], axis=1)) for bt, a, u_ in zip(be_t, alp, uv)]
        kv = [_bdot(kt, v_) for kt, v_ in zip(ka_t, v_c)]
        a_mat = [jnp.where(same_head, jnp.where(diag, w, 0.0) - pp[:, :LANES], 0.0) for w, pp in zip(wc, pz)]
        b_mat = [jnp.where(same_head, k_ - pp[:, LANES:], 0.0) for k_, pp in zip(kv, pz)]
        for i, d in enumerate(dirs):
            st = st_s[d]
            xs = _bdot(jnp.concatenate([alp[i], rh[i]], axis=0), st)
            u = uv[i] + xs[0:CHUNK]
            m2u = pick(_bdot(m2[2 * i], u), _bdot(m2[2 * i + 1], u))
            y_s[d, rows[i], :] = xs[CHUNK:] + yv[i] - m2u
            st_s[d] = _bdot(a_mat[i], st) + b_mat[i]

    def body(g, carry):
        chunk_group(g)
        return carry

    if has_s0:
        st_s[0] = s0f_ref[0, 0]
        st_s[1] = s0b_ref[0, 0]
    else:
        st_s[...] = jnp.zeros_like(st_s)
    lax.fori_loop(0, n_chunks // SCAN_UNROLL, body, 0)
    sf_ref[0, 0] = st_s[0]
    sb_ref[0, 0] = st_s[1]

    y = (y_s[0] + y_s[1]) + bonus
    mean = _half_sum(y, lo) * (1.0 / RWKV_HS)
    yc = y - mean
    var = _half_sum(yc * yc, lo) * (1.0 / RWKV_HS)
    o_ref[...] = ((yc * lax.rsqrt(var + GN_EPS)) * ln_g + ln_b) * gate


def _rwkv(p, seq, mu, w0, a0, wa_up, g_up, vecs, s0=None):
    t = p.shape[0]
    nb = t // seq
    n_hp = RWKV_HEADS // 2
    base = (2 * DIFF_QW + DIFF_WIDTH) // LANES
    tail = base + 3 * n_hp
    col = lambda blk: pl.BlockSpec((seq, LANES), lambda b, hp: (b, blk(hp)))
    mu_col = lambda blk: pl.BlockSpec((2, LANES), lambda b, hp: (0, blk(hp)))
    in_specs = [
        col(lambda hp: base + hp), col(lambda hp: base + n_hp + hp), col(lambda hp: base + 2 * n_hp + hp),
        col(lambda hp: tail), col(lambda hp: tail + 1),
        mu_col(lambda hp: hp), mu_col(lambda hp: n_hp + hp), mu_col(lambda hp: 2 * n_hp + hp),
        mu_col(lambda hp: 3 * n_hp), mu_col(lambda hp: 3 * n_hp + 1),
        pl.BlockSpec((2, LANES), lambda b, hp: (0, hp)),
        pl.BlockSpec((2, LANES), lambda b, hp: (0, hp)),
        pl.BlockSpec((2, LANES, LANES), lambda b, hp: (0, 0, hp)),
        pl.BlockSpec((LANES, LANES), lambda b, hp: (0, hp)),
        pl.BlockSpec((8, LANES), lambda b, hp: (0, hp)),
    ]
    args = [p] * 5 + [mu] * 5 + [w0, a0, wa_up, g_up, vecs]
    st_spec = pl.BlockSpec((1, 1, LANES, LANES), lambda b, hp: (b, hp, 0, 0))
    if s0 is not None:
        in_specs += [st_spec, st_spec]
        args += list(s0)
    st_shape = jax.ShapeDtypeStruct((nb, n_hp, LANES, LANES), F32)
    return pl.pallas_call(
        functools.partial(_rwkv_kernel, s0 is not None),
        out_shape=(jax.ShapeDtypeStruct((t, RWKV_WIDTH), F32), st_shape, st_shape),
        grid=(nb, n_hp),
        in_specs=in_specs,
        out_specs=(pl.BlockSpec((seq, LANES), lambda b, hp: (b, hp)), st_spec, st_spec),
        scratch_shapes=[pltpu.VMEM((seq, LANES), F32)] * 3 + [pltpu.VMEM((2, seq, LANES), F32)] * 4
        + [pltpu.VMEM((2, LANES, LANES), F32)],
        compiler_params=_params("parallel", "parallel"),
        name="rwkv",
    )(*args)


def _pair_states_in(s):
    nb = s.shape[0]
    st = jnp.swapaxes(s.astype(F32), -1, -2).reshape(nb, RWKV_HEADS // 2, 2, RWKV_HS, RWKV_HS)
    z = jnp.zeros_like(st[:, :, 0])
    top = jnp.concatenate([st[:, :, 0], z], axis=-1)
    bot = jnp.concatenate([z, st[:, :, 1]], axis=-1)
    return jnp.concatenate([top, bot], axis=-2)


def _pair_states_out(st):
    nb = st.shape[0]
    h0 = st[:, :, :RWKV_HS, :RWKV_HS]
    h1 = st[:, :, RWKV_HS:, RWKV_HS:]
    s = jnp.stack([h0, h1], axis=2).reshape(nb, RWKV_HEADS, RWKV_HS, RWKV_HS)
    return jnp.swapaxes(s, -1, -2)


def _dft_kernel(seq, f_ref, ft_ref):
    t = pl.program_id(0)

    def tile(freq_axis):
        shape = (2 * FREQ_TILE, seq) if freq_axis == 0 else (seq, 2 * FREQ_TILE)
        fr = _iota(shape, freq_axis)
        tau = _iota(shape, 1 - freq_axis)
        is_sin = fr >= FREQ_TILE
        kf = t * FREQ_TILE + jnp.where(is_sin, fr - FREQ_TILE, fr)
        m = (kf * tau) & (2 * seq - 1)
        ang = m.astype(F32) * (math.pi / seq)
        val = jnp.where(is_sin, -jnp.sin(ang), jnp.cos(ang))
        nyq = jnp.where((tau & 1) == 0, 1.0, -1.0)
        return jnp.where(is_sin & (kf == 0), nyq, val)

    f_ref[0] = tile(0).astype(BF16)
    ft_ref[0] = tile(1).astype(BF16)


def _dft_mats(seq):
    nt = seq // FREQ_TILE
    return pl.pallas_call(
        functools.partial(_dft_kernel, seq),
        out_shape=(jax.ShapeDtypeStruct((nt, 2 * FREQ_TILE, seq), BF16),
                   jax.ShapeDtypeStruct((nt, seq, 2 * FREQ_TILE), BF16)),
        grid=(nt,),
        out_specs=(pl.BlockSpec((1, 2 * FREQ_TILE, seq), lambda t: (t, 0, 0)),
                   pl.BlockSpec((1, seq, 2 * FREQ_TILE), lambda t: (t, 0, 0))),
        compiler_params=_params("parallel"),
        name="dft_mats",
    )()


def _filter_kernel(seq, w0t_ref, w0c_ref, w0s_ref, b0_ref, w1_ref, b1_ref, w2_ref, b2_ref, fr_ref,
                   w3f_ref, w3b_ref, f_ref, o_ref, hf_scr, hb_scr):
    t = pl.program_id(2)
    cb = pl.program_id(1)
    width = w3f_ref.shape[1]

    @pl.when(t == 0)
    def _():
        tau = _iota((seq, BANDS), 0).astype(F32)
        band = _iota((seq, BANDS), 1).astype(F32)
        freq = 1e-4 + band * ((BANDS - 1 - 1e-4) / (BANDS - 1))
        wpos = tau * (2.0 * math.pi / seq)
        pos = _iota((seq, 1), 0).astype(F32) * (1.0 / (seq - 1))
        fr = fr_ref[...]
        pre = (pos * w0t_ref[...] + _dot_f32(jnp.cos(freq * wpos), w0c_ref[...])
               + _dot_f32(-jnp.sin(freq * wpos), w0s_ref[...]) + b0_ref[...])
        hdn = jnp.sin(fr[0:1] * pre)
        hdn = jnp.sin(fr[1:2] * (_dot_f32(hdn, w1_ref[...]) + b1_ref[...]))
        hdn = jnp.sin(fr[2:3] * (_dot_f32(hdn, w2_ref[...]) + b2_ref[...]))
        chan = (cb * width + _iota((1, width), 1)).astype(F32)
        delta = jnp.abs(MIN_DECAY + chan * ((MAX_DECAY - MIN_DECAY) / (D_MODEL - 1)))
        window = jnp.exp(-pos * delta)
        hf_scr[...] = (_dot_f32(hdn, w3f_ref[...]) * window).astype(BF16)
        hb = _dot_f32(hdn, w3b_ref[...]) * window
        hb_scr[...] = jnp.where(_iota((seq, width), 0) == 0, 0.0, hb).astype(BF16)

    f = f_ref[0]
    kf = jnp.dot(f, hf_scr[...], preferred_element_type=F32)
    kb = jnp.dot(f, hb_scr[...], preferred_element_type=F32)
    row = _iota(kf.shape, 0)
    conj = (row >= FREQ_TILE) & ~((row == FREQ_TILE) & (t == 0))
    o_ref[0] = kf + jnp.where(conj, -kb, kb)


def _hyena_filters(seq, fmat, w0, b0, w1, b1, w2, b2, w3, freq):
    nt = seq // FREQ_TILE
    width = 512
    ncb = D_MODEL // width
    full = lambda a: pl.BlockSpec(a.shape, lambda o, cb, t: (0,) * a.ndim)
    small = [w0[0:1], w0[1:1 + BANDS], w0[1 + BANDS:], b0.reshape(1, -1), w1, b1.reshape(1, -1),
             w2, b2.reshape(1, -1), freq]
    return pl.pallas_call(
        functools.partial(_filter_kernel, seq),
        out_shape=jax.ShapeDtypeStruct((nt, 2 * FREQ_TILE, 2 * D_MODEL), F32),
        grid=(2, ncb, nt),
        in_specs=[full(a) for a in small] + [
            pl.BlockSpec((FILTER_WIDTH, width), lambda o, cb, t: (0, o * 2 * ncb + cb)),
            pl.BlockSpec((FILTER_WIDTH, width), lambda o, cb, t: (0, o * 2 * ncb + ncb + cb)),
            pl.BlockSpec((1, 2 * FREQ_TILE, seq), lambda o, cb, t: (t, 0, 0)),
        ],
        out_specs=pl.BlockSpec((1, 2 * FREQ_TILE, width), lambda o, cb, t: (t, 0, o * ncb + cb)),
        scratch_shapes=[pltpu.VMEM((seq, width), BF16), pltpu.VMEM((seq, width), BF16)],
        compiler_params=_params("parallel", "parallel", "arbitrary"),
        name="hyena_filters",
    )(*small, w3, w3, fmat)


def _short_conv(z, w, b):
    prev, nxt = _shift_rows(z)
    return prev * w[0:1] + z * w[1:2] + nxt * w[2:3] + b


def _conv_kernel(conv_v, seq, v_ref, x_ref, wv_ref, bv_ref, wx_ref, bx_ref, f_ref, ft_ref, kf_ref, bias_ref,
                 o_ref, vb_scr, v_scr, x_scr, acc_scr):
    t = pl.program_id(2)

    @pl.when(t == 0)
    def _():
        v = v_ref[...]
        if conv_v:
            v = _short_conv(v, wv_ref[...], bv_ref[...])
        v_scr[...] = v
        vb_scr[...] = v.astype(BF16)
        x_scr[...] = _short_conv(x_ref[...], wx_ref[...], bx_ref[...])
        acc_scr[...] = jnp.zeros_like(acc_scr)

    uf = jnp.dot(f_ref[0], vb_scr[...], preferred_element_type=F32)
    kf = kf_ref[0]
    ur, ui = uf[0:FREQ_TILE], uf[FREQ_TILE:]
    kr, ki = kf[0:FREQ_TILE], kf[FREQ_TILE:]
    real_pair = (_iota(ur.shape, 0) == 0) & (t == 0)
    scale = jnp.where(real_pair, 0.5 / seq, 1.0 / seq)
    uiki = ui * ki
    yr = (ur * kr - jnp.where(real_pair, 0.0, uiki)) * scale
    yi = jnp.where(real_pair, uiki, ur * ki + ui * kr) * scale
    yf = jnp.concatenate([yr, yi], axis=0).astype(BF16)
    acc_scr[...] += jnp.dot(ft_ref[0], yf, preferred_element_type=F32)

    @pl.when(t == pl.num_programs(2) - 1)
    def _():
        o_ref[...] = x_scr[...] * (acc_scr[...] + v_scr[...] * bias_ref[...])


def _hyena_conv(v_arr, v_blk0, x_arr, x_blk0, conv_v, seq, cw, cwb, fmat, fmat_t, kf, kf_blk0, bias, width):
    t = v_arr.shape[0]
    nb, ncb, nt = t // seq, D_MODEL // width, seq // FREQ_TILE
    wv_blk0 = v_blk0 if conv_v else 0
    return pl.pallas_call(
        functools.partial(_conv_kernel, conv_v, seq),
        out_shape=jax.ShapeDtypeStruct((t, D_MODEL), F32),
        grid=(nb, ncb, nt),
        in_specs=[
            pl.BlockSpec((seq, width), lambda b, c, f: (b, v_blk0 + c)),
            pl.BlockSpec((seq, width), lambda b, c, f: (b, x_blk0 + c)),
            pl.BlockSpec((3, width), lambda b, c, f: (0, wv_blk0 + c)),
            pl.BlockSpec((1, width), lambda b, c, f: (0, wv_blk0 + c)),
            pl.BlockSpec((3, width), lambda b, c, f: (0, x_blk0 + c)),
            pl.BlockSpec((1, width), lambda b, c, f: (0, x_blk0 + c)),
            pl.BlockSpec((1, 2 * FREQ_TILE, seq), lambda b, c, f: (f, 0, 0)),
            pl.BlockSpec((1, seq, 2 * FREQ_TILE), lambda b, c, f: (f, 0, 0)),
            pl.BlockSpec((1, 2 * FREQ_TILE, width), lambda b, c, f: (f, 0, kf_blk0 + c)),
            pl.BlockSpec((1, width), lambda b, c, f: (0, c)),
        ],
        out_specs=pl.BlockSpec((seq, width), lambda b, c, f: (b, c)),
        scratch_shapes=[pltpu.VMEM((seq, width), BF16), pltpu.VMEM((seq, width), F32),
                        pltpu.VMEM((seq, width), F32), pltpu.VMEM((seq, width), F32)],
        compiler_params=_params("parallel", "parallel", "arbitrary"),
        name="hyena_conv",
    )(v_arr, x_arr, cw, cwb, cw, cwb, fmat, fmat_t, kf, bias.reshape(1, D_MODEL))


def _axial_rope_tables(seq):
    n_rows = seq // GRID_W
    row = jnp.repeat(jnp.arange(n_rows, dtype=F32), GRID_W)
    col = jnp.tile(jnp.arange(GRID_W, dtype=F32), n_rows)
    inv = ROPE_BASE ** (-jnp.arange(ROPE_PAIRS, dtype=F32) / ROPE_PAIRS)
    ar = row[:, None] * inv[None]
    ac = col[:, None] * inv[None]
    ang = jnp.concatenate([ar, ar, ac, ac], axis=-1)
    reps = LANES // DIFF_QK
    return jnp.tile(jnp.cos(ang), (1, reps)), jnp.tile(jnp.sin(ang), (1, reps))


def _pair_lanes(a, seq_axis):
    nb = a.shape[0]
    if a.ndim == 5:
        a = a.reshape(nb, DIFF_HEADS // 2, 2, 2, a.shape[3], DIFF_QK).transpose(0, 1, 4, 2, 3, 5)
    else:
        a = a.reshape(nb, DIFF_HEADS // 2, 2, a.shape[2], DIFF_V).transpose(0, 1, 3, 2, 4)
    return a.reshape(nb, DIFF_HEADS // 2, a.shape[2], LANES)


def kernel(x_prompt, x_sample, cache_diff_k, cache_diff_v, state_rwkv_fwd, state_rwkv_bwd, c, c_ctx, ada_w, ada_b, norm1_g, norm2_g, ffn_w1, ffn_w3, ffn_w2, final_g, ab_w_in, ab_w_out, diff_lambda, diff_subln_g, rwkv_mu, rwkv_w0, rwkv_w_up, rwkv_a0, rwkv_a_up, rwkv_g_up, rwkv_k_k, rwkv_k_a, rwkv_r_k, rwkv_ln_g, rwkv_ln_b, hy_w_in, hy_b_in, hy_conv_w, hy_conv_b, hy_f_w0, hy_f_b0, hy_f_w1, hy_f_b1, hy_f_w2, hy_f_b2, hy_f_w3, hy_f_freq, hy_bias, hy_w_out, hy_b_out):
    n_ctx_seqs, ctx_len, _ = x_prompt.shape
    n_lat_seqs, lat_len, _ = x_sample.shape
    tm = 1024

    cond = jnp.zeros((N_MOD_ROWS, D_MODEL), F32).at[0].set(c_ctx).at[1:1 + n_lat_seqs].set(c)
    mods = _ada_all(cond, ada_w, ada_b).reshape(DEPTH * N_MOD_ROWS * 6, 1, D_MODEL)

    groups = [
        dict(x=x_prompt.reshape(-1, D_MODEL), seq=ctx_len, row=lambda i: 0, tq=ctx_len, conv_width=1024),
        dict(x=x_sample.reshape(-1, D_MODEL), seq=lat_len, row=lambda i: 1 + i // (lat_len // tm), tq=256,
             conv_width=256),
    ]
    rope = _axial_rope_tables(lat_len)
    dft = {g["seq"]: _dft_mats(g["seq"]) for g in groups}
    zero_b = jnp.zeros((AB_IN,), F32)
    zero_o = jnp.zeros((D_MODEL,), F32)
    new_k, new_v, new_sf, new_sb = [], [], [], []

    for l in range(DEPTH):
        i = l // 2
        if l % 2 == 0:
            wa_up = jnp.concatenate([rwkv_w_up[i], rwkv_a_up[i]], axis=1)
            vecs = jnp.zeros((8, RWKV_WIDTH), F32).at[0].set(rwkv_k_k[i]).at[1].set(rwkv_k_a[i]) \
                .at[2].set(rwkv_r_k[i].reshape(-1)).at[3].set(rwkv_ln_g[i]).at[4].set(rwkv_ln_b[i])
        for gi, g in enumerate(groups):
            x, seq, row = g["x"], g["seq"], g["row"]
            if l % 2 == 0:
                p = _proj_in(x, norm1_g[l], mods, l, row, ab_w_in[i], zero_b, tm, AB_IN // 2)
                if gi == 0:
                    nb = x.shape[0] // seq
                    kq = p[:, DIFF_QW:2 * DIFF_QW].reshape(nb, seq, DIFF_HEADS, 2, DIFF_QK)
                    new_k.append(kq.transpose(0, 2, 3, 1, 4))
                    vq = p[:, 2 * DIFF_QW:2 * DIFF_QW + DIFF_WIDTH].reshape(nb, seq, DIFF_HEADS, DIFF_V)
                    new_v.append(vq.transpose(0, 2, 1, 3))
                    att = _attention(p, seq, g["tq"], _lambda_init(l), diff_lambda[i], diff_subln_g[i])
                    mix, sf, sb = _rwkv(p, seq, rwkv_mu[i], rwkv_w0[i], rwkv_a0[i], wa_up, rwkv_g_up[i], vecs)
                    new_sf.append(_pair_states_out(sf))
                    new_sb.append(_pair_states_out(sb))
                else:
                    ctx = (_pair_lanes(cache_diff_k[:, i], 3), _pair_lanes(cache_diff_v[:, i], 2))
                    att = _attention(p, seq, g["tq"], _lambda_init(l), diff_lambda[i], diff_subln_g[i],
                                     ctx=ctx, rope=rope)
                    s0 = (_pair_states_in(state_rwkv_fwd[:, i]), _pair_states_in(state_rwkv_bwd[:, i]))
                    mix, _, _ = _rwkv(p, seq, rwkv_mu[i], rwkv_w0[i], rwkv_a0[i], wa_up, rwkv_g_up[i], vecs, s0=s0)
                x = _proj_out(x, mods, l, row, [att, mix], ab_w_out[i], zero_o, tm)
            else:
                fmat, fmat_t = dft[seq]
                z = _proj_in(x, norm1_g[l], mods, l, row, hy_w_in[i], hy_b_in[i], tm, 1536)
                kf = _hyena_filters(seq, fmat, hy_f_w0[i], hy_f_b0[i], hy_f_w1[i], hy_f_b1[i], hy_f_w2[i],
                                    hy_f_b2[i], hy_f_w3[i], hy_f_freq[i])
                width = g["conv_width"]
                ncb = D_MODEL // width
                cw, cwb = hy_conv_w[i], hy_conv_b[i].reshape(1, -1)
                u = _hyena_conv(z, 2 * ncb, z, 0, True, seq, cw, cwb, fmat, fmat_t, kf, 0, hy_bias[i, 0], width)
                u = _hyena_conv(u, 0, z, ncb, False, seq, cw, cwb, fmat, fmat_t, kf, ncb, hy_bias[i, 1], width)
                x = _proj_out(x, mods, l, row, [u], hy_w_out[i], hy_b_out[i], tm)
            g["x"] = _ffn(x, norm2_g[l], mods, l, row, ffn_w1[l], ffn_w3[l], ffn_w2[l], final_g,
                          l == DEPTH - 1, tm, 256)

    y_prompt = groups[0]["x"].reshape(x_prompt.shape)
    y_sample = groups[1]["x"].reshape(x_sample.shape)
    return (y_prompt, y_sample, jnp.stack(new_k, axis=1), jnp.stack(new_v, axis=1),
            jnp.stack(new_sf, axis=1), jnp.stack(new_sb, axis=1))
```

```python
import functools
import math

import jax
import jax.numpy as jnp
from jax import lax
from jax.experimental import pallas as pl
from jax.experimental.pallas import tpu as pltpu

F32 = jnp.float32
BF16 = jnp.bfloat16

D_MODEL = 1024
DEPTH = 4
GRID_W = 64
DIFF_HEADS = 8
DIFF_QK = 32
DIFF_V = 64
DIFF_QW = DIFF_HEADS * 2 * DIFF_QK
DIFF_WIDTH = DIFF_HEADS * DIFF_V
ROPE_PAIRS = DIFF_QK // 4
ROPE_BASE = 10000.0
RWKV_HEADS = 8
RWKV_HS = 64
RWKV_WIDTH = RWKV_HEADS * RWKV_HS
RWKV_IN = 3 * RWKV_WIDTH + 64 + 64 + 128
AB_IN = 2 * DIFF_QW + DIFF_WIDTH + RWKV_IN
BANDS = 16
FILTER_WIDTH = 64
MAX_DECAY = math.log(1e-2) / 0.3
MIN_DECAY = math.log(1e-2) / 1.5
D_FF = 2816
RMS_EPS = 1e-6
GN_EPS = 64e-5
HEAD_RMS_EPS = 1e-5
LOG2_E = math.log2(math.e)

LANES = 128
CHUNK = 64
SCAN_UNROLL = 4
CONV_BLOCK = 256
MIX_ROWS = 8
VMEM_LIMIT = 56 * 1024 * 1024

N_MOD_ROWS = 8


def _lambda_init(l):
    return 0.8 - 0.6 * math.exp(-0.3 * l)


def _params(*sem):
    return pltpu.CompilerParams(dimension_semantics=sem, vmem_limit_bytes=VMEM_LIMIT)


def _bdot(a, b):
    return jnp.dot(a.astype(BF16), b.astype(BF16), preferred_element_type=F32)


def _bdot_nt(a, b):
    return lax.dot_general(a.astype(BF16), b.astype(BF16), (((1,), (1,)), ((), ())),
                           preferred_element_type=F32)


def _split3(x):
    h1 = x.astype(BF16)
    r1 = x - h1.astype(F32)
    h2 = r1.astype(BF16)
    h3 = (r1 - h2.astype(F32)).astype(BF16)
    return h1, h2, h3


def _dot_f32(a, b):
    a1, a2, a3 = _split3(a)
    b1, b2, b3 = _split3(b)
    d = lambda x, y: jnp.dot(x, y, preferred_element_type=F32)
    return (d(a1, b1) + (d(a1, b2) + d(a2, b1))) + ((d(a1, b3) + d(a3, b1)) + d(a2, b2))


def _dot_exact_lhs(a_bf16, b):
    b1, b2, b3 = _split3(b)
    d = lambda y: jnp.dot(a_bf16, y, preferred_element_type=F32)
    return d(b1) + (d(b2) + d(b3))


def _sigmoid(x):
    return 1.0 / (1.0 + jnp.exp(-x))


def _iota(shape, dim):
    return lax.broadcasted_iota(jnp.int32, shape, dim)


def _shift_rows(x):
    n = x.shape[0]
    row = _iota(x.shape, 0)
    prev = jnp.where(row == 0, 0.0, pltpu.roll(x, 1, 0))
    nxt = jnp.where(row == n - 1, 0.0, pltpu.roll(x, n - 1, 0))
    return prev, nxt


def _half_sum(x, lo_mask):
    s_lo = jnp.sum(jnp.where(lo_mask, x, 0.0), axis=-1, keepdims=True)
    s_hi = jnp.sum(jnp.where(lo_mask, 0.0, x), axis=-1, keepdims=True)
    return jnp.where(lo_mask, s_lo, s_hi)


def _ada_kernel(c_ref, w_ref, b_ref, o_ref):
    c = c_ref[...]
    o_ref[0] = _bdot(c * _sigmoid(c), w_ref[0]) + b_ref[0]


def _ada_all(cond, ada_w, ada_b):
    tn = 1536
    n_out = 6 * D_MODEL
    return pl.pallas_call(
        _ada_kernel,
        out_shape=jax.ShapeDtypeStruct((DEPTH, N_MOD_ROWS, n_out), F32),
        grid=(DEPTH, n_out // tn),
        in_specs=[
            pl.BlockSpec((N_MOD_ROWS, D_MODEL), lambda l, j: (0, 0)),
            pl.BlockSpec((1, D_MODEL, tn), lambda l, j: (l, 0, j)),
            pl.BlockSpec((1, 1, tn), lambda l, j: (l, 0, j)),
        ],
        out_specs=pl.BlockSpec((1, N_MOD_ROWS, tn), lambda l, j: (l, 0, j)),
        compiler_params=_params("parallel", "parallel"),
        name="ada",
    )(cond, ada_w, ada_b.reshape(DEPTH, 1, n_out))


def _mod_spec(layer, chunk, row_of_block):
    def index_map(i, *_):
        return ((layer * N_MOD_ROWS + row_of_block(i)) * 6 + chunk, 0, 0)
    return pl.BlockSpec((1, 1, D_MODEL), index_map)


def _normed(x, g, sc, sh):
    ms = jnp.mean(x * x, axis=-1, keepdims=True)
    return (x * lax.rsqrt(ms + RMS_EPS)) * g * (1.0 + sc) + sh


def _proj_in_kernel(x_ref, g_ref, sc_ref, sh_ref, w_ref, b_ref, o_ref, h_scr):
    @pl.when(pl.program_id(1) == 0)
    def _():
        h_scr[...] = _normed(x_ref[...], g_ref[...], sc_ref[0], sh_ref[0]).astype(BF16)

    o_ref[...] = jnp.dot(h_scr[...], w_ref[...].astype(BF16), preferred_element_type=F32) + b_ref[...]


def _proj_in(x, g, mods, layer, row_of_block, w, b, tm, tn):
    t, n_out = x.shape[0], w.shape[1]
    return pl.pallas_call(
        _proj_in_kernel,
        out_shape=jax.ShapeDtypeStruct((t, n_out), F32),
        grid=(t // tm, n_out // tn),
        in_specs=[
            pl.BlockSpec((tm, D_MODEL), lambda i, j: (i, 0)),
            pl.BlockSpec((1, D_MODEL), lambda i, j: (0, 0)),
            _mod_spec(layer, 1, row_of_block),
            _mod_spec(layer, 0, row_of_block),
            pl.BlockSpec((D_MODEL, tn), lambda i, j: (0, j)),
            pl.BlockSpec((1, tn), lambda i, j: (0, j)),
        ],
        out_specs=pl.BlockSpec((tm, tn), lambda i, j: (i, j)),
        scratch_shapes=[pltpu.VMEM((tm, D_MODEL), BF16)],
        compiler_params=_params("parallel", "arbitrary"),
        name="proj_in",
    )(x, g.reshape(1, D_MODEL), mods, mods, w, b.reshape(1, n_out))


def _proj_out_kernel(n_u, x_ref, gt_ref, *refs):
    u_refs, (w_ref, b_ref, o_ref) = refs[:n_u], refs[n_u:]
    acc = b_ref[...]
    off = 0
    for u_ref in u_refs:
        k = u_ref.shape[1]
        acc = acc + _bdot(u_ref[...], w_ref[off:off + k, :])
        off += k
    o_ref[...] = x_ref[...] + gt_ref[0] * acc


def _proj_out(x, mods, layer, row_of_block, us, w, b, tm):
    t = x.shape[0]
    return pl.pallas_call(
        functools.partial(_proj_out_kernel, len(us)),
        out_shape=jax.ShapeDtypeStruct((t, D_MODEL), F32),
        grid=(t // tm,),
        in_specs=[
            pl.BlockSpec((tm, D_MODEL), lambda i: (i, 0)),
            _mod_spec(layer, 2, row_of_block),
            *[pl.BlockSpec((tm, u.shape[1]), lambda i: (i, 0)) for u in us],
            pl.BlockSpec(w.shape, lambda i: (0, 0)),
            pl.BlockSpec((1, D_MODEL), lambda i: (0, 0)),
        ],
        out_specs=pl.BlockSpec((tm, D_MODEL), lambda i: (i, 0)),
        compiler_params=_params("parallel"),
        name="proj_out",
    )(x, mods, *us, w, b.reshape(1, D_MODEL))


def _ffn_kernel(final, x_ref, g_ref, sc_ref, sh_ref, gt_ref, w1_ref, w3_ref, w2_ref, fg_ref, o_ref,
                h_scr, acc_scr):
    k = pl.program_id(1)

    @pl.when(k == 0)
    def _():
        h_scr[...] = _normed(x_ref[...], g_ref[...], sc_ref[0], sh_ref[0]).astype(BF16)
        acc_scr[...] = jnp.zeros_like(acc_scr)

    h = h_scr[...]
    a1 = jnp.dot(h, w1_ref[...].astype(BF16), preferred_element_type=F32)
    a3 = jnp.dot(h, w3_ref[...].astype(BF16), preferred_element_type=F32)
    acc_scr[...] += _bdot((a1 * _sigmoid(a1)) * a3, w2_ref[...])

    @pl.when(k == pl.num_programs(1) - 1)
    def _():
        y = x_ref[...] + gt_ref[0] * acc_scr[...]
        if final:
            ms = jnp.mean(y * y, axis=-1, keepdims=True)
            y = (y * lax.rsqrt(ms + RMS_EPS)) * fg_ref[...]
        o_ref[...] = y


def _ffn(x, g, mods, layer, row_of_block, w1, w3, w2, final_g, final, tm, tk):
    t = x.shape[0]
    return pl.pallas_call(
        functools.partial(_ffn_kernel, final),
        out_shape=jax.ShapeDtypeStruct((t, D_MODEL), F32),
        grid=(t // tm, D_FF // tk),
        in_specs=[
            pl.BlockSpec((tm, D_MODEL), lambda i, k: (i, 0)),
            pl.BlockSpec((1, D_MODEL), lambda i, k: (0, 0)),
            _mod_spec(layer, 4, row_of_block),
            _mod_spec(layer, 3, row_of_block),
            _mod_spec(layer, 5, row_of_block),
            pl.BlockSpec((D_MODEL, tk), lambda i, k: (0, k)),
            pl.BlockSpec((D_MODEL, tk), lambda i, k: (0, k)),
            pl.BlockSpec((tk, D_MODEL), lambda i, k: (k, 0)),
            pl.BlockSpec((1, D_MODEL), lambda i, k: (0, 0)),
        ],
        out_specs=pl.BlockSpec((tm, D_MODEL), lambda i, k: (i, 0)),
        scratch_shapes=[pltpu.VMEM((tm, D_MODEL), BF16), pltpu.VMEM((tm, D_MODEL), F32)],
        compiler_params=_params("parallel", "arbitrary"),
        name="ffn",
    )(x, g.reshape(1, D_MODEL), mods, mods, mods, w1, w3, w2, final_g.reshape(1, D_MODEL))


def _rope(x, cos, sin):
    lane = _iota(x.shape, 1)
    rot = jnp.where((lane % 16) < 8, -pltpu.roll(x, LANES - 8, 1), pltpu.roll(x, 8, 1))
    return x * cos + rot * sin


def _attn_kernel(n_ctx, use_rope, lam_init, *refs):
    it = iter(refs)
    q_ref, k_ref, v_ref = next(it), next(it), next(it)
    kc_ref = vc_ref = cq_ref = sq_ref = ck_ref = sk_ref = None
    if n_ctx:
        kc_ref, vc_ref = next(it), next(it)
    if use_rope:
        cq_ref, sq_ref, ck_ref, sk_ref = next(it), next(it), next(it), next(it)
    lam_ref, g_ref, o_ref, kall, vall = next(it), next(it), next(it), next(it), next(it)
    n_own = k_ref.shape[0]

    @pl.when(pl.program_id(2) == 0)
    def _():
        k = k_ref[...]
        if use_rope:
            k = _rope(k, ck_ref[...], sk_ref[...])
        if n_ctx:
            kall[0:n_ctx, :] = kc_ref[0, 0].astype(BF16)
            vall[0:n_ctx, :] = vc_ref[0, 0].astype(BF16)
        kall[n_ctx:n_ctx + n_own, :] = k.astype(BF16)
        vall[n_ctx:n_ctx + n_own, :] = v_ref[...].astype(BF16)

    lv = lam_ref[...]
    lam = (jnp.exp(jnp.sum(lv[0:1] * lv[1:2], axis=-1, keepdims=True))
           - jnp.exp(jnp.sum(lv[2:3] * lv[3:4], axis=-1, keepdims=True)) + lam_init)

    q = q_ref[...]
    if use_rope:
        q = _rope(q, cq_ref[...], sq_ref[...])
    lane = _iota(q.shape, 1)
    scale = DIFF_QK ** -0.5
    ks, vs = kall[...], vall[...]
    outs = []
    for h in range(2):
        pv, inv = [], []
        for m in range(2):
            j = 2 * h + m
            qm = jnp.where((lane >= DIFF_QK * j) & (lane < DIFF_QK * (j + 1)), q, 0.0)
            s = _bdot_nt(qm, ks)
            e = jnp.exp2((s - jnp.max(s, axis=-1, keepdims=True)) * (scale * LOG2_E))
            inv.append(1.0 / jnp.sum(e, axis=-1, keepdims=True))
            pv.append(_bdot(e, vs))
        outs.append(pv[0] * inv[0] - pv[1] * (lam * inv[1]))
    lo = lane < DIFF_V
    o = jnp.where(lo, outs[0], outs[1])
    ms = _half_sum(o * o, lo) * (1.0 / DIFF_V)
    o_ref[...] = (o * lax.rsqrt(ms + HEAD_RMS_EPS)) * g_ref[...] * (1.0 - lam_init)


def _attention(p, seq, tq, lam_init, lam, subln_g, ctx=None, rope=None):
    t = p.shape[0]
    nb, nq = t // seq, seq // tq
    n_ctx = 0 if ctx is None else ctx[0].shape[2]
    k_blk, v_blk = DIFF_QW // LANES, 2 * DIFF_QW // LANES
    in_specs = [
        pl.BlockSpec((tq, LANES), lambda b, hp, qi: (b * nq + qi, hp)),
        pl.BlockSpec((seq, LANES), lambda b, hp, qi: (b, k_blk + hp)),
        pl.BlockSpec((seq, LANES), lambda b, hp, qi: (b, v_blk + hp)),
    ]
    args = [p, p, p]
    if ctx is not None:
        in_specs += [pl.BlockSpec((1, 1, n_ctx, LANES), lambda b, hp, qi: (b, hp, 0, 0))] * 2
        args += list(ctx)
    if rope is not None:
        in_specs += [pl.BlockSpec((tq, LANES), lambda b, hp, qi: (qi, 0))] * 2
        in_specs += [pl.BlockSpec((seq, LANES), lambda b, hp, qi: (0, 0))] * 2
        args += [rope[0], rope[1], rope[0], rope[1]]
    in_specs += [pl.BlockSpec((4, DIFF_QK), lambda b, hp, qi: (0, 0)),
                 pl.BlockSpec((1, LANES), lambda b, hp, qi: (0, 0))]
    args += [lam, jnp.tile(subln_g, 2).reshape(1, LANES)]
    return pl.pallas_call(
        functools.partial(_attn_kernel, n_ctx, rope is not None, lam_init),
        out_shape=jax.ShapeDtypeStruct((t, DIFF_WIDTH), F32),
        grid=(nb, DIFF_HEADS // 2, nq),
        in_specs=in_specs,
        out_specs=pl.BlockSpec((tq, LANES), lambda b, hp, qi: (b * nq + qi, hp)),
        scratch_shapes=[pltpu.VMEM((n_ctx + seq, LANES), BF16), pltpu.VMEM((n_ctx + seq, LANES), BF16)],
        compiler_params=_params("parallel", "parallel", "arbitrary"),
        name="diff_attn",
    )(*args)


def _rwkv_kernel(has_s0, *refs):
    it = iter(refs)
    r_ref, k_ref, v_ref, wa_ref, gd_ref = (next(it) for _ in range(5))
    mu_r, mu_k, mu_v, mu_wa, mu_gd = (next(it) for _ in range(5))
    w0_ref, a0_ref, wa_up_ref, g_up_ref, vec_ref = (next(it) for _ in range(5))
    s0f_ref = s0b_ref = None
    if has_s0:
        s0f_ref, s0b_ref = next(it), next(it)
    o_ref, sf_ref, sb_ref = next(it), next(it), next(it)
    r_s, v_s, kk_s = next(it), next(it), next(it)
    lw_s, kd_s, b_s, y_s = next(it), next(it), next(it), next(it)
    st_s = next(it)

    seq = r_ref.shape[0]
    n_chunks = seq // CHUNK

    def shifted(ref, mu_ref):
        x = ref[...]
        prev, nxt = _shift_rows(x)
        mu = mu_ref[...]
        return x + mu[0:1] * (prev - x) + mu[1:2] * (nxt - x)

    r = shifted(r_ref, mu_r)
    k = shifted(k_ref, mu_k)
    v = shifted(v_ref, mu_v)
    wa = shifted(wa_ref, mu_wa)
    gd = shifted(gd_ref, mu_gd)
    vec = vec_ref[...]
    k_k, k_a, r_k, ln_g, ln_b = (vec[i:i + 1] for i in range(5))

    lane = _iota((seq, LANES), 1)
    lo = lane < RWKV_HS
    gate = _bdot(_sigmoid(gd), g_up_ref[...])
    kk = k * k_k
    kk = kk / jnp.maximum(jnp.sqrt(_half_sum(kk * kk, lo)), 1e-12)
    r_s[...] = r
    v_s[...] = v
    kk_s[...] = kk
    wd_in = jnp.where(lo, jnp.tanh(wa), 0.0)
    ad_in = jnp.where(lo, 0.0, wa)
    bonus = jnp.zeros((seq, LANES), F32)
    for d in range(2):
        wpre = w0_ref[d:d + 1, :] + _bdot(wd_in, wa_up_ref[d])
        sp = jnp.maximum(-wpre, 0.0) + jnp.log(1.0 + jnp.exp(-jnp.abs(wpre)))
        lw_s[d] = -jnp.exp(-sp - 0.5)
        a = _sigmoid(a0_ref[d:d + 1, :] + _bdot(ad_in, wa_up_ref[d]))
        kd = k * (1.0 + (a - 1.0) * k_a)
        kd_s[d] = kd
        b_s[d] = kk * a
        bonus = bonus + _half_sum(r * kd * r_k, lo) * v

    ci = _iota((CHUNK, CHUNK), 0)
    cj = _iota((CHUNK, CHUNK), 1)
    eye = jnp.where(ci == cj, 1.0, 0.0)
    blk8 = (ci // 8) == (cj // 8)
    merge_masks = [((ci // (2 * s)) == (cj // (2 * s))) & ((ci // s) != (cj // s)) for s in (8, 16, 32)]
    before = (cj < ci, cj > ci)
    upto = (cj <= ci, cj >= ci)
    tri = tuple(jnp.where(m, 1.0, 0.0).astype(BF16) for m in upto)
    clane = _iota((CHUNK, LANES), 1)
    head_lanes = (clane < RWKV_HS, clane >= RWKV_HS)
    si = _iota((LANES, LANES), 0)
    sj = _iota((LANES, LANES), 1)
    same_head = (si < RWKV_HS) == (sj < RWKV_HS)
    diag = si == sj

    def chunk_group(g):
        items = ([(0, g * SCAN_UNROLL + u) for u in range(SCAN_UNROLL)]
                 + [(1, n_chunks - 1 - (g * SCAN_UNROLL + u)) for u in range(SCAN_UNROLL)])
        n_it = len(items)
        dirs = [d for d, _ in items]
        rows = [pl.ds(pl.multiple_of(c * CHUNK, CHUNK), CHUNK) for _, c in items]
        r_c = [r_s[rw, :] for rw in rows]
        v_c = [v_s[rw, :] for rw in rows]
        kk_c = [kk_s[rw, :] for rw in rows]
        lw_c = [lw_s[d, rw, :] for d, rw in zip(dirs, rows)]
        kd_c = [kd_s[d, rw, :] for d, rw in zip(dirs, rows)]
        b_c = [b_s[d, rw, :] for d, rw in zip(dirs, rows)]
        cum = [_dot_exact_lhs(tri[d], lw) for d, lw in zip(dirs, lw_c)]
        tot = [cm[CHUNK - 1:CHUNK, :] if d == 0 else cm[0:1, :] for d, cm in zip(dirs, cum)]
        e_neg = [jnp.exp(-cm) for cm in cum]
        al = [jnp.exp(cm - lw) * kk for cm, lw, kk in zip(cum, lw_c, kk_c)]
        be = [b * e for b, e in zip(b_c, e_neg)]
        ka = [kd * e for kd, e in zip(kd_c, e_neg)]
        rh = [r * jnp.exp(cm) for r, cm in zip(r_c, cum)]
        wc = [jnp.exp(tt) for tt in tot]
        combos = [(i, h) for i in range(n_it) for h in range(2)]
        x_h = [jnp.concatenate([jnp.where(head_lanes[h], al[i], 0.0), jnp.where(head_lanes[h], rh[i], 0.0)], axis=0)
               for i, h in combos]
        xb = [_bdot_nt(x, be[i]) for x, (i, _) in zip(x_h, combos)]
        xk = [_bdot_nt(x, ka[i]) for x, (i, _) in zip(x_h, combos)]
        n_mat = [jnp.where(before[dirs[i]], -x[0:CHUNK], 0.0) for x, (i, _) in zip(xb, combos)]
        g_mat = [jnp.where(before[dirs[i]], x[0:CHUNK], 0.0) for x, (i, _) in zip(xk, combos)]
        m2 = [jnp.where(upto[dirs[i]], x[CHUNK:], 0.0) for x, (i, _) in zip(xb, combos)]
        m1 = [jnp.where(upto[dirs[i]], x[CHUNK:], 0.0) for x, (i, _) in zip(xk, combos)]
        nd = [jnp.where(blk8, n, 0.0) for n in n_mat]
        t = [eye + x for x in nd]
        p = [_bdot(x, x) for x in nd]
        t = [a + _bdot(a, b) for a, b in zip(t, p)]
        p = [_bdot(x, x) for x in p]
        t = [a + _bdot(a, b) for a, b in zip(t, p)]
        for msk in merge_masks:
            q = [_bdot(jnp.where(msk, n, 0.0), a) for n, a in zip(n_mat, t)]
            t = [a + _bdot(a, b) for a, b in zip(t, q)]
        gv = [_bdot(g_, v_c[i]) for g_, (i, _) in zip(g_mat, combos)]
        z = [_bdot(t_, jnp.concatenate([al[i], g_], axis=1)) for t_, g_, (i, _) in zip(t, gv, combos)]
        yv_h = [_bdot(m, v_c[i]) for m, (i, _) in zip(m1, combos)]
        pick = lambda a, b: jnp.where(head_lanes[0], a, b)
        alp = [pick(z[2 * i][:, :LANES], z[2 * i + 1][:, :LANES]) for i in range(n_it)]
        uv = [pick(z[2 * i][:, LANES:], z[2 * i + 1][:, LANES:]) for i in range(n_it)]
        yv = [pick(yv_h[2 * i], yv_h[2 * i + 1]) for i in range(n_it)]
        be_t = [(b * w).T for b, w in zip(be, wc)]
        ka_t = [(k_ * w).T for k_, w in zip(ka, wc)]
        pz = [_bdot(bt, jnp.concatenate([a, u_], axis=1)) for bt, a, u_ in zip(be_t, alp, uv)]
        kv = [_bdot(kt, v_) for kt, v_ in zip(ka_t, v_c)]
        a_mat = [jnp.where(same_head, jnp.where(diag, w, 0.0) - pp[:, :LANES], 0.0) for w, pp in zip(wc, pz)]
        b_mat = [jnp.where(same_head, k_ - pp[:, LANES:], 0.0) for k_, pp in zip(kv, pz)]
        for i, d in enumerate(dirs):
            st = st_s[d]
            xs = _bdot(jnp.concatenate([alp[i], rh[i]], axis=0), st)
            u = uv[i] + xs[0:CHUNK]
            m2u = pick(_bdot(m2[2 * i], u), _bdot(m2[2 * i + 1], u))
            y_s[d, rows[i], :] = xs[CHUNK:] + yv[i] - m2u
            st_s[d] = _bdot(a_mat[i], st) + b_mat[i]

    def body(g, carry):
        chunk_group(g)
        return carry

    if has_s0:
        st_s[0] = s0f_ref[0, 0]
        st_s[1] = s0b_ref[0, 0]
    else:
        st_s[...] = jnp.zeros_like(st_s)
    lax.fori_loop(0, n_chunks // SCAN_UNROLL, body, 0)
    sf_ref[0, 0] = st_s[0]
    sb_ref[0, 0] = st_s[1]

    y = (y_s[0] + y_s[1]) + bonus
    mean = _half_sum(y, lo) * (1.0 / RWKV_HS)
    yc = y - mean
    var = _half_sum(yc * yc, lo) * (1.0 / RWKV_HS)
    o_ref[...] = ((yc * lax.rsqrt(var + GN_EPS)) * ln_g + ln_b) * gate


def _rwkv(p, seq, mu, w0, a0, wa_up, g_up, vecs, s0=None):
    t = p.shape[0]
    nb = t // seq
    n_hp = RWKV_HEADS // 2
    base = (2 * DIFF_QW + DIFF_WIDTH) // LANES
    tail = base + 3 * n_hp
    col = lambda blk: pl.BlockSpec((seq, LANES), lambda b, hp: (b, blk(hp)))
    mu_col = lambda blk: pl.BlockSpec((2, LANES), lambda b, hp: (0, blk(hp)))
    in_specs = [
        col(lambda hp: base + hp), col(lambda hp: base + n_hp + hp), col(lambda hp: base + 2 * n_hp + hp),
        col(lambda hp: tail), col(lambda hp: tail + 1),
        mu_col(lambda hp: hp), mu_col(lambda hp: n_hp + hp), mu_col(lambda hp: 2 * n_hp + hp),
        mu_col(lambda hp: 3 * n_hp), mu_col(lambda hp: 3 * n_hp + 1),
        pl.BlockSpec((2, LANES), lambda b, hp: (0, hp)),
        pl.BlockSpec((2, LANES), lambda b, hp: (0, hp)),
        pl.BlockSpec((2, LANES, LANES), lambda b, hp: (0, 0, hp)),
        pl.BlockSpec((LANES, LANES), lambda b, hp: (0, hp)),
        pl.BlockSpec((8, LANES), lambda b, hp: (0, hp)),
    ]
    args = [p] * 5 + [mu] * 5 + [w0, a0, wa_up, g_up, vecs]
    st_spec = pl.BlockSpec((1, 1, LANES, LANES), lambda b, hp: (b, hp, 0, 0))
    if s0 is not None:
        in_specs += [st_spec, st_spec]
        args += list(s0)
    st_shape = jax.ShapeDtypeStruct((nb, n_hp, LANES, LANES), F32)
    return pl.pallas_call(
        functools.partial(_rwkv_kernel, s0 is not None),
        out_shape=(jax.ShapeDtypeStruct((t, RWKV_WIDTH), F32), st_shape, st_shape),
        grid=(nb, n_hp),
        in_specs=in_specs,
        out_specs=(pl.BlockSpec((seq, LANES), lambda b, hp: (b, hp)), st_spec, st_spec),
        scratch_shapes=[pltpu.VMEM((seq, LANES), F32)] * 3 + [pltpu.VMEM((2, seq, LANES), F32)] * 4
        + [pltpu.VMEM((2, LANES, LANES), F32)],
        compiler_params=_params("parallel", "parallel"),
        name="rwkv",
    )(*args)


def _pair_states_in(s):
    nb = s.shape[0]
    st = jnp.swapaxes(s.astype(F32), -1, -2).reshape(nb, RWKV_HEADS // 2, 2, RWKV_HS, RWKV_HS)
    z = jnp.zeros_like(st[:, :, 0])
    top = jnp.concatenate([st[:, :, 0], z], axis=-1)
    bot = jnp.concatenate([z, st[:, :, 1]], axis=-1)
    return jnp.concatenate([top, bot], axis=-2)


def _pair_states_out(st):
    nb = st.shape[0]
    h0 = st[:, :, :RWKV_HS, :RWKV_HS]
    h1 = st[:, :, RWKV_HS:, RWKV_HS:]
    s = jnp.stack([h0, h1], axis=2).reshape(nb, RWKV_HEADS, RWKV_HS, RWKV_HS)
    return jnp.swapaxes(s, -1, -2)


def _dft_kernel(f_ref, ft_ref):
    n = CONV_BLOCK

    def tile(freq_axis):
        shape = (2 * n, n) if freq_axis == 0 else (n, 2 * n)
        fr = _iota(shape, freq_axis)
        tau = _iota(shape, 1 - freq_axis)
        is_sin = fr >= n
        kf = jnp.where(is_sin, fr - n, fr)
        ang = ((kf * tau) & (2 * n - 1)).astype(F32) * (math.pi / n)
        val = jnp.where(is_sin, -jnp.sin(ang), jnp.cos(ang))
        nyq = jnp.where((tau & 1) == 0, 1.0, -1.0)
        return jnp.where(is_sin & (kf == 0), nyq, val)

    f_ref[...] = tile(0).astype(BF16)
    ft_ref[...] = tile(1).astype(BF16)


def _dft_mats():
    n = CONV_BLOCK
    return pl.pallas_call(
        _dft_kernel,
        out_shape=(jax.ShapeDtypeStruct((2 * n, n), BF16), jax.ShapeDtypeStruct((n, 2 * n), BF16)),
        name="dft_mats",
    )()


def _filter_kernel(seq, w0t_ref, w0c_ref, w0s_ref, b0_ref, w1_ref, b1_ref, w2_ref, b2_ref, fr_ref,
                   w3a_ref, w3b_ref, f_ref, o_ref, prev_scr):
    n_blk = seq // CONV_BLOCK
    e = pl.program_id(0)
    lag = (e - n_blk) * CONV_BLOCK + _iota((CONV_BLOCK, 1), 0)
    tap = jnp.abs(lag)
    valid = tap < seq
    tapf = tap.astype(F32)
    pos = tapf * (1.0 / (seq - 1))
    band = _iota((1, BANDS), 1).astype(F32)
    freq = 1e-4 + band * ((BANDS - 1 - 1e-4) / (BANDS - 1))
    arg = freq * (tapf * (2.0 * math.pi / seq))
    fr = fr_ref[...]
    pre = (pos * w0t_ref[...] + _dot_f32(jnp.cos(arg), w0c_ref[...]) + _dot_f32(-jnp.sin(arg), w0s_ref[...])
           + b0_ref[...])
    hdn = jnp.sin(fr[0:1] * pre)
    hdn = jnp.sin(fr[1:2] * (_dot_f32(hdn, w1_ref[...]) + b1_ref[...]))
    hdn = jnp.sin(fr[2:3] * (_dot_f32(hdn, w2_ref[...]) + b2_ref[...]))
    chan = _iota((1, D_MODEL), 1).astype(F32)
    delta = jnp.abs(MIN_DECAY + chan * ((MAX_DECAY - MIN_DECAY) / (D_MODEL - 1)))
    window = jnp.where(valid, jnp.exp(-pos * delta), 0.0)
    row = _iota((2 * CONV_BLOCK, 1), 0)
    sign = jnp.where((row & 1) == 1, -1.0, 1.0)
    for o, w3_ref in enumerate((w3a_ref, w3b_ref)):
        cols = slice(o * D_MODEL, (o + 1) * D_MODEL)
        taps = (_dot_f32(hdn, w3_ref[...]) * window).astype(BF16)
        spec = jnp.dot(f_ref[...], taps, preferred_element_type=F32)

        @pl.when(e > 0)
        def _():
            o_ref[0, :, cols] = spec + sign * prev_scr[:, cols]

        prev_scr[:, cols] = spec - jnp.where(row <= CONV_BLOCK, taps[0:1, :].astype(F32), 0.0)


def _hyena_filters(seq, fmat, w0, b0, w1, b1, w2, b2, w3, freq):
    n_blk = seq // CONV_BLOCK
    full = lambda a: pl.BlockSpec(a.shape, lambda e: (0,) * a.ndim)
    small = [w0[0:1], w0[1:1 + BANDS], w0[1 + BANDS:], b0.reshape(1, -1), w1, b1.reshape(1, -1),
             w2, b2.reshape(1, -1), freq]
    w3_spec = lambda o: pl.BlockSpec((FILTER_WIDTH, D_MODEL), lambda e: (0, 2 * o + jnp.where(e < n_blk, 1, 0)))
    return pl.pallas_call(
        functools.partial(_filter_kernel, seq),
        out_shape=jax.ShapeDtypeStruct((2 * n_blk - 1, 2 * CONV_BLOCK, 2 * D_MODEL), F32),
        grid=(2 * n_blk,),
        in_specs=[full(a) for a in small] + [w3_spec(0), w3_spec(1),
                                             pl.BlockSpec((2 * CONV_BLOCK, CONV_BLOCK), lambda e: (0, 0))],
        out_specs=pl.BlockSpec((1, 2 * CONV_BLOCK, 2 * D_MODEL), lambda e: (jnp.maximum(e - 1, 0), 0, 0)),
        scratch_shapes=[pltpu.VMEM((2 * CONV_BLOCK, 2 * D_MODEL), F32)],
        compiler_params=_params("arbitrary"),
        name="hyena_filters",
    )(*small, w3, w3, fmat)


def _short_conv(z, w, b):
    prev, nxt = _shift_rows(z)
    return prev * w[0:1] + z * w[1:2] + nxt * w[2:3] + b


def _conv_kernel(conv_v, n_blk, v_ref, x_ref, wv_ref, bv_ref, wx_ref, bx_ref, f_ref, ft_ref, kc_ref, bias_ref,
                 o_ref, uf_scr, yf_scr):
    cb = CONV_BLOCK
    v = v_ref[...]
    if conv_v:
        v = _short_conv(v, wv_ref[...], bv_ref[...])
    fmat = f_ref[...]
    for j in range(n_blk):
        uf_scr[j] = jnp.dot(fmat, v[j * cb:(j + 1) * cb].astype(BF16), preferred_element_type=F32)

    def mix(r0, first):
        re, im = pl.ds(r0, MIX_ROWS), pl.ds(cb + r0, MIX_ROWS)
        real_pair = (_iota((MIX_ROWS, 1), 0) == 0) if first else None
        for i in range(n_blk):
            acc_r = acc_i = None
            for j in range(n_blk):
                d = i - j + n_blk - 1
                ur, ui = uf_scr[j, re, :], uf_scr[j, im, :]
                kr, ki = kc_ref[d, re, :], kc_ref[d, im, :]
                uiki = ui * ki
                if first:
                    t_r = ur * kr - jnp.where(real_pair, 0.0, uiki)
                    t_i = jnp.where(real_pair, uiki, ur * ki + ui * kr)
                else:
                    t_r = ur * kr - uiki
                    t_i = ur * ki + ui * kr
                acc_r = t_r if acc_r is None else acc_r + t_r
                acc_i = t_i if acc_i is None else acc_i + t_i
            scale = jnp.where(real_pair, 0.5 / cb, 1.0 / cb) if first else 1.0 / cb
            yf_scr[i, re, :] = acc_r * scale
            yf_scr[i, im, :] = acc_i * scale

    mix(0, True)

    def body(r, carry):
        mix(pl.multiple_of(r * MIX_ROWS, MIX_ROWS), False)
        return carry

    lax.fori_loop(1, cb // MIX_ROWS, body, 0)

    x = _short_conv(x_ref[...], wx_ref[...], bx_ref[...])
    ftm = ft_ref[...]
    bias = bias_ref[...]
    for i in range(n_blk):
        rows = slice(i * cb, (i + 1) * cb)
        y = jnp.dot(ftm, yf_scr[i].astype(BF16), preferred_element_type=F32)
        o_ref[rows, :] = x[rows] * (y + v[rows] * bias)


def _hyena_conv(v_arr, v_blk0, x_arr, x_blk0, conv_v, seq, cw, cwb, fmat, fmat_t, kc, kc_blk0, bias, width):
    t = v_arr.shape[0]
    nb, ncb, n_blk = t // seq, D_MODEL // width, seq // CONV_BLOCK
    wv_blk0 = v_blk0 if conv_v else 0
    return pl.pallas_call(
        functools.partial(_conv_kernel, conv_v, n_blk),
        out_shape=jax.ShapeDtypeStruct((t, D_MODEL), F32),
        grid=(nb, ncb),
        in_specs=[
            pl.BlockSpec((seq, width), lambda b, c: (b, v_blk0 + c)),
            pl.BlockSpec((seq, width), lambda b, c: (b, x_blk0 + c)),
            pl.BlockSpec((3, width), lambda b, c: (0, wv_blk0 + c)),
            pl.BlockSpec((1, width), lambda b, c: (0, wv_blk0 + c)),
            pl.BlockSpec((3, width), lambda b, c: (0, x_blk0 + c)),
            pl.BlockSpec((1, width), lambda b, c: (0, x_blk0 + c)),
            pl.BlockSpec((2 * CONV_BLOCK, CONV_BLOCK), lambda b, c: (0, 0)),
            pl.BlockSpec((CONV_BLOCK, 2 * CONV_BLOCK), lambda b, c: (0, 0)),
            pl.BlockSpec((2 * n_blk - 1, 2 * CONV_BLOCK, width), lambda b, c: (0, 0, kc_blk0 + c)),
            pl.BlockSpec((1, width), lambda b, c: (0, c)),
        ],
        out_specs=pl.BlockSpec((seq, width), lambda b, c: (b, c)),
        scratch_shapes=[pltpu.VMEM((n_blk, 2 * CONV_BLOCK, width), F32),
                        pltpu.VMEM((n_blk, 2 * CONV_BLOCK, width), F32)],
        compiler_params=_params("parallel", "parallel"),
        name="hyena_conv",
    )(v_arr, x_arr, cw, cwb, cw, cwb, fmat, fmat_t, kc, bias.reshape(1, D_MODEL))


def _axial_rope_tables(seq):
    n_rows = seq // GRID_W
    row = jnp.repeat(jnp.arange(n_rows, dtype=F32), GRID_W)
    col = jnp.tile(jnp.arange(GRID_W, dtype=F32), n_rows)
    inv = ROPE_BASE ** (-jnp.arange(ROPE_PAIRS, dtype=F32) / ROPE_PAIRS)
    ar = row[:, None] * inv[None]
    ac = col[:, None] * inv[None]
    ang = jnp.concatenate([ar, ar, ac, ac], axis=-1)
    reps = LANES // DIFF_QK
    return jnp.tile(jnp.cos(ang), (1, reps)), jnp.tile(jnp.sin(ang), (1, reps))


def _pair_lanes(a, seq_axis):
    nb = a.shape[0]
    if a.ndim == 5:
        a = a.reshape(nb, DIFF_HEADS // 2, 2, 2, a.shape[3], DIFF_QK).transpose(0, 1, 4, 2, 3, 5)
    else:
        a = a.reshape(nb, DIFF_HEADS // 2, 2, a.shape[2], DIFF_V).transpose(0, 1, 3, 2, 4)
    return a.reshape(nb, DIFF_HEADS // 2, a.shape[2], LANES)


def kernel(x_prompt, x_sample, cache_diff_k, cache_diff_v, state_rwkv_fwd, state_rwkv_bwd, c, c_ctx, ada_w, ada_b, norm1_g, norm2_g, ffn_w1, ffn_w3, ffn_w2, final_g, ab_w_in, ab_w_out, diff_lambda, diff_subln_g, rwkv_mu, rwkv_w0, rwkv_w_up, rwkv_a0, rwkv_a_up, rwkv_g_up, rwkv_k_k, rwkv_k_a, rwkv_r_k, rwkv_ln_g, rwkv_ln_b, hy_w_in, hy_b_in, hy_conv_w, hy_conv_b, hy_f_w0, hy_f_b0, hy_f_w1, hy_f_b1, hy_f_w2, hy_f_b2, hy_f_w3, hy_f_freq, hy_bias, hy_w_out, hy_b_out):
    n_ctx_seqs, ctx_len, _ = x_prompt.shape
    n_lat_seqs, lat_len, _ = x_sample.shape
    tm = 1024

    cond = jnp.zeros((N_MOD_ROWS, D_MODEL), F32).at[0].set(c_ctx).at[1:1 + n_lat_seqs].set(c)
    mods = _ada_all(cond, ada_w, ada_b).reshape(DEPTH * N_MOD_ROWS * 6, 1, D_MODEL)

    groups = [
        dict(x=x_prompt.reshape(-1, D_MODEL), seq=ctx_len, row=lambda i: 0, tq=ctx_len, conv_width=1024),
        dict(x=x_sample.reshape(-1, D_MODEL), seq=lat_len, row=lambda i: 1 + i // (lat_len // tm), tq=256,
             conv_width=256),
    ]
    rope = _axial_rope_tables(lat_len)
    fmat, fmat_t = _dft_mats()
    zero_b = jnp.zeros((AB_IN,), F32)
    zero_o = jnp.zeros((D_MODEL,), F32)
    new_k, new_v, new_sf, new_sb = [], [], [], []

    for l in range(DEPTH):
        i = l // 2
        if l % 2 == 0:
            wa_up = jnp.concatenate([rwkv_w_up[i], rwkv_a_up[i]], axis=1)
            vecs = jnp.zeros((8, RWKV_WIDTH), F32).at[0].set(rwkv_k_k[i]).at[1].set(rwkv_k_a[i]) \
                .at[2].set(rwkv_r_k[i].reshape(-1)).at[3].set(rwkv_ln_g[i]).at[4].set(rwkv_ln_b[i])
        for gi, g in enumerate(groups):
            x, seq, row = g["x"], g["seq"], g["row"]
            if l % 2 == 0:
                p = _proj_in(x, norm1_g[l], mods, l, row, ab_w_in[i], zero_b, tm, AB_IN // 2)
                if gi == 0:
                    nb = x.shape[0] // seq
                    kq = p[:, DIFF_QW:2 * DIFF_QW].reshape(nb, seq, DIFF_HEADS, 2, DIFF_QK)
                    new_k.append(kq.transpose(0, 2, 3, 1, 4))
                    vq = p[:, 2 * DIFF_QW:2 * DIFF_QW + DIFF_WIDTH].reshape(nb, seq, DIFF_HEADS, DIFF_V)
                    new_v.append(vq.transpose(0, 2, 1, 3))
                    att = _attention(p, seq, g["tq"], _lambda_init(l), diff_lambda[i], diff_subln_g[i])
                    mix, sf, sb = _rwkv(p, seq, rwkv_mu[i], rwkv_w0[i], rwkv_a0[i], wa_up, rwkv_g_up[i], vecs)
                    new_sf.append(_pair_states_out(sf))
                    new_sb.append(_pair_states_out(sb))
                else:
                    ctx = (_pair_lanes(cache_diff_k[:, i], 3), _pair_lanes(cache_diff_v[:, i], 2))
                    att = _attention(p, seq, g["tq"], _lambda_init(l), diff_lambda[i], diff_subln_g[i],
                                     ctx=ctx, rope=rope)
                    s0 = (_pair_states_in(state_rwkv_fwd[:, i]), _pair_states_in(state_rwkv_bwd[:, i]))
                    mix, _, _ = _rwkv(p, seq, rwkv_mu[i], rwkv_w0[i], rwkv_a0[i], wa_up, rwkv_g_up[i], vecs, s0=s0)
                x = _proj_out(x, mods, l, row, [att, mix], ab_w_out[i], zero_o, tm)
            else:
                z = _proj_in(x, norm1_g[l], mods, l, row, hy_w_in[i], hy_b_in[i], tm, 1536)
                kf = _hyena_filters(seq, fmat, hy_f_w0[i], hy_f_b0[i], hy_f_w1[i], hy_f_b1[i], hy_f_w2[i],
                                    hy_f_b2[i], hy_f_w3[i], hy_f_freq[i])
                width = g["conv_width"]
                ncb = D_MODEL // width
                cw, cwb = hy_conv_w[i], hy_conv_b[i].reshape(1, -1)
                u = _hyena_conv(z, 2 * ncb, z, 0, True, seq, cw, cwb, fmat, fmat_t, kf, 0, hy_bias[i, 0], width)
                u = _hyena_conv(u, 0, z, ncb, False, seq, cw, cwb, fmat, fmat_t, kf, ncb, hy_bias[i, 1], width)
                x = _proj_out(x, mods, l, row, [u], hy_w_out[i], hy_b_out[i], tm)
            g["x"] = _ffn(x, norm2_g[l], mods, l, row, ffn_w1[l], ffn_w3[l], ffn_w2[l], final_g,
                          l == DEPTH - 1, tm, 256)

    y_prompt = groups[0]["x"].reshape(x_prompt.shape)
    y_sample = groups[1]["x"].reshape(x_sample.shape)
    return (y_prompt, y_sample, jnp.stack(new_k, axis=1), jnp.stack(new_v, axis=1),
            jnp.stack(new_sf, axis=1), jnp.stack(new_sb, axis=1))
```

```python
import functools
import math

import jax
import jax.numpy as jnp
import numpy as np
from jax import lax
from jax.experimental import pallas as pl
from jax.experimental.pallas import tpu as pltpu

F32 = jnp.float32
BF16 = jnp.bfloat16

D_MODEL = 1024
DEPTH = 4
GRID_W = 64
DIFF_HEADS = 8
DIFF_QK = 32
DIFF_V = 64
DIFF_QW = DIFF_HEADS * 2 * DIFF_QK
DIFF_WIDTH = DIFF_HEADS * DIFF_V
ROPE_PAIRS = DIFF_QK // 4
ROPE_BASE = 10000.0
RWKV_HEADS = 8
RWKV_HS = 64
RWKV_WIDTH = RWKV_HEADS * RWKV_HS
RWKV_IN = 3 * RWKV_WIDTH + 64 + 64 + 128
AB_IN = 2 * DIFF_QW + DIFF_WIDTH + RWKV_IN
BANDS = 16
FILTER_WIDTH = 64
MAX_DECAY = math.log(1e-2) / 0.3
MIN_DECAY = math.log(1e-2) / 1.5
D_FF = 2816
RMS_EPS = 1e-6
GN_EPS = 64e-5
HEAD_RMS_EPS = 1e-5
LOG2_E = math.log2(math.e)

LANES = 128
CHUNK = 64
SCAN_UNROLL = 8
CONV_BLOCK = 256
MIX_ROWS = 8
VMEM_LIMIT = 56 * 1024 * 1024

N_MOD_ROWS = 8


def _lambda_init(l):
    return 0.8 - 0.6 * math.exp(-0.3 * l)


def _params(*sem):
    return pltpu.CompilerParams(dimension_semantics=sem, vmem_limit_bytes=VMEM_LIMIT)


def _bdot(a, b):
    return jnp.dot(a.astype(BF16), b.astype(BF16), preferred_element_type=F32)


def _bdot_nt(a, b):
    return lax.dot_general(a.astype(BF16), b.astype(BF16), (((1,), (1,)), ((), ())),
                           preferred_element_type=F32)


def _split3(x):
    h1 = x.astype(BF16)
    r1 = x - h1.astype(F32)
    h2 = r1.astype(BF16)
    h3 = (r1 - h2.astype(F32)).astype(BF16)
    return h1, h2, h3


def _dot_f32(a, b):
    a1, a2, a3 = _split3(a)
    b1, b2, b3 = _split3(b)
    d = lambda x, y: jnp.dot(x, y, preferred_element_type=F32)
    return (d(a1, b1) + (d(a1, b2) + d(a2, b1))) + ((d(a1, b3) + d(a3, b1)) + d(a2, b2))


def _dot_exact_lhs(a_bf16, b):
    b1, b2, b3 = _split3(b)
    d = lambda y: jnp.dot(a_bf16, y, preferred_element_type=F32)
    return d(b1) + (d(b2) + d(b3))


def _sigmoid(x):
    return 1.0 / (1.0 + jnp.exp(-x))


def _iota(shape, dim):
    return lax.broadcasted_iota(jnp.int32, shape, dim)


def _shift_rows(x):
    n = x.shape[0]
    row = _iota(x.shape, 0)
    prev = jnp.where(row == 0, 0.0, pltpu.roll(x, 1, 0))
    nxt = jnp.where(row == n - 1, 0.0, pltpu.roll(x, n - 1, 0))
    return prev, nxt


def _half_sum(x, lo_mask):
    s_lo = jnp.sum(jnp.where(lo_mask, x, 0.0), axis=-1, keepdims=True)
    s_hi = jnp.sum(jnp.where(lo_mask, 0.0, x), axis=-1, keepdims=True)
    return jnp.where(lo_mask, s_lo, s_hi)


def _ada_kernel(c_ref, w_ref, b_ref, o_ref):
    c = c_ref[...]
    o_ref[0] = _bdot(c * _sigmoid(c), w_ref[0]) + b_ref[0]


def _ada_all(cond, ada_w, ada_b):
    tn = 1536
    n_out = 6 * D_MODEL
    return pl.pallas_call(
        _ada_kernel,
        out_shape=jax.ShapeDtypeStruct((DEPTH, N_MOD_ROWS, n_out), F32),
        grid=(DEPTH, n_out // tn),
        in_specs=[
            pl.BlockSpec((N_MOD_ROWS, D_MODEL), lambda l, j: (0, 0)),
            pl.BlockSpec((1, D_MODEL, tn), lambda l, j: (l, 0, j)),
            pl.BlockSpec((1, 1, tn), lambda l, j: (l, 0, j)),
        ],
        out_specs=pl.BlockSpec((1, N_MOD_ROWS, tn), lambda l, j: (l, 0, j)),
        compiler_params=_params("parallel", "parallel"),
        name="ada",
    )(cond, ada_w, ada_b.reshape(DEPTH, 1, n_out))


def _mod_spec(layer, chunk, row_of_block):
    def index_map(i, *_):
        return ((layer * N_MOD_ROWS + row_of_block(i)) * 6 + chunk, 0, 0)
    return pl.BlockSpec((1, 1, D_MODEL), index_map)


def _normed(x, g, sc, sh):
    ms = jnp.mean(x * x, axis=-1, keepdims=True)
    return (x * lax.rsqrt(ms + RMS_EPS)) * g * (1.0 + sc) + sh


def _proj_in_kernel(has_bias, x_ref, g_ref, sc_ref, sh_ref, w_ref, *refs):
    o_ref, h_scr = refs[-2:]

    @pl.when(pl.program_id(1) == 0)
    def _():
        h_scr[...] = _normed(x_ref[...], g_ref[...], sc_ref[0], sh_ref[0]).astype(BF16)

    y = jnp.dot(h_scr[...], w_ref[...].astype(BF16), preferred_element_type=F32)
    o_ref[...] = y + refs[0][...] if has_bias else y


def _proj_in(x, gains, mods, layer, row_of_block, ws, wi, bs, tm, tn):
    t, n_out = x.shape[0], ws.shape[2]
    in_specs = [
        pl.BlockSpec((tm, D_MODEL), lambda i, j: (i, 0)),
        pl.BlockSpec((None, 1, D_MODEL), lambda i, j: (layer, 0, 0)),
        _mod_spec(layer, 1, row_of_block),
        _mod_spec(layer, 0, row_of_block),
        pl.BlockSpec((None, D_MODEL, tn), lambda i, j: (wi, 0, j)),
    ]
    args = [x, gains, mods, mods, ws]
    if bs is not None:
        in_specs.append(pl.BlockSpec((None, 1, tn), lambda i, j: (wi, 0, j)))
        args.append(bs)
    return pl.pallas_call(
        functools.partial(_proj_in_kernel, bs is not None),
        out_shape=jax.ShapeDtypeStruct((t, n_out), F32),
        grid=(t // tm, n_out // tn),
        in_specs=in_specs,
        out_specs=pl.BlockSpec((tm, tn), lambda i, j: (i, j)),
        scratch_shapes=[pltpu.VMEM((tm, D_MODEL), BF16)],
        compiler_params=_params("parallel", "arbitrary"),
        name="proj_in",
    )(*args)


def _proj_out_kernel(n_u, has_bias, x_ref, gt_ref, *refs):
    u_refs, w_ref, o_ref = refs[:n_u], refs[n_u], refs[-1]
    acc = None
    off = 0
    for u_ref in u_refs:
        k = u_ref.shape[1]
        part = _bdot(u_ref[...], w_ref[off:off + k, :])
        acc = part if acc is None else acc + part
        off += k
    if has_bias:
        acc = acc + refs[n_u + 1][...]
    o_ref[...] = x_ref[...] + gt_ref[0] * acc


def _proj_out(x, mods, layer, row_of_block, us, ws, wi, bs, tm):
    t = x.shape[0]
    in_specs = [
        pl.BlockSpec((tm, D_MODEL), lambda i: (i, 0)),
        _mod_spec(layer, 2, row_of_block),
        *[pl.BlockSpec((tm, u.shape[1]), lambda i: (i, 0)) for u in us],
        pl.BlockSpec((None,) + ws.shape[1:], lambda i: (wi, 0, 0)),
    ]
    args = [x, mods, *us, ws]
    if bs is not None:
        in_specs.append(pl.BlockSpec((None, 1, D_MODEL), lambda i: (wi, 0, 0)))
        args.append(bs)
    return pl.pallas_call(
        functools.partial(_proj_out_kernel, len(us), bs is not None),
        out_shape=jax.ShapeDtypeStruct((t, D_MODEL), F32),
        grid=(t // tm,),
        in_specs=in_specs,
        out_specs=pl.BlockSpec((tm, D_MODEL), lambda i: (i, 0)),
        compiler_params=_params("parallel"),
        name="proj_out",
    )(*args)


def _ffn_kernel(final, x_ref, g_ref, sc_ref, sh_ref, gt_ref, w1_ref, w3_ref, w2_ref, fg_ref, o_ref,
                h_scr, acc_scr):
    k = pl.program_id(1)

    @pl.when(k == 0)
    def _():
        h_scr[...] = _normed(x_ref[...], g_ref[...], sc_ref[0], sh_ref[0]).astype(BF16)
        acc_scr[...] = jnp.zeros_like(acc_scr)

    h = h_scr[...]
    a1 = jnp.dot(h, w1_ref[...].astype(BF16), preferred_element_type=F32)
    a3 = jnp.dot(h, w3_ref[...].astype(BF16), preferred_element_type=F32)
    acc_scr[...] += _bdot((a1 * _sigmoid(a1)) * a3, w2_ref[...])

    @pl.when(k == pl.num_programs(1) - 1)
    def _():
        y = x_ref[...] + gt_ref[0] * acc_scr[...]
        if final:
            ms = jnp.mean(y * y, axis=-1, keepdims=True)
            y = (y * lax.rsqrt(ms + RMS_EPS)) * fg_ref[...]
        o_ref[...] = y


def _ffn(x, gains, mods, layer, row_of_block, w1s, w3s, w2s, final_g, final, tm, tk):
    t = x.shape[0]
    return pl.pallas_call(
        functools.partial(_ffn_kernel, final),
        out_shape=jax.ShapeDtypeStruct((t, D_MODEL), F32),
        grid=(t // tm, D_FF // tk),
        in_specs=[
            pl.BlockSpec((tm, D_MODEL), lambda i, k: (i, 0)),
            pl.BlockSpec((None, 1, D_MODEL), lambda i, k: (layer, 0, 0)),
            _mod_spec(layer, 4, row_of_block),
            _mod_spec(layer, 3, row_of_block),
            _mod_spec(layer, 5, row_of_block),
            pl.BlockSpec((None, D_MODEL, tk), lambda i, k: (layer, 0, k)),
            pl.BlockSpec((None, D_MODEL, tk), lambda i, k: (layer, 0, k)),
            pl.BlockSpec((None, tk, D_MODEL), lambda i, k: (layer, k, 0)),
            pl.BlockSpec((1, D_MODEL), lambda i, k: (0, 0)),
        ],
        out_specs=pl.BlockSpec((tm, D_MODEL), lambda i, k: (i, 0)),
        scratch_shapes=[pltpu.VMEM((tm, D_MODEL), BF16), pltpu.VMEM((tm, D_MODEL), F32)],
        compiler_params=_params("parallel", "arbitrary"),
        name="ffn",
    )(x, gains, mods, mods, mods, w1s, w3s, w2s, final_g.reshape(1, D_MODEL))


def _rope(x, cos, sin):
    lane = _iota(x.shape, 1)
    rot = jnp.where((lane % 16) < 8, -pltpu.roll(x, LANES - 8, 1), pltpu.roll(x, 8, 1))
    return x * cos + rot * sin


def _attn_kernel(n_ctx, use_rope, emit_kv, lam_init, *refs):
    it = iter(refs)
    q_ref, k_ref, v_ref = next(it), next(it), next(it)
    kc_ref = vc_ref = cq_ref = sq_ref = ck_ref = sk_ref = ko_ref = vo_ref = None
    if n_ctx:
        kc_ref, vc_ref = next(it), next(it)
    if use_rope:
        cq_ref, sq_ref, ck_ref, sk_ref = next(it), next(it), next(it), next(it)
    lam_ref, g_ref, o_ref = next(it), next(it), next(it)
    if emit_kv:
        ko_ref, vo_ref = next(it), next(it)
    kall, vall = next(it), next(it)
    n_own = k_ref.shape[0]
    n_pairs = q_ref.shape[1] // LANES

    @pl.when(pl.program_id(2) == 0)
    def _():
        k = k_ref[...]
        v = v_ref[...]
        if emit_kv:
            for h in range(2 * n_pairs):
                for m in range(2):
                    c0 = (2 * h + m) * DIFF_QK
                    ko_ref[0, h, m] = k[:, c0:c0 + DIFF_QK]
                vo_ref[0, h] = v[:, h * DIFF_V:(h + 1) * DIFF_V]
        if use_rope:
            k = _rope(k, ck_ref[...], sk_ref[...])
        if n_ctx:
            kall[0:n_ctx, :] = kc_ref[0, 0].astype(BF16)
            vall[0:n_ctx, :] = vc_ref[0, 0].astype(BF16)
        kall[n_ctx:n_ctx + n_own, :] = k.astype(BF16)
        vall[n_ctx:n_ctx + n_own, :] = v.astype(BF16)

    lv = lam_ref[...]
    lam = (jnp.exp(jnp.sum(lv[0:1] * lv[1:2], axis=-1, keepdims=True))
           - jnp.exp(jnp.sum(lv[2:3] * lv[3:4], axis=-1, keepdims=True)) + lam_init)

    lane = _iota((q_ref.shape[0], LANES), 1)
    lo = lane < DIFF_V
    scale = DIFF_QK ** -0.5
    for hp in range(n_pairs):
        cols = slice(hp * LANES, (hp + 1) * LANES)
        q = q_ref[:, cols]
        if use_rope:
            q = _rope(q, cq_ref[...], sq_ref[...])
        ks, vs = kall[:, cols], vall[:, cols]
        outs = []
        for h in range(2):
            pv, inv = [], []
            for m in range(2):
                j = 2 * h + m
                qm = jnp.where((lane >= DIFF_QK * j) & (lane < DIFF_QK * (j + 1)), q, 0.0)
                s = _bdot_nt(qm, ks)
                e = jnp.exp2((s - jnp.max(s, axis=-1, keepdims=True)) * (scale * LOG2_E))
                inv.append(1.0 / jnp.sum(e, axis=-1, keepdims=True))
                pv.append(_bdot(e, vs))
            outs.append(pv[0] * inv[0] - pv[1] * (lam * inv[1]))
        o = jnp.where(lo, outs[0], outs[1])
        ms = _half_sum(o * o, lo) * (1.0 / DIFF_V)
        o_ref[:, cols] = (o * lax.rsqrt(ms + HEAD_RMS_EPS)) * g_ref[...] * (1.0 - lam_init)


def _attention(p, seq, tq, n_pairs, lam_init, lams, g2s, li, ctx=None, rope=None, emit_kv=False):
    assert rope is None or n_pairs == 1
    t = p.shape[0]
    nb, nq = t // seq, seq // tq
    n_ctx = 0 if ctx is None else ctx[0].shape[2]
    width = n_pairs * LANES
    k_blk, v_blk = DIFF_QW // width, 2 * DIFF_QW // width
    in_specs = [
        pl.BlockSpec((tq, width), lambda b, hp, qi: (b * nq + qi, hp)),
        pl.BlockSpec((seq, width), lambda b, hp, qi: (b, k_blk + hp)),
        pl.BlockSpec((seq, width), lambda b, hp, qi: (b, v_blk + hp)),
    ]
    args = [p, p, p]
    if ctx is not None:
        in_specs += [pl.BlockSpec((1, 1, n_ctx, width), lambda b, hp, qi: (b, hp, 0, 0))] * 2
        args += list(ctx)
    if rope is not None:
        in_specs += [pl.BlockSpec((tq, LANES), lambda b, hp, qi: (qi, 0))] * 2
        in_specs += [pl.BlockSpec((seq, width), lambda b, hp, qi: (0, 0))] * 2
        args += [rope[0], rope[1], rope[0], rope[1]]
    in_specs += [pl.BlockSpec((None, 4, DIFF_QK), lambda b, hp, qi: (li, 0, 0)),
                 pl.BlockSpec((None, 1, LANES), lambda b, hp, qi: (li, 0, 0))]
    args += [lams, g2s]
    out_shape = [jax.ShapeDtypeStruct((t, DIFF_WIDTH), F32)]
    out_specs = [pl.BlockSpec((tq, width), lambda b, hp, qi: (b * nq + qi, hp))]
    if emit_kv:
        out_shape += [jax.ShapeDtypeStruct((nb, DIFF_HEADS, 2, seq, DIFF_QK), F32),
                      jax.ShapeDtypeStruct((nb, DIFF_HEADS, seq, DIFF_V), F32)]
        out_specs += [pl.BlockSpec((1, 2 * n_pairs, 2, seq, DIFF_QK), lambda b, hp, qi: (b, hp, 0, 0, 0)),
                      pl.BlockSpec((1, 2 * n_pairs, seq, DIFF_V), lambda b, hp, qi: (b, hp, 0, 0))]
    return pl.pallas_call(
        functools.partial(_attn_kernel, n_ctx, rope is not None, emit_kv, lam_init),
        out_shape=out_shape,
        grid=(nb, DIFF_HEADS // 2 // n_pairs, nq),
        in_specs=in_specs,
        out_specs=out_specs,
        scratch_shapes=[pltpu.VMEM((n_ctx + seq, width), BF16), pltpu.VMEM((n_ctx + seq, width), BF16)],
        compiler_params=_params("parallel", "parallel", "arbitrary"),
        name="diff_attn",
    )(*args)


def _rwkv_kernel(has_s0, *refs):
    it = iter(refs)
    r_ref, k_ref, v_ref, wa_ref, gd_ref = (next(it) for _ in range(5))
    mu_r, mu_k, mu_v, mu_wa, mu_gd = (next(it) for _ in range(5))
    w0_ref, a0_ref, wa_up_ref, g_up_ref, vec_ref = (next(it) for _ in range(5))
    s0f_ref = s0b_ref = None
    if has_s0:
        s0f_ref, s0b_ref = next(it), next(it)
    o_ref, sf_ref, sb_ref = next(it), next(it), next(it)
    r_s, v_s, kk_s = next(it), next(it), next(it)
    lw_s, kd_s, b_s, y_s = next(it), next(it), next(it), next(it)
    st_s = next(it)

    seq = r_ref.shape[0]
    n_chunks = seq // CHUNK
    unroll = min(SCAN_UNROLL, n_chunks)

    def shifted(ref, mu_ref):
        x = ref[...]
        prev, nxt = _shift_rows(x)
        mu = mu_ref[...]
        return x + mu[0:1] * (prev - x) + mu[1:2] * (nxt - x)

    r = shifted(r_ref, mu_r)
    k = shifted(k_ref, mu_k)
    v = shifted(v_ref, mu_v)
    wa = shifted(wa_ref, mu_wa)
    gd = shifted(gd_ref, mu_gd)
    vec = vec_ref[...]
    k_k, k_a, r_k, ln_g, ln_b = (vec[i:i + 1] for i in range(5))

    lane = _iota((seq, LANES), 1)
    lo = lane < RWKV_HS
    gate = _bdot(_sigmoid(gd), g_up_ref[...])
    kk = k * k_k
    kk = kk / jnp.maximum(jnp.sqrt(_half_sum(kk * kk, lo)), 1e-12)
    r_s[...] = r
    v_s[...] = v
    kk_s[...] = kk
    wd_in = jnp.where(lo, jnp.tanh(wa), 0.0)
    ad_in = jnp.where(lo, 0.0, wa)
    bonus = jnp.zeros((seq, LANES), F32)
    for d in range(2):
        wpre = w0_ref[d:d + 1, :] + _bdot(wd_in, wa_up_ref[d])
        sp = jnp.maximum(-wpre, 0.0) + jnp.log(1.0 + jnp.exp(-jnp.abs(wpre)))
        lw_s[d] = -jnp.exp(-sp - 0.5)
        a = _sigmoid(a0_ref[d:d + 1, :] + _bdot(ad_in, wa_up_ref[d]))
        kd = k * (1.0 + (a - 1.0) * k_a)
        kd_s[d] = kd
        b_s[d] = kk * a
        bonus = bonus + _half_sum(r * kd * r_k, lo) * v

    ci = _iota((CHUNK, CHUNK), 0)
    cj = _iota((CHUNK, CHUNK), 1)
    eye = jnp.where(ci == cj, 1.0, 0.0)
    blk8 = (ci // 8) == (cj // 8)
    merge_masks = [((ci // (2 * s)) == (cj // (2 * s))) & ((ci // s) != (cj // s)) for s in (8, 16, 32)]
    before = (cj < ci, cj > ci)
    upto = (cj <= ci, cj >= ci)
    tri = tuple(jnp.where(m, 1.0, 0.0).astype(BF16) for m in upto)
    clane = _iota((CHUNK, LANES), 1)
    head_lanes = (clane < RWKV_HS, clane >= RWKV_HS)
    si = _iota((LANES, LANES), 0)
    sj = _iota((LANES, LANES), 1)
    same_head = (si < RWKV_HS) == (sj < RWKV_HS)
    diag = si == sj

    def chunk_group(g):
        items = ([(0, g * unroll + u) for u in range(unroll)]
                 + [(1, n_chunks - 1 - (g * unroll + u)) for u in range(unroll)])
        n_it = len(items)
        dirs = [d for d, _ in items]
        rows = [pl.ds(pl.multiple_of(c * CHUNK, CHUNK), CHUNK) for _, c in items]
        r_c = [r_s[rw, :] for rw in rows]
        v_c = [v_s[rw, :] for rw in rows]
        kk_c = [kk_s[rw, :] for rw in rows]
        lw_c = [lw_s[d, rw, :] for d, rw in zip(dirs, rows)]
        kd_c = [kd_s[d, rw, :] for d, rw in zip(dirs, rows)]
        b_c = [b_s[d, rw, :] for d, rw in zip(dirs, rows)]
        cum = [_dot_exact_lhs(tri[d], lw) for d, lw in zip(dirs, lw_c)]
        tot = [cm[CHUNK - 1:CHUNK, :] if d == 0 else cm[0:1, :] for d, cm in zip(dirs, cum)]
        e_neg = [jnp.exp(-cm) for cm in cum]
        al = [jnp.exp(cm - lw) * kk for cm, lw, kk in zip(cum, lw_c, kk_c)]
        be = [b * e for b, e in zip(b_c, e_neg)]
        ka = [kd * e for kd, e in zip(kd_c, e_neg)]
        rh = [r * jnp.exp(cm) for r, cm in zip(r_c, cum)]
        wc = [jnp.exp(tt) for tt in tot]
        combos = [(i, h) for i in range(n_it) for h in range(2)]
        x_h = [jnp.concatenate([jnp.where(head_lanes[h], al[i], 0.0), jnp.where(head_lanes[h], rh[i], 0.0)], axis=0)
               for i, h in combos]
        xb = [_bdot_nt(x, be[i]) for x, (i, _) in zip(x_h, combos)]
        xk = [_bdot_nt(x, ka[i]) for x, (i, _) in zip(x_h, combos)]
        n_mat = [jnp.where(before[dirs[i]], -x[0:CHUNK], 0.0) for x, (i, _) in zip(xb, combos)]
        g_mat = [jnp.where(before[dirs[i]], x[0:CHUNK], 0.0) for x, (i, _) in zip(xk, combos)]
        m2 = [jnp.where(upto[dirs[i]], x[CHUNK:], 0.0) for x, (i, _) in zip(xb, combos)]
        m1 = [jnp.where(upto[dirs[i]], x[CHUNK:], 0.0) for x, (i, _) in zip(xk, combos)]
        nd = [jnp.where(blk8, n, 0.0) for n in n_mat]
        t = [eye + x for x in nd]
        p = [_bdot(x, x) for x in nd]
        t = [a + _bdot(a, b) for a, b in zip(t, p)]
        p = [_bdot(x, x) for x in p]
        t = [a + _bdot(a, b) for a, b in zip(t, p)]
        for msk in merge_masks:
            q = [_bdot(jnp.where(msk, n, 0.0), a) for n, a in zip(n_mat, t)]
            t = [a + _bdot(a, b) for a, b in zip(t, q)]
        gv = [_bdot(g_, v_c[i]) for g_, (i, _) in zip(g_mat, combos)]
        z = [_bdot(t_, jnp.concatenate([al[i], g_], axis=1)) for t_, g_, (i, _) in zip(t, gv, combos)]
        yv_h = [_bdot(m, v_c[i]) for m, (i, _) in zip(m1, combos)]
        pick = lambda a, b: jnp.where(head_lanes[0], a, b)
        alp = [pick(z[2 * i][:, :LANES], z[2 * i + 1][:, :LANES]) for i in range(n_it)]
        uv = [pick(z[2 * i][:, LANES:], z[2 * i + 1][:, LANES:]) for i in range(n_it)]
        yv = [pick(yv_h[2 * i], yv_h[2 * i + 1]) for i in range(n_it)]
        be_t = [(b * w).T for b, w in zip(be, wc)]
        ka_t = [(k_ * w).T for k_, w in zip(ka, wc)]
        pz = [_bdot(bt, jnp.concatenate([a, u_], axis=1)) for bt, a, u_ in zip(be_t, alp, uv)]
        kv = [_bdot(kt, v_) for kt, v_ in zip(ka_t, v_c)]
        a_mat = [jnp.where(same_head, jnp.where(diag, w, 0.0) - pp[:, :LANES], 0.0) for w, pp in zip(wc, pz)]
        b_mat = [jnp.where(same_head, k_ - pp[:, LANES:], 0.0) for k_, pp in zip(kv, pz)]
        for i, d in enumerate(dirs):
            st = st_s[d]
            xs = _bdot(jnp.concatenate([alp[i], rh[i]], axis=0), st)
            u = uv[i] + xs[0:CHUNK]
            m2u = pick(_bdot(m2[2 * i], u), _bdot(m2[2 * i + 1], u))
            y_s[d, rows[i], :] = xs[CHUNK:] + yv[i] - m2u
            st_s[d] = _bdot(a_mat[i], st) + b_mat[i]

    def body(g, carry):
        chunk_group(g)
        return carry

    if has_s0:
        st_s[0] = s0f_ref[0, 0]
        st_s[1] = s0b_ref[0, 0]
    else:
        st_s[...] = jnp.zeros_like(st_s)
    lax.fori_loop(0, n_chunks // unroll, body, 0)
    sf_ref[0, 0] = st_s[0]
    sb_ref[0, 0] = st_s[1]

    y = (y_s[0] + y_s[1]) + bonus
    mean = _half_sum(y, lo) * (1.0 / RWKV_HS)
    yc = y - mean
    var = _half_sum(yc * yc, lo) * (1.0 / RWKV_HS)
    o_ref[...] = ((yc * lax.rsqrt(var + GN_EPS)) * ln_g + ln_b) * gate


def _rwkv(p, seq, li, mu, w0, a0, wa_up, g_up, vecs, s0=None):
    t = p.shape[0]
    nb = t // seq
    n_hp = RWKV_HEADS // 2
    base = (2 * DIFF_QW + DIFF_WIDTH) // LANES
    tail = base + 3 * n_hp
    col = lambda blk: pl.BlockSpec((seq, LANES), lambda b, hp: (b, blk(hp)))
    mu_col = lambda blk: pl.BlockSpec((None, 2, LANES), lambda b, hp: (li, 0, blk(hp)))
    in_specs = [
        col(lambda hp: base + hp), col(lambda hp: base + n_hp + hp), col(lambda hp: base + 2 * n_hp + hp),
        col(lambda hp: tail), col(lambda hp: tail + 1),
        mu_col(lambda hp: hp), mu_col(lambda hp: n_hp + hp), mu_col(lambda hp: 2 * n_hp + hp),
        mu_col(lambda hp: 3 * n_hp), mu_col(lambda hp: 3 * n_hp + 1),
        pl.BlockSpec((None, 2, LANES), lambda b, hp: (li, 0, hp)),
        pl.BlockSpec((None, 2, LANES), lambda b, hp: (li, 0, hp)),
        pl.BlockSpec((None, 2, LANES, LANES), lambda b, hp: (li, 0, 0, hp)),
        pl.BlockSpec((None, LANES, LANES), lambda b, hp: (li, 0, hp)),
        pl.BlockSpec((None, 8, LANES), lambda b, hp: (li, 0, hp)),
    ]
    args = [p] * 5 + [mu] * 5 + [w0, a0, wa_up, g_up, vecs]
    st_spec = pl.BlockSpec((1, 1, LANES, LANES), lambda b, hp: (b, hp, 0, 0))
    if s0 is not None:
        in_specs += [st_spec, st_spec]
        args += list(s0)
    st_shape = jax.ShapeDtypeStruct((nb, n_hp, LANES, LANES), F32)
    return pl.pallas_call(
        functools.partial(_rwkv_kernel, s0 is not None),
        out_shape=(jax.ShapeDtypeStruct((t, RWKV_WIDTH), F32), st_shape, st_shape),
        grid=(nb, n_hp),
        in_specs=in_specs,
        out_specs=(pl.BlockSpec((seq, LANES), lambda b, hp: (b, hp)), st_spec, st_spec),
        scratch_shapes=[pltpu.VMEM((seq, LANES), F32)] * 3 + [pltpu.VMEM((2, seq, LANES), F32)] * 4
        + [pltpu.VMEM((2, LANES, LANES), F32)],
        compiler_params=_params("parallel", "parallel"),
        name="rwkv",
    )(*args)


def _pair_states_in(s):
    nb = s.shape[0]
    st = jnp.swapaxes(s.astype(F32), -1, -2).reshape(nb, RWKV_HEADS // 2, 2, RWKV_HS, RWKV_HS)
    z = jnp.zeros_like(st[:, :, 0])
    top = jnp.concatenate([st[:, :, 0], z], axis=-1)
    bot = jnp.concatenate([z, st[:, :, 1]], axis=-1)
    return jnp.concatenate([top, bot], axis=-2)


def _pair_states_out(st):
    nb = st.shape[0]
    h0 = st[:, :, :RWKV_HS, :RWKV_HS]
    h1 = st[:, :, RWKV_HS:, RWKV_HS:]
    s = jnp.stack([h0, h1], axis=2).reshape(nb, RWKV_HEADS, RWKV_HS, RWKV_HS)
    return jnp.swapaxes(s, -1, -2)


def _dft_kernel(f_ref, ft_ref):
    n = CONV_BLOCK

    def tile(freq_axis):
        shape = (2 * n, n) if freq_axis == 0 else (n, 2 * n)
        fr = _iota(shape, freq_axis)
        tau = _iota(shape, 1 - freq_axis)
        is_sin = fr >= n
        kf = jnp.where(is_sin, fr - n, fr)
        ang = ((kf * tau) & (2 * n - 1)).astype(F32) * (math.pi / n)
        val = jnp.where(is_sin, -jnp.sin(ang), jnp.cos(ang))
        nyq = jnp.where((tau & 1) == 0, 1.0, -1.0)
        return jnp.where(is_sin & (kf == 0), nyq, val)

    f_ref[...] = tile(0).astype(BF16)
    ft_ref[...] = tile(1).astype(BF16)


def _dft_mats():
    n = CONV_BLOCK
    return pl.pallas_call(
        _dft_kernel,
        out_shape=(jax.ShapeDtypeStruct((2 * n, n), BF16), jax.ShapeDtypeStruct((n, 2 * n), BF16)),
        name="dft_mats",
    )()


def _filter_kernel(seq, w0t_ref, w0c_ref, w0s_ref, b0_ref, w1_ref, b1_ref, w2_ref, b2_ref, fr_ref,
                   w3a_ref, w3b_ref, f_ref, o_ref, prev_scr):
    n_blk = seq // CONV_BLOCK
    e = pl.program_id(0)
    lag = (e - n_blk) * CONV_BLOCK + _iota((CONV_BLOCK, 1), 0)
    tap = jnp.abs(lag)
    valid = tap < seq
    tapf = tap.astype(F32)
    pos = tapf * (1.0 / (seq - 1))
    band = _iota((1, BANDS), 1).astype(F32)
    freq = 1e-4 + band * ((BANDS - 1 - 1e-4) / (BANDS - 1))
    arg = freq * (tapf * (2.0 * math.pi / seq))
    fr = fr_ref[...]
    pre = (pos * w0t_ref[...] + _dot_f32(jnp.cos(arg), w0c_ref[...]) + _dot_f32(-jnp.sin(arg), w0s_ref[...])
           + b0_ref[...])
    hdn = jnp.sin(fr[0:1] * pre)
    hdn = jnp.sin(fr[1:2] * (_dot_f32(hdn, w1_ref[...]) + b1_ref[...]))
    hdn = jnp.sin(fr[2:3] * (_dot_f32(hdn, w2_ref[...]) + b2_ref[...]))
    chan = _iota((1, D_MODEL), 1).astype(F32)
    delta = jnp.abs(MIN_DECAY + chan * ((MAX_DECAY - MIN_DECAY) / (D_MODEL - 1)))
    window = jnp.where(valid, jnp.exp(-pos * delta), 0.0)
    row = _iota((2 * CONV_BLOCK, 1), 0)
    sign = jnp.where((row & 1) == 1, -1.0, 1.0)
    for o, w3_ref in enumerate((w3a_ref, w3b_ref)):
        cols = slice(o * D_MODEL, (o + 1) * D_MODEL)
        taps = (_dot_f32(hdn, w3_ref[...]) * window).astype(BF16)
        spec = jnp.dot(f_ref[...], taps, preferred_element_type=F32)

        @pl.when(e > 0)
        def _():
            o_ref[0, :, cols] = spec + sign * prev_scr[:, cols]

        prev_scr[:, cols] = spec - jnp.where(row <= CONV_BLOCK, taps[0:1, :].astype(F32), 0.0)


def _hyena_filters(seq, li, fmat, small, w3):
    n_blk = seq // CONV_BLOCK
    layer_slice = lambda a: pl.BlockSpec((None,) + a.shape[1:], lambda e: (li, 0, 0))
    w3_spec = lambda o: pl.BlockSpec((None, FILTER_WIDTH, D_MODEL),
                                     lambda e: (li, 0, 2 * o + jnp.where(e < n_blk, 1, 0)))
    return pl.pallas_call(
        functools.partial(_filter_kernel, seq),
        out_shape=jax.ShapeDtypeStruct((2 * n_blk - 1, 2 * CONV_BLOCK, 2 * D_MODEL), F32),
        grid=(2 * n_blk,),
        in_specs=[layer_slice(a) for a in small] + [w3_spec(0), w3_spec(1),
                                                    pl.BlockSpec((2 * CONV_BLOCK, CONV_BLOCK), lambda e: (0, 0))],
        out_specs=pl.BlockSpec((1, 2 * CONV_BLOCK, 2 * D_MODEL), lambda e: (jnp.maximum(e - 1, 0), 0, 0)),
        scratch_shapes=[pltpu.VMEM((2 * CONV_BLOCK, 2 * D_MODEL), F32)],
        compiler_params=_params("arbitrary"),
        name="hyena_filters",
    )(*small, w3, w3, fmat)


def _short_conv(z, w, b):
    prev, nxt = _shift_rows(z)
    return prev * w[0:1] + z * w[1:2] + nxt * w[2:3] + b


def _conv_kernel(conv_v, n_blk, v_ref, x_ref, wv_ref, bv_ref, wx_ref, bx_ref, f_ref, ft_ref, kc_ref, bias_ref,
                 o_ref, uf_scr, yf_scr):
    cb = CONV_BLOCK
    v = v_ref[...]
    if conv_v:
        v = _short_conv(v, wv_ref[...], bv_ref[...])
    fmat = f_ref[...]
    for j in range(n_blk):
        uf_scr[j] = jnp.dot(fmat, v[j * cb:(j + 1) * cb].astype(BF16), preferred_element_type=F32)

    def mix(r0, first):
        re, im = pl.ds(r0, MIX_ROWS), pl.ds(cb + r0, MIX_ROWS)
        real_pair = (_iota((MIX_ROWS, 1), 0) == 0) if first else None
        for i in range(n_blk):
            acc_r = acc_i = None
            for j in range(n_blk):
                d = i - j + n_blk - 1
                ur, ui = uf_scr[j, re, :], uf_scr[j, im, :]
                kr, ki = kc_ref[d, re, :], kc_ref[d, im, :]
                uiki = ui * ki
                if first:
                    t_r = ur * kr - jnp.where(real_pair, 0.0, uiki)
                    t_i = jnp.where(real_pair, uiki, ur * ki + ui * kr)
                else:
                    t_r = ur * kr - uiki
                    t_i = ur * ki + ui * kr
                acc_r = t_r if acc_r is None else acc_r + t_r
                acc_i = t_i if acc_i is None else acc_i + t_i
            scale = jnp.where(real_pair, 0.5 / cb, 1.0 / cb) if first else 1.0 / cb
            yf_scr[i, re, :] = acc_r * scale
            yf_scr[i, im, :] = acc_i * scale

    mix(0, True)

    def body(r, carry):
        mix(pl.multiple_of(r * MIX_ROWS, MIX_ROWS), False)
        return carry

    lax.fori_loop(1, cb // MIX_ROWS, body, 0)

    x = _short_conv(x_ref[...], wx_ref[...], bx_ref[...])
    ftm = ft_ref[...]
    bias = bias_ref[...]
    for i in range(n_blk):
        rows = slice(i * cb, (i + 1) * cb)
        y = jnp.dot(ftm, yf_scr[i].astype(BF16), preferred_element_type=F32)
        o_ref[rows, :] = x[rows] * (y + v[rows] * bias)


def _hyena_conv(v_arr, v_blk0, x_arr, x_blk0, conv_v, seq, li, order, cw, cwb, fmat, fmat_t, kc, biases, width):
    t = v_arr.shape[0]
    nb, ncb, n_blk = t // seq, D_MODEL // width, seq // CONV_BLOCK
    wv_blk0 = v_blk0 if conv_v else 0
    kc_blk0 = order * ncb
    return pl.pallas_call(
        functools.partial(_conv_kernel, conv_v, n_blk),
        out_shape=jax.ShapeDtypeStruct((t, D_MODEL), F32),
        grid=(nb, ncb),
        in_specs=[
            pl.BlockSpec((seq, width), lambda b, c: (b, v_blk0 + c)),
            pl.BlockSpec((seq, width), lambda b, c: (b, x_blk0 + c)),
            pl.BlockSpec((None, 3, width), lambda b, c: (li, 0, wv_blk0 + c)),
            pl.BlockSpec((None, 1, width), lambda b, c: (li, 0, wv_blk0 + c)),
            pl.BlockSpec((None, 3, width), lambda b, c: (li, 0, x_blk0 + c)),
            pl.BlockSpec((None, 1, width), lambda b, c: (li, 0, x_blk0 + c)),
            pl.BlockSpec((2 * CONV_BLOCK, CONV_BLOCK), lambda b, c: (0, 0)),
            pl.BlockSpec((CONV_BLOCK, 2 * CONV_BLOCK), lambda b, c: (0, 0)),
            pl.BlockSpec((2 * n_blk - 1, 2 * CONV_BLOCK, width), lambda b, c: (0, 0, kc_blk0 + c)),
            pl.BlockSpec((None, 1, width), lambda b, c: (li * 2 + order, 0, c)),
        ],
        out_specs=pl.BlockSpec((seq, width), lambda b, c: (b, c)),
        scratch_shapes=[pltpu.VMEM((n_blk, 2 * CONV_BLOCK, width), F32),
                        pltpu.VMEM((n_blk, 2 * CONV_BLOCK, width), F32)],
        compiler_params=_params("parallel", "parallel"),
        name="hyena_conv",
    )(v_arr, x_arr, cw, cwb, cw, cwb, fmat, fmat_t, kc, biases)


def _axial_rope_tables(seq):
    n_rows = seq // GRID_W
    row = np.repeat(np.arange(n_rows, dtype=np.float64), GRID_W)
    col = np.tile(np.arange(GRID_W, dtype=np.float64), n_rows)
    inv = ROPE_BASE ** (-np.arange(ROPE_PAIRS, dtype=np.float64) / ROPE_PAIRS)
    ar = row[:, None] * inv[None]
    ac = col[:, None] * inv[None]
    ang = np.concatenate([ar, ar, ac, ac], axis=-1)
    reps = LANES // DIFF_QK
    return (jnp.asarray(np.tile(np.cos(ang), (1, reps)), F32), jnp.asarray(np.tile(np.sin(ang), (1, reps)), F32))


def _pair_lanes(a, seq_axis):
    nb = a.shape[0]
    if a.ndim == 5:
        a = a.reshape(nb, DIFF_HEADS // 2, 2, 2, a.shape[3], DIFF_QK).transpose(0, 1, 4, 2, 3, 5)
    else:
        a = a.reshape(nb, DIFF_HEADS // 2, 2, a.shape[2], DIFF_V).transpose(0, 1, 3, 2, 4)
    return a.reshape(nb, DIFF_HEADS // 2, a.shape[2], LANES)


def kernel(x_prompt, x_sample, cache_diff_k, cache_diff_v, state_rwkv_fwd, state_rwkv_bwd, c, c_ctx, ada_w, ada_b, norm1_g, norm2_g, ffn_w1, ffn_w3, ffn_w2, final_g, ab_w_in, ab_w_out, diff_lambda, diff_subln_g, rwkv_mu, rwkv_w0, rwkv_w_up, rwkv_a0, rwkv_a_up, rwkv_g_up, rwkv_k_k, rwkv_k_a, rwkv_r_k, rwkv_ln_g, rwkv_ln_b, hy_w_in, hy_b_in, hy_conv_w, hy_conv_b, hy_f_w0, hy_f_b0, hy_f_w1, hy_f_b1, hy_f_w2, hy_f_b2, hy_f_w3, hy_f_freq, hy_bias, hy_w_out, hy_b_out):
    n_ctx_seqs, ctx_len, _ = x_prompt.shape
    n_lat_seqs, lat_len, _ = x_sample.shape
    tm = 1024

    cond = jnp.zeros((N_MOD_ROWS, D_MODEL), F32).at[0].set(c_ctx).at[1:1 + n_lat_seqs].set(c)
    mods = _ada_all(cond, ada_w, ada_b).reshape(DEPTH * N_MOD_ROWS * 6, 1, D_MODEL)

    groups = [
        dict(x=x_prompt.reshape(-1, D_MODEL), seq=ctx_len, row=lambda i: 0, tq=ctx_len, pairs=4, conv_width=1024),
        dict(x=x_sample.reshape(-1, D_MODEL), seq=lat_len, row=lambda i: 1 + i // (lat_len // tm), tq=256,
             pairs=1, conv_width=256),
    ]
    rope = _axial_rope_tables(lat_len)
    fmat, fmat_t = _dft_mats()
    n_ab, n_c = ab_w_in.shape[0], hy_w_in.shape[0]

    gains1 = norm1_g.reshape(DEPTH, 1, D_MODEL)
    gains2 = norm2_g.reshape(DEPTH, 1, D_MODEL)
    g2s = jnp.tile(diff_subln_g, (1, 2)).reshape(n_ab, 1, LANES)
    wa_up = jnp.concatenate([rwkv_w_up, rwkv_a_up], axis=2)
    vec_rows = [rwkv_k_k, rwkv_k_a, rwkv_r_k.reshape(n_ab, RWKV_WIDTH), rwkv_ln_g, rwkv_ln_b]
    vecs = jnp.stack(vec_rows + [jnp.zeros_like(rwkv_k_k)] * (8 - len(vec_rows)), axis=1)
    filt_small = [hy_f_w0[:, 0:1], hy_f_w0[:, 1:1 + BANDS], hy_f_w0[:, 1 + BANDS:], hy_f_b0[:, None],
                  hy_f_w1, hy_f_b1[:, None], hy_f_w2, hy_f_b2[:, None], hy_f_freq]
    hy_b_in_s = hy_b_in.reshape(n_c, 1, -1)
    hy_b_out_s = hy_b_out.reshape(n_c, 1, D_MODEL)
    conv_b = hy_conv_b.reshape(n_c, 1, -1)
    conv_bias = hy_bias.reshape(n_c * 2, 1, D_MODEL)
    new_k, new_v, new_sf, new_sb = [], [], [], []

    for l in range(DEPTH):
        i = l // 2
        for gi, g in enumerate(groups):
            x, seq, row = g["x"], g["seq"], g["row"]
            if l % 2 == 0:
                p = _proj_in(x, gains1, mods, l, row, ab_w_in, i, None, tm, AB_IN // 2)
                if gi == 0:
                    att, k_own, v_own = _attention(p, seq, g["tq"], g["pairs"], _lambda_init(l), diff_lambda, g2s, i,
                                                   emit_kv=True)
                    new_k.append(k_own)
                    new_v.append(v_own)
                    mix, sf, sb = _rwkv(p, seq, i, rwkv_mu, rwkv_w0, rwkv_a0, wa_up, rwkv_g_up, vecs)
                    new_sf.append(_pair_states_out(sf))
                    new_sb.append(_pair_states_out(sb))
                else:
                    ctx = (_pair_lanes(cache_diff_k[:, i], 3), _pair_lanes(cache_diff_v[:, i], 2))
                    att, = _attention(p, seq, g["tq"], g["pairs"], _lambda_init(l), diff_lambda, g2s, i,
                                      ctx=ctx, rope=rope)
                    s0 = (_pair_states_in(state_rwkv_fwd[:, i]), _pair_states_in(state_rwkv_bwd[:, i]))
                    mix, _, _ = _rwkv(p, seq, i, rwkv_mu, rwkv_w0, rwkv_a0, wa_up, rwkv_g_up, vecs, s0=s0)
                x = _proj_out(x, mods, l, row, [att, mix], ab_w_out, i, None, tm)
            else:
                z = _proj_in(x, gains1, mods, l, row, hy_w_in, i, hy_b_in_s, tm, 1536)
                kc = _hyena_filters(seq, i, fmat, filt_small, hy_f_w3)
                width = g["conv_width"]
                ncb = D_MODEL // width
                u = _hyena_conv(z, 2 * ncb, z, 0, True, seq, i, 0, hy_conv_w, conv_b, fmat, fmat_t, kc, conv_bias, width)
                u = _hyena_conv(u, 0, z, ncb, False, seq, i, 1, hy_conv_w, conv_b, fmat, fmat_t, kc, conv_bias, width)
                x = _proj_out(x, mods, l, row, [u], hy_w_out, i, hy_b_out_s, tm)
            g["x"] = _ffn(x, gains2, mods, l, row, ffn_w1, ffn_w3, ffn_w2, final_g, l == DEPTH - 1, tm, 256)

    y_prompt = groups[0]["x"].reshape(x_prompt.shape)
    y_sample = groups[1]["x"].reshape(x_sample.shape)
    return (y_prompt, y_sample, jnp.stack(new_k, axis=1), jnp.stack(new_v, axis=1),
            jnp.stack(new_sf, axis=1), jnp.stack(new_sb, axis=1))
```

```python
import functools
import math

import jax
import jax.numpy as jnp
import numpy as np
from jax import lax
from jax.experimental import pallas as pl
from jax.experimental.pallas import tpu as pltpu

F32 = jnp.float32
BF16 = jnp.bfloat16

D_MODEL = 1024
DEPTH = 4
GRID_W = 64
DIFF_HEADS = 8
DIFF_QK = 32
DIFF_V = 64
DIFF_QW = DIFF_HEADS * 2 * DIFF_QK
DIFF_WIDTH = DIFF_HEADS * DIFF_V
ROPE_PAIRS = DIFF_QK // 4
ROPE_BASE = 10000.0
RWKV_HEADS = 8
RWKV_HS = 64
RWKV_WIDTH = RWKV_HEADS * RWKV_HS
RWKV_IN = 3 * RWKV_WIDTH + 64 + 64 + 128
AB_IN = 2 * DIFF_QW + DIFF_WIDTH + RWKV_IN
BANDS = 16
FILTER_WIDTH = 64
MAX_DECAY = math.log(1e-2) / 0.3
MIN_DECAY = math.log(1e-2) / 1.5
D_FF = 2816
RMS_EPS = 1e-6
GN_EPS = 64e-5
HEAD_RMS_EPS = 1e-5
LOG2_E = math.log2(math.e)

LANES = 128
CHUNK = 64
SCAN_UNROLL = 8
CONV_BLOCK = 256
MIX_ROWS = 8
VMEM_LIMIT = 56 * 1024 * 1024

N_MOD_ROWS = 8


def _lambda_init(l):
    return 0.8 - 0.6 * math.exp(-0.3 * l)


def _params(*sem):
    return pltpu.CompilerParams(dimension_semantics=sem, vmem_limit_bytes=VMEM_LIMIT)


def _bdot(a, b):
    return jnp.dot(a.astype(BF16), b.astype(BF16), preferred_element_type=F32)


def _bdot_nt(a, b):
    return lax.dot_general(a.astype(BF16), b.astype(BF16), (((1,), (1,)), ((), ())),
                           preferred_element_type=F32)


def _split3(x):
    h1 = x.astype(BF16)
    r1 = x - h1.astype(F32)
    h2 = r1.astype(BF16)
    h3 = (r1 - h2.astype(F32)).astype(BF16)
    return h1, h2, h3


def _dot_f32(a, b):
    a1, a2, a3 = _split3(a)
    b1, b2, b3 = _split3(b)
    d = lambda x, y: jnp.dot(x, y, preferred_element_type=F32)
    return (d(a1, b1) + (d(a1, b2) + d(a2, b1))) + ((d(a1, b3) + d(a3, b1)) + d(a2, b2))


def _dot_exact_lhs(a_bf16, b):
    b1, b2, b3 = _split3(b)
    d = lambda y: jnp.dot(a_bf16, y, preferred_element_type=F32)
    return d(b1) + (d(b2) + d(b3))


def _sigmoid(x):
    return 0.5 * (jnp.tanh(0.5 * x) + 1.0)


def _iota(shape, dim):
    return lax.broadcasted_iota(jnp.int32, shape, dim)


def _shift_rows(x):
    n = x.shape[0]
    row = _iota(x.shape, 0)
    prev = jnp.where(row == 0, 0.0, pltpu.roll(x, 1, 0))
    nxt = jnp.where(row == n - 1, 0.0, pltpu.roll(x, n - 1, 0))
    return prev, nxt


def _half_sum(x, lo_mask):
    s_lo = jnp.sum(jnp.where(lo_mask, x, 0.0), axis=-1, keepdims=True)
    s_hi = jnp.sum(jnp.where(lo_mask, 0.0, x), axis=-1, keepdims=True)
    return jnp.where(lo_mask, s_lo, s_hi)


def _ada_kernel(c_ref, w_ref, b_ref, o_ref):
    c = c_ref[...]
    o_ref[0] = _bdot(c * _sigmoid(c), w_ref[0]) + b_ref[0]


def _ada_all(cond, ada_w, ada_b):
    tn = 1536
    n_out = 6 * D_MODEL
    return pl.pallas_call(
        _ada_kernel,
        out_shape=jax.ShapeDtypeStruct((DEPTH, N_MOD_ROWS, n_out), F32),
        grid=(DEPTH, n_out // tn),
        in_specs=[
            pl.BlockSpec((N_MOD_ROWS, D_MODEL), lambda l, j: (0, 0)),
            pl.BlockSpec((1, D_MODEL, tn), lambda l, j: (l, 0, j)),
            pl.BlockSpec((1, 1, tn), lambda l, j: (l, 0, j)),
        ],
        out_specs=pl.BlockSpec((1, N_MOD_ROWS, tn), lambda l, j: (l, 0, j)),
        compiler_params=_params("parallel", "parallel"),
        name="ada",
    )(cond, ada_w, ada_b.reshape(DEPTH, 1, n_out))


def _mod_spec(layer, chunk, cond_rows, tm, axis=0):
    first, span = cond_rows

    def index_map(*idx):
        return ((layer * N_MOD_ROWS + first + (idx[axis] * tm) // span) * 6 + chunk, 0, 0)
    return pl.BlockSpec((1, 1, D_MODEL), index_map)


def _normed(x, g, sc, sh):
    ms = jnp.mean(x * x, axis=-1, keepdims=True)
    return (x * lax.rsqrt(ms + RMS_EPS)) * g * (1.0 + sc) + sh


def _proj_in_kernel(has_bias, x_ref, g_ref, sc_ref, sh_ref, w_ref, *refs):
    o_ref = refs[-1]
    h = _normed(x_ref[...], g_ref[...], sc_ref[0], sh_ref[0])
    y = _bdot(h, w_ref[...])
    o_ref[...] = y + refs[0][...] if has_bias else y


def _proj_in(x, gains, mods, layer, cond_rows, ws, wi, bs, tm, tn):
    t, n_out = x.shape[0], ws.shape[2]
    in_specs = [
        pl.BlockSpec((tm, D_MODEL), lambda j, i: (i, 0)),
        pl.BlockSpec((None, 1, D_MODEL), lambda j, i: (layer, 0, 0)),
        _mod_spec(layer, 1, cond_rows, tm, axis=1),
        _mod_spec(layer, 0, cond_rows, tm, axis=1),
        pl.BlockSpec((None, D_MODEL, tn), lambda j, i: (wi, 0, j)),
    ]
    args = [x, gains, mods, mods, ws]
    if bs is not None:
        in_specs.append(pl.BlockSpec((None, 1, tn), lambda j, i: (wi, 0, j)))
        args.append(bs)
    return pl.pallas_call(
        functools.partial(_proj_in_kernel, bs is not None),
        out_shape=jax.ShapeDtypeStruct((t, n_out), F32),
        grid=(n_out // tn, t // tm),
        in_specs=in_specs,
        out_specs=pl.BlockSpec((tm, tn), lambda j, i: (i, j)),
        compiler_params=_params("parallel", "parallel"),
        name="proj_in",
    )(*args)


def _proj_out_kernel(n_u, has_bias, x_ref, gt_ref, *refs):
    u_refs, w_ref, o_ref = refs[:n_u], refs[n_u], refs[-1]
    acc = None
    off = 0
    for u_ref in u_refs:
        k = u_ref.shape[1]
        part = _bdot(u_ref[...], w_ref[off:off + k, :])
        acc = part if acc is None else acc + part
        off += k
    if has_bias:
        acc = acc + refs[n_u + 1][...]
    o_ref[...] = x_ref[...] + gt_ref[0] * acc


def _proj_out(x, mods, layer, cond_rows, us, ws, wi, bs, tm):
    t = x.shape[0]
    in_specs = [
        pl.BlockSpec((tm, D_MODEL), lambda i: (i, 0)),
        _mod_spec(layer, 2, cond_rows, tm),
        *[pl.BlockSpec((tm, u.shape[1]), lambda i: (i, 0)) for u in us],
        pl.BlockSpec((None,) + ws.shape[1:], lambda i: (wi, 0, 0)),
    ]
    args = [x, mods, *us, ws]
    if bs is not None:
        in_specs.append(pl.BlockSpec((None, 1, D_MODEL), lambda i: (wi, 0, 0)))
        args.append(bs)
    return pl.pallas_call(
        functools.partial(_proj_out_kernel, len(us), bs is not None),
        out_shape=jax.ShapeDtypeStruct((t, D_MODEL), F32),
        grid=(t // tm,),
        in_specs=in_specs,
        out_specs=pl.BlockSpec((tm, D_MODEL), lambda i: (i, 0)),
        compiler_params=_params("parallel"),
        name="proj_out",
    )(*args)


def _ffn_kernel(final, x_ref, g_ref, sc_ref, sh_ref, gt_ref, w1_ref, w3_ref, w2_ref, fg_ref, o_ref, h_scr):
    k = pl.program_id(1)

    @pl.when(k == 0)
    def _():
        h_scr[...] = _normed(x_ref[...], g_ref[...], sc_ref[0], sh_ref[0]).astype(BF16)
        o_ref[...] = jnp.zeros_like(o_ref)

    h = h_scr[...]
    a1 = jnp.dot(h, w1_ref[...].astype(BF16), preferred_element_type=F32)
    a3 = jnp.dot(h, w3_ref[...].astype(BF16), preferred_element_type=F32)
    o_ref[...] += _bdot((a1 * _sigmoid(a1)) * a3, w2_ref[...])

    @pl.when(k == pl.num_programs(1) - 1)
    def _():
        y = x_ref[...] + gt_ref[0] * o_ref[...]
        if final:
            ms = jnp.mean(y * y, axis=-1, keepdims=True)
            y = (y * lax.rsqrt(ms + RMS_EPS)) * fg_ref[...]
        o_ref[...] = y


def _ffn(x, gains, mods, layer, cond_rows, w1s, w3s, w2s, final_g, final, tm, tk):
    t = x.shape[0]
    return pl.pallas_call(
        functools.partial(_ffn_kernel, final),
        out_shape=jax.ShapeDtypeStruct((t, D_MODEL), F32),
        grid=(t // tm, D_FF // tk),
        in_specs=[
            pl.BlockSpec((tm, D_MODEL), lambda i, k: (i, 0)),
            pl.BlockSpec((None, 1, D_MODEL), lambda i, k: (layer, 0, 0)),
            _mod_spec(layer, 4, cond_rows, tm),
            _mod_spec(layer, 3, cond_rows, tm),
            _mod_spec(layer, 5, cond_rows, tm),
            pl.BlockSpec((None, D_MODEL, tk), lambda i, k: (layer, 0, k)),
            pl.BlockSpec((None, D_MODEL, tk), lambda i, k: (layer, 0, k)),
            pl.BlockSpec((None, tk, D_MODEL), lambda i, k: (layer, k, 0)),
            pl.BlockSpec((1, D_MODEL), lambda i, k: (0, 0)),
        ],
        out_specs=pl.BlockSpec((tm, D_MODEL), lambda i, k: (i, 0)),
        scratch_shapes=[pltpu.VMEM((tm, D_MODEL), BF16)],
        compiler_params=_params("parallel", "arbitrary"),
        name="ffn",
    )(x, gains, mods, mods, mods, w1s, w3s, w2s, final_g.reshape(1, D_MODEL))


def _rope(x, cos, sin):
    lane = _iota(x.shape, 1)
    rot = jnp.where((lane % 16) < 8, -pltpu.roll(x, LANES - 8, 1), pltpu.roll(x, 8, 1))
    return x * cos + rot * sin


def _attn_kernel(n_ctx, use_rope, emit_kv, lam_init, *refs):
    it = iter(refs)
    q_ref, k_ref, v_ref = next(it), next(it), next(it)
    kc_ref = vc_ref = cq_ref = sq_ref = ck_ref = sk_ref = ko_ref = vo_ref = None
    if n_ctx:
        kc_ref, vc_ref = next(it), next(it)
    if use_rope:
        cq_ref, sq_ref, ck_ref, sk_ref = next(it), next(it), next(it), next(it)
    lam_ref, g_ref, o_ref = next(it), next(it), next(it)
    if emit_kv:
        ko_ref, vo_ref = next(it), next(it)
    kall, vall = next(it), next(it)
    n_own = k_ref.shape[0]
    n_pairs = q_ref.shape[1] // LANES

    @pl.when(pl.program_id(2) == 0)
    def _():
        k = k_ref[...]
        v = v_ref[...]
        if emit_kv:
            for h in range(2 * n_pairs):
                for m in range(2):
                    c0 = (2 * h + m) * DIFF_QK
                    ko_ref[0, h, m] = k[:, c0:c0 + DIFF_QK]
                vo_ref[0, h] = v[:, h * DIFF_V:(h + 1) * DIFF_V]
        if use_rope:
            k = _rope(k, ck_ref[...], sk_ref[...])
        if n_ctx:
            kall[0:n_ctx, :] = kc_ref[0, 0].astype(BF16)
            vall[0:n_ctx, :] = vc_ref[0, 0].astype(BF16)
        kall[n_ctx:n_ctx + n_own, :] = k.astype(BF16)
        vall[n_ctx:n_ctx + n_own, :] = v.astype(BF16)

    lv = lam_ref[...]
    lam = (jnp.exp(jnp.sum(lv[0:1] * lv[1:2], axis=-1, keepdims=True))
           - jnp.exp(jnp.sum(lv[2:3] * lv[3:4], axis=-1, keepdims=True)) + lam_init)

    lane = _iota((q_ref.shape[0], LANES), 1)
    lo = lane < DIFF_V
    scale = DIFF_QK ** -0.5
    for hp in range(n_pairs):
        cols = slice(hp * LANES, (hp + 1) * LANES)
        q = q_ref[:, cols]
        if use_rope:
            q = _rope(q, cq_ref[...], sq_ref[...])
        ks, vs = kall[:, cols], vall[:, cols]
        outs = []
        for h in range(2):
            pv, inv = [], []
            for m in range(2):
                j = 2 * h + m
                qm = jnp.where((lane >= DIFF_QK * j) & (lane < DIFF_QK * (j + 1)), q, 0.0)
                s = _bdot_nt(qm, ks)
                e = jnp.exp2((s - jnp.max(s, axis=-1, keepdims=True)) * (scale * LOG2_E))
                inv.append(1.0 / jnp.sum(e, axis=-1, keepdims=True))
                pv.append(_bdot(e, vs))
            outs.append(pv[0] * inv[0] - pv[1] * (lam * inv[1]))
        o = jnp.where(lo, outs[0], outs[1])
        ms = _half_sum(o * o, lo) * (1.0 / DIFF_V)
        o_ref[:, cols] = (o * lax.rsqrt(ms + HEAD_RMS_EPS)) * g_ref[...] * (1.0 - lam_init)


def _attention(p, seq, tq, n_pairs, lam_init, lams, g2s, li, ctx=None, rope=None, emit_kv=False):
    assert rope is None or n_pairs == 1
    t = p.shape[0]
    nb, nq = t // seq, seq // tq
    n_ctx = 0 if ctx is None else ctx[0].shape[2]
    width = n_pairs * LANES
    k_blk, v_blk = DIFF_QW // width, 2 * DIFF_QW // width
    in_specs = [
        pl.BlockSpec((tq, width), lambda b, hp, qi: (b * nq + qi, hp)),
        pl.BlockSpec((seq, width), lambda b, hp, qi: (b, k_blk + hp)),
        pl.BlockSpec((seq, width), lambda b, hp, qi: (b, v_blk + hp)),
    ]
    args = [p, p, p]
    if ctx is not None:
        in_specs += [pl.BlockSpec((1, 1, n_ctx, width), lambda b, hp, qi: (b, hp, 0, 0))] * 2
        args += list(ctx)
    if rope is not None:
        in_specs += [pl.BlockSpec((tq, LANES), lambda b, hp, qi: (qi, 0))] * 2
        in_specs += [pl.BlockSpec((seq, width), lambda b, hp, qi: (0, 0))] * 2
        args += [rope[0], rope[1], rope[0], rope[1]]
    in_specs += [pl.BlockSpec((None, 4, DIFF_QK), lambda b, hp, qi: (li, 0, 0)),
                 pl.BlockSpec((None, 1, LANES), lambda b, hp, qi: (li, 0, 0))]
    args += [lams, g2s]
    out_shape = [jax.ShapeDtypeStruct((t, DIFF_WIDTH), F32)]
    out_specs = [pl.BlockSpec((tq, width), lambda b, hp, qi: (b * nq + qi, hp))]
    if emit_kv:
        out_shape += [jax.ShapeDtypeStruct((nb, DIFF_HEADS, 2, seq, DIFF_QK), F32),
                      jax.ShapeDtypeStruct((nb, DIFF_HEADS, seq, DIFF_V), F32)]
        out_specs += [pl.BlockSpec((1, 2 * n_pairs, 2, seq, DIFF_QK), lambda b, hp, qi: (b, hp, 0, 0, 0)),
                      pl.BlockSpec((1, 2 * n_pairs, seq, DIFF_V), lambda b, hp, qi: (b, hp, 0, 0))]
    return pl.pallas_call(
        functools.partial(_attn_kernel, n_ctx, rope is not None, emit_kv, lam_init),
        out_shape=out_shape,
        grid=(nb, DIFF_HEADS // 2 // n_pairs, nq),
        in_specs=in_specs,
        out_specs=out_specs,
        scratch_shapes=[pltpu.VMEM((n_ctx + seq, width), BF16), pltpu.VMEM((n_ctx + seq, width), BF16)],
        compiler_params=_params("parallel", "parallel", "arbitrary"),
        name="diff_attn",
    )(*args)


def _rwkv_kernel(has_s0, *refs):
    it = iter(refs)
    r_ref, k_ref, v_ref, wa_ref, gd_ref = (next(it) for _ in range(5))
    mu_r, mu_k, mu_v, mu_wa, mu_gd = (next(it) for _ in range(5))
    w0_ref, a0_ref, wa_up_ref, g_up_ref, vec_ref = (next(it) for _ in range(5))
    s0f_ref = s0b_ref = None
    if has_s0:
        s0f_ref, s0b_ref = next(it), next(it)
    o_ref, sf_ref, sb_ref = next(it), next(it), next(it)
    r_s, v_s, kk_s = next(it), next(it), next(it)
    lw_s, kd_s, b_s, y_s = next(it), next(it), next(it), next(it)
    st_s = next(it)

    seq = r_ref.shape[0]
    n_chunks = seq // CHUNK
    unroll = min(SCAN_UNROLL, n_chunks)

    def shifted(ref, mu_ref):
        x = ref[...]
        prev, nxt = _shift_rows(x)
        mu = mu_ref[...]
        return x + mu[0:1] * (prev - x) + mu[1:2] * (nxt - x)

    r = shifted(r_ref, mu_r)
    k = shifted(k_ref, mu_k)
    v = shifted(v_ref, mu_v)
    wa = shifted(wa_ref, mu_wa)
    gd = shifted(gd_ref, mu_gd)
    vec = vec_ref[...]
    k_k, k_a, r_k, ln_g, ln_b = (vec[i:i + 1] for i in range(5))

    lane = _iota((seq, LANES), 1)
    lo = lane < RWKV_HS
    gate = _bdot(_sigmoid(gd), g_up_ref[...])
    kk = k * k_k
    kk = kk * jnp.minimum(lax.rsqrt(_half_sum(kk * kk, lo)), 1e12)
    r_s[...] = r
    v_s[...] = v
    kk_s[...] = kk
    wd_in = jnp.where(lo, jnp.tanh(wa), 0.0)
    ad_in = jnp.where(lo, 0.0, wa)
    bonus = jnp.zeros((seq, LANES), F32)
    for d in range(2):
        wpre = w0_ref[d:d + 1, :] + _bdot(wd_in, wa_up_ref[d])
        sp = jnp.maximum(-wpre, 0.0) + jnp.log(1.0 + jnp.exp(-jnp.abs(wpre)))
        lw_s[d] = -jnp.exp(-sp - 0.5)
        a = _sigmoid(a0_ref[d:d + 1, :] + _bdot(ad_in, wa_up_ref[d]))
        kd = k * (1.0 + (a - 1.0) * k_a)
        kd_s[d] = kd
        b_s[d] = kk * a
        bonus = bonus + _half_sum(r * kd * r_k, lo) * v

    ci = _iota((CHUNK, CHUNK), 0)
    cj = _iota((CHUNK, CHUNK), 1)
    eye = jnp.where(ci == cj, 1.0, 0.0)
    blk8 = (ci // 8) == (cj // 8)
    merge_masks = [((ci // (2 * s)) == (cj // (2 * s))) & ((ci // s) != (cj // s)) for s in (8, 16, 32)]
    before = (cj < ci, cj > ci)
    upto = (cj <= ci, cj >= ci)
    tri = tuple(jnp.where(m, 1.0, 0.0).astype(BF16) for m in upto)
    clane = _iota((CHUNK, LANES), 1)
    head_lanes = (clane < RWKV_HS, clane >= RWKV_HS)
    si = _iota((LANES, LANES), 0)
    sj = _iota((LANES, LANES), 1)
    same_head = (si < RWKV_HS) == (sj < RWKV_HS)
    diag = si == sj

    def chunk_group(g):
        items = ([(0, g * unroll + u) for u in range(unroll)]
                 + [(1, n_chunks - 1 - (g * unroll + u)) for u in range(unroll)])
        n_it = len(items)
        dirs = [d for d, _ in items]
        rows = [pl.ds(pl.multiple_of(c * CHUNK, CHUNK), CHUNK) for _, c in items]
        r_c = [r_s[rw, :] for rw in rows]
        v_c = [v_s[rw, :] for rw in rows]
        kk_c = [kk_s[rw, :] for rw in rows]
        lw_c = [lw_s[d, rw, :] for d, rw in zip(dirs, rows)]
        kd_c = [kd_s[d, rw, :] for d, rw in zip(dirs, rows)]
        b_c = [b_s[d, rw, :] for d, rw in zip(dirs, rows)]
        cum = [_dot_exact_lhs(tri[d], lw) for d, lw in zip(dirs, lw_c)]
        tot = [cm[CHUNK - 1:CHUNK, :] if d == 0 else cm[0:1, :] for d, cm in zip(dirs, cum)]
        e_neg = [jnp.exp(-cm) for cm in cum]
        al = [jnp.exp(cm - lw) * kk for cm, lw, kk in zip(cum, lw_c, kk_c)]
        be = [b * e for b, e in zip(b_c, e_neg)]
        ka = [kd * e for kd, e in zip(kd_c, e_neg)]
        rh = [r * jnp.exp(cm) for r, cm in zip(r_c, cum)]
        wc = [jnp.exp(tt) for tt in tot]
        combos = [(i, h) for i in range(n_it) for h in range(2)]
        x_h = [jnp.concatenate([jnp.where(head_lanes[h], al[i], 0.0), jnp.where(head_lanes[h], rh[i], 0.0)], axis=0)
               for i, h in combos]
        xb = [_bdot_nt(x, be[i]) for x, (i, _) in zip(x_h, combos)]
        xk = [_bdot_nt(x, ka[i]) for x, (i, _) in zip(x_h, combos)]
        n_mat = [jnp.where(before[dirs[i]], -x[0:CHUNK], 0.0) for x, (i, _) in zip(xb, combos)]
        g_mat = [jnp.where(before[dirs[i]], x[0:CHUNK], 0.0) for x, (i, _) in zip(xk, combos)]
        m2 = [jnp.where(upto[dirs[i]], x[CHUNK:], 0.0) for x, (i, _) in zip(xb, combos)]
        m1 = [jnp.where(upto[dirs[i]], x[CHUNK:], 0.0) for x, (i, _) in zip(xk, combos)]
        nd = [jnp.where(blk8, n, 0.0) for n in n_mat]
        t = [eye + x for x in nd]
        p = [_bdot(x, x) for x in nd]
        t = [a + _bdot(a, b) for a, b in zip(t, p)]
        p = [_bdot(x, x) for x in p]
        t = [a + _bdot(a, b) for a, b in zip(t, p)]
        for msk in merge_masks:
            q = [_bdot(jnp.where(msk, n, 0.0), a) for n, a in zip(n_mat, t)]
            t = [a + _bdot(a, b) for a, b in zip(t, q)]
        gv = [_bdot(g_, v_c[i]) for g_, (i, _) in zip(g_mat, combos)]
        z = [_bdot(t_, jnp.concatenate([al[i], g_], axis=1)) for t_, g_, (i, _) in zip(t, gv, combos)]
        yv_h = [_bdot(m, v_c[i]) for m, (i, _) in zip(m1, combos)]
        pick = lambda a, b: jnp.where(head_lanes[0], a, b)
        alp = [pick(z[2 * i][:, :LANES], z[2 * i + 1][:, :LANES]) for i in range(n_it)]
        uv = [pick(z[2 * i][:, LANES:], z[2 * i + 1][:, LANES:]) for i in range(n_it)]
        yv = [pick(yv_h[2 * i], yv_h[2 * i + 1]) for i in range(n_it)]
        be_t = [(b * w).T for b, w in zip(be, wc)]
        ka_t = [(k_ * w).T for k_, w in zip(ka, wc)]
        pz = [_bdot(bt, jnp.concatenate([a, u_], axis=1)) for bt, a, u_ in zip(be_t, alp, uv)]
        kv = [_bdot(kt, v_) for kt, v_ in zip(ka_t, v_c)]
        a_mat = [jnp.where(same_head, jnp.where(diag, w, 0.0) - pp[:, :LANES], 0.0) for w, pp in zip(wc, pz)]
        b_mat = [jnp.where(same_head, k_ - pp[:, LANES:], 0.0) for k_, pp in zip(kv, pz)]
        for i, d in enumerate(dirs):
            st = st_s[d]
            xs = _bdot(jnp.concatenate([alp[i], rh[i]], axis=0), st)
            u = uv[i] + xs[0:CHUNK]
            m2u = pick(_bdot(m2[2 * i], u), _bdot(m2[2 * i + 1], u))
            y_s[d, rows[i], :] = xs[CHUNK:] + yv[i] - m2u
            st_s[d] = _bdot(a_mat[i], st) + b_mat[i]

    def body(g, carry):
        chunk_group(g)
        return carry

    if has_s0:
        st_s[0] = s0f_ref[0, 0]
        st_s[1] = s0b_ref[0, 0]
    else:
        st_s[...] = jnp.zeros_like(st_s)
    lax.fori_loop(0, n_chunks // unroll, body, 0)
    sf_ref[0, 0] = st_s[0]
    sb_ref[0, 0] = st_s[1]

    y = (y_s[0] + y_s[1]) + bonus
    mean = _half_sum(y, lo) * (1.0 / RWKV_HS)
    yc = y - mean
    var = _half_sum(yc * yc, lo) * (1.0 / RWKV_HS)
    o_ref[...] = ((yc * lax.rsqrt(var + GN_EPS)) * ln_g + ln_b) * gate


def _rwkv(p, seq, li, mu, w0, a0, wa_up, g_up, vecs, s0=None):
    t = p.shape[0]
    nb = t // seq
    n_hp = RWKV_HEADS // 2
    base = (2 * DIFF_QW + DIFF_WIDTH) // LANES
    tail = base + 3 * n_hp
    col = lambda blk: pl.BlockSpec((seq, LANES), lambda b, hp: (b, blk(hp)))
    mu_col = lambda blk: pl.BlockSpec((None, 2, LANES), lambda b, hp: (li, 0, blk(hp)))
    in_specs = [
        col(lambda hp: base + hp), col(lambda hp: base + n_hp + hp), col(lambda hp: base + 2 * n_hp + hp),
        col(lambda hp: tail), col(lambda hp: tail + 1),
        mu_col(lambda hp: hp), mu_col(lambda hp: n_hp + hp), mu_col(lambda hp: 2 * n_hp + hp),
        mu_col(lambda hp: 3 * n_hp), mu_col(lambda hp: 3 * n_hp + 1),
        pl.BlockSpec((None, 2, LANES), lambda b, hp: (li, 0, hp)),
        pl.BlockSpec((None, 2, LANES), lambda b, hp: (li, 0, hp)),
        pl.BlockSpec((None, 2, LANES, LANES), lambda b, hp: (li, 0, 0, hp)),
        pl.BlockSpec((None, LANES, LANES), lambda b, hp: (li, 0, hp)),
        pl.BlockSpec((None, 8, LANES), lambda b, hp: (li, 0, hp)),
    ]
    args = [p] * 5 + [mu] * 5 + [w0, a0, wa_up, g_up, vecs]
    st_spec = pl.BlockSpec((1, 1, LANES, LANES), lambda b, hp: (b, hp, 0, 0))
    if s0 is not None:
        in_specs += [st_spec, st_spec]
        args += list(s0)
    st_shape = jax.ShapeDtypeStruct((nb, n_hp, LANES, LANES), F32)
    return pl.pallas_call(
        functools.partial(_rwkv_kernel, s0 is not None),
        out_shape=(jax.ShapeDtypeStruct((t, RWKV_WIDTH), F32), st_shape, st_shape),
        grid=(nb, n_hp),
        in_specs=in_specs,
        out_specs=(pl.BlockSpec((seq, LANES), lambda b, hp: (b, hp)), st_spec, st_spec),
        scratch_shapes=[pltpu.VMEM((seq, LANES), F32)] * 3 + [pltpu.VMEM((2, seq, LANES), F32)] * 4
        + [pltpu.VMEM((2, LANES, LANES), F32)],
        compiler_params=_params("parallel", "parallel"),
        name="rwkv",
    )(*args)


def _pair_states_in(s):
    nb = s.shape[0]
    st = jnp.swapaxes(s.astype(F32), -1, -2).reshape(nb, RWKV_HEADS // 2, 2, RWKV_HS, RWKV_HS)
    z = jnp.zeros_like(st[:, :, 0])
    top = jnp.concatenate([st[:, :, 0], z], axis=-1)
    bot = jnp.concatenate([z, st[:, :, 1]], axis=-1)
    return jnp.concatenate([top, bot], axis=-2)


def _pair_states_out(st):
    nb = st.shape[0]
    h0 = st[:, :, :RWKV_HS, :RWKV_HS]
    h1 = st[:, :, RWKV_HS:, RWKV_HS:]
    s = jnp.stack([h0, h1], axis=2).reshape(nb, RWKV_HEADS, RWKV_HS, RWKV_HS)
    return jnp.swapaxes(s, -1, -2)


def _dft_kernel(f_ref, ft_ref):
    n = CONV_BLOCK

    def tile(freq_axis):
        shape = (2 * n, n) if freq_axis == 0 else (n, 2 * n)
        fr = _iota(shape, freq_axis)
        tau = _iota(shape, 1 - freq_axis)
        is_sin = fr >= n
        kf = jnp.where(is_sin, fr - n, fr)
        ang = ((kf * tau) & (2 * n - 1)).astype(F32) * (math.pi / n)
        val = jnp.where(is_sin, -jnp.sin(ang), jnp.cos(ang))
        nyq = jnp.where((tau & 1) == 0, 1.0, -1.0)
        return jnp.where(is_sin & (kf == 0), nyq, val)

    f_ref[...] = tile(0).astype(BF16)
    ft_ref[...] = tile(1).astype(BF16)


def _dft_mats():
    n = CONV_BLOCK
    return pl.pallas_call(
        _dft_kernel,
        out_shape=(jax.ShapeDtypeStruct((2 * n, n), BF16), jax.ShapeDtypeStruct((n, 2 * n), BF16)),
        name="dft_mats",
    )()


def _filter_kernel(seq, w0t_ref, w0c_ref, w0s_ref, b0_ref, w1_ref, b1_ref, w2_ref, b2_ref, fr_ref,
                   w3a_ref, w3b_ref, f_ref, o_ref, prev_scr):
    n_blk = seq // CONV_BLOCK
    e = pl.program_id(0)
    lag = (e - n_blk) * CONV_BLOCK + _iota((CONV_BLOCK, 1), 0)
    tap = jnp.abs(lag)
    valid = tap < seq
    tapf = tap.astype(F32)
    pos = tapf * (1.0 / (seq - 1))
    band = _iota((1, BANDS), 1).astype(F32)
    freq = 1e-4 + band * ((BANDS - 1 - 1e-4) / (BANDS - 1))
    arg = freq * (tapf * (2.0 * math.pi / seq))
    fr = fr_ref[...]
    pre = (pos * w0t_ref[...] + _dot_f32(jnp.cos(arg), w0c_ref[...]) + _dot_f32(-jnp.sin(arg), w0s_ref[...])
           + b0_ref[...])
    hdn = jnp.sin(fr[0:1] * pre)
    hdn = jnp.sin(fr[1:2] * (_dot_f32(hdn, w1_ref[...]) + b1_ref[...]))
    hdn = jnp.sin(fr[2:3] * (_dot_f32(hdn, w2_ref[...]) + b2_ref[...]))
    chan = _iota((1, D_MODEL), 1).astype(F32)
    delta = jnp.abs(MIN_DECAY + chan * ((MAX_DECAY - MIN_DECAY) / (D_MODEL - 1)))
    window = jnp.where(valid, jnp.exp(-pos * delta), 0.0)
    row = _iota((2 * CONV_BLOCK, 1), 0)
    sign = jnp.where((row & 1) == 1, -1.0, 1.0)
    for o, w3_ref in enumerate((w3a_ref, w3b_ref)):
        cols = slice(o * D_MODEL, (o + 1) * D_MODEL)
        taps = (_dot_f32(hdn, w3_ref[...]) * window).astype(BF16)
        spec = jnp.dot(f_ref[...], taps, preferred_element_type=F32)

        @pl.when(e > 0)
        def _():
            o_ref[0, :, cols] = spec + sign * prev_scr[:, cols]

        prev_scr[:, cols] = spec - jnp.where(row <= CONV_BLOCK, taps[0:1, :].astype(F32), 0.0)


def _hyena_filters(seq, li, fmat, small, w3):
    n_blk = seq // CONV_BLOCK
    layer_slice = lambda a: pl.BlockSpec((None,) + a.shape[1:], lambda e: (li, 0, 0))
    w3_spec = lambda o: pl.BlockSpec((None, FILTER_WIDTH, D_MODEL),
                                     lambda e: (li, 0, 2 * o + jnp.where(e < n_blk, 1, 0)))
    return pl.pallas_call(
        functools.partial(_filter_kernel, seq),
        out_shape=jax.ShapeDtypeStruct((2 * n_blk - 1, 2 * CONV_BLOCK, 2 * D_MODEL), F32),
        grid=(2 * n_blk,),
        in_specs=[layer_slice(a) for a in small] + [w3_spec(0), w3_spec(1),
                                                    pl.BlockSpec((2 * CONV_BLOCK, CONV_BLOCK), lambda e: (0, 0))],
        out_specs=pl.BlockSpec((1, 2 * CONV_BLOCK, 2 * D_MODEL), lambda e: (jnp.maximum(e - 1, 0), 0, 0)),
        scratch_shapes=[pltpu.VMEM((2 * CONV_BLOCK, 2 * D_MODEL), F32)],
        compiler_params=_params("arbitrary"),
        name="hyena_filters",
    )(*small, w3, w3, fmat)


def _short_conv(z, w, b):
    prev, nxt = _shift_rows(z)
    return prev * w[0:1] + z * w[1:2] + nxt * w[2:3] + b


def _conv_kernel(conv_v, n_blk, v_ref, x_ref, wv_ref, bv_ref, wx_ref, bx_ref, f_ref, ft_ref, kc_ref, bias_ref,
                 o_ref, uf_scr, yf_scr):
    cb = CONV_BLOCK
    v = v_ref[...]
    if conv_v:
        v = _short_conv(v, wv_ref[...], bv_ref[...])
    fmat = f_ref[...]
    for j in range(n_blk):
        uf_scr[j] = jnp.dot(fmat, v[j * cb:(j + 1) * cb].astype(BF16), preferred_element_type=F32)

    def mix(r0, first):
        re, im = pl.ds(r0, MIX_ROWS), pl.ds(cb + r0, MIX_ROWS)
        real_pair = (_iota((MIX_ROWS, 1), 0) == 0) if first else None
        for i in range(n_blk):
            acc_r = acc_i = None
            for j in range(n_blk):
                d = i - j + n_blk - 1
                ur, ui = uf_scr[j, re, :], uf_scr[j, im, :]
                kr, ki = kc_ref[d, re, :], kc_ref[d, im, :]
                uiki = ui * ki
                if first:
                    t_r = ur * kr - jnp.where(real_pair, 0.0, uiki)
                    t_i = jnp.where(real_pair, uiki, ur * ki + ui * kr)
                else:
                    t_r = ur * kr - uiki
                    t_i = ur * ki + ui * kr
                acc_r = t_r if acc_r is None else acc_r + t_r
                acc_i = t_i if acc_i is None else acc_i + t_i
            scale = jnp.where(real_pair, 0.5 / cb, 1.0 / cb) if first else 1.0 / cb
            yf_scr[i, re, :] = acc_r * scale
            yf_scr[i, im, :] = acc_i * scale

    mix(0, True)

    def body(r, carry):
        mix(pl.multiple_of(r * MIX_ROWS, MIX_ROWS), False)
        return carry

    lax.fori_loop(1, cb // MIX_ROWS, body, 0)

    x = _short_conv(x_ref[...], wx_ref[...], bx_ref[...])
    ftm = ft_ref[...]
    bias = bias_ref[...]
    for i in range(n_blk):
        rows = slice(i * cb, (i + 1) * cb)
        y = jnp.dot(ftm, yf_scr[i].astype(BF16), preferred_element_type=F32)
        o_ref[rows, :] = x[rows] * (y + v[rows] * bias)


def _hyena_conv(v_arr, v_blk0, x_arr, x_blk0, conv_v, seq, li, order, cw, cwb, fmat, fmat_t, kc, biases, width):
    t = v_arr.shape[0]
    nb, ncb, n_blk = t // seq, D_MODEL // width, seq // CONV_BLOCK
    wv_blk0 = v_blk0 if conv_v else 0
    kc_blk0 = order * ncb
    return pl.pallas_call(
        functools.partial(_conv_kernel, conv_v, n_blk),
        out_shape=jax.ShapeDtypeStruct((t, D_MODEL), F32),
        grid=(nb, ncb),
        in_specs=[
            pl.BlockSpec((seq, width), lambda b, c: (b, v_blk0 + c)),
            pl.BlockSpec((seq, width), lambda b, c: (b, x_blk0 + c)),
            pl.BlockSpec((None, 3, width), lambda b, c: (li, 0, wv_blk0 + c)),
            pl.BlockSpec((None, 1, width), lambda b, c: (li, 0, wv_blk0 + c)),
            pl.BlockSpec((None, 3, width), lambda b, c: (li, 0, x_blk0 + c)),
            pl.BlockSpec((None, 1, width), lambda b, c: (li, 0, x_blk0 + c)),
            pl.BlockSpec((2 * CONV_BLOCK, CONV_BLOCK), lambda b, c: (0, 0)),
            pl.BlockSpec((CONV_BLOCK, 2 * CONV_BLOCK), lambda b, c: (0, 0)),
            pl.BlockSpec((2 * n_blk - 1, 2 * CONV_BLOCK, width), lambda b, c: (0, 0, kc_blk0 + c)),
            pl.BlockSpec((None, 1, width), lambda b, c: (li * 2 + order, 0, c)),
        ],
        out_specs=pl.BlockSpec((seq, width), lambda b, c: (b, c)),
        scratch_shapes=[pltpu.VMEM((n_blk, 2 * CONV_BLOCK, width), F32),
                        pltpu.VMEM((n_blk, 2 * CONV_BLOCK, width), F32)],
        compiler_params=_params("parallel", "parallel"),
        name="hyena_conv",
    )(v_arr, x_arr, cw, cwb, cw, cwb, fmat, fmat_t, kc, biases)


def _axial_rope_tables(seq):
    n_rows = seq // GRID_W
    row = np.repeat(np.arange(n_rows, dtype=np.float64), GRID_W)
    col = np.tile(np.arange(GRID_W, dtype=np.float64), n_rows)
    inv = ROPE_BASE ** (-np.arange(ROPE_PAIRS, dtype=np.float64) / ROPE_PAIRS)
    ar = row[:, None] * inv[None]
    ac = col[:, None] * inv[None]
    ang = np.concatenate([ar, ar, ac, ac], axis=-1)
    reps = LANES // DIFF_QK
    return (jnp.asarray(np.tile(np.cos(ang), (1, reps)), F32), jnp.asarray(np.tile(np.sin(ang), (1, reps)), F32))


def _pair_lanes(a, seq_axis):
    nb = a.shape[0]
    if a.ndim == 5:
        a = a.reshape(nb, DIFF_HEADS // 2, 2, 2, a.shape[3], DIFF_QK).transpose(0, 1, 4, 2, 3, 5)
    else:
        a = a.reshape(nb, DIFF_HEADS // 2, 2, a.shape[2], DIFF_V).transpose(0, 1, 3, 2, 4)
    return a.reshape(nb, DIFF_HEADS // 2, a.shape[2], LANES)


def kernel(x_prompt, x_sample, cache_diff_k, cache_diff_v, state_rwkv_fwd, state_rwkv_bwd, c, c_ctx, ada_w, ada_b, norm1_g, norm2_g, ffn_w1, ffn_w3, ffn_w2, final_g, ab_w_in, ab_w_out, diff_lambda, diff_subln_g, rwkv_mu, rwkv_w0, rwkv_w_up, rwkv_a0, rwkv_a_up, rwkv_g_up, rwkv_k_k, rwkv_k_a, rwkv_r_k, rwkv_ln_g, rwkv_ln_b, hy_w_in, hy_b_in, hy_conv_w, hy_conv_b, hy_f_w0, hy_f_b0, hy_f_w1, hy_f_b1, hy_f_w2, hy_f_b2, hy_f_w3, hy_f_freq, hy_bias, hy_w_out, hy_b_out):
    n_ctx_seqs, ctx_len, _ = x_prompt.shape
    n_lat_seqs, lat_len, _ = x_sample.shape
    tm = 1024

    cond = jnp.zeros((N_MOD_ROWS, D_MODEL), F32).at[0].set(c_ctx).at[1:1 + n_lat_seqs].set(c)
    mods = _ada_all(cond, ada_w, ada_b).reshape(DEPTH * N_MOD_ROWS * 6, 1, D_MODEL)

    groups = [
        dict(x=x_prompt.reshape(-1, D_MODEL), seq=ctx_len, rows=(0, n_ctx_seqs * ctx_len), tq=ctx_len, pairs=4,
             conv_width=1024),
        dict(x=x_sample.reshape(-1, D_MODEL), seq=lat_len, rows=(1, lat_len), tq=512, pairs=1, conv_width=256),
    ]
    rope = _axial_rope_tables(lat_len)
    fmat, fmat_t = _dft_mats()
    n_ab, n_c = ab_w_in.shape[0], hy_w_in.shape[0]

    gains1 = norm1_g.reshape(DEPTH, 1, D_MODEL)
    gains2 = norm2_g.reshape(DEPTH, 1, D_MODEL)
    g2s = jnp.tile(diff_subln_g, (1, 2)).reshape(n_ab, 1, LANES)
    wa_up = jnp.concatenate([rwkv_w_up, rwkv_a_up], axis=2)
    vec_rows = [rwkv_k_k, rwkv_k_a, rwkv_r_k.reshape(n_ab, RWKV_WIDTH), rwkv_ln_g, rwkv_ln_b]
    vecs = jnp.stack(vec_rows + [jnp.zeros_like(rwkv_k_k)] * (8 - len(vec_rows)), axis=1)
    filt_small = [hy_f_w0[:, 0:1], hy_f_w0[:, 1:1 + BANDS], hy_f_w0[:, 1 + BANDS:], hy_f_b0[:, None],
                  hy_f_w1, hy_f_b1[:, None], hy_f_w2, hy_f_b2[:, None], hy_f_freq]
    hy_b_in_s = hy_b_in.reshape(n_c, 1, -1)
    hy_b_out_s = hy_b_out.reshape(n_c, 1, D_MODEL)
    conv_b = hy_conv_b.reshape(n_c, 1, -1)
    conv_bias = hy_bias.reshape(n_c * 2, 1, D_MODEL)
    new_k, new_v, new_sf, new_sb = [], [], [], []

    for l in range(DEPTH):
        i = l // 2
        for gi, g in enumerate(groups):
            x, seq, row = g["x"], g["seq"], g["rows"]
            if l % 2 == 0:
                p = _proj_in(x, gains1, mods, l, row, ab_w_in, i, None, tm, AB_IN // 2)
                if gi == 0:
                    att, k_own, v_own = _attention(p, seq, g["tq"], g["pairs"], _lambda_init(l), diff_lambda, g2s, i,
                                                   emit_kv=True)
                    new_k.append(k_own)
                    new_v.append(v_own)
                    mix, sf, sb = _rwkv(p, seq, i, rwkv_mu, rwkv_w0, rwkv_a0, wa_up, rwkv_g_up, vecs)
                    new_sf.append(_pair_states_out(sf))
                    new_sb.append(_pair_states_out(sb))
                else:
                    ctx = (_pair_lanes(cache_diff_k[:, i], 3), _pair_lanes(cache_diff_v[:, i], 2))
                    att, = _attention(p, seq, g["tq"], g["pairs"], _lambda_init(l), diff_lambda, g2s, i,
                                      ctx=ctx, rope=rope)
                    s0 = (_pair_states_in(state_rwkv_fwd[:, i]), _pair_states_in(state_rwkv_bwd[:, i]))
                    mix, _, _ = _rwkv(p, seq, i, rwkv_mu, rwkv_w0, rwkv_a0, wa_up, rwkv_g_up, vecs, s0=s0)
                x = _proj_out(x, mods, l, row, [att, mix], ab_w_out, i, None, tm)
            else:
                z = _proj_in(x, gains1, mods, l, row, hy_w_in, i, hy_b_in_s, tm, 1536)
                kc = _hyena_filters(seq, i, fmat, filt_small, hy_f_w3)
                width = g["conv_width"]
                ncb = D_MODEL // width
                u = _hyena_conv(z, 2 * ncb, z, 0, True, seq, i, 0, hy_conv_w, conv_b, fmat, fmat_t, kc, conv_bias, width)
                u = _hyena_conv(u, 0, z, ncb, False, seq, i, 1, hy_conv_w, conv_b, fmat, fmat_t, kc, conv_bias, width)
                x = _proj_out(x, mods, l, row, [u], hy_w_out, i, hy_b_out_s, tm)
            g["x"] = _ffn(x, gains2, mods, l, row, ffn_w1, ffn_w3, ffn_w2, final_g, l == DEPTH - 1, 2 * tm, 256)

    y_prompt = groups[0]["x"].reshape(x_prompt.shape)
    y_sample = groups[1]["x"].reshape(x_sample.shape)
    return (y_prompt, y_sample, jnp.stack(new_k, axis=1), jnp.stack(new_v, axis=1),
            jnp.stack(new_sf, axis=1), jnp.stack(new_sb, axis=1))
```

```python
import functools
import math

import jax
import jax.numpy as jnp
import numpy as np
from jax import lax
from jax.experimental import pallas as pl
from jax.experimental.pallas import tpu as pltpu

F32 = jnp.float32
BF16 = jnp.bfloat16

D_MODEL = 1024
DEPTH = 4
GRID_W = 64
DIFF_HEADS = 8
DIFF_QK = 32
DIFF_V = 64
DIFF_QW = DIFF_HEADS * 2 * DIFF_QK
DIFF_WIDTH = DIFF_HEADS * DIFF_V
ROPE_PAIRS = DIFF_QK // 4
ROPE_BASE = 10000.0
RWKV_HEADS = 8
RWKV_HS = 64
RWKV_WIDTH = RWKV_HEADS * RWKV_HS
RWKV_IN = 3 * RWKV_WIDTH + 64 + 64 + 128
AB_IN = 2 * DIFF_QW + DIFF_WIDTH + RWKV_IN
BANDS = 16
FILTER_WIDTH = 64
MAX_DECAY = math.log(1e-2) / 0.3
MIN_DECAY = math.log(1e-2) / 1.5
D_FF = 2816
RMS_EPS = 1e-6
GN_EPS = 64e-5
HEAD_RMS_EPS = 1e-5
LOG2_E = math.log2(math.e)

LANES = 128
CHUNK = 128
SCAN_UNROLL = 4
CONV_BLOCK = 256
MIX_ROWS = 8
VMEM_LIMIT = 56 * 1024 * 1024

N_MOD_ROWS = 8


def _lambda_init(l):
    return 0.8 - 0.6 * math.exp(-0.3 * l)


def _params(*sem):
    return pltpu.CompilerParams(dimension_semantics=sem, vmem_limit_bytes=VMEM_LIMIT)


def _bdot(a, b):
    return jnp.dot(a.astype(BF16), b.astype(BF16), preferred_element_type=F32)


def _bdot_nt(a, b):
    return lax.dot_general(a.astype(BF16), b.astype(BF16), (((1,), (1,)), ((), ())),
                           preferred_element_type=F32)


def _split3(x):
    h1 = x.astype(BF16)
    r1 = x - h1.astype(F32)
    h2 = r1.astype(BF16)
    h3 = (r1 - h2.astype(F32)).astype(BF16)
    return h1, h2, h3


def _dot_f32(a, b):
    a1, a2, a3 = _split3(a)
    b1, b2, b3 = _split3(b)
    d = lambda x, y: jnp.dot(x, y, preferred_element_type=F32)
    return (d(a1, b1) + (d(a1, b2) + d(a2, b1))) + ((d(a1, b3) + d(a3, b1)) + d(a2, b2))


def _dot_exact_lhs(a_bf16, b):
    b1, b2, b3 = _split3(b)
    d = lambda y: jnp.dot(a_bf16, y, preferred_element_type=F32)
    return d(b1) + (d(b2) + d(b3))


def _sigmoid(x):
    return 0.5 * (jnp.tanh(0.5 * x) + 1.0)


def _iota(shape, dim):
    return lax.broadcasted_iota(jnp.int32, shape, dim)


def _shift_rows(x):
    n = x.shape[0]
    row = _iota(x.shape, 0)
    prev = jnp.where(row == 0, 0.0, pltpu.roll(x, 1, 0))
    nxt = jnp.where(row == n - 1, 0.0, pltpu.roll(x, n - 1, 0))
    return prev, nxt


def _half_sum(x, lo_mask):
    s_lo = jnp.sum(jnp.where(lo_mask, x, 0.0), axis=-1, keepdims=True)
    s_hi = jnp.sum(jnp.where(lo_mask, 0.0, x), axis=-1, keepdims=True)
    return jnp.where(lo_mask, s_lo, s_hi)


def _ada_kernel(c_ref, w_ref, b_ref, o_ref):
    c = c_ref[...]
    o_ref[0] = _bdot(c * _sigmoid(c), w_ref[0]) + b_ref[0]


def _ada_all(cond, ada_w, ada_b):
    tn = 1536
    n_out = 6 * D_MODEL
    return pl.pallas_call(
        _ada_kernel,
        out_shape=jax.ShapeDtypeStruct((DEPTH, N_MOD_ROWS, n_out), F32),
        grid=(DEPTH, n_out // tn),
        in_specs=[
            pl.BlockSpec((N_MOD_ROWS, D_MODEL), lambda l, j: (0, 0)),
            pl.BlockSpec((1, D_MODEL, tn), lambda l, j: (l, 0, j)),
            pl.BlockSpec((1, 1, tn), lambda l, j: (l, 0, j)),
        ],
        out_specs=pl.BlockSpec((1, N_MOD_ROWS, tn), lambda l, j: (l, 0, j)),
        compiler_params=_params("parallel", "parallel"),
        name="ada",
    )(cond, ada_w, ada_b.reshape(DEPTH, 1, n_out))


def _mod_spec(layer, chunk, cond_rows, tm, axis=0):
    first, span = cond_rows

    def index_map(*idx):
        return ((layer * N_MOD_ROWS + first + (idx[axis] * tm) // span) * 6 + chunk, 0, 0)
    return pl.BlockSpec((1, 1, D_MODEL), index_map)


def _normed(x, g, sc, sh):
    ms = jnp.mean(x * x, axis=-1, keepdims=True)
    return (x * lax.rsqrt(ms + RMS_EPS)) * g * (1.0 + sc) + sh


def _proj_in_kernel(has_bias, x_ref, g_ref, sc_ref, sh_ref, w_ref, *refs):
    o_ref = refs[-1]
    h = _normed(x_ref[...], g_ref[...], sc_ref[0], sh_ref[0])
    y = _bdot(h, w_ref[...])
    o_ref[...] = y + refs[0][...] if has_bias else y


def _proj_in(x, gains, mods, layer, cond_rows, ws, wi, bs, tm, tn):
    t, n_out = x.shape[0], ws.shape[2]
    in_specs = [
        pl.BlockSpec((tm, D_MODEL), lambda j, i: (i, 0)),
        pl.BlockSpec((None, 1, D_MODEL), lambda j, i: (layer, 0, 0)),
        _mod_spec(layer, 1, cond_rows, tm, axis=1),
        _mod_spec(layer, 0, cond_rows, tm, axis=1),
        pl.BlockSpec((None, D_MODEL, tn), lambda j, i: (wi, 0, j)),
    ]
    args = [x, gains, mods, mods, ws]
    if bs is not None:
        in_specs.append(pl.BlockSpec((None, 1, tn), lambda j, i: (wi, 0, j)))
        args.append(bs)
    return pl.pallas_call(
        functools.partial(_proj_in_kernel, bs is not None),
        out_shape=jax.ShapeDtypeStruct((t, n_out), F32),
        grid=(n_out // tn, t // tm),
        in_specs=in_specs,
        out_specs=pl.BlockSpec((tm, tn), lambda j, i: (i, j)),
        compiler_params=_params("parallel", "parallel"),
        name="proj_in",
    )(*args)


def _proj_out_kernel(n_u, has_bias, x_ref, gt_ref, *refs):
    u_refs, w_ref, o_ref = refs[:n_u], refs[n_u], refs[-1]
    acc = None
    off = 0
    for u_ref in u_refs:
        k = u_ref.shape[1]
        part = _bdot(u_ref[...], w_ref[off:off + k, :])
        acc = part if acc is None else acc + part
        off += k
    if has_bias:
        acc = acc + refs[n_u + 1][...]
    o_ref[...] = x_ref[...] + gt_ref[0] * acc


def _proj_out(x, mods, layer, cond_rows, us, ws, wi, bs, tm):
    t = x.shape[0]
    in_specs = [
        pl.BlockSpec((tm, D_MODEL), lambda i: (i, 0)),
        _mod_spec(layer, 2, cond_rows, tm),
        *[pl.BlockSpec((tm, u.shape[1]), lambda i: (i, 0)) for u in us],
        pl.BlockSpec((None,) + ws.shape[1:], lambda i: (wi, 0, 0)),
    ]
    args = [x, mods, *us, ws]
    if bs is not None:
        in_specs.append(pl.BlockSpec((None, 1, D_MODEL), lambda i: (wi, 0, 0)))
        args.append(bs)
    return pl.pallas_call(
        functools.partial(_proj_out_kernel, len(us), bs is not None),
        out_shape=jax.ShapeDtypeStruct((t, D_MODEL), F32),
        grid=(t // tm,),
        in_specs=in_specs,
        out_specs=pl.BlockSpec((tm, D_MODEL), lambda i: (i, 0)),
        compiler_params=_params("parallel"),
        name="proj_out",
    )(*args)


def _ffn_kernel(final, x_ref, g_ref, sc_ref, sh_ref, gt_ref, w1_ref, w3_ref, w2_ref, fg_ref, o_ref, h_scr):
    k = pl.program_id(1)

    @pl.when(k == 0)
    def _():
        h_scr[...] = _normed(x_ref[...], g_ref[...], sc_ref[0], sh_ref[0]).astype(BF16)
        o_ref[...] = jnp.zeros_like(o_ref)

    h = h_scr[...]
    a1 = jnp.dot(h, w1_ref[...].astype(BF16), preferred_element_type=F32)
    a3 = jnp.dot(h, w3_ref[...].astype(BF16), preferred_element_type=F32)
    o_ref[...] += _bdot((a1 * _sigmoid(a1)) * a3, w2_ref[...])

    @pl.when(k == pl.num_programs(1) - 1)
    def _():
        y = x_ref[...] + gt_ref[0] * o_ref[...]
        if final:
            ms = jnp.mean(y * y, axis=-1, keepdims=True)
            y = (y * lax.rsqrt(ms + RMS_EPS)) * fg_ref[...]
        o_ref[...] = y


def _ffn(x, gains, mods, layer, cond_rows, w1s, w3s, w2s, final_g, final, tm, tk):
    t = x.shape[0]
    return pl.pallas_call(
        functools.partial(_ffn_kernel, final),
        out_shape=jax.ShapeDtypeStruct((t, D_MODEL), F32),
        grid=(t // tm, D_FF // tk),
        in_specs=[
            pl.BlockSpec((tm, D_MODEL), lambda i, k: (i, 0)),
            pl.BlockSpec((None, 1, D_MODEL), lambda i, k: (layer, 0, 0)),
            _mod_spec(layer, 4, cond_rows, tm),
            _mod_spec(layer, 3, cond_rows, tm),
            _mod_spec(layer, 5, cond_rows, tm),
            pl.BlockSpec((None, D_MODEL, tk), lambda i, k: (layer, 0, k)),
            pl.BlockSpec((None, D_MODEL, tk), lambda i, k: (layer, 0, k)),
            pl.BlockSpec((None, tk, D_MODEL), lambda i, k: (layer, k, 0)),
            pl.BlockSpec((1, D_MODEL), lambda i, k: (0, 0)),
        ],
        out_specs=pl.BlockSpec((tm, D_MODEL), lambda i, k: (i, 0)),
        scratch_shapes=[pltpu.VMEM((tm, D_MODEL), BF16)],
        compiler_params=_params("parallel", "arbitrary"),
        name="ffn",
    )(x, gains, mods, mods, mods, w1s, w3s, w2s, final_g.reshape(1, D_MODEL))


def _rope(x, cos, sin):
    lane = _iota(x.shape, 1)
    rot = jnp.where((lane % 16) < 8, -pltpu.roll(x, LANES - 8, 1), pltpu.roll(x, 8, 1))
    return x * cos + rot * sin


def _attn_kernel(n_ctx, use_rope, emit_kv, lam_init, *refs):
    it = iter(refs)
    q_ref, k_ref, v_ref = next(it), next(it), next(it)
    kc_ref = vc_ref = cq_ref = sq_ref = ck_ref = sk_ref = ko_ref = vo_ref = None
    if n_ctx:
        kc_ref, vc_ref = next(it), next(it)
    if use_rope:
        cq_ref, sq_ref, ck_ref, sk_ref = next(it), next(it), next(it), next(it)
    lam_ref, g_ref, o_ref = next(it), next(it), next(it)
    if emit_kv:
        ko_ref, vo_ref = next(it), next(it)
    kall, vall = next(it), next(it)
    n_own = k_ref.shape[0]
    n_pairs = q_ref.shape[1] // LANES

    @pl.when(pl.program_id(2) == 0)
    def _():
        k = k_ref[...]
        v = v_ref[...]
        if emit_kv:
            for h in range(2 * n_pairs):
                for m in range(2):
                    c0 = (2 * h + m) * DIFF_QK
                    ko_ref[0, h, m] = k[:, c0:c0 + DIFF_QK]
                vo_ref[0, h] = v[:, h * DIFF_V:(h + 1) * DIFF_V]
        if use_rope:
            k = _rope(k, ck_ref[...], sk_ref[...])
        if n_ctx:
            kall[0:n_ctx, :] = kc_ref[0, 0].astype(BF16)
            vall[0:n_ctx, :] = vc_ref[0, 0].astype(BF16)
        kall[n_ctx:n_ctx + n_own, :] = k.astype(BF16)
        vall[n_ctx:n_ctx + n_own, :] = v.astype(BF16)

    lv = lam_ref[...]
    lam = (jnp.exp(jnp.sum(lv[0:1] * lv[1:2], axis=-1, keepdims=True))
           - jnp.exp(jnp.sum(lv[2:3] * lv[3:4], axis=-1, keepdims=True)) + lam_init)

    lane = _iota((q_ref.shape[0], LANES), 1)
    lo = lane < DIFF_V
    scale = DIFF_QK ** -0.5
    for hp in range(n_pairs):
        cols = slice(hp * LANES, (hp + 1) * LANES)
        q = q_ref[:, cols]
        if use_rope:
            q = _rope(q, cq_ref[...], sq_ref[...])
        ks, vs = kall[:, cols], vall[:, cols]
        outs = []
        for h in range(2):
            pv, inv = [], []
            for m in range(2):
                j = 2 * h + m
                qm = jnp.where((lane >= DIFF_QK * j) & (lane < DIFF_QK * (j + 1)), q, 0.0)
                s = _bdot_nt(qm, ks)
                e = jnp.exp2((s - jnp.max(s, axis=-1, keepdims=True)) * (scale * LOG2_E))
                inv.append(1.0 / jnp.sum(e, axis=-1, keepdims=True))
                pv.append(_bdot(e, vs))
            outs.append(pv[0] * inv[0] - pv[1] * (lam * inv[1]))
        o = jnp.where(lo, outs[0], outs[1])
        ms = _half_sum(o * o, lo) * (1.0 / DIFF_V)
        o_ref[:, cols] = (o * lax.rsqrt(ms + HEAD_RMS_EPS)) * g_ref[...] * (1.0 - lam_init)


def _attention(p, seq, tq, n_pairs, lam_init, lams, g2s, li, ctx=None, rope=None, emit_kv=False):
    assert rope is None or n_pairs == 1
    t = p.shape[0]
    nb, nq = t // seq, seq // tq
    n_ctx = 0 if ctx is None else ctx[0].shape[2]
    width = n_pairs * LANES
    k_blk, v_blk = DIFF_QW // width, 2 * DIFF_QW // width
    in_specs = [
        pl.BlockSpec((tq, width), lambda b, hp, qi: (b * nq + qi, hp)),
        pl.BlockSpec((seq, width), lambda b, hp, qi: (b, k_blk + hp)),
        pl.BlockSpec((seq, width), lambda b, hp, qi: (b, v_blk + hp)),
    ]
    args = [p, p, p]
    if ctx is not None:
        in_specs += [pl.BlockSpec((1, 1, n_ctx, width), lambda b, hp, qi: (b, hp, 0, 0))] * 2
        args += list(ctx)
    if rope is not None:
        in_specs += [pl.BlockSpec((tq, LANES), lambda b, hp, qi: (qi, 0))] * 2
        in_specs += [pl.BlockSpec((seq, width), lambda b, hp, qi: (0, 0))] * 2
        args += [rope[0], rope[1], rope[0], rope[1]]
    in_specs += [pl.BlockSpec((None, 4, DIFF_QK), lambda b, hp, qi: (li, 0, 0)),
                 pl.BlockSpec((None, 1, LANES), lambda b, hp, qi: (li, 0, 0))]
    args += [lams, g2s]
    out_shape = [jax.ShapeDtypeStruct((t, DIFF_WIDTH), F32)]
    out_specs = [pl.BlockSpec((tq, width), lambda b, hp, qi: (b * nq + qi, hp))]
    if emit_kv:
        out_shape += [jax.ShapeDtypeStruct((nb, DIFF_HEADS, 2, seq, DIFF_QK), F32),
                      jax.ShapeDtypeStruct((nb, DIFF_HEADS, seq, DIFF_V), F32)]
        out_specs += [pl.BlockSpec((1, 2 * n_pairs, 2, seq, DIFF_QK), lambda b, hp, qi: (b, hp, 0, 0, 0)),
                      pl.BlockSpec((1, 2 * n_pairs, seq, DIFF_V), lambda b, hp, qi: (b, hp, 0, 0))]
    return pl.pallas_call(
        functools.partial(_attn_kernel, n_ctx, rope is not None, emit_kv, lam_init),
        out_shape=out_shape,
        grid=(nb, DIFF_HEADS // 2 // n_pairs, nq),
        in_specs=in_specs,
        out_specs=out_specs,
        scratch_shapes=[pltpu.VMEM((n_ctx + seq, width), BF16), pltpu.VMEM((n_ctx + seq, width), BF16)],
        compiler_params=_params("parallel", "parallel", "arbitrary"),
        name="diff_attn",
    )(*args)


def _rwkv_kernel(has_s0, *refs):
    it = iter(refs)
    r_ref, k_ref, v_ref, wa_ref, gd_ref = (next(it) for _ in range(5))
    mu_r, mu_k, mu_v, mu_wa, mu_gd = (next(it) for _ in range(5))
    w0_ref, a0_ref, wa_up_ref, g_up_ref, vec_ref = (next(it) for _ in range(5))
    s0f_ref = s0b_ref = None
    if has_s0:
        s0f_ref, s0b_ref = next(it), next(it)
    o_ref, sf_ref, sb_ref = next(it), next(it), next(it)
    r_s, v_s, kk_s = next(it), next(it), next(it)
    lw_s, kd_s, b_s, y_s = next(it), next(it), next(it), next(it)
    st_s = next(it)

    seq = r_ref.shape[0]
    n_chunks = seq // CHUNK
    unroll = min(SCAN_UNROLL, n_chunks)

    def shifted(ref, mu_ref):
        x = ref[...]
        prev, nxt = _shift_rows(x)
        mu = mu_ref[...]
        return x + mu[0:1] * (prev - x) + mu[1:2] * (nxt - x)

    r = shifted(r_ref, mu_r)
    k = shifted(k_ref, mu_k)
    v = shifted(v_ref, mu_v)
    wa = shifted(wa_ref, mu_wa)
    gd = shifted(gd_ref, mu_gd)
    vec = vec_ref[...]
    k_k, k_a, r_k, ln_g, ln_b = (vec[i:i + 1] for i in range(5))

    lane = _iota((seq, LANES), 1)
    lo = lane < RWKV_HS
    gate = _bdot(_sigmoid(gd), g_up_ref[...])
    kk = k * k_k
    kk = kk * jnp.minimum(lax.rsqrt(_half_sum(kk * kk, lo)), 1e12)
    r_s[...] = r
    v_s[...] = v
    kk_s[...] = kk
    wd_in = jnp.where(lo, jnp.tanh(wa), 0.0)
    ad_in = jnp.where(lo, 0.0, wa)
    bonus = jnp.zeros((seq, LANES), F32)
    for d in range(2):
        wpre = w0_ref[d:d + 1, :] + _bdot(wd_in, wa_up_ref[d])
        sp = jnp.maximum(-wpre, 0.0) + jnp.log(1.0 + jnp.exp(-jnp.abs(wpre)))
        lw_s[d] = -jnp.exp(-sp - 0.5)
        a = _sigmoid(a0_ref[d:d + 1, :] + _bdot(ad_in, wa_up_ref[d]))
        kd = k * (1.0 + (a - 1.0) * k_a)
        kd_s[d] = kd
        b_s[d] = kk * a
        bonus = bonus + _half_sum(r * kd * r_k, lo) * v

    ci = _iota((CHUNK, CHUNK), 0)
    cj = _iota((CHUNK, CHUNK), 1)
    eye = jnp.where(ci == cj, 1.0, 0.0)
    blk8 = (ci // 8) == (cj // 8)
    merge_sizes = [8 * 2 ** n for n in range(int(math.log2(CHUNK // 8)))]
    merge_masks = [((ci // (2 * s)) == (cj // (2 * s))) & ((ci // s) != (cj // s)) for s in merge_sizes]
    before = (cj < ci, cj > ci)
    upto = (cj <= ci, cj >= ci)
    tri = tuple(jnp.where(m, 1.0, 0.0).astype(BF16) for m in upto)
    clane = _iota((CHUNK, LANES), 1)
    head_lanes = (clane < RWKV_HS, clane >= RWKV_HS)
    si = _iota((LANES, LANES), 0)
    sj = _iota((LANES, LANES), 1)
    same_head = (si < RWKV_HS) == (sj < RWKV_HS)
    diag = si == sj
    row_h0 = _iota((LANES, CHUNK), 0) < RWKV_HS

    def chunk_group(g):
        items = ([(0, g * unroll + u) for u in range(unroll)]
                 + [(1, n_chunks - 1 - (g * unroll + u)) for u in range(unroll)])
        n_it = len(items)
        dirs = [d for d, _ in items]
        rows = [pl.ds(pl.multiple_of(c * CHUNK, CHUNK), CHUNK) for _, c in items]
        r_c = [r_s[rw, :] for rw in rows]
        v_c = [v_s[rw, :] for rw in rows]
        kk_c = [kk_s[rw, :] for rw in rows]
        lw_c = [lw_s[d, rw, :] for d, rw in zip(dirs, rows)]
        kd_c = [kd_s[d, rw, :] for d, rw in zip(dirs, rows)]
        b_c = [b_s[d, rw, :] for d, rw in zip(dirs, rows)]
        cum = [_dot_exact_lhs(tri[d], lw) for d, lw in zip(dirs, lw_c)]
        tot = [cm[CHUNK - 1:CHUNK, :] if d == 0 else cm[0:1, :] for d, cm in zip(dirs, cum)]
        e_neg = [jnp.exp(-cm) for cm in cum]
        al = [jnp.exp(cm - lw) * kk for cm, lw, kk in zip(cum, lw_c, kk_c)]
        be = [b * e for b, e in zip(b_c, e_neg)]
        ka = [kd * e for kd, e in zip(kd_c, e_neg)]
        rh = [r * jnp.exp(cm) for r, cm in zip(r_c, cum)]
        wc = [jnp.exp(tt) for tt in tot]
        combos = [(i, h) for i in range(n_it) for h in range(2)]
        x2 = [jnp.concatenate([a, r], axis=0) for a, r in zip(al, rh)]
        heads_of = lambda m_t: jnp.concatenate([jnp.where(row_h0, m_t, 0.0), jnp.where(row_h0, 0.0, m_t)], axis=1)
        xb2 = [_bdot(x, heads_of(b.T)) for x, b in zip(x2, be)]
        xk2 = [_bdot(x, heads_of(k_.T)) for x, k_ in zip(x2, ka)]
        xb = [xb2[i][:, h * CHUNK:(h + 1) * CHUNK] for i, h in combos]
        xk = [xk2[i][:, h * CHUNK:(h + 1) * CHUNK] for i, h in combos]
        n_mat = [jnp.where(before[dirs[i]], -x[0:CHUNK], 0.0) for x, (i, _) in zip(xb, combos)]
        g_mat = [jnp.where(before[dirs[i]], x[0:CHUNK], 0.0) for x, (i, _) in zip(xk, combos)]
        m2 = [jnp.where(upto[dirs[i]], x[CHUNK:], 0.0) for x, (i, _) in zip(xb, combos)]
        m1 = [jnp.where(upto[dirs[i]], x[CHUNK:], 0.0) for x, (i, _) in zip(xk, combos)]
        nd = [jnp.where(blk8, n, 0.0) for n in n_mat]
        t = [eye + x for x in nd]
        p = [_bdot(x, x) for x in nd]
        t = [a + _bdot(a, b) for a, b in zip(t, p)]
        p = [_bdot(x, x) for x in p]
        t = [a + _bdot(a, b) for a, b in zip(t, p)]
        for msk in merge_masks:
            q = [_bdot(jnp.where(msk, n, 0.0), a) for n, a in zip(n_mat, t)]
            t = [a + _bdot(a, b) for a, b in zip(t, q)]
        gv = [_bdot(g_, v_c[i]) for g_, (i, _) in zip(g_mat, combos)]
        z = [_bdot(t_, jnp.concatenate([al[i], g_], axis=1)) for t_, g_, (i, _) in zip(t, gv, combos)]
        yv_h = [_bdot(m, v_c[i]) for m, (i, _) in zip(m1, combos)]
        pick = lambda a, b: jnp.where(head_lanes[0], a, b)
        alp = [pick(z[2 * i][:, :LANES], z[2 * i + 1][:, :LANES]) for i in range(n_it)]
        uv = [pick(z[2 * i][:, LANES:], z[2 * i + 1][:, LANES:]) for i in range(n_it)]
        yv = [pick(yv_h[2 * i], yv_h[2 * i + 1]) for i in range(n_it)]
        be_t = [(b * w).T for b, w in zip(be, wc)]
        ka_t = [(k_ * w).T for k_, w in zip(ka, wc)]
        pz = [_bdot(bt, jnp.concatenate([a, u_], axis=1)) for bt, a, u_ in zip(be_t, alp, uv)]
        kv = [_bdot(kt, v_) for kt, v_ in zip(ka_t, v_c)]
        a_mat = [jnp.where(same_head, jnp.where(diag, w, 0.0) - pp[:, :LANES], 0.0) for w, pp in zip(wc, pz)]
        b_mat = [jnp.where(same_head, k_ - pp[:, LANES:], 0.0) for k_, pp in zip(kv, pz)]
        for i, d in enumerate(dirs):
            st = st_s[d]
            xs = _bdot(jnp.concatenate([alp[i], rh[i]], axis=0), st)
            u = uv[i] + xs[0:CHUNK]
            m2u = pick(_bdot(m2[2 * i], u), _bdot(m2[2 * i + 1], u))
            y_s[d, rows[i], :] = xs[CHUNK:] + yv[i] - m2u
            st_s[d] = _bdot(a_mat[i], st) + b_mat[i]

    def body(g, carry):
        chunk_group(g)
        return carry

    if has_s0:
        st_s[0] = s0f_ref[0, 0]
        st_s[1] = s0b_ref[0, 0]
    else:
        st_s[...] = jnp.zeros_like(st_s)
    lax.fori_loop(0, n_chunks // unroll, body, 0)
    sf_ref[0, 0] = st_s[0]
    sb_ref[0, 0] = st_s[1]

    y = (y_s[0] + y_s[1]) + bonus
    mean = _half_sum(y, lo) * (1.0 / RWKV_HS)
    yc = y - mean
    var = _half_sum(yc * yc, lo) * (1.0 / RWKV_HS)
    o_ref[...] = ((yc * lax.rsqrt(var + GN_EPS)) * ln_g + ln_b) * gate


def _rwkv(p, seq, li, mu, w0, a0, wa_up, g_up, vecs, s0=None):
    t = p.shape[0]
    nb = t // seq
    n_hp = RWKV_HEADS // 2
    base = (2 * DIFF_QW + DIFF_WIDTH) // LANES
    tail = base + 3 * n_hp
    col = lambda blk: pl.BlockSpec((seq, LANES), lambda b, hp: (b, blk(hp)))
    mu_col = lambda blk: pl.BlockSpec((None, 2, LANES), lambda b, hp: (li, 0, blk(hp)))
    in_specs = [
        col(lambda hp: base + hp), col(lambda hp: base + n_hp + hp), col(lambda hp: base + 2 * n_hp + hp),
        col(lambda hp: tail), col(lambda hp: tail + 1),
        mu_col(lambda hp: hp), mu_col(lambda hp: n_hp + hp), mu_col(lambda hp: 2 * n_hp + hp),
        mu_col(lambda hp: 3 * n_hp), mu_col(lambda hp: 3 * n_hp + 1),
        pl.BlockSpec((None, 2, LANES), lambda b, hp: (li, 0, hp)),
        pl.BlockSpec((None, 2, LANES), lambda b, hp: (li, 0, hp)),
        pl.BlockSpec((None, 2, LANES, LANES), lambda b, hp: (li, 0, 0, hp)),
        pl.BlockSpec((None, LANES, LANES), lambda b, hp: (li, 0, hp)),
        pl.BlockSpec((None, 8, LANES), lambda b, hp: (li, 0, hp)),
    ]
    args = [p] * 5 + [mu] * 5 + [w0, a0, wa_up, g_up, vecs]
    st_spec = pl.BlockSpec((1, 1, LANES, LANES), lambda b, hp: (b, hp, 0, 0))
    if s0 is not None:
        in_specs += [st_spec, st_spec]
        args += list(s0)
    st_shape = jax.ShapeDtypeStruct((nb, n_hp, LANES, LANES), F32)
    return pl.pallas_call(
        functools.partial(_rwkv_kernel, s0 is not None),
        out_shape=(jax.ShapeDtypeStruct((t, RWKV_WIDTH), F32), st_shape, st_shape),
        grid=(nb, n_hp),
        in_specs=in_specs,
        out_specs=(pl.BlockSpec((seq, LANES), lambda b, hp: (b, hp)), st_spec, st_spec),
        scratch_shapes=[pltpu.VMEM((seq, LANES), F32)] * 3 + [pltpu.VMEM((2, seq, LANES), F32)] * 4
        + [pltpu.VMEM((2, LANES, LANES), F32)],
        compiler_params=_params("parallel", "parallel"),
        name="rwkv",
    )(*args)


def _pair_states_in(s):
    nb = s.shape[0]
    st = jnp.swapaxes(s.astype(F32), -1, -2).reshape(nb, RWKV_HEADS // 2, 2, RWKV_HS, RWKV_HS)
    z = jnp.zeros_like(st[:, :, 0])
    top = jnp.concatenate([st[:, :, 0], z], axis=-1)
    bot = jnp.concatenate([z, st[:, :, 1]], axis=-1)
    return jnp.concatenate([top, bot], axis=-2)


def _pair_states_out(st):
    nb = st.shape[0]
    h0 = st[:, :, :RWKV_HS, :RWKV_HS]
    h1 = st[:, :, RWKV_HS:, RWKV_HS:]
    s = jnp.stack([h0, h1], axis=2).reshape(nb, RWKV_HEADS, RWKV_HS, RWKV_HS)
    return jnp.swapaxes(s, -1, -2)


def _dft_kernel(f_ref, ft_ref):
    n = CONV_BLOCK

    def tile(freq_axis):
        shape = (2 * n, n) if freq_axis == 0 else (n, 2 * n)
        fr = _iota(shape, freq_axis)
        tau = _iota(shape, 1 - freq_axis)
        is_sin = fr >= n
        kf = jnp.where(is_sin, fr - n, fr)
        ang = ((kf * tau) & (2 * n - 1)).astype(F32) * (math.pi / n)
        val = jnp.where(is_sin, -jnp.sin(ang), jnp.cos(ang))
        nyq = jnp.where((tau & 1) == 0, 1.0, -1.0)
        return jnp.where(is_sin & (kf == 0), nyq, val)

    f_ref[...] = tile(0).astype(BF16)
    ft_ref[...] = tile(1).astype(BF16)


def _dft_mats():
    n = CONV_BLOCK
    return pl.pallas_call(
        _dft_kernel,
        out_shape=(jax.ShapeDtypeStruct((2 * n, n), BF16), jax.ShapeDtypeStruct((n, 2 * n), BF16)),
        name="dft_mats",
    )()


def _filter_kernel(seq, w0t_ref, w0c_ref, w0s_ref, b0_ref, w1_ref, b1_ref, w2_ref, b2_ref, fr_ref,
                   w3a_ref, w3b_ref, f_ref, o_ref, prev_scr):
    n_blk = seq // CONV_BLOCK
    e = pl.program_id(0)
    lag = (e - n_blk) * CONV_BLOCK + _iota((CONV_BLOCK, 1), 0)
    tap = jnp.abs(lag)
    valid = tap < seq
    tapf = tap.astype(F32)
    pos = tapf * (1.0 / (seq - 1))
    band = _iota((1, BANDS), 1).astype(F32)
    freq = 1e-4 + band * ((BANDS - 1 - 1e-4) / (BANDS - 1))
    arg = freq * (tapf * (2.0 * math.pi / seq))
    fr = fr_ref[...]
    pre = (pos * w0t_ref[...] + _dot_f32(jnp.cos(arg), w0c_ref[...]) + _dot_f32(-jnp.sin(arg), w0s_ref[...])
           + b0_ref[...])
    hdn = jnp.sin(fr[0:1] * pre)
    hdn = jnp.sin(fr[1:2] * (_dot_f32(hdn, w1_ref[...]) + b1_ref[...]))
    hdn = jnp.sin(fr[2:3] * (_dot_f32(hdn, w2_ref[...]) + b2_ref[...]))
    chan = _iota((1, D_MODEL), 1).astype(F32)
    delta = jnp.abs(MIN_DECAY + chan * ((MAX_DECAY - MIN_DECAY) / (D_MODEL - 1)))
    window = jnp.where(valid, jnp.exp(-pos * delta), 0.0)
    row = _iota((2 * CONV_BLOCK, 1), 0)
    sign = jnp.where((row & 1) == 1, -1.0, 1.0)
    for o, w3_ref in enumerate((w3a_ref, w3b_ref)):
        cols = slice(o * D_MODEL, (o + 1) * D_MODEL)
        taps = (_dot_f32(hdn, w3_ref[...]) * window).astype(BF16)
        spec = jnp.dot(f_ref[...], taps, preferred_element_type=F32)

        @pl.when(e > 0)
        def _():
            o_ref[0, :, cols] = spec + sign * prev_scr[:, cols]

        prev_scr[:, cols] = spec - jnp.where(row <= CONV_BLOCK, taps[0:1, :].astype(F32), 0.0)


def _hyena_filters(seq, li, fmat, small, w3):
    n_blk = seq // CONV_BLOCK
    layer_slice = lambda a: pl.BlockSpec((None,) + a.shape[1:], lambda e: (li, 0, 0))
    w3_spec = lambda o: pl.BlockSpec((None, FILTER_WIDTH, D_MODEL),
                                     lambda e: (li, 0, 2 * o + jnp.where(e < n_blk, 1, 0)))
    return pl.pallas_call(
        functools.partial(_filter_kernel, seq),
        out_shape=jax.ShapeDtypeStruct((2 * n_blk - 1, 2 * CONV_BLOCK, 2 * D_MODEL), F32),
        grid=(2 * n_blk,),
        in_specs=[layer_slice(a) for a in small] + [w3_spec(0), w3_spec(1),
                                                    pl.BlockSpec((2 * CONV_BLOCK, CONV_BLOCK), lambda e: (0, 0))],
        out_specs=pl.BlockSpec((1, 2 * CONV_BLOCK, 2 * D_MODEL), lambda e: (jnp.maximum(e - 1, 0), 0, 0)),
        scratch_shapes=[pltpu.VMEM((2 * CONV_BLOCK, 2 * D_MODEL), F32)],
        compiler_params=_params("arbitrary"),
        name="hyena_filters",
    )(*small, w3, w3, fmat)


def _short_conv(z, w, b):
    prev, nxt = _shift_rows(z)
    return prev * w[0:1] + z * w[1:2] + nxt * w[2:3] + b


def _conv_kernel(conv_v, n_blk, v_ref, x_ref, wv_ref, bv_ref, wx_ref, bx_ref, f_ref, ft_ref, kc_ref, bias_ref,
                 o_ref, uf_scr, yf_scr):
    cb = CONV_BLOCK
    v = v_ref[...]
    if conv_v:
        v = _short_conv(v, wv_ref[...], bv_ref[...])
    fmat = f_ref[...]
    for j in range(n_blk):
        uf_scr[j] = jnp.dot(fmat, v[j * cb:(j + 1) * cb].astype(BF16), preferred_element_type=F32)

    def mix(r0, first):
        re, im = pl.ds(r0, MIX_ROWS), pl.ds(cb + r0, MIX_ROWS)
        real_pair = (_iota((MIX_ROWS, 1), 0) == 0) if first else None
        for i in range(n_blk):
            acc_r = acc_i = None
            for j in range(n_blk):
                d = i - j + n_blk - 1
                ur, ui = uf_scr[j, re, :], uf_scr[j, im, :]
                kr, ki = kc_ref[d, re, :], kc_ref[d, im, :]
                uiki = ui * ki
                if first:
                    t_r = ur * kr - jnp.where(real_pair, 0.0, uiki)
                    t_i = jnp.where(real_pair, uiki, ur * ki + ui * kr)
                else:
                    t_r = ur * kr - uiki
                    t_i = ur * ki + ui * kr
                acc_r = t_r if acc_r is None else acc_r + t_r
                acc_i = t_i if acc_i is None else acc_i + t_i
            scale = jnp.where(real_pair, 0.5 / cb, 1.0 / cb) if first else 1.0 / cb
            yf_scr[i, re, :] = acc_r * scale
            yf_scr[i, im, :] = acc_i * scale

    mix(0, True)

    def body(r, carry):
        mix(pl.multiple_of(r * MIX_ROWS, MIX_ROWS), False)
        return carry

    lax.fori_loop(1, cb // MIX_ROWS, body, 0)

    x = _short_conv(x_ref[...], wx_ref[...], bx_ref[...])
    ftm = ft_ref[...]
    bias = bias_ref[...]
    for i in range(n_blk):
        rows = slice(i * cb, (i + 1) * cb)
        y = jnp.dot(ftm, yf_scr[i].astype(BF16), preferred_element_type=F32)
        o_ref[rows, :] = x[rows] * (y + v[rows] * bias)


def _hyena_conv(v_arr, v_blk0, x_arr, x_blk0, conv_v, seq, li, order, cw, cwb, fmat, fmat_t, kc, biases, width):
    t = v_arr.shape[0]
    nb, ncb, n_blk = t // seq, D_MODEL // width, seq // CONV_BLOCK
    wv_blk0 = v_blk0 if conv_v else 0
    kc_blk0 = order * ncb
    return pl.pallas_call(
        functools.partial(_conv_kernel, conv_v, n_blk),
        out_shape=jax.ShapeDtypeStruct((t, D_MODEL), F32),
        grid=(nb, ncb),
        in_specs=[
            pl.BlockSpec((seq, width), lambda b, c: (b, v_blk0 + c)),
            pl.BlockSpec((seq, width), lambda b, c: (b, x_blk0 + c)),
            pl.BlockSpec((None, 3, width), lambda b, c: (li, 0, wv_blk0 + c)),
            pl.BlockSpec((None, 1, width), lambda b, c: (li, 0, wv_blk0 + c)),
            pl.BlockSpec((None, 3, width), lambda b, c: (li, 0, x_blk0 + c)),
            pl.BlockSpec((None, 1, width), lambda b, c: (li, 0, x_blk0 + c)),
            pl.BlockSpec((2 * CONV_BLOCK, CONV_BLOCK), lambda b, c: (0, 0)),
            pl.BlockSpec((CONV_BLOCK, 2 * CONV_BLOCK), lambda b, c: (0, 0)),
            pl.BlockSpec((2 * n_blk - 1, 2 * CONV_BLOCK, width), lambda b, c: (0, 0, kc_blk0 + c)),
            pl.BlockSpec((None, 1, width), lambda b, c: (li * 2 + order, 0, c)),
        ],
        out_specs=pl.BlockSpec((seq, width), lambda b, c: (b, c)),
        scratch_shapes=[pltpu.VMEM((n_blk, 2 * CONV_BLOCK, width), F32),
                        pltpu.VMEM((n_blk, 2 * CONV_BLOCK, width), F32)],
        compiler_params=_params("parallel", "parallel"),
        name="hyena_conv",
    )(v_arr, x_arr, cw, cwb, cw, cwb, fmat, fmat_t, kc, biases)


def _axial_rope_tables(seq):
    n_rows = seq // GRID_W
    row = np.repeat(np.arange(n_rows, dtype=np.float64), GRID_W)
    col = np.tile(np.arange(GRID_W, dtype=np.float64), n_rows)
    inv = ROPE_BASE ** (-np.arange(ROPE_PAIRS, dtype=np.float64) / ROPE_PAIRS)
    ar = row[:, None] * inv[None]
    ac = col[:, None] * inv[None]
    ang = np.concatenate([ar, ar, ac, ac], axis=-1)
    reps = LANES // DIFF_QK
    return (jnp.asarray(np.tile(np.cos(ang), (1, reps)), F32), jnp.asarray(np.tile(np.sin(ang), (1, reps)), F32))


def _pair_lanes(a, seq_axis):
    nb = a.shape[0]
    if a.ndim == 5:
        a = a.reshape(nb, DIFF_HEADS // 2, 2, 2, a.shape[3], DIFF_QK).transpose(0, 1, 4, 2, 3, 5)
    else:
        a = a.reshape(nb, DIFF_HEADS // 2, 2, a.shape[2], DIFF_V).transpose(0, 1, 3, 2, 4)
    return a.reshape(nb, DIFF_HEADS // 2, a.shape[2], LANES)


def kernel(x_prompt, x_sample, cache_diff_k, cache_diff_v, state_rwkv_fwd, state_rwkv_bwd, c, c_ctx, ada_w, ada_b, norm1_g, norm2_g, ffn_w1, ffn_w3, ffn_w2, final_g, ab_w_in, ab_w_out, diff_lambda, diff_subln_g, rwkv_mu, rwkv_w0, rwkv_w_up, rwkv_a0, rwkv_a_up, rwkv_g_up, rwkv_k_k, rwkv_k_a, rwkv_r_k, rwkv_ln_g, rwkv_ln_b, hy_w_in, hy_b_in, hy_conv_w, hy_conv_b, hy_f_w0, hy_f_b0, hy_f_w1, hy_f_b1, hy_f_w2, hy_f_b2, hy_f_w3, hy_f_freq, hy_bias, hy_w_out, hy_b_out):
    n_ctx_seqs, ctx_len, _ = x_prompt.shape
    n_lat_seqs, lat_len, _ = x_sample.shape
    tm = 1024

    cond = jnp.zeros((N_MOD_ROWS, D_MODEL), F32).at[0].set(c_ctx).at[1:1 + n_lat_seqs].set(c)
    mods = _ada_all(cond, ada_w, ada_b).reshape(DEPTH * N_MOD_ROWS * 6, 1, D_MODEL)

    groups = [
        dict(x=x_prompt.reshape(-1, D_MODEL), seq=ctx_len, rows=(0, n_ctx_seqs * ctx_len), tq=ctx_len, pairs=4,
             conv_width=1024),
        dict(x=x_sample.reshape(-1, D_MODEL), seq=lat_len, rows=(1, lat_len), tq=512, pairs=1, conv_width=256),
    ]
    rope = _axial_rope_tables(lat_len)
    fmat, fmat_t = _dft_mats()
    n_ab, n_c = ab_w_in.shape[0], hy_w_in.shape[0]

    gains1 = norm1_g.reshape(DEPTH, 1, D_MODEL)
    gains2 = norm2_g.reshape(DEPTH, 1, D_MODEL)
    g2s = jnp.tile(diff_subln_g, (1, 2)).reshape(n_ab, 1, LANES)
    wa_up = jnp.concatenate([rwkv_w_up, rwkv_a_up], axis=2)
    vec_rows = [rwkv_k_k, rwkv_k_a, rwkv_r_k.reshape(n_ab, RWKV_WIDTH), rwkv_ln_g, rwkv_ln_b]
    vecs = jnp.stack(vec_rows + [jnp.zeros_like(rwkv_k_k)] * (8 - len(vec_rows)), axis=1)
    filt_small = [hy_f_w0[:, 0:1], hy_f_w0[:, 1:1 + BANDS], hy_f_w0[:, 1 + BANDS:], hy_f_b0[:, None],
                  hy_f_w1, hy_f_b1[:, None], hy_f_w2, hy_f_b2[:, None], hy_f_freq]
    hy_b_in_s = hy_b_in.reshape(n_c, 1, -1)
    hy_b_out_s = hy_b_out.reshape(n_c, 1, D_MODEL)
    conv_b = hy_conv_b.reshape(n_c, 1, -1)
    conv_bias = hy_bias.reshape(n_c * 2, 1, D_MODEL)
    new_k, new_v, new_sf, new_sb = [], [], [], []

    for l in range(DEPTH):
        i = l // 2
        for gi, g in enumerate(groups):
            x, seq, row = g["x"], g["seq"], g["rows"]
            if l % 2 == 0:
                p = _proj_in(x, gains1, mods, l, row, ab_w_in, i, None, tm, AB_IN // 2)
                if gi == 0:
                    att, k_own, v_own = _attention(p, seq, g["tq"], g["pairs"], _lambda_init(l), diff_lambda, g2s, i,
                                                   emit_kv=True)
                    new_k.append(k_own)
                    new_v.append(v_own)
                    mix, sf, sb = _rwkv(p, seq, i, rwkv_mu, rwkv_w0, rwkv_a0, wa_up, rwkv_g_up, vecs)
                    new_sf.append(_pair_states_out(sf))
                    new_sb.append(_pair_states_out(sb))
                else:
                    ctx = (_pair_lanes(cache_diff_k[:, i], 3), _pair_lanes(cache_diff_v[:, i], 2))
                    att, = _attention(p, seq, g["tq"], g["pairs"], _lambda_init(l), diff_lambda, g2s, i,
                                      ctx=ctx, rope=rope)
                    s0 = (_pair_states_in(state_rwkv_fwd[:, i]), _pair_states_in(state_rwkv_bwd[:, i]))
                    mix, _, _ = _rwkv(p, seq, i, rwkv_mu, rwkv_w0, rwkv_a0, wa_up, rwkv_g_up, vecs, s0=s0)
                x = _proj_out(x, mods, l, row, [att, mix], ab_w_out, i, None, tm)
            else:
                z = _proj_in(x, gains1, mods, l, row, hy_w_in, i, hy_b_in_s, tm, 1536)
                kc = _hyena_filters(seq, i, fmat, filt_small, hy_f_w3)
                width = g["conv_width"]
                ncb = D_MODEL // width
                u = _hyena_conv(z, 2 * ncb, z, 0, True, seq, i, 0, hy_conv_w, conv_b, fmat, fmat_t, kc, conv_bias, width)
                u = _hyena_conv(u, 0, z, ncb, False, seq, i, 1, hy_conv_w, conv_b, fmat, fmat_t, kc, conv_bias, width)
                x = _proj_out(x, mods, l, row, [u], hy_w_out, i, hy_b_out_s, tm)
            g["x"] = _ffn(x, gains2, mods, l, row, ffn_w1, ffn_w3, ffn_w2, final_g, l == DEPTH - 1, 2 * tm, 256)

    y_prompt = groups[0]["x"].reshape(x_prompt.shape)
    y_sample = groups[1]["x"].reshape(x_sample.shape)
    return (y_prompt, y_sample, jnp.stack(new_k, axis=1), jnp.stack(new_v, axis=1),
            jnp.stack(new_sf, axis=1), jnp.stack(new_sb, axis=1))
```

```python
import functools
import math

import jax
import jax.numpy as jnp
import numpy as np
from jax import lax
from jax.experimental import pallas as pl
from jax.experimental.pallas import tpu as pltpu

F32 = jnp.float32
BF16 = jnp.bfloat16

D_MODEL = 1024
DEPTH = 4
GRID_W = 64
DIFF_HEADS = 8
DIFF_QK = 32
DIFF_V = 64
DIFF_QW = DIFF_HEADS * 2 * DIFF_QK
DIFF_WIDTH = DIFF_HEADS * DIFF_V
ROPE_PAIRS = DIFF_QK // 4
ROPE_BASE = 10000.0
RWKV_HEADS = 8
RWKV_HS = 64
RWKV_WIDTH = RWKV_HEADS * RWKV_HS
RWKV_IN = 3 * RWKV_WIDTH + 64 + 64 + 128
AB_IN = 2 * DIFF_QW + DIFF_WIDTH + RWKV_IN
BANDS = 16
FILTER_WIDTH = 64
MAX_DECAY = math.log(1e-2) / 0.3
MIN_DECAY = math.log(1e-2) / 1.5
D_FF = 2816
RMS_EPS = 1e-6
GN_EPS = 64e-5
HEAD_RMS_EPS = 1e-5
LOG2_E = math.log2(math.e)

LANES = 128
CHUNK = 128
SCAN_UNROLL = 4
CONV_BLOCK_MAX = 512
MIX_ROWS = 8
VMEM_LIMIT = 56 * 1024 * 1024

N_MOD_ROWS = 8


def _lambda_init(l):
    return 0.8 - 0.6 * math.exp(-0.3 * l)


def _params(*sem):
    return pltpu.CompilerParams(dimension_semantics=sem, vmem_limit_bytes=VMEM_LIMIT)


def _bdot(a, b):
    return jnp.dot(a.astype(BF16), b.astype(BF16), preferred_element_type=F32)


def _bdot_nt(a, b):
    return lax.dot_general(a.astype(BF16), b.astype(BF16), (((1,), (1,)), ((), ())),
                           preferred_element_type=F32)


def _split3(x):
    h1 = x.astype(BF16)
    r1 = x - h1.astype(F32)
    h2 = r1.astype(BF16)
    h3 = (r1 - h2.astype(F32)).astype(BF16)
    return h1, h2, h3


def _dot_f32(a, b):
    a1, a2, a3 = _split3(a)
    b1, b2, b3 = _split3(b)
    d = lambda x, y: jnp.dot(x, y, preferred_element_type=F32)
    return (d(a1, b1) + (d(a1, b2) + d(a2, b1))) + ((d(a1, b3) + d(a3, b1)) + d(a2, b2))


def _dot_exact_lhs(a_bf16, b):
    b1, b2, b3 = _split3(b)
    d = lambda y: jnp.dot(a_bf16, y, preferred_element_type=F32)
    return d(b1) + (d(b2) + d(b3))


def _sigmoid(x):
    return 0.5 * (jnp.tanh(0.5 * x) + 1.0)


def _iota(shape, dim):
    return lax.broadcasted_iota(jnp.int32, shape, dim)


def _shift_rows(x):
    n = x.shape[0]
    row = _iota(x.shape, 0)
    prev = jnp.where(row == 0, 0.0, pltpu.roll(x, 1, 0))
    nxt = jnp.where(row == n - 1, 0.0, pltpu.roll(x, n - 1, 0))
    return prev, nxt


def _half_sum(x, lo_mask):
    s_lo = jnp.sum(jnp.where(lo_mask, x, 0.0), axis=-1, keepdims=True)
    s_hi = jnp.sum(jnp.where(lo_mask, 0.0, x), axis=-1, keepdims=True)
    return jnp.where(lo_mask, s_lo, s_hi)


def _ada_kernel(c_ref, w_ref, b_ref, o_ref):
    c = c_ref[...]
    o_ref[0] = _bdot(c * _sigmoid(c), w_ref[0]) + b_ref[0]


def _ada_all(cond, ada_w, ada_b):
    tn = 1536
    n_out = 6 * D_MODEL
    return pl.pallas_call(
        _ada_kernel,
        out_shape=jax.ShapeDtypeStruct((DEPTH, N_MOD_ROWS, n_out), F32),
        grid=(DEPTH, n_out // tn),
        in_specs=[
            pl.BlockSpec((N_MOD_ROWS, D_MODEL), lambda l, j: (0, 0)),
            pl.BlockSpec((1, D_MODEL, tn), lambda l, j: (l, 0, j)),
            pl.BlockSpec((1, 1, tn), lambda l, j: (l, 0, j)),
        ],
        out_specs=pl.BlockSpec((1, N_MOD_ROWS, tn), lambda l, j: (l, 0, j)),
        compiler_params=_params("parallel", "parallel"),
        name="ada",
    )(cond, ada_w, ada_b.reshape(DEPTH, 1, n_out))


def _mod_spec(layer, chunk, cond_rows, tm, axis=0):
    first, span = cond_rows

    def index_map(*idx):
        return ((layer * N_MOD_ROWS + first + (idx[axis] * tm) // span) * 6 + chunk, 0, 0)
    return pl.BlockSpec((1, 1, D_MODEL), index_map)


def _normed(x, g, sc, sh):
    ms = jnp.mean(x * x, axis=-1, keepdims=True)
    return (x * lax.rsqrt(ms + RMS_EPS)) * g * (1.0 + sc) + sh


def _proj_in_kernel(has_bias, x_ref, g_ref, sc_ref, sh_ref, w_ref, *refs):
    o_ref = refs[-1]
    h = _normed(x_ref[...], g_ref[...], sc_ref[0], sh_ref[0])
    y = _bdot(h, w_ref[...])
    o_ref[...] = y + refs[0][...] if has_bias else y


def _proj_in(x, gains, mods, layer, cond_rows, ws, wi, bs, tm, tn):
    t, n_out = x.shape[0], ws.shape[2]
    in_specs = [
        pl.BlockSpec((tm, D_MODEL), lambda j, i: (i, 0)),
        pl.BlockSpec((None, 1, D_MODEL), lambda j, i: (layer, 0, 0)),
        _mod_spec(layer, 1, cond_rows, tm, axis=1),
        _mod_spec(layer, 0, cond_rows, tm, axis=1),
        pl.BlockSpec((None, D_MODEL, tn), lambda j, i: (wi, 0, j)),
    ]
    args = [x, gains, mods, mods, ws]
    if bs is not None:
        in_specs.append(pl.BlockSpec((None, 1, tn), lambda j, i: (wi, 0, j)))
        args.append(bs)
    return pl.pallas_call(
        functools.partial(_proj_in_kernel, bs is not None),
        out_shape=jax.ShapeDtypeStruct((t, n_out), F32),
        grid=(n_out // tn, t // tm),
        in_specs=in_specs,
        out_specs=pl.BlockSpec((tm, tn), lambda j, i: (i, j)),
        compiler_params=_params("parallel", "parallel"),
        name="proj_in",
    )(*args)


def _proj_out_kernel(n_u, has_bias, x_ref, gt_ref, *refs):
    u_refs, w_ref, o_ref = refs[:n_u], refs[n_u], refs[-1]
    acc = None
    off = 0
    for u_ref in u_refs:
        k = u_ref.shape[1]
        part = _bdot(u_ref[...], w_ref[off:off + k, :])
        acc = part if acc is None else acc + part
        off += k
    if has_bias:
        acc = acc + refs[n_u + 1][...]
    o_ref[...] = x_ref[...] + gt_ref[0] * acc


def _proj_out(x, mods, layer, cond_rows, us, ws, wi, bs, tm):
    t = x.shape[0]
    in_specs = [
        pl.BlockSpec((tm, D_MODEL), lambda i: (i, 0)),
        _mod_spec(layer, 2, cond_rows, tm),
        *[pl.BlockSpec((tm, u.shape[1]), lambda i: (i, 0)) for u in us],
        pl.BlockSpec((None,) + ws.shape[1:], lambda i: (wi, 0, 0)),
    ]
    args = [x, mods, *us, ws]
    if bs is not None:
        in_specs.append(pl.BlockSpec((None, 1, D_MODEL), lambda i: (wi, 0, 0)))
        args.append(bs)
    return pl.pallas_call(
        functools.partial(_proj_out_kernel, len(us), bs is not None),
        out_shape=jax.ShapeDtypeStruct((t, D_MODEL), F32),
        grid=(t // tm,),
        in_specs=in_specs,
        out_specs=pl.BlockSpec((tm, D_MODEL), lambda i: (i, 0)),
        compiler_params=_params("parallel"),
        name="proj_out",
    )(*args)


def _ffn_kernel(final, x_ref, g_ref, sc_ref, sh_ref, gt_ref, w1_ref, w3_ref, w2_ref, fg_ref, o_ref, h_scr):
    k = pl.program_id(1)

    @pl.when(k == 0)
    def _():
        h_scr[...] = _normed(x_ref[...], g_ref[...], sc_ref[0], sh_ref[0]).astype(BF16)
        o_ref[...] = jnp.zeros_like(o_ref)

    h = h_scr[...]
    a1 = jnp.dot(h, w1_ref[...].astype(BF16), preferred_element_type=F32)
    a3 = jnp.dot(h, w3_ref[...].astype(BF16), preferred_element_type=F32)
    o_ref[...] += _bdot((a1 * _sigmoid(a1)) * a3, w2_ref[...])

    @pl.when(k == pl.num_programs(1) - 1)
    def _():
        y = x_ref[...] + gt_ref[0] * o_ref[...]
        if final:
            ms = jnp.mean(y * y, axis=-1, keepdims=True)
            y = (y * lax.rsqrt(ms + RMS_EPS)) * fg_ref[...]
        o_ref[...] = y


def _ffn(x, gains, mods, layer, cond_rows, w1s, w3s, w2s, final_g, final, tm, tk):
    t = x.shape[0]
    return pl.pallas_call(
        functools.partial(_ffn_kernel, final),
        out_shape=jax.ShapeDtypeStruct((t, D_MODEL), F32),
        grid=(t // tm, D_FF // tk),
        in_specs=[
            pl.BlockSpec((tm, D_MODEL), lambda i, k: (i, 0)),
            pl.BlockSpec((None, 1, D_MODEL), lambda i, k: (layer, 0, 0)),
            _mod_spec(layer, 4, cond_rows, tm),
            _mod_spec(layer, 3, cond_rows, tm),
            _mod_spec(layer, 5, cond_rows, tm),
            pl.BlockSpec((None, D_MODEL, tk), lambda i, k: (layer, 0, k)),
            pl.BlockSpec((None, D_MODEL, tk), lambda i, k: (layer, 0, k)),
            pl.BlockSpec((None, tk, D_MODEL), lambda i, k: (layer, k, 0)),
            pl.BlockSpec((1, D_MODEL), lambda i, k: (0, 0)),
        ],
        out_specs=pl.BlockSpec((tm, D_MODEL), lambda i, k: (i, 0)),
        scratch_shapes=[pltpu.VMEM((tm, D_MODEL), BF16)],
        compiler_params=_params("parallel", "arbitrary"),
        name="ffn",
    )(x, gains, mods, mods, mods, w1s, w3s, w2s, final_g.reshape(1, D_MODEL))


def _rope(x, cos, sin):
    lane = _iota(x.shape, 1)
    rot = jnp.where((lane % 16) < 8, -pltpu.roll(x, LANES - 8, 1), pltpu.roll(x, 8, 1))
    return x * cos + rot * sin


def _attn_kernel(n_ctx, use_rope, emit_kv, lam_init, *refs):
    it = iter(refs)
    q_ref, k_ref, v_ref = next(it), next(it), next(it)
    kc_ref = vc_ref = cq_ref = sq_ref = ck_ref = sk_ref = ko_ref = vo_ref = None
    if n_ctx:
        kc_ref, vc_ref = next(it), next(it)
    if use_rope:
        cq_ref, sq_ref, ck_ref, sk_ref = next(it), next(it), next(it), next(it)
    lam_ref, g_ref, o_ref = next(it), next(it), next(it)
    if emit_kv:
        ko_ref, vo_ref = next(it), next(it)
    kall, vall = next(it), next(it)
    n_own = k_ref.shape[0]
    n_pairs = q_ref.shape[1] // LANES

    @pl.when(pl.program_id(2) == 0)
    def _():
        k = k_ref[...]
        v = v_ref[...]
        if emit_kv:
            for h in range(2 * n_pairs):
                for m in range(2):
                    c0 = (2 * h + m) * DIFF_QK
                    ko_ref[0, h, m] = k[:, c0:c0 + DIFF_QK]
                vo_ref[0, h] = v[:, h * DIFF_V:(h + 1) * DIFF_V]
        if use_rope:
            k = _rope(k, ck_ref[...], sk_ref[...])
        if n_ctx:
            kall[0:n_ctx, :] = kc_ref[0, 0].astype(BF16)
            vall[0:n_ctx, :] = vc_ref[0, 0].astype(BF16)
        kall[n_ctx:n_ctx + n_own, :] = k.astype(BF16)
        vall[n_ctx:n_ctx + n_own, :] = v.astype(BF16)

    lv = lam_ref[...]
    lam = (jnp.exp(jnp.sum(lv[0:1] * lv[1:2], axis=-1, keepdims=True))
           - jnp.exp(jnp.sum(lv[2:3] * lv[3:4], axis=-1, keepdims=True)) + lam_init)

    lane = _iota((q_ref.shape[0], LANES), 1)
    lo = lane < DIFF_V
    scale = DIFF_QK ** -0.5
    for hp in range(n_pairs):
        cols = slice(hp * LANES, (hp + 1) * LANES)
        q = q_ref[:, cols]
        if use_rope:
            q = _rope(q, cq_ref[...], sq_ref[...])
        ks, vs = kall[:, cols], vall[:, cols]
        outs = []
        for h in range(2):
            pv, inv = [], []
            for m in range(2):
                j = 2 * h + m
                qm = jnp.where((lane >= DIFF_QK * j) & (lane < DIFF_QK * (j + 1)), q, 0.0)
                s = _bdot_nt(qm, ks)
                e = jnp.exp2((s - jnp.max(s, axis=-1, keepdims=True)) * (scale * LOG2_E))
                inv.append(1.0 / jnp.sum(e, axis=-1, keepdims=True))
                pv.append(_bdot(e, vs))
            outs.append(pv[0] * inv[0] - pv[1] * (lam * inv[1]))
        o = jnp.where(lo, outs[0], outs[1])
        ms = _half_sum(o * o, lo) * (1.0 / DIFF_V)
        o_ref[:, cols] = (o * lax.rsqrt(ms + HEAD_RMS_EPS)) * g_ref[...] * (1.0 - lam_init)


def _attention(p, seq, tq, n_pairs, lam_init, lams, g2s, li, ctx=None, rope=None, emit_kv=False):
    assert rope is None or n_pairs == 1
    t = p.shape[0]
    nb, nq = t // seq, seq // tq
    n_ctx = 0 if ctx is None else ctx[0].shape[2]
    width = n_pairs * LANES
    k_blk, v_blk = DIFF_QW // width, 2 * DIFF_QW // width
    in_specs = [
        pl.BlockSpec((tq, width), lambda b, hp, qi: (b * nq + qi, hp)),
        pl.BlockSpec((seq, width), lambda b, hp, qi: (b, k_blk + hp)),
        pl.BlockSpec((seq, width), lambda b, hp, qi: (b, v_blk + hp)),
    ]
    args = [p, p, p]
    if ctx is not None:
        in_specs += [pl.BlockSpec((1, 1, n_ctx, width), lambda b, hp, qi: (b, hp, 0, 0))] * 2
        args += list(ctx)
    if rope is not None:
        in_specs += [pl.BlockSpec((tq, LANES), lambda b, hp, qi: (qi, 0))] * 2
        in_specs += [pl.BlockSpec((seq, width), lambda b, hp, qi: (0, 0))] * 2
        args += [rope[0], rope[1], rope[0], rope[1]]
    in_specs += [pl.BlockSpec((None, 4, DIFF_QK), lambda b, hp, qi: (li, 0, 0)),
                 pl.BlockSpec((None, 1, LANES), lambda b, hp, qi: (li, 0, 0))]
    args += [lams, g2s]
    out_shape = [jax.ShapeDtypeStruct((t, DIFF_WIDTH), F32)]
    out_specs = [pl.BlockSpec((tq, width), lambda b, hp, qi: (b * nq + qi, hp))]
    if emit_kv:
        out_shape += [jax.ShapeDtypeStruct((nb, DIFF_HEADS, 2, seq, DIFF_QK), F32),
                      jax.ShapeDtypeStruct((nb, DIFF_HEADS, seq, DIFF_V), F32)]
        out_specs += [pl.BlockSpec((1, 2 * n_pairs, 2, seq, DIFF_QK), lambda b, hp, qi: (b, hp, 0, 0, 0)),
                      pl.BlockSpec((1, 2 * n_pairs, seq, DIFF_V), lambda b, hp, qi: (b, hp, 0, 0))]
    return pl.pallas_call(
        functools.partial(_attn_kernel, n_ctx, rope is not None, emit_kv, lam_init),
        out_shape=out_shape,
        grid=(nb, DIFF_HEADS // 2 // n_pairs, nq),
        in_specs=in_specs,
        out_specs=out_specs,
        scratch_shapes=[pltpu.VMEM((n_ctx + seq, width), BF16), pltpu.VMEM((n_ctx + seq, width), BF16)],
        compiler_params=_params("parallel", "parallel", "arbitrary"),
        name="diff_attn",
    )(*args)


def _rwkv_kernel(has_s0, *refs):
    it = iter(refs)
    r_ref, k_ref, v_ref, wa_ref, gd_ref = (next(it) for _ in range(5))
    mu_r, mu_k, mu_v, mu_wa, mu_gd = (next(it) for _ in range(5))
    w0_ref, a0_ref, wa_up_ref, g_up_ref, vec_ref = (next(it) for _ in range(5))
    s0f_ref = s0b_ref = None
    if has_s0:
        s0f_ref, s0b_ref = next(it), next(it)
    o_ref, sf_ref, sb_ref = next(it), next(it), next(it)
    r_s, v_s, kk_s = next(it), next(it), next(it)
    lw_s, kd_s, b_s, y_s = next(it), next(it), next(it), next(it)
    st_s = next(it)

    seq = r_ref.shape[0]
    n_chunks = seq // CHUNK
    unroll = min(SCAN_UNROLL, n_chunks)

    def shifted(ref, mu_ref):
        x = ref[...]
        prev, nxt = _shift_rows(x)
        mu = mu_ref[...]
        return x + mu[0:1] * (prev - x) + mu[1:2] * (nxt - x)

    r = shifted(r_ref, mu_r)
    k = shifted(k_ref, mu_k)
    v = shifted(v_ref, mu_v)
    wa = shifted(wa_ref, mu_wa)
    gd = shifted(gd_ref, mu_gd)
    vec = vec_ref[...]
    k_k, k_a, r_k, ln_g, ln_b = (vec[i:i + 1] for i in range(5))

    lane = _iota((seq, LANES), 1)
    lo = lane < RWKV_HS
    gate = _bdot(_sigmoid(gd), g_up_ref[...])
    kk = k * k_k
    kk = kk * jnp.minimum(lax.rsqrt(_half_sum(kk * kk, lo)), 1e12)
    r_s[...] = r
    v_s[...] = v
    kk_s[...] = kk
    wd_in = jnp.where(lo, jnp.tanh(wa), 0.0)
    ad_in = jnp.where(lo, 0.0, wa)
    bonus = jnp.zeros((seq, LANES), F32)
    for d in range(2):
        wpre = w0_ref[d:d + 1, :] + _bdot(wd_in, wa_up_ref[d])
        sp = jnp.maximum(-wpre, 0.0) + jnp.log(1.0 + jnp.exp(-jnp.abs(wpre)))
        lw_s[d] = -jnp.exp(-sp - 0.5)
        a = _sigmoid(a0_ref[d:d + 1, :] + _bdot(ad_in, wa_up_ref[d]))
        kd = k * (1.0 + (a - 1.0) * k_a)
        kd_s[d] = kd
        b_s[d] = kk * a
        bonus = bonus + _half_sum(r * kd * r_k, lo) * v

    ci = _iota((CHUNK, CHUNK), 0)
    cj = _iota((CHUNK, CHUNK), 1)
    eye = jnp.where(ci == cj, 1.0, 0.0)
    blk8 = (ci // 8) == (cj // 8)
    merge_sizes = [8 * 2 ** n for n in range(int(math.log2(CHUNK // 8)))]
    merge_masks = [((ci // (2 * s)) == (cj // (2 * s))) & ((ci // s) != (cj // s)) for s in merge_sizes]
    before = (cj < ci, cj > ci)
    upto = (cj <= ci, cj >= ci)
    tri = tuple(jnp.where(m, 1.0, 0.0).astype(BF16) for m in upto)
    clane = _iota((CHUNK, LANES), 1)
    head_lanes = (clane < RWKV_HS, clane >= RWKV_HS)
    si = _iota((LANES, LANES), 0)
    sj = _iota((LANES, LANES), 1)
    same_head = (si < RWKV_HS) == (sj < RWKV_HS)
    diag = si == sj
    row_h0 = _iota((LANES, CHUNK), 0) < RWKV_HS

    def chunk_group(g):
        items = ([(0, g * unroll + u) for u in range(unroll)]
                 + [(1, n_chunks - 1 - (g * unroll + u)) for u in range(unroll)])
        n_it = len(items)
        dirs = [d for d, _ in items]
        rows = [pl.ds(pl.multiple_of(c * CHUNK, CHUNK), CHUNK) for _, c in items]
        r_c = [r_s[rw, :] for rw in rows]
        v_c = [v_s[rw, :] for rw in rows]
        kk_c = [kk_s[rw, :] for rw in rows]
        lw_c = [lw_s[d, rw, :] for d, rw in zip(dirs, rows)]
        kd_c = [kd_s[d, rw, :] for d, rw in zip(dirs, rows)]
        b_c = [b_s[d, rw, :] for d, rw in zip(dirs, rows)]
        cum = [_dot_exact_lhs(tri[d], lw) for d, lw in zip(dirs, lw_c)]
        tot = [cm[CHUNK - 1:CHUNK, :] if d == 0 else cm[0:1, :] for d, cm in zip(dirs, cum)]
        e_neg = [jnp.exp(-cm) for cm in cum]
        al = [jnp.exp(cm - lw) * kk for cm, lw, kk in zip(cum, lw_c, kk_c)]
        be = [b * e for b, e in zip(b_c, e_neg)]
        ka = [kd * e for kd, e in zip(kd_c, e_neg)]
        rh = [r * jnp.exp(cm) for r, cm in zip(r_c, cum)]
        wc = [jnp.exp(tt) for tt in tot]
        combos = [(i, h) for i in range(n_it) for h in range(2)]
        x2 = [jnp.concatenate([a, r], axis=0) for a, r in zip(al, rh)]
        heads_of = lambda m_t: jnp.concatenate([jnp.where(row_h0, m_t, 0.0), jnp.where(row_h0, 0.0, m_t)], axis=1)
        xb2 = [_bdot(x, heads_of(b.T)) for x, b in zip(x2, be)]
        xk2 = [_bdot(x, heads_of(k_.T)) for x, k_ in zip(x2, ka)]
        xb = [xb2[i][:, h * CHUNK:(h + 1) * CHUNK] for i, h in combos]
        xk = [xk2[i][:, h * CHUNK:(h + 1) * CHUNK] for i, h in combos]
        n_mat = [jnp.where(before[dirs[i]], -x[0:CHUNK], 0.0) for x, (i, _) in zip(xb, combos)]
        g_mat = [jnp.where(before[dirs[i]], x[0:CHUNK], 0.0) for x, (i, _) in zip(xk, combos)]
        m2 = [jnp.where(upto[dirs[i]], x[CHUNK:], 0.0) for x, (i, _) in zip(xb, combos)]
        m1 = [jnp.where(upto[dirs[i]], x[CHUNK:], 0.0) for x, (i, _) in zip(xk, combos)]
        nd = [jnp.where(blk8, n, 0.0) for n in n_mat]
        t = [eye + x for x in nd]
        p = [_bdot(x, x) for x in nd]
        t = [a + _bdot(a, b) for a, b in zip(t, p)]
        p = [_bdot(x, x) for x in p]
        t = [a + _bdot(a, b) for a, b in zip(t, p)]
        for msk in merge_masks:
            q = [_bdot(jnp.where(msk, n, 0.0), a) for n, a in zip(n_mat, t)]
            t = [a + _bdot(a, b) for a, b in zip(t, q)]
        gv = [_bdot(g_, v_c[i]) for g_, (i, _) in zip(g_mat, combos)]
        z = [_bdot(t_, jnp.concatenate([al[i], g_], axis=1)) for t_, g_, (i, _) in zip(t, gv, combos)]
        yv_h = [_bdot(m, v_c[i]) for m, (i, _) in zip(m1, combos)]
        pick = lambda a, b: jnp.where(head_lanes[0], a, b)
        alp = [pick(z[2 * i][:, :LANES], z[2 * i + 1][:, :LANES]) for i in range(n_it)]
        uv = [pick(z[2 * i][:, LANES:], z[2 * i + 1][:, LANES:]) for i in range(n_it)]
        yv = [pick(yv_h[2 * i], yv_h[2 * i + 1]) for i in range(n_it)]
        be_t = [(b * w).T for b, w in zip(be, wc)]
        ka_t = [(k_ * w).T for k_, w in zip(ka, wc)]
        pz = [_bdot(bt, jnp.concatenate([a, u_], axis=1)) for bt, a, u_ in zip(be_t, alp, uv)]
        kv = [_bdot(kt, v_) for kt, v_ in zip(ka_t, v_c)]
        a_mat = [jnp.where(same_head, jnp.where(diag, w, 0.0) - pp[:, :LANES], 0.0) for w, pp in zip(wc, pz)]
        b_mat = [jnp.where(same_head, k_ - pp[:, LANES:], 0.0) for k_, pp in zip(kv, pz)]
        for i, d in enumerate(dirs):
            st = st_s[d]
            xs = _bdot(jnp.concatenate([alp[i], rh[i]], axis=0), st)
            u = uv[i] + xs[0:CHUNK]
            m2u = pick(_bdot(m2[2 * i], u), _bdot(m2[2 * i + 1], u))
            y_s[d, rows[i], :] = xs[CHUNK:] + yv[i] - m2u
            st_s[d] = _bdot(a_mat[i], st) + b_mat[i]

    def body(g, carry):
        chunk_group(g)
        return carry

    if has_s0:
        st_s[0] = s0f_ref[0, 0]
        st_s[1] = s0b_ref[0, 0]
    else:
        st_s[...] = jnp.zeros_like(st_s)
    lax.fori_loop(0, n_chunks // unroll, body, 0)
    sf_ref[0, 0] = st_s[0]
    sb_ref[0, 0] = st_s[1]

    y = (y_s[0] + y_s[1]) + bonus
    mean = _half_sum(y, lo) * (1.0 / RWKV_HS)
    yc = y - mean
    var = _half_sum(yc * yc, lo) * (1.0 / RWKV_HS)
    o_ref[...] = ((yc * lax.rsqrt(var + GN_EPS)) * ln_g + ln_b) * gate


def _rwkv(p, seq, li, mu, w0, a0, wa_up, g_up, vecs, s0=None):
    t = p.shape[0]
    nb = t // seq
    n_hp = RWKV_HEADS // 2
    base = (2 * DIFF_QW + DIFF_WIDTH) // LANES
    tail = base + 3 * n_hp
    col = lambda blk: pl.BlockSpec((seq, LANES), lambda b, hp: (b, blk(hp)))
    mu_col = lambda blk: pl.BlockSpec((None, 2, LANES), lambda b, hp: (li, 0, blk(hp)))
    in_specs = [
        col(lambda hp: base + hp), col(lambda hp: base + n_hp + hp), col(lambda hp: base + 2 * n_hp + hp),
        col(lambda hp: tail), col(lambda hp: tail + 1),
        mu_col(lambda hp: hp), mu_col(lambda hp: n_hp + hp), mu_col(lambda hp: 2 * n_hp + hp),
        mu_col(lambda hp: 3 * n_hp), mu_col(lambda hp: 3 * n_hp + 1),
        pl.BlockSpec((None, 2, LANES), lambda b, hp: (li, 0, hp)),
        pl.BlockSpec((None, 2, LANES), lambda b, hp: (li, 0, hp)),
        pl.BlockSpec((None, 2, LANES, LANES), lambda b, hp: (li, 0, 0, hp)),
        pl.BlockSpec((None, LANES, LANES), lambda b, hp: (li, 0, hp)),
        pl.BlockSpec((None, 8, LANES), lambda b, hp: (li, 0, hp)),
    ]
    args = [p] * 5 + [mu] * 5 + [w0, a0, wa_up, g_up, vecs]
    st_spec = pl.BlockSpec((1, 1, LANES, LANES), lambda b, hp: (b, hp, 0, 0))
    if s0 is not None:
        in_specs += [st_spec, st_spec]
        args += list(s0)
    st_shape = jax.ShapeDtypeStruct((nb, n_hp, LANES, LANES), F32)
    return pl.pallas_call(
        functools.partial(_rwkv_kernel, s0 is not None),
        out_shape=(jax.ShapeDtypeStruct((t, RWKV_WIDTH), F32), st_shape, st_shape),
        grid=(nb, n_hp),
        in_specs=in_specs,
        out_specs=(pl.BlockSpec((seq, LANES), lambda b, hp: (b, hp)), st_spec, st_spec),
        scratch_shapes=[pltpu.VMEM((seq, LANES), F32)] * 3 + [pltpu.VMEM((2, seq, LANES), F32)] * 4
        + [pltpu.VMEM((2, LANES, LANES), F32)],
        compiler_params=_params("parallel", "parallel"),
        name="rwkv",
    )(*args)


def _pair_states_in(s):
    nb = s.shape[0]
    st = jnp.swapaxes(s.astype(F32), -1, -2).reshape(nb, RWKV_HEADS // 2, 2, RWKV_HS, RWKV_HS)
    z = jnp.zeros_like(st[:, :, 0])
    top = jnp.concatenate([st[:, :, 0], z], axis=-1)
    bot = jnp.concatenate([z, st[:, :, 1]], axis=-1)
    return jnp.concatenate([top, bot], axis=-2)


def _pair_states_out(st):
    nb = st.shape[0]
    h0 = st[:, :, :RWKV_HS, :RWKV_HS]
    h1 = st[:, :, RWKV_HS:, RWKV_HS:]
    s = jnp.stack([h0, h1], axis=2).reshape(nb, RWKV_HEADS, RWKV_HS, RWKV_HS)
    return jnp.swapaxes(s, -1, -2)


def _dft_kernel(f_ref, ft_ref):
    n = f_ref.shape[1]
    shape = (2 * n, n)
    fr = _iota(shape, 0)
    tau = _iota(shape, 1)
    is_sin = fr >= n
    kf = jnp.where(is_sin, fr - n, fr)
    ang = ((kf * tau) & (2 * n - 1)).astype(F32) * (math.pi / n)
    val = jnp.where(is_sin, -jnp.sin(ang), jnp.cos(ang))
    nyq = jnp.where((tau & 1) == 0, 1.0, -1.0)
    val = jnp.where(is_sin & (kf == 0), nyq, val)
    f_ref[...] = val.astype(BF16)
    ft_ref[...] = val.T.astype(BF16)


def _dft_mats(n):
    return pl.pallas_call(
        _dft_kernel,
        out_shape=(jax.ShapeDtypeStruct((2 * n, n), BF16), jax.ShapeDtypeStruct((n, 2 * n), BF16)),
        name="dft_mats",
    )()


def _filter_kernel(seq, w0t_ref, w0c_ref, w0s_ref, b0_ref, w1_ref, b1_ref, w2_ref, b2_ref, fr_ref,
                   w3a_ref, w3b_ref, f_ref, o_ref, prev_scr):
    cb = f_ref.shape[1]
    n_blk = seq // cb
    e = pl.program_id(0)
    first_lag = (e - n_blk) * cb
    tap_row = jnp.abs(first_lag + _iota((1, cb), 1)).astype(F32)
    pos_row = tap_row * (1.0 / (seq - 1))
    band = _iota((BANDS, 1), 0).astype(F32)
    freq = 1e-4 + band * ((BANDS - 1 - 1e-4) / (BANDS - 1))
    arg = freq * (tap_row * (2.0 * math.pi / seq))
    fr = fr_ref[...]
    pre = (w0t_ref[...] * pos_row + _dot_f32(w0c_ref[...], jnp.cos(arg)) + _dot_f32(w0s_ref[...], -jnp.sin(arg))
           + b0_ref[...])
    hdn = jnp.sin(fr[:, 0:1] * pre)
    hdn = jnp.sin(fr[:, 1:2] * (_dot_f32(w1_ref[...], hdn) + b1_ref[...]))
    hdn = jnp.sin(fr[:, 2:3] * (_dot_f32(w2_ref[...], hdn) + b2_ref[...]))
    hdn = hdn.T
    tap = jnp.abs(first_lag + _iota((cb, 1), 0))
    pos = tap.astype(F32) * (1.0 / (seq - 1))
    chan = _iota((1, D_MODEL), 1).astype(F32)
    delta = jnp.abs(MIN_DECAY + chan * ((MAX_DECAY - MIN_DECAY) / (D_MODEL - 1)))
    window = jnp.where(tap < seq, jnp.exp(-pos * delta), 0.0)
    row = _iota((2 * cb, 1), 0)
    sign = jnp.where((row & 1) == 1, -1.0, 1.0)
    for o, w3_ref in enumerate((w3a_ref, w3b_ref)):
        cols = slice(o * D_MODEL, (o + 1) * D_MODEL)
        taps = (_dot_f32(hdn, w3_ref[...]) * window).astype(BF16)
        spec = jnp.dot(f_ref[...], taps, preferred_element_type=F32)

        @pl.when(e > 0)
        def _():
            o_ref[0, :, cols] = spec + sign * prev_scr[:, cols]

        prev_scr[:, cols] = spec - jnp.where(row <= cb, taps[0:1, :].astype(F32), 0.0)


def _hyena_filters(seq, li, fmat, small, w3):
    cb = fmat.shape[1]
    n_blk = seq // cb
    layer_slice = lambda a: pl.BlockSpec((None,) + a.shape[1:], lambda e: (li, 0, 0))
    w3_spec = lambda o: pl.BlockSpec((None, FILTER_WIDTH, D_MODEL),
                                     lambda e: (li, 0, 2 * o + jnp.where(e < n_blk, 1, 0)))
    return pl.pallas_call(
        functools.partial(_filter_kernel, seq),
        out_shape=jax.ShapeDtypeStruct((2 * n_blk - 1, 2 * cb, 2 * D_MODEL), F32),
        grid=(2 * n_blk,),
        in_specs=[layer_slice(a) for a in small] + [w3_spec(0), w3_spec(1),
                                                    pl.BlockSpec((2 * cb, cb), lambda e: (0, 0))],
        out_specs=pl.BlockSpec((1, 2 * cb, 2 * D_MODEL), lambda e: (jnp.maximum(e - 1, 0), 0, 0)),
        scratch_shapes=[pltpu.VMEM((2 * cb, 2 * D_MODEL), F32)],
        compiler_params=_params("arbitrary"),
        name="hyena_filters",
    )(*small, w3, w3, fmat)


def _short_conv(z, w, b):
    prev, nxt = _shift_rows(z)
    return prev * w[0:1] + z * w[1:2] + nxt * w[2:3] + b


def _conv_kernel(conv_v, n_blk, v_ref, x_ref, wv_ref, bv_ref, wx_ref, bx_ref, f_ref, ft_ref, kc_ref, bias_ref,
                 o_ref, uf_scr, yf_scr):
    cb = f_ref.shape[1]
    v = v_ref[...]
    if conv_v:
        v = _short_conv(v, wv_ref[...], bv_ref[...])
    fmat = f_ref[...]
    for j in range(n_blk):
        uf_scr[j] = jnp.dot(fmat, v[j * cb:(j + 1) * cb].astype(BF16), preferred_element_type=F32)

    def mix(r0, first):
        re, im = pl.ds(r0, MIX_ROWS), pl.ds(cb + r0, MIX_ROWS)
        real_pair = (_iota((MIX_ROWS, 1), 0) == 0) if first else None
        for i in range(n_blk):
            acc_r = acc_i = None
            for j in range(n_blk):
                d = i - j + n_blk - 1
                ur, ui = uf_scr[j, re, :], uf_scr[j, im, :]
                kr, ki = kc_ref[d, re, :], kc_ref[d, im, :]
                uiki = ui * ki
                if first:
                    t_r = ur * kr - jnp.where(real_pair, 0.0, uiki)
                    t_i = jnp.where(real_pair, uiki, ur * ki + ui * kr)
                else:
                    t_r = ur * kr - uiki
                    t_i = ur * ki + ui * kr
                acc_r = t_r if acc_r is None else acc_r + t_r
                acc_i = t_i if acc_i is None else acc_i + t_i
            scale = jnp.where(real_pair, 0.5 / cb, 1.0 / cb) if first else 1.0 / cb
            yf_scr[i, re, :] = acc_r * scale
            yf_scr[i, im, :] = acc_i * scale

    mix(0, True)

    def body(r, carry):
        mix(pl.multiple_of(r * MIX_ROWS, MIX_ROWS), False)
        return carry

    lax.fori_loop(1, cb // MIX_ROWS, body, 0)

    x = _short_conv(x_ref[...], wx_ref[...], bx_ref[...])
    ftm = ft_ref[...]
    bias = bias_ref[...]
    for i in range(n_blk):
        rows = slice(i * cb, (i + 1) * cb)
        y = jnp.dot(ftm, yf_scr[i].astype(BF16), preferred_element_type=F32)
        o_ref[rows, :] = x[rows] * (y + v[rows] * bias)


def _hyena_conv(v_arr, v_blk0, x_arr, x_blk0, conv_v, seq, li, order, cw, cwb, fmat, fmat_t, kc, biases, width):
    t = v_arr.shape[0]
    cb = fmat.shape[1]
    nb, ncb, n_blk = t // seq, D_MODEL // width, seq // cb
    wv_blk0 = v_blk0 if conv_v else 0
    kc_blk0 = order * ncb
    return pl.pallas_call(
        functools.partial(_conv_kernel, conv_v, n_blk),
        out_shape=jax.ShapeDtypeStruct((t, D_MODEL), F32),
        grid=(nb, ncb),
        in_specs=[
            pl.BlockSpec((seq, width), lambda b, c: (b, v_blk0 + c)),
            pl.BlockSpec((seq, width), lambda b, c: (b, x_blk0 + c)),
            pl.BlockSpec((None, 3, width), lambda b, c: (li, 0, wv_blk0 + c)),
            pl.BlockSpec((None, 1, width), lambda b, c: (li, 0, wv_blk0 + c)),
            pl.BlockSpec((None, 3, width), lambda b, c: (li, 0, x_blk0 + c)),
            pl.BlockSpec((None, 1, width), lambda b, c: (li, 0, x_blk0 + c)),
            pl.BlockSpec((2 * cb, cb), lambda b, c: (0, 0)),
            pl.BlockSpec((cb, 2 * cb), lambda b, c: (0, 0)),
            pl.BlockSpec((2 * n_blk - 1, 2 * cb, width), lambda b, c: (0, 0, kc_blk0 + c)),
            pl.BlockSpec((None, 1, width), lambda b, c: (li * 2 + order, 0, c)),
        ],
        out_specs=pl.BlockSpec((seq, width), lambda b, c: (b, c)),
        scratch_shapes=[pltpu.VMEM((n_blk, 2 * cb, width), F32), pltpu.VMEM((n_blk, 2 * cb, width), F32)],
        compiler_params=_params("parallel", "parallel"),
        name="hyena_conv",
    )(v_arr, x_arr, cw, cwb, cw, cwb, fmat, fmat_t, kc, biases)


def _axial_rope_tables(seq):
    n_rows = seq // GRID_W
    row = np.repeat(np.arange(n_rows, dtype=np.float64), GRID_W)
    col = np.tile(np.arange(GRID_W, dtype=np.float64), n_rows)
    inv = ROPE_BASE ** (-np.arange(ROPE_PAIRS, dtype=np.float64) / ROPE_PAIRS)
    ar = row[:, None] * inv[None]
    ac = col[:, None] * inv[None]
    ang = np.concatenate([ar, ar, ac, ac], axis=-1)
    reps = LANES // DIFF_QK
    return (jnp.asarray(np.tile(np.cos(ang), (1, reps)), F32), jnp.asarray(np.tile(np.sin(ang), (1, reps)), F32))


def _pair_lanes(a, seq_axis):
    nb = a.shape[0]
    if a.ndim == 5:
        a = a.reshape(nb, DIFF_HEADS // 2, 2, 2, a.shape[3], DIFF_QK).transpose(0, 1, 4, 2, 3, 5)
    else:
        a = a.reshape(nb, DIFF_HEADS // 2, 2, a.shape[2], DIFF_V).transpose(0, 1, 3, 2, 4)
    return a.reshape(nb, DIFF_HEADS // 2, a.shape[2], LANES)


def kernel(x_prompt, x_sample, cache_diff_k, cache_diff_v, state_rwkv_fwd, state_rwkv_bwd, c, c_ctx, ada_w, ada_b, norm1_g, norm2_g, ffn_w1, ffn_w3, ffn_w2, final_g, ab_w_in, ab_w_out, diff_lambda, diff_subln_g, rwkv_mu, rwkv_w0, rwkv_w_up, rwkv_a0, rwkv_a_up, rwkv_g_up, rwkv_k_k, rwkv_k_a, rwkv_r_k, rwkv_ln_g, rwkv_ln_b, hy_w_in, hy_b_in, hy_conv_w, hy_conv_b, hy_f_w0, hy_f_b0, hy_f_w1, hy_f_b1, hy_f_w2, hy_f_b2, hy_f_w3, hy_f_freq, hy_bias, hy_w_out, hy_b_out):
    n_ctx_seqs, ctx_len, _ = x_prompt.shape
    n_lat_seqs, lat_len, _ = x_sample.shape
    tm = 1024

    cond = jnp.zeros((N_MOD_ROWS, D_MODEL), F32).at[0].set(c_ctx).at[1:1 + n_lat_seqs].set(c)
    mods = _ada_all(cond, ada_w, ada_b).reshape(DEPTH * N_MOD_ROWS * 6, 1, D_MODEL)

    groups = [
        dict(x=x_prompt.reshape(-1, D_MODEL), seq=ctx_len, rows=(0, n_ctx_seqs * ctx_len), tq=ctx_len, pairs=4,
             conv_width=1024),
        dict(x=x_sample.reshape(-1, D_MODEL), seq=lat_len, rows=(1, lat_len), tq=512, pairs=1, conv_width=256),
    ]
    rope = _axial_rope_tables(lat_len)
    dft = {cb: _dft_mats(cb) for cb in {min(CONV_BLOCK_MAX, g["seq"]) for g in groups}}
    n_ab, n_c = ab_w_in.shape[0], hy_w_in.shape[0]

    gains1 = norm1_g.reshape(DEPTH, 1, D_MODEL)
    gains2 = norm2_g.reshape(DEPTH, 1, D_MODEL)
    g2s = jnp.tile(diff_subln_g, (1, 2)).reshape(n_ab, 1, LANES)
    wa_up = jnp.concatenate([rwkv_w_up, rwkv_a_up], axis=2)
    vec_rows = [rwkv_k_k, rwkv_k_a, rwkv_r_k.reshape(n_ab, RWKV_WIDTH), rwkv_ln_g, rwkv_ln_b]
    vecs = jnp.stack(vec_rows + [jnp.zeros_like(rwkv_k_k)] * (8 - len(vec_rows)), axis=1)
    w0_t = jnp.swapaxes(hy_f_w0, 1, 2)
    filt_small = [w0_t[:, :, 0:1], w0_t[:, :, 1:1 + BANDS], w0_t[:, :, 1 + BANDS:], hy_f_b0[:, :, None],
                  jnp.swapaxes(hy_f_w1, 1, 2), hy_f_b1[:, :, None], jnp.swapaxes(hy_f_w2, 1, 2), hy_f_b2[:, :, None],
                  jnp.swapaxes(hy_f_freq, 1, 2)]
    hy_b_in_s = hy_b_in.reshape(n_c, 1, -1)
    hy_b_out_s = hy_b_out.reshape(n_c, 1, D_MODEL)
    conv_b = hy_conv_b.reshape(n_c, 1, -1)
    conv_bias = hy_bias.reshape(n_c * 2, 1, D_MODEL)
    new_k, new_v, new_sf, new_sb = [], [], [], []

    for l in range(DEPTH):
        i = l // 2
        for gi, g in enumerate(groups):
            x, seq, row = g["x"], g["seq"], g["rows"]
            if l % 2 == 0:
                p = _proj_in(x, gains1, mods, l, row, ab_w_in, i, None, tm, AB_IN // 2)
                if gi == 0:
                    att, k_own, v_own = _attention(p, seq, g["tq"], g["pairs"], _lambda_init(l), diff_lambda, g2s, i,
                                                   emit_kv=True)
                    new_k.append(k_own)
                    new_v.append(v_own)
                    mix, sf, sb = _rwkv(p, seq, i, rwkv_mu, rwkv_w0, rwkv_a0, wa_up, rwkv_g_up, vecs)
                    new_sf.append(_pair_states_out(sf))
                    new_sb.append(_pair_states_out(sb))
                else:
                    ctx = (_pair_lanes(cache_diff_k[:, i], 3), _pair_lanes(cache_diff_v[:, i], 2))
                    att, = _attention(p, seq, g["tq"], g["pairs"], _lambda_init(l), diff_lambda, g2s, i,
                                      ctx=ctx, rope=rope)
                    s0 = (_pair_states_in(state_rwkv_fwd[:, i]), _pair_states_in(state_rwkv_bwd[:, i]))
                    mix, _, _ = _rwkv(p, seq, i, rwkv_mu, rwkv_w0, rwkv_a0, wa_up, rwkv_g_up, vecs, s0=s0)
                x = _proj_out(x, mods, l, row, [att, mix], ab_w_out, i, None, tm)
            else:
                fmat, fmat_t = dft[min(CONV_BLOCK_MAX, seq)]
                z = _proj_in(x, gains1, mods, l, row, hy_w_in, i, hy_b_in_s, tm, 1536)
                kc = _hyena_filters(seq, i, fmat, filt_small, hy_f_w3)
                width = g["conv_width"]
                ncb = D_MODEL // width
                u = _hyena_conv(z, 2 * ncb, z, 0, True, seq, i, 0, hy_conv_w, conv_b, fmat, fmat_t, kc, conv_bias, width)
                u = _hyena_conv(u, 0, z, ncb, False, seq, i, 1, hy_conv_w, conv_b, fmat, fmat_t, kc, conv_bias, width)
                x = _proj_out(x, mods, l, row, [u], hy_w_out, i, hy_b_out_s, tm)
            g["x"] = _ffn(x, gains2, mods, l, row, ffn_w1, ffn_w3, ffn_w2, final_g, l == DEPTH - 1, 2 * tm, 256)

    y_prompt = groups[0]["x"].reshape(x_prompt.shape)
    y_sample = groups[1]["x"].reshape(x_sample.shape)
    return (y_prompt, y_sample, jnp.stack(new_k, axis=1), jnp.stack(new_v, axis=1),
            jnp.stack(new_sf, axis=1), jnp.stack(new_sb, axis=1))
```

```python
import functools
import math

import jax
import jax.numpy as jnp
import numpy as np
from jax import lax
from jax.experimental import pallas as pl
from jax.experimental.pallas import tpu as pltpu

F32 = jnp.float32
BF16 = jnp.bfloat16

D_MODEL = 1024
DEPTH = 4
GRID_W = 64
DIFF_HEADS = 8
DIFF_QK = 32
DIFF_V = 64
DIFF_QW = DIFF_HEADS * 2 * DIFF_QK
DIFF_WIDTH = DIFF_HEADS * DIFF_V
ROPE_PAIRS = DIFF_QK // 4
ROPE_BASE = 10000.0
RWKV_HEADS = 8
RWKV_HS = 64
RWKV_WIDTH = RWKV_HEADS * RWKV_HS
RWKV_IN = 3 * RWKV_WIDTH + 64 + 64 + 128
AB_IN = 2 * DIFF_QW + DIFF_WIDTH + RWKV_IN
BANDS = 16
FILTER_WIDTH = 64
MAX_DECAY = math.log(1e-2) / 0.3
MIN_DECAY = math.log(1e-2) / 1.5
D_FF = 2816
RMS_EPS = 1e-6
GN_EPS = 64e-5
HEAD_RMS_EPS = 1e-5
LOG2_E = math.log2(math.e)

LANES = 128
CHUNK = 128
SCAN_UNROLL = 4
CONV_BLOCK_MAX = 512
MIX_ROWS = 8
VMEM_LIMIT = 56 * 1024 * 1024

N_MOD_ROWS = 8


def _lambda_init(l):
    return 0.8 - 0.6 * math.exp(-0.3 * l)


def _params(*sem):
    return pltpu.CompilerParams(dimension_semantics=sem, vmem_limit_bytes=VMEM_LIMIT)


def _bdot(a, b):
    return jnp.dot(a.astype(BF16), b.astype(BF16), preferred_element_type=F32)


def _bdot_nt(a, b):
    return lax.dot_general(a.astype(BF16), b.astype(BF16), (((1,), (1,)), ((), ())),
                           preferred_element_type=F32)


def _split3(x):
    h1 = x.astype(BF16)
    r1 = x - h1.astype(F32)
    h2 = r1.astype(BF16)
    h3 = (r1 - h2.astype(F32)).astype(BF16)
    return h1, h2, h3


def _dot_f32(a, b):
    a1, a2, a3 = _split3(a)
    b1, b2, b3 = _split3(b)
    d = lambda x, y: jnp.dot(x, y, preferred_element_type=F32)
    return (d(a1, b1) + (d(a1, b2) + d(a2, b1))) + ((d(a1, b3) + d(a3, b1)) + d(a2, b2))


def _dot_exact_lhs(a_bf16, b):
    b1, b2, b3 = _split3(b)
    d = lambda y: jnp.dot(a_bf16, y, preferred_element_type=F32)
    return d(b1) + (d(b2) + d(b3))


def _sigmoid(x):
    return 0.5 * (jnp.tanh(0.5 * x) + 1.0)


def _iota(shape, dim):
    return lax.broadcasted_iota(jnp.int32, shape, dim)


def _shift_rows(x):
    n = x.shape[0]
    row = _iota(x.shape, 0)
    prev = jnp.where(row == 0, 0.0, pltpu.roll(x, 1, 0))
    nxt = jnp.where(row == n - 1, 0.0, pltpu.roll(x, n - 1, 0))
    return prev, nxt


def _half_sum(x, lo_mask):
    s_lo = jnp.sum(jnp.where(lo_mask, x, 0.0), axis=-1, keepdims=True)
    s_hi = jnp.sum(jnp.where(lo_mask, 0.0, x), axis=-1, keepdims=True)
    return jnp.where(lo_mask, s_lo, s_hi)


def _ada_kernel(c_ref, w_ref, b_ref, o_ref):
    c = c_ref[...]
    o_ref[0] = _bdot(c * _sigmoid(c), w_ref[0]) + b_ref[0]


def _ada_all(cond, ada_w, ada_b):
    tn = 1536
    n_out = 6 * D_MODEL
    return pl.pallas_call(
        _ada_kernel,
        out_shape=jax.ShapeDtypeStruct((DEPTH, N_MOD_ROWS, n_out), F32),
        grid=(DEPTH, n_out // tn),
        in_specs=[
            pl.BlockSpec((N_MOD_ROWS, D_MODEL), lambda l, j: (0, 0)),
            pl.BlockSpec((1, D_MODEL, tn), lambda l, j: (l, 0, j)),
            pl.BlockSpec((1, 1, tn), lambda l, j: (l, 0, j)),
        ],
        out_specs=pl.BlockSpec((1, N_MOD_ROWS, tn), lambda l, j: (l, 0, j)),
        compiler_params=_params("parallel", "parallel"),
        name="ada",
    )(cond, ada_w, ada_b.reshape(DEPTH, 1, n_out))


def _mod_spec(layer, chunk, cond_rows, tm, axis=0):
    first, span = cond_rows

    def index_map(*idx):
        return ((layer * N_MOD_ROWS + first + (idx[axis] * tm) // span) * 6 + chunk, 0, 0)
    return pl.BlockSpec((1, 1, D_MODEL), index_map)


def _normed(x, g, sc, sh):
    ms = jnp.mean(x * x, axis=-1, keepdims=True)
    return (x * lax.rsqrt(ms + RMS_EPS)) * g * (1.0 + sc) + sh


def _proj_in_kernel(has_bias, x_ref, g_ref, sc_ref, sh_ref, w_ref, *refs):
    o_ref = refs[-1]
    h = _normed(x_ref[...], g_ref[...], sc_ref[0], sh_ref[0])
    y = _bdot(h, w_ref[...])
    o_ref[...] = y + refs[0][...] if has_bias else y


def _proj_in(x, gains, mods, layer, cond_rows, ws, wi, bs, tm, tn):
    t, n_out = x.shape[0], ws.shape[2]
    in_specs = [
        pl.BlockSpec((tm, D_MODEL), lambda j, i: (i, 0)),
        pl.BlockSpec((None, 1, D_MODEL), lambda j, i: (layer, 0, 0)),
        _mod_spec(layer, 1, cond_rows, tm, axis=1),
        _mod_spec(layer, 0, cond_rows, tm, axis=1),
        pl.BlockSpec((None, D_MODEL, tn), lambda j, i: (wi, 0, j)),
    ]
    args = [x, gains, mods, mods, ws]
    if bs is not None:
        in_specs.append(pl.BlockSpec((None, 1, tn), lambda j, i: (wi, 0, j)))
        args.append(bs)
    return pl.pallas_call(
        functools.partial(_proj_in_kernel, bs is not None),
        out_shape=jax.ShapeDtypeStruct((t, n_out), F32),
        grid=(n_out // tn, t // tm),
        in_specs=in_specs,
        out_specs=pl.BlockSpec((tm, tn), lambda j, i: (i, j)),
        compiler_params=_params("parallel", "parallel"),
        name="proj_in",
    )(*args)


def _proj_out_kernel(n_u, has_bias, x_ref, gt_ref, *refs):
    u_refs, w_ref, o_ref = refs[:n_u], refs[n_u], refs[-1]
    acc = None
    off = 0
    for u_ref in u_refs:
        k = u_ref.shape[1]
        part = _bdot(u_ref[...], w_ref[off:off + k, :])
        acc = part if acc is None else acc + part
        off += k
    if has_bias:
        acc = acc + refs[n_u + 1][...]
    o_ref[...] = x_ref[...] + gt_ref[0] * acc


def _proj_out(x, mods, layer, cond_rows, us, ws, wi, bs, tm):
    t = x.shape[0]
    in_specs = [
        pl.BlockSpec((tm, D_MODEL), lambda i: (i, 0)),
        _mod_spec(layer, 2, cond_rows, tm),
        *[pl.BlockSpec((tm, u.shape[1]), lambda i: (i, 0)) for u in us],
        pl.BlockSpec((None,) + ws.shape[1:], lambda i: (wi, 0, 0)),
    ]
    args = [x, mods, *us, ws]
    if bs is not None:
        in_specs.append(pl.BlockSpec((None, 1, D_MODEL), lambda i: (wi, 0, 0)))
        args.append(bs)
    return pl.pallas_call(
        functools.partial(_proj_out_kernel, len(us), bs is not None),
        out_shape=jax.ShapeDtypeStruct((t, D_MODEL), F32),
        grid=(t // tm,),
        in_specs=in_specs,
        out_specs=pl.BlockSpec((tm, D_MODEL), lambda i: (i, 0)),
        compiler_params=_params("parallel"),
        name="proj_out",
    )(*args)


def _ffn_kernel(final, x_ref, g_ref, sc_ref, sh_ref, gt_ref, w1_ref, w3_ref, w2_ref, fg_ref, o_ref, h_scr):
    k = pl.program_id(1)

    @pl.when(k == 0)
    def _():
        h_scr[...] = _normed(x_ref[...], g_ref[...], sc_ref[0], sh_ref[0]).astype(BF16)
        o_ref[...] = jnp.zeros_like(o_ref)

    h = h_scr[...]
    a1 = jnp.dot(h, w1_ref[...].astype(BF16), preferred_element_type=F32)
    a3 = jnp.dot(h, w3_ref[...].astype(BF16), preferred_element_type=F32)
    o_ref[...] += _bdot((a1 * _sigmoid(a1)) * a3, w2_ref[...])

    @pl.when(k == pl.num_programs(1) - 1)
    def _():
        y = x_ref[...] + gt_ref[0] * o_ref[...]
        if final:
            ms = jnp.mean(y * y, axis=-1, keepdims=True)
            y = (y * lax.rsqrt(ms + RMS_EPS)) * fg_ref[...]
        o_ref[...] = y


def _ffn(x, gains, mods, layer, cond_rows, w1s, w3s, w2s, final_g, final, tm, tk):
    t = x.shape[0]
    return pl.pallas_call(
        functools.partial(_ffn_kernel, final),
        out_shape=jax.ShapeDtypeStruct((t, D_MODEL), F32),
        grid=(t // tm, D_FF // tk),
        in_specs=[
            pl.BlockSpec((tm, D_MODEL), lambda i, k: (i, 0)),
            pl.BlockSpec((None, 1, D_MODEL), lambda i, k: (layer, 0, 0)),
            _mod_spec(layer, 4, cond_rows, tm),
            _mod_spec(layer, 3, cond_rows, tm),
            _mod_spec(layer, 5, cond_rows, tm),
            pl.BlockSpec((None, D_MODEL, tk), lambda i, k: (layer, 0, k)),
            pl.BlockSpec((None, D_MODEL, tk), lambda i, k: (layer, 0, k)),
            pl.BlockSpec((None, tk, D_MODEL), lambda i, k: (layer, k, 0)),
            pl.BlockSpec((1, D_MODEL), lambda i, k: (0, 0)),
        ],
        out_specs=pl.BlockSpec((tm, D_MODEL), lambda i, k: (i, 0)),
        scratch_shapes=[pltpu.VMEM((tm, D_MODEL), BF16)],
        compiler_params=_params("parallel", "arbitrary"),
        name="ffn",
    )(x, gains, mods, mods, mods, w1s, w3s, w2s, final_g.reshape(1, D_MODEL))


def _rope(x, cos, sin):
    lane = _iota(x.shape, 1)
    rot = jnp.where((lane % 16) < 8, -pltpu.roll(x, LANES - 8, 1), pltpu.roll(x, 8, 1))
    return x * cos + rot * sin


def _attn_kernel(n_ctx, use_rope, emit_kv, lam_init, *refs):
    it = iter(refs)
    q_ref, k_ref, v_ref = next(it), next(it), next(it)
    kc_ref = vc_ref = cq_ref = sq_ref = ck_ref = sk_ref = ko_ref = vo_ref = None
    if n_ctx:
        kc_ref, vc_ref = next(it), next(it)
    if use_rope:
        cq_ref, sq_ref, ck_ref, sk_ref = next(it), next(it), next(it), next(it)
    lam_ref, g_ref, o_ref = next(it), next(it), next(it)
    if emit_kv:
        ko_ref, vo_ref = next(it), next(it)
    kall, vall = next(it), next(it)
    n_own = k_ref.shape[0]
    n_pairs = q_ref.shape[1] // LANES

    @pl.when(pl.program_id(2) == 0)
    def _():
        k = k_ref[...]
        v = v_ref[...]
        if emit_kv:
            for h in range(2 * n_pairs):
                for m in range(2):
                    c0 = (2 * h + m) * DIFF_QK
                    ko_ref[0, h, m] = k[:, c0:c0 + DIFF_QK]
                vo_ref[0, h] = v[:, h * DIFF_V:(h + 1) * DIFF_V]
        if use_rope:
            k = _rope(k, ck_ref[...], sk_ref[...])
        if n_ctx:
            kall[0:n_ctx, :] = kc_ref[0, 0].astype(BF16)
            vall[0:n_ctx, :] = vc_ref[0, 0].astype(BF16)
        kall[n_ctx:n_ctx + n_own, :] = k.astype(BF16)
        vall[n_ctx:n_ctx + n_own, :] = v.astype(BF16)

    lv = lam_ref[...]
    lam = (jnp.exp(jnp.sum(lv[0:1] * lv[1:2], axis=-1, keepdims=True))
           - jnp.exp(jnp.sum(lv[2:3] * lv[3:4], axis=-1, keepdims=True)) + lam_init)

    lane = _iota((q_ref.shape[0], LANES), 1)
    lo = lane < DIFF_V
    scale = DIFF_QK ** -0.5
    for hp in range(n_pairs):
        cols = slice(hp * LANES, (hp + 1) * LANES)
        q = q_ref[:, cols]
        if use_rope:
            q = _rope(q, cq_ref[...], sq_ref[...])
        ks, vs = kall[:, cols], vall[:, cols]
        outs = []
        for h in range(2):
            pv, inv = [], []
            for m in range(2):
                j = 2 * h + m
                qm = jnp.where((lane >= DIFF_QK * j) & (lane < DIFF_QK * (j + 1)), q, 0.0)
                s = _bdot_nt(qm, ks)
                e = jnp.exp2((s - jnp.max(s, axis=-1, keepdims=True)) * (scale * LOG2_E))
                inv.append(1.0 / jnp.sum(e, axis=-1, keepdims=True))
                pv.append(_bdot(e, vs))
            outs.append(pv[0] * inv[0] - pv[1] * (lam * inv[1]))
        o = jnp.where(lo, outs[0], outs[1])
        ms = _half_sum(o * o, lo) * (1.0 / DIFF_V)
        o_ref[:, cols] = (o * lax.rsqrt(ms + HEAD_RMS_EPS)) * g_ref[...] * (1.0 - lam_init)


def _attention(p, seq, tq, n_pairs, lam_init, lams, g2s, li, ctx=None, rope=None, emit_kv=False):
    assert rope is None or n_pairs == 1
    t = p.shape[0]
    nb, nq = t // seq, seq // tq
    n_ctx = 0 if ctx is None else ctx[0].shape[2]
    width = n_pairs * LANES
    k_blk, v_blk = DIFF_QW // width, 2 * DIFF_QW // width
    in_specs = [
        pl.BlockSpec((tq, width), lambda b, hp, qi: (b * nq + qi, hp)),
        pl.BlockSpec((seq, width), lambda b, hp, qi: (b, k_blk + hp)),
        pl.BlockSpec((seq, width), lambda b, hp, qi: (b, v_blk + hp)),
    ]
    args = [p, p, p]
    if ctx is not None:
        in_specs += [pl.BlockSpec((1, 1, n_ctx, width), lambda b, hp, qi: (b, hp, 0, 0))] * 2
        args += list(ctx)
    if rope is not None:
        in_specs += [pl.BlockSpec((tq, LANES), lambda b, hp, qi: (qi, 0))] * 2
        in_specs += [pl.BlockSpec((seq, width), lambda b, hp, qi: (0, 0))] * 2
        args += [rope[0], rope[1], rope[0], rope[1]]
    in_specs += [pl.BlockSpec((None, 4, DIFF_QK), lambda b, hp, qi: (li, 0, 0)),
                 pl.BlockSpec((None, 1, LANES), lambda b, hp, qi: (li, 0, 0))]
    args += [lams, g2s]
    out_shape = [jax.ShapeDtypeStruct((t, DIFF_WIDTH), F32)]
    out_specs = [pl.BlockSpec((tq, width), lambda b, hp, qi: (b * nq + qi, hp))]
    if emit_kv:
        out_shape += [jax.ShapeDtypeStruct((nb, DIFF_HEADS, 2, seq, DIFF_QK), F32),
                      jax.ShapeDtypeStruct((nb, DIFF_HEADS, seq, DIFF_V), F32)]
        out_specs += [pl.BlockSpec((1, 2 * n_pairs, 2, seq, DIFF_QK), lambda b, hp, qi: (b, hp, 0, 0, 0)),
                      pl.BlockSpec((1, 2 * n_pairs, seq, DIFF_V), lambda b, hp, qi: (b, hp, 0, 0))]
    return pl.pallas_call(
        functools.partial(_attn_kernel, n_ctx, rope is not None, emit_kv, lam_init),
        out_shape=out_shape,
        grid=(nb, DIFF_HEADS // 2 // n_pairs, nq),
        in_specs=in_specs,
        out_specs=out_specs,
        scratch_shapes=[pltpu.VMEM((n_ctx + seq, width), BF16), pltpu.VMEM((n_ctx + seq, width), BF16)],
        compiler_params=_params("parallel", "parallel", "arbitrary"),
        name="diff_attn",
    )(*args)


def _rwkv_kernel(has_s0, *refs):
    it = iter(refs)
    r_ref, k_ref, v_ref, wa_ref, gd_ref = (next(it) for _ in range(5))
    mu_r, mu_k, mu_v, mu_wa, mu_gd = (next(it) for _ in range(5))
    w0_ref, a0_ref, wa_up_ref, g_up_ref, vec_ref = (next(it) for _ in range(5))
    s0f_ref = s0b_ref = None
    if has_s0:
        s0f_ref, s0b_ref = next(it), next(it)
    o_ref, sf_ref, sb_ref = next(it), next(it), next(it)
    r_s, v_s, kk_s = next(it), next(it), next(it)
    lw_s, kd_s, b_s, y_s = next(it), next(it), next(it), next(it)
    st_s = next(it)

    seq = r_ref.shape[0]
    n_pairs = r_ref.shape[1] // LANES
    n_chunks = seq // CHUNK
    unroll = min(SCAN_UNROLL, n_chunks)

    def shifted(x, mu):
        prev, nxt = _shift_rows(x)
        return x + mu[0:1] * (prev - x) + mu[1:2] * (nxt - x)

    lane = _iota((seq, LANES), 1)
    lo = lane < RWKV_HS
    wa = shifted(wa_ref[...], mu_wa[...])
    gd_act = _sigmoid(shifted(gd_ref[...], mu_gd[...]))
    wd_in = jnp.where(lo, jnp.tanh(wa), 0.0)
    ad_in = jnp.where(lo, 0.0, wa)
    gates, bonuses, ln_gs, ln_bs = [], [], [], []
    for hp in range(n_pairs):
        cols = slice(hp * LANES, (hp + 1) * LANES)
        r = shifted(r_ref[:, cols], mu_r[:, cols])
        k = shifted(k_ref[:, cols], mu_k[:, cols])
        v = shifted(v_ref[:, cols], mu_v[:, cols])
        vec = vec_ref[:, cols]
        k_k, k_a, r_k, ln_g, ln_b = (vec[i:i + 1] for i in range(5))
        gates.append(_bdot(gd_act, g_up_ref[:, cols]))
        ln_gs.append(ln_g)
        ln_bs.append(ln_b)
        kk = k * k_k
        kk = kk * jnp.minimum(lax.rsqrt(_half_sum(kk * kk, lo)), 1e12)
        r_s[hp] = r
        v_s[hp] = v
        kk_s[hp] = kk
        bonus = jnp.zeros((seq, LANES), F32)
        for d in range(2):
            wpre = w0_ref[d:d + 1, cols] + _bdot(wd_in, wa_up_ref[d, :, cols])
            sp = jnp.maximum(-wpre, 0.0) + jnp.log(1.0 + jnp.exp(-jnp.abs(wpre)))
            lw_s[2 * hp + d] = -jnp.exp(-sp - 0.5)
            a = _sigmoid(a0_ref[d:d + 1, cols] + _bdot(ad_in, wa_up_ref[d, :, cols]))
            kd = k * (1.0 + (a - 1.0) * k_a)
            kd_s[2 * hp + d] = kd
            b_s[2 * hp + d] = kk * a
            bonus = bonus + _half_sum(r * kd * r_k, lo) * v
        bonuses.append(bonus)

    ci = _iota((CHUNK, CHUNK), 0)
    cj = _iota((CHUNK, CHUNK), 1)
    eye = jnp.where(ci == cj, 1.0, 0.0)
    blk8 = (ci // 8) == (cj // 8)
    merge_sizes = [8 * 2 ** n for n in range(int(math.log2(CHUNK // 8)))]
    merge_masks = [((ci // (2 * s)) == (cj // (2 * s))) & ((ci // s) != (cj // s)) for s in merge_sizes]
    before = (cj < ci, cj > ci)
    upto = (cj <= ci, cj >= ci)
    tri = tuple(jnp.where(m, 1.0, 0.0).astype(BF16) for m in upto)
    clane = _iota((CHUNK, LANES), 1)
    head_lanes = (clane < RWKV_HS, clane >= RWKV_HS)
    si = _iota((LANES, LANES), 0)
    sj = _iota((LANES, LANES), 1)
    same_head = (si < RWKV_HS) == (sj < RWKV_HS)
    diag = si == sj
    row_h0 = _iota((LANES, CHUNK), 0) < RWKV_HS

    def chunk_group(g):
        items = []
        for hp in range(n_pairs):
            items += [(2 * hp, g * unroll + u) for u in range(unroll)]
            items += [(2 * hp + 1, n_chunks - 1 - (g * unroll + u)) for u in range(unroll)]
        n_it = len(items)
        scans = [sc for sc, _ in items]
        dirs = [sc % 2 for sc in scans]
        rows = [pl.ds(pl.multiple_of(c * CHUNK, CHUNK), CHUNK) for _, c in items]
        r_c = [r_s[sc // 2, rw, :] for sc, rw in zip(scans, rows)]
        v_c = [v_s[sc // 2, rw, :] for sc, rw in zip(scans, rows)]
        kk_c = [kk_s[sc // 2, rw, :] for sc, rw in zip(scans, rows)]
        lw_c = [lw_s[sc, rw, :] for sc, rw in zip(scans, rows)]
        kd_c = [kd_s[sc, rw, :] for sc, rw in zip(scans, rows)]
        b_c = [b_s[sc, rw, :] for sc, rw in zip(scans, rows)]
        cum = [_dot_exact_lhs(tri[d], lw) for d, lw in zip(dirs, lw_c)]
        tot = [cm[CHUNK - 1:CHUNK, :] if d == 0 else cm[0:1, :] for d, cm in zip(dirs, cum)]
        e_neg = [jnp.exp(-cm) for cm in cum]
        al = [jnp.exp(cm - lw) * kk for cm, lw, kk in zip(cum, lw_c, kk_c)]
        be = [b * e for b, e in zip(b_c, e_neg)]
        ka = [kd * e for kd, e in zip(kd_c, e_neg)]
        rh = [r * jnp.exp(cm) for r, cm in zip(r_c, cum)]
        wc = [jnp.exp(tt) for tt in tot]
        combos = [(i, h) for i in range(n_it) for h in range(2)]
        x2 = [jnp.concatenate([a, r], axis=0) for a, r in zip(al, rh)]
        heads_of = lambda m_t: jnp.concatenate([jnp.where(row_h0, m_t, 0.0), jnp.where(row_h0, 0.0, m_t)], axis=1)
        xb2 = [_bdot(x, heads_of(b.T)) for x, b in zip(x2, be)]
        xk2 = [_bdot(x, heads_of(k_.T)) for x, k_ in zip(x2, ka)]
        xb = [xb2[i][:, h * CHUNK:(h + 1) * CHUNK] for i, h in combos]
        xk = [xk2[i][:, h * CHUNK:(h + 1) * CHUNK] for i, h in combos]
        n_mat = [jnp.where(before[dirs[i]], -x[0:CHUNK], 0.0) for x, (i, _) in zip(xb, combos)]
        g_mat = [jnp.where(before[dirs[i]], x[0:CHUNK], 0.0) for x, (i, _) in zip(xk, combos)]
        m2 = [jnp.where(upto[dirs[i]], x[CHUNK:], 0.0) for x, (i, _) in zip(xb, combos)]
        m1 = [jnp.where(upto[dirs[i]], x[CHUNK:], 0.0) for x, (i, _) in zip(xk, combos)]
        nd = [jnp.where(blk8, n, 0.0) for n in n_mat]
        t = [eye + x for x in nd]
        p = [_bdot(x, x) for x in nd]
        t = [a + _bdot(a, b) for a, b in zip(t, p)]
        p = [_bdot(x, x) for x in p]
        t = [a + _bdot(a, b) for a, b in zip(t, p)]
        for msk in merge_masks:
            q = [_bdot(jnp.where(msk, n, 0.0), a) for n, a in zip(n_mat, t)]
            t = [a + _bdot(a, b) for a, b in zip(t, q)]
        gv = [_bdot(g_, v_c[i]) for g_, (i, _) in zip(g_mat, combos)]
        z = [_bdot(t_, jnp.concatenate([al[i], g_], axis=1)) for t_, g_, (i, _) in zip(t, gv, combos)]
        yv_h = [_bdot(m, v_c[i]) for m, (i, _) in zip(m1, combos)]
        pick = lambda a, b: jnp.where(head_lanes[0], a, b)
        alp = [pick(z[2 * i][:, :LANES], z[2 * i + 1][:, :LANES]) for i in range(n_it)]
        uv = [pick(z[2 * i][:, LANES:], z[2 * i + 1][:, LANES:]) for i in range(n_it)]
        yv = [pick(yv_h[2 * i], yv_h[2 * i + 1]) for i in range(n_it)]
        be_t = [(b * w).T for b, w in zip(be, wc)]
        ka_t = [(k_ * w).T for k_, w in zip(ka, wc)]
        pz = [_bdot(bt, jnp.concatenate([a, u_], axis=1)) for bt, a, u_ in zip(be_t, alp, uv)]
        kv = [_bdot(kt, v_) for kt, v_ in zip(ka_t, v_c)]
        a_mat = [jnp.where(same_head, jnp.where(diag, w, 0.0) - pp[:, :LANES], 0.0) for w, pp in zip(wc, pz)]
        b_mat = [jnp.where(same_head, k_ - pp[:, LANES:], 0.0) for k_, pp in zip(kv, pz)]
        for i, sc in enumerate(scans):
            st = st_s[sc]
            xs = _bdot(jnp.concatenate([alp[i], rh[i]], axis=0), st)
            u = uv[i] + xs[0:CHUNK]
            m2u = pick(_bdot(m2[2 * i], u), _bdot(m2[2 * i + 1], u))
            y_s[sc, rows[i], :] = xs[CHUNK:] + yv[i] - m2u
            st_s[sc] = _bdot(a_mat[i], st) + b_mat[i]

    def body(g, carry):
        chunk_group(g)
        return carry

    for hp in range(n_pairs):
        if has_s0:
            st_s[2 * hp] = s0f_ref[0, hp]
            st_s[2 * hp + 1] = s0b_ref[0, hp]
        else:
            st_s[2 * hp] = jnp.zeros((LANES, LANES), F32)
            st_s[2 * hp + 1] = jnp.zeros((LANES, LANES), F32)
    lax.fori_loop(0, n_chunks // unroll, body, 0)
    for hp in range(n_pairs):
        sf_ref[0, hp] = st_s[2 * hp]
        sb_ref[0, hp] = st_s[2 * hp + 1]
        y = (y_s[2 * hp] + y_s[2 * hp + 1]) + bonuses[hp]
        mean = _half_sum(y, lo) * (1.0 / RWKV_HS)
        yc = y - mean
        var = _half_sum(yc * yc, lo) * (1.0 / RWKV_HS)
        o_ref[:, hp * LANES:(hp + 1) * LANES] = ((yc * lax.rsqrt(var + GN_EPS)) * ln_gs[hp] + ln_bs[hp]) * gates[hp]


def _rwkv(p, seq, n_pairs, li, mu, w0, a0, wa_up, g_up, vecs, s0=None):
    t = p.shape[0]
    nb = t // seq
    width = n_pairs * LANES
    n_grp = RWKV_WIDTH // width
    base = (2 * DIFF_QW + DIFF_WIDTH) // width
    tail = (2 * DIFF_QW + DIFF_WIDTH + 3 * RWKV_WIDTH) // LANES
    col = lambda blk: pl.BlockSpec((seq, width), lambda b, hp: (b, blk(hp)))
    mu_col = lambda blk: pl.BlockSpec((None, 2, width), lambda b, hp: (li, 0, blk(hp)))
    lane_blk = lambda k: pl.BlockSpec((seq, LANES), lambda b, hp: (b, tail + k))
    mu_lane_blk = lambda k: pl.BlockSpec((None, 2, LANES), lambda b, hp: (li, 0, 3 * RWKV_WIDTH // LANES + k))
    in_specs = [
        col(lambda hp: base + hp), col(lambda hp: base + n_grp + hp), col(lambda hp: base + 2 * n_grp + hp),
        lane_blk(0), lane_blk(1),
        mu_col(lambda hp: hp), mu_col(lambda hp: n_grp + hp), mu_col(lambda hp: 2 * n_grp + hp),
        mu_lane_blk(0), mu_lane_blk(1),
        pl.BlockSpec((None, 2, width), lambda b, hp: (li, 0, hp)),
        pl.BlockSpec((None, 2, width), lambda b, hp: (li, 0, hp)),
        pl.BlockSpec((None, 2, LANES, width), lambda b, hp: (li, 0, 0, hp)),
        pl.BlockSpec((None, LANES, width), lambda b, hp: (li, 0, hp)),
        pl.BlockSpec((None, 8, width), lambda b, hp: (li, 0, hp)),
    ]
    args = [p] * 5 + [mu] * 5 + [w0, a0, wa_up, g_up, vecs]
    st_spec = pl.BlockSpec((1, n_pairs, LANES, LANES), lambda b, hp: (b, hp, 0, 0))
    if s0 is not None:
        in_specs += [st_spec, st_spec]
        args += list(s0)
    st_shape = jax.ShapeDtypeStruct((nb, RWKV_HEADS // 2, LANES, LANES), F32)
    return pl.pallas_call(
        functools.partial(_rwkv_kernel, s0 is not None),
        out_shape=(jax.ShapeDtypeStruct((t, RWKV_WIDTH), F32), st_shape, st_shape),
        grid=(nb, n_grp),
        in_specs=in_specs,
        out_specs=(pl.BlockSpec((seq, width), lambda b, hp: (b, hp)), st_spec, st_spec),
        scratch_shapes=[pltpu.VMEM((n_pairs, seq, LANES), F32)] * 3 + [pltpu.VMEM((2 * n_pairs, seq, LANES), F32)] * 4
        + [pltpu.VMEM((2 * n_pairs, LANES, LANES), F32)],
        compiler_params=_params("parallel", "parallel"),
        name="rwkv",
    )(*args)


def _pair_states_in(s):
    nb = s.shape[0]
    st = jnp.swapaxes(s.astype(F32), -1, -2).reshape(nb, RWKV_HEADS // 2, 2, RWKV_HS, RWKV_HS)
    z = jnp.zeros_like(st[:, :, 0])
    top = jnp.concatenate([st[:, :, 0], z], axis=-1)
    bot = jnp.concatenate([z, st[:, :, 1]], axis=-1)
    return jnp.concatenate([top, bot], axis=-2)


def _pair_states_out(st):
    nb = st.shape[0]
    h0 = st[:, :, :RWKV_HS, :RWKV_HS]
    h1 = st[:, :, RWKV_HS:, RWKV_HS:]
    s = jnp.stack([h0, h1], axis=2).reshape(nb, RWKV_HEADS, RWKV_HS, RWKV_HS)
    return jnp.swapaxes(s, -1, -2)


def _dft_kernel(f_ref, ft_ref):
    n = f_ref.shape[1]
    shape = (2 * n, n)
    fr = _iota(shape, 0)
    tau = _iota(shape, 1)
    is_sin = fr >= n
    kf = jnp.where(is_sin, fr - n, fr)
    ang = ((kf * tau) & (2 * n - 1)).astype(F32) * (math.pi / n)
    val = jnp.where(is_sin, -jnp.sin(ang), jnp.cos(ang))
    nyq = jnp.where((tau & 1) == 0, 1.0, -1.0)
    val = jnp.where(is_sin & (kf == 0), nyq, val)
    f_ref[...] = val.astype(BF16)
    ft_ref[...] = val.T.astype(BF16)


def _dft_mats(n):
    return pl.pallas_call(
        _dft_kernel,
        out_shape=(jax.ShapeDtypeStruct((2 * n, n), BF16), jax.ShapeDtypeStruct((n, 2 * n), BF16)),
        name="dft_mats",
    )()


def _filter_kernel(seq, w0t_ref, w0c_ref, w0s_ref, b0_ref, w1_ref, b1_ref, w2_ref, b2_ref, fr_ref,
                   w3a_ref, w3b_ref, f_ref, o_ref, prev_scr):
    cb = f_ref.shape[1]
    n_blk = seq // cb
    e = pl.program_id(0)
    first_lag = (e - n_blk) * cb
    tap_row = jnp.abs(first_lag + _iota((1, cb), 1)).astype(F32)
    pos_row = tap_row * (1.0 / (seq - 1))
    band = _iota((BANDS, 1), 0).astype(F32)
    freq = 1e-4 + band * ((BANDS - 1 - 1e-4) / (BANDS - 1))
    arg = freq * (tap_row * (2.0 * math.pi / seq))
    fr = fr_ref[...]
    pre = (w0t_ref[...] * pos_row + _dot_f32(w0c_ref[...], jnp.cos(arg)) + _dot_f32(w0s_ref[...], -jnp.sin(arg))
           + b0_ref[...])
    hdn = jnp.sin(fr[:, 0:1] * pre)
    hdn = jnp.sin(fr[:, 1:2] * (_dot_f32(w1_ref[...], hdn) + b1_ref[...]))
    hdn = jnp.sin(fr[:, 2:3] * (_dot_f32(w2_ref[...], hdn) + b2_ref[...]))
    hdn = hdn.T
    tap = jnp.abs(first_lag + _iota((cb, 1), 0))
    pos = tap.astype(F32) * (1.0 / (seq - 1))
    chan = _iota((1, D_MODEL), 1).astype(F32)
    delta = jnp.abs(MIN_DECAY + chan * ((MAX_DECAY - MIN_DECAY) / (D_MODEL - 1)))
    window = jnp.where(tap < seq, jnp.exp(-pos * delta), 0.0)
    row = _iota((2 * cb, 1), 0)
    sign = jnp.where((row & 1) == 1, -1.0, 1.0)
    for o, w3_ref in enumerate((w3a_ref, w3b_ref)):
        cols = slice(o * D_MODEL, (o + 1) * D_MODEL)
        taps = (_dot_f32(hdn, w3_ref[...]) * window).astype(BF16)
        spec = jnp.dot(f_ref[...], taps, preferred_element_type=F32)

        @pl.when(e > 0)
        def _():
            o_ref[0, :, cols] = spec + sign * prev_scr[:, cols]

        prev_scr[:, cols] = spec - jnp.where(row <= cb, taps[0:1, :].astype(F32), 0.0)


def _hyena_filters(seq, li, fmat, small, w3):
    cb = fmat.shape[1]
    n_blk = seq // cb
    layer_slice = lambda a: pl.BlockSpec((None,) + a.shape[1:], lambda e: (li, 0, 0))
    w3_spec = lambda o: pl.BlockSpec((None, FILTER_WIDTH, D_MODEL),
                                     lambda e: (li, 0, 2 * o + jnp.where(e < n_blk, 1, 0)))
    return pl.pallas_call(
        functools.partial(_filter_kernel, seq),
        out_shape=jax.ShapeDtypeStruct((2 * n_blk - 1, 2 * cb, 2 * D_MODEL), F32),
        grid=(2 * n_blk,),
        in_specs=[layer_slice(a) for a in small] + [w3_spec(0), w3_spec(1),
                                                    pl.BlockSpec((2 * cb, cb), lambda e: (0, 0))],
        out_specs=pl.BlockSpec((1, 2 * cb, 2 * D_MODEL), lambda e: (jnp.maximum(e - 1, 0), 0, 0)),
        scratch_shapes=[pltpu.VMEM((2 * cb, 2 * D_MODEL), F32)],
        compiler_params=_params("arbitrary"),
        name="hyena_filters",
    )(*small, w3, w3, fmat)


def _short_conv(z, w, b):
    prev, nxt = _shift_rows(z)
    return prev * w[0:1] + z * w[1:2] + nxt * w[2:3] + b


def _conv_kernel(conv_v, n_blk, v_ref, x_ref, wv_ref, bv_ref, wx_ref, bx_ref, f_ref, ft_ref, kc_ref, bias_ref,
                 o_ref, uf_scr, yf_scr):
    cb = f_ref.shape[1]
    v = v_ref[...]
    if conv_v:
        v = _short_conv(v, wv_ref[...], bv_ref[...])
    fmat = f_ref[...]
    for j in range(n_blk):
        uf_scr[j] = jnp.dot(fmat, v[j * cb:(j + 1) * cb].astype(BF16), preferred_element_type=F32)

    def mix(r0, first):
        re, im = pl.ds(r0, MIX_ROWS), pl.ds(cb + r0, MIX_ROWS)
        real_pair = (_iota((MIX_ROWS, 1), 0) == 0) if first else None
        for i in range(n_blk):
            acc_r = acc_i = None
            for j in range(n_blk):
                d = i - j + n_blk - 1
                ur, ui = uf_scr[j, re, :], uf_scr[j, im, :]
                kr, ki = kc_ref[d, re, :], kc_ref[d, im, :]
                uiki = ui * ki
                if first:
                    t_r = ur * kr - jnp.where(real_pair, 0.0, uiki)
                    t_i = jnp.where(real_pair, uiki, ur * ki + ui * kr)
                else:
                    t_r = ur * kr - uiki
                    t_i = ur * ki + ui * kr
                acc_r = t_r if acc_r is None else acc_r + t_r
                acc_i = t_i if acc_i is None else acc_i + t_i
            scale = jnp.where(real_pair, 0.5 / cb, 1.0 / cb) if first else 1.0 / cb
            yf_scr[i, re, :] = acc_r * scale
            yf_scr[i, im, :] = acc_i * scale

    mix(0, True)

    def body(r, carry):
        mix(pl.multiple_of(r * MIX_ROWS, MIX_ROWS), False)
        return carry

    lax.fori_loop(1, cb // MIX_ROWS, body, 0)

    x = _short_conv(x_ref[...], wx_ref[...], bx_ref[...])
    ftm = ft_ref[...]
    bias = bias_ref[...]
    for i in range(n_blk):
        rows = slice(i * cb, (i + 1) * cb)
        y = jnp.dot(ftm, yf_scr[i].astype(BF16), preferred_element_type=F32)
        o_ref[rows, :] = x[rows] * (y + v[rows] * bias)


def _hyena_conv(v_arr, v_blk0, x_arr, x_blk0, conv_v, seq, li, order, cw, cwb, fmat, fmat_t, kc, biases, width):
    t = v_arr.shape[0]
    cb = fmat.shape[1]
    nb, ncb, n_blk = t // seq, D_MODEL // width, seq // cb
    wv_blk0 = v_blk0 if conv_v else 0
    kc_blk0 = order * ncb
    return pl.pallas_call(
        functools.partial(_conv_kernel, conv_v, n_blk),
        out_shape=jax.ShapeDtypeStruct((t, D_MODEL), F32),
        grid=(nb, ncb),
        in_specs=[
            pl.BlockSpec((seq, width), lambda b, c: (b, v_blk0 + c)),
            pl.BlockSpec((seq, width), lambda b, c: (b, x_blk0 + c)),
            pl.BlockSpec((None, 3, width), lambda b, c: (li, 0, wv_blk0 + c)),
            pl.BlockSpec((None, 1, width), lambda b, c: (li, 0, wv_blk0 + c)),
            pl.BlockSpec((None, 3, width), lambda b, c: (li, 0, x_blk0 + c)),
            pl.BlockSpec((None, 1, width), lambda b, c: (li, 0, x_blk0 + c)),
            pl.BlockSpec((2 * cb, cb), lambda b, c: (0, 0)),
            pl.BlockSpec((cb, 2 * cb), lambda b, c: (0, 0)),
            pl.BlockSpec((2 * n_blk - 1, 2 * cb, width), lambda b, c: (0, 0, kc_blk0 + c)),
            pl.BlockSpec((None, 1, width), lambda b, c: (li * 2 + order, 0, c)),
        ],
        out_specs=pl.BlockSpec((seq, width), lambda b, c: (b, c)),
        scratch_shapes=[pltpu.VMEM((n_blk, 2 * cb, width), F32), pltpu.VMEM((n_blk, 2 * cb, width), F32)],
        compiler_params=_params("parallel", "parallel"),
        name="hyena_conv",
    )(v_arr, x_arr, cw, cwb, cw, cwb, fmat, fmat_t, kc, biases)


def _axial_rope_tables(seq):
    n_rows = seq // GRID_W
    row = np.repeat(np.arange(n_rows, dtype=np.float64), GRID_W)
    col = np.tile(np.arange(GRID_W, dtype=np.float64), n_rows)
    inv = ROPE_BASE ** (-np.arange(ROPE_PAIRS, dtype=np.float64) / ROPE_PAIRS)
    ar = row[:, None] * inv[None]
    ac = col[:, None] * inv[None]
    ang = np.concatenate([ar, ar, ac, ac], axis=-1)
    reps = LANES // DIFF_QK
    return (jnp.asarray(np.tile(np.cos(ang), (1, reps)), F32), jnp.asarray(np.tile(np.sin(ang), (1, reps)), F32))


def _pair_lanes(a, seq_axis):
    nb = a.shape[0]
    if a.ndim == 5:
        a = a.reshape(nb, DIFF_HEADS // 2, 2, 2, a.shape[3], DIFF_QK).transpose(0, 1, 4, 2, 3, 5)
    else:
        a = a.reshape(nb, DIFF_HEADS // 2, 2, a.shape[2], DIFF_V).transpose(0, 1, 3, 2, 4)
    return a.reshape(nb, DIFF_HEADS // 2, a.shape[2], LANES)


def kernel(x_prompt, x_sample, cache_diff_k, cache_diff_v, state_rwkv_fwd, state_rwkv_bwd, c, c_ctx, ada_w, ada_b, norm1_g, norm2_g, ffn_w1, ffn_w3, ffn_w2, final_g, ab_w_in, ab_w_out, diff_lambda, diff_subln_g, rwkv_mu, rwkv_w0, rwkv_w_up, rwkv_a0, rwkv_a_up, rwkv_g_up, rwkv_k_k, rwkv_k_a, rwkv_r_k, rwkv_ln_g, rwkv_ln_b, hy_w_in, hy_b_in, hy_conv_w, hy_conv_b, hy_f_w0, hy_f_b0, hy_f_w1, hy_f_b1, hy_f_w2, hy_f_b2, hy_f_w3, hy_f_freq, hy_bias, hy_w_out, hy_b_out):
    n_ctx_seqs, ctx_len, _ = x_prompt.shape
    n_lat_seqs, lat_len, _ = x_sample.shape
    tm = 1024

    cond = jnp.zeros((N_MOD_ROWS, D_MODEL), F32).at[0].set(c_ctx).at[1:1 + n_lat_seqs].set(c)
    mods = _ada_all(cond, ada_w, ada_b).reshape(DEPTH * N_MOD_ROWS * 6, 1, D_MODEL)

    groups = [
        dict(x=x_prompt.reshape(-1, D_MODEL), seq=ctx_len, rows=(0, n_ctx_seqs * ctx_len), tq=ctx_len, pairs=4,
             scan_pairs=2, conv_width=1024),
        dict(x=x_sample.reshape(-1, D_MODEL), seq=lat_len, rows=(1, lat_len), tq=512, pairs=1, scan_pairs=1,
             conv_width=256),
    ]
    rope = _axial_rope_tables(lat_len)
    dft = {cb: _dft_mats(cb) for cb in {min(CONV_BLOCK_MAX, g["seq"]) for g in groups}}
    n_ab, n_c = ab_w_in.shape[0], hy_w_in.shape[0]

    gains1 = norm1_g.reshape(DEPTH, 1, D_MODEL)
    gains2 = norm2_g.reshape(DEPTH, 1, D_MODEL)
    g2s = jnp.tile(diff_subln_g, (1, 2)).reshape(n_ab, 1, LANES)
    wa_up = jnp.concatenate([rwkv_w_up, rwkv_a_up], axis=2)
    vec_rows = [rwkv_k_k, rwkv_k_a, rwkv_r_k.reshape(n_ab, RWKV_WIDTH), rwkv_ln_g, rwkv_ln_b]
    vecs = jnp.stack(vec_rows + [jnp.zeros_like(rwkv_k_k)] * (8 - len(vec_rows)), axis=1)
    w0_t = jnp.swapaxes(hy_f_w0, 1, 2)
    filt_small = [w0_t[:, :, 0:1], w0_t[:, :, 1:1 + BANDS], w0_t[:, :, 1 + BANDS:], hy_f_b0[:, :, None],
                  jnp.swapaxes(hy_f_w1, 1, 2), hy_f_b1[:, :, None], jnp.swapaxes(hy_f_w2, 1, 2), hy_f_b2[:, :, None],
                  jnp.swapaxes(hy_f_freq, 1, 2)]
    hy_b_in_s = hy_b_in.reshape(n_c, 1, -1)
    hy_b_out_s = hy_b_out.reshape(n_c, 1, D_MODEL)
    conv_b = hy_conv_b.reshape(n_c, 1, -1)
    conv_bias = hy_bias.reshape(n_c * 2, 1, D_MODEL)
    new_k, new_v, new_sf, new_sb = [], [], [], []

    for l in range(DEPTH):
        i = l // 2
        for gi, g in enumerate(groups):
            x, seq, row = g["x"], g["seq"], g["rows"]
            if l % 2 == 0:
                p = _proj_in(x, gains1, mods, l, row, ab_w_in, i, None, tm, AB_IN // 2)
                if gi == 0:
                    att, k_own, v_own = _attention(p, seq, g["tq"], g["pairs"], _lambda_init(l), diff_lambda, g2s, i,
                                                   emit_kv=True)
                    new_k.append(k_own)
                    new_v.append(v_own)
                    mix, sf, sb = _rwkv(p, seq, g["scan_pairs"], i, rwkv_mu, rwkv_w0, rwkv_a0, wa_up, rwkv_g_up, vecs)
                    new_sf.append(_pair_states_out(sf))
                    new_sb.append(_pair_states_out(sb))
                else:
                    ctx = (_pair_lanes(cache_diff_k[:, i], 3), _pair_lanes(cache_diff_v[:, i], 2))
                    att, = _attention(p, seq, g["tq"], g["pairs"], _lambda_init(l), diff_lambda, g2s, i,
                                      ctx=ctx, rope=rope)
                    s0 = (_pair_states_in(state_rwkv_fwd[:, i]), _pair_states_in(state_rwkv_bwd[:, i]))
                    mix, _, _ = _rwkv(p, seq, g["scan_pairs"], i, rwkv_mu, rwkv_w0, rwkv_a0, wa_up, rwkv_g_up, vecs, s0=s0)
                x = _proj_out(x, mods, l, row, [att, mix], ab_w_out, i, None, tm)
            else:
                fmat, fmat_t = dft[min(CONV_BLOCK_MAX, seq)]
                z = _proj_in(x, gains1, mods, l, row, hy_w_in, i, hy_b_in_s, tm, 1536)
                kc = _hyena_filters(seq, i, fmat, filt_small, hy_f_w3)
                width = g["conv_width"]
                ncb = D_MODEL // width
                u = _hyena_conv(z, 2 * ncb, z, 0, True, seq, i, 0, hy_conv_w, conv_b, fmat, fmat_t, kc, conv_bias, width)
                u = _hyena_conv(u, 0, z, ncb, False, seq, i, 1, hy_conv_w, conv_b, fmat, fmat_t, kc, conv_bias, width)
                x = _proj_out(x, mods, l, row, [u], hy_w_out, i, hy_b_out_s, tm)
            g["x"] = _ffn(x, gains2, mods, l, row, ffn_w1, ffn_w3, ffn_w2, final_g, l == DEPTH - 1, 2 * tm, 256)

    y_prompt = groups[0]["x"].reshape(x_prompt.shape)
    y_sample = groups[1]["x"].reshape(x_sample.shape)
    return (y_prompt, y_sample, jnp.stack(new_k, axis=1), jnp.stack(new_v, axis=1),
            jnp.stack(new_sf, axis=1), jnp.stack(new_sb, axis=1))
```

```python
import functools
import math

import jax
import jax.numpy as jnp
import numpy as np
from jax import lax
from jax.experimental import pallas as pl
from jax.experimental.pallas import tpu as pltpu

F32 = jnp.float32
BF16 = jnp.bfloat16

D_MODEL = 1024
DEPTH = 4
GRID_W = 64
DIFF_HEADS = 8
DIFF_QK = 32
DIFF_V = 64
DIFF_QW = DIFF_HEADS * 2 * DIFF_QK
DIFF_WIDTH = DIFF_HEADS * DIFF_V
ROPE_PAIRS = DIFF_QK // 4
ROPE_BASE = 10000.0
RWKV_HEADS = 8
RWKV_HS = 64
RWKV_WIDTH = RWKV_HEADS * RWKV_HS
RWKV_IN = 3 * RWKV_WIDTH + 64 + 64 + 128
AB_IN = 2 * DIFF_QW + DIFF_WIDTH + RWKV_IN
BANDS = 16
FILTER_WIDTH = 64
MAX_DECAY = math.log(1e-2) / 0.3
MIN_DECAY = math.log(1e-2) / 1.5
D_FF = 2816
RMS_EPS = 1e-6
GN_EPS = 64e-5
HEAD_RMS_EPS = 1e-5
LOG2_E = math.log2(math.e)

LANES = 128
CHUNK = 128
SCAN_UNROLL = 4
CONV_BLOCK_MAX = 512
MIX_ROWS = 8
VMEM_LIMIT = 56 * 1024 * 1024

N_MOD_ROWS = 8


def _lambda_init(l):
    return 0.8 - 0.6 * math.exp(-0.3 * l)


def _params(*sem):
    return pltpu.CompilerParams(dimension_semantics=sem, vmem_limit_bytes=VMEM_LIMIT)


def _bdot(a, b):
    return jnp.dot(a.astype(BF16), b.astype(BF16), preferred_element_type=F32)


def _bdot_nt(a, b):
    return lax.dot_general(a.astype(BF16), b.astype(BF16), (((1,), (1,)), ((), ())),
                           preferred_element_type=F32)


def _split3(x):
    h1 = x.astype(BF16)
    r1 = x - h1.astype(F32)
    h2 = r1.astype(BF16)
    h3 = (r1 - h2.astype(F32)).astype(BF16)
    return h1, h2, h3


def _dot_f32(a, b):
    a1, a2, a3 = _split3(a)
    b1, b2, b3 = _split3(b)
    d = lambda x, y: jnp.dot(x, y, preferred_element_type=F32)
    return (d(a1, b1) + (d(a1, b2) + d(a2, b1))) + ((d(a1, b3) + d(a3, b1)) + d(a2, b2))


def _dot_exact_lhs(a_bf16, b):
    b1, b2, b3 = _split3(b)
    d = lambda y: jnp.dot(a_bf16, y, preferred_element_type=F32)
    return d(b1) + (d(b2) + d(b3))


def _sigmoid(x):
    return 0.5 * (jnp.tanh(0.5 * x) + 1.0)


def _iota(shape, dim):
    return lax.broadcasted_iota(jnp.int32, shape, dim)


def _shift_rows(x):
    n = x.shape[0]
    row = _iota(x.shape, 0)
    prev = jnp.where(row == 0, 0.0, pltpu.roll(x, 1, 0))
    nxt = jnp.where(row == n - 1, 0.0, pltpu.roll(x, n - 1, 0))
    return prev, nxt


def _half_sum(x, lo_mask):
    s_lo = jnp.sum(jnp.where(lo_mask, x, 0.0), axis=-1, keepdims=True)
    s_hi = jnp.sum(jnp.where(lo_mask, 0.0, x), axis=-1, keepdims=True)
    return jnp.where(lo_mask, s_lo, s_hi)


def _ada_kernel(c_ref, w_ref, b_ref, o_ref):
    c = c_ref[...]
    o_ref[0] = _bdot(c * _sigmoid(c), w_ref[0]) + b_ref[0]


def _ada_all(cond, ada_w, ada_b):
    tn = 1536
    n_out = 6 * D_MODEL
    return pl.pallas_call(
        _ada_kernel,
        out_shape=jax.ShapeDtypeStruct((DEPTH, N_MOD_ROWS, n_out), F32),
        grid=(DEPTH, n_out // tn),
        in_specs=[
            pl.BlockSpec((N_MOD_ROWS, D_MODEL), lambda l, j: (0, 0)),
            pl.BlockSpec((1, D_MODEL, tn), lambda l, j: (l, 0, j)),
            pl.BlockSpec((1, 1, tn), lambda l, j: (l, 0, j)),
        ],
        out_specs=pl.BlockSpec((1, N_MOD_ROWS, tn), lambda l, j: (l, 0, j)),
        compiler_params=_params("parallel", "parallel"),
        name="ada",
    )(cond, ada_w, ada_b.reshape(DEPTH, 1, n_out))


def _mod_spec(layer, chunk, cond_rows, tm, axis=0):
    first, span = cond_rows

    def index_map(*idx):
        return ((layer * N_MOD_ROWS + first + (idx[axis] * tm) // span) * 6 + chunk, 0, 0)
    return pl.BlockSpec((1, 1, D_MODEL), index_map)


def _normed(x, g, sc, sh):
    ms = jnp.mean(x * x, axis=-1, keepdims=True)
    return (x * lax.rsqrt(ms + RMS_EPS)) * g * (1.0 + sc) + sh


def _proj_in_kernel(has_bias, x_ref, g_ref, sc_ref, sh_ref, w_ref, *refs):
    o_ref = refs[-1]
    h = _normed(x_ref[...], g_ref[...], sc_ref[0], sh_ref[0])
    y = _bdot(h, w_ref[...])
    o_ref[...] = y + refs[0][...] if has_bias else y


def _proj_in(x, gains, mods, layer, cond_rows, ws, wi, bs, tm, tn):
    t, n_out = x.shape[0], ws.shape[2]
    in_specs = [
        pl.BlockSpec((tm, D_MODEL), lambda j, i: (i, 0)),
        pl.BlockSpec((None, 1, D_MODEL), lambda j, i: (layer, 0, 0)),
        _mod_spec(layer, 1, cond_rows, tm, axis=1),
        _mod_spec(layer, 0, cond_rows, tm, axis=1),
        pl.BlockSpec((None, D_MODEL, tn), lambda j, i: (wi, 0, j)),
    ]
    args = [x, gains, mods, mods, ws]
    if bs is not None:
        in_specs.append(pl.BlockSpec((None, 1, tn), lambda j, i: (wi, 0, j)))
        args.append(bs)
    return pl.pallas_call(
        functools.partial(_proj_in_kernel, bs is not None),
        out_shape=jax.ShapeDtypeStruct((t, n_out), F32),
        grid=(n_out // tn, t // tm),
        in_specs=in_specs,
        out_specs=pl.BlockSpec((tm, tn), lambda j, i: (i, j)),
        compiler_params=_params("parallel", "parallel"),
        name="proj_in",
    )(*args)


def _proj_out_kernel(n_u, has_bias, x_ref, gt_ref, *refs):
    u_refs, w_ref, o_ref = refs[:n_u], refs[n_u], refs[-1]
    acc = None
    off = 0
    for u_ref in u_refs:
        k = u_ref.shape[1]
        part = _bdot(u_ref[...], w_ref[off:off + k, :])
        acc = part if acc is None else acc + part
        off += k
    if has_bias:
        acc = acc + refs[n_u + 1][...]
    o_ref[...] = x_ref[...] + gt_ref[0] * acc


def _proj_out(x, mods, layer, cond_rows, us, ws, wi, bs, tm):
    t = x.shape[0]
    in_specs = [
        pl.BlockSpec((tm, D_MODEL), lambda i: (i, 0)),
        _mod_spec(layer, 2, cond_rows, tm),
        *[pl.BlockSpec((tm, u.shape[1]), lambda i: (i, 0)) for u in us],
        pl.BlockSpec((None,) + ws.shape[1:], lambda i: (wi, 0, 0)),
    ]
    args = [x, mods, *us, ws]
    if bs is not None:
        in_specs.append(pl.BlockSpec((None, 1, D_MODEL), lambda i: (wi, 0, 0)))
        args.append(bs)
    return pl.pallas_call(
        functools.partial(_proj_out_kernel, len(us), bs is not None),
        out_shape=jax.ShapeDtypeStruct((t, D_MODEL), F32),
        grid=(t // tm,),
        in_specs=in_specs,
        out_specs=pl.BlockSpec((tm, D_MODEL), lambda i: (i, 0)),
        compiler_params=_params("parallel"),
        name="proj_out",
    )(*args)


def _ffn_kernel(final, x_ref, g_ref, sc_ref, sh_ref, gt_ref, w1_ref, w3_ref, w2_ref, fg_ref, o_ref, h_scr):
    k = pl.program_id(1)

    @pl.when(k == 0)
    def _():
        h_scr[...] = _normed(x_ref[...], g_ref[...], sc_ref[0], sh_ref[0]).astype(BF16)
        o_ref[...] = jnp.zeros_like(o_ref)

    h = h_scr[...]
    a1 = jnp.dot(h, w1_ref[...].astype(BF16), preferred_element_type=F32)
    a3 = jnp.dot(h, w3_ref[...].astype(BF16), preferred_element_type=F32)
    o_ref[...] += _bdot((a1 * _sigmoid(a1)) * a3, w2_ref[...])

    @pl.when(k == pl.num_programs(1) - 1)
    def _():
        y = x_ref[...] + gt_ref[0] * o_ref[...]
        if final:
            ms = jnp.mean(y * y, axis=-1, keepdims=True)
            y = (y * lax.rsqrt(ms + RMS_EPS)) * fg_ref[...]
        o_ref[...] = y


def _ffn(x, gains, mods, layer, cond_rows, w1s, w3s, w2s, final_g, final, tm, tk):
    t = x.shape[0]
    return pl.pallas_call(
        functools.partial(_ffn_kernel, final),
        out_shape=jax.ShapeDtypeStruct((t, D_MODEL), F32),
        grid=(t // tm, D_FF // tk),
        in_specs=[
            pl.BlockSpec((tm, D_MODEL), lambda i, k: (i, 0)),
            pl.BlockSpec((None, 1, D_MODEL), lambda i, k: (layer, 0, 0)),
            _mod_spec(layer, 4, cond_rows, tm),
            _mod_spec(layer, 3, cond_rows, tm),
            _mod_spec(layer, 5, cond_rows, tm),
            pl.BlockSpec((None, D_MODEL, tk), lambda i, k: (layer, 0, k)),
            pl.BlockSpec((None, D_MODEL, tk), lambda i, k: (layer, 0, k)),
            pl.BlockSpec((None, tk, D_MODEL), lambda i, k: (layer, k, 0)),
            pl.BlockSpec((1, D_MODEL), lambda i, k: (0, 0)),
        ],
        out_specs=pl.BlockSpec((tm, D_MODEL), lambda i, k: (i, 0)),
        scratch_shapes=[pltpu.VMEM((tm, D_MODEL), BF16)],
        compiler_params=_params("parallel", "arbitrary"),
        name="ffn",
    )(x, gains, mods, mods, mods, w1s, w3s, w2s, final_g.reshape(1, D_MODEL))


def _rope(x, cos, sin):
    lane = _iota(x.shape, 1)
    rot = jnp.where((lane % 16) < 8, -pltpu.roll(x, LANES - 8, 1), pltpu.roll(x, 8, 1))
    return x * cos + rot * sin


def _attn_kernel(n_ctx, use_rope, emit_kv, lam_init, *refs):
    it = iter(refs)
    q_ref, k_ref, v_ref = next(it), next(it), next(it)
    kc_ref = vc_ref = cq_ref = sq_ref = ck_ref = sk_ref = ko_ref = vo_ref = None
    if n_ctx:
        kc_ref, vc_ref = next(it), next(it)
    if use_rope:
        cq_ref, sq_ref, ck_ref, sk_ref = next(it), next(it), next(it), next(it)
    lam_ref, g_ref = next(it), next(it)
    if emit_kv:
        next(it), next(it)
    o_ref = next(it)
    if emit_kv:
        ko_ref, vo_ref = next(it), next(it)
    kall, vall = next(it), next(it)
    n_own = k_ref.shape[0]
    n_pairs = q_ref.shape[1] // LANES

    @pl.when(pl.program_id(2) == 0)
    def _():
        k = k_ref[...]
        v = v_ref[...]
        if emit_kv:
            for h in range(2 * n_pairs):
                for m in range(2):
                    c0 = (2 * h + m) * DIFF_QK
                    ko_ref[0, h, m] = k[:, c0:c0 + DIFF_QK]
                vo_ref[0, h] = v[:, h * DIFF_V:(h + 1) * DIFF_V]
        if use_rope:
            k = _rope(k, ck_ref[...], sk_ref[...])
        if n_ctx:
            kall[0:n_ctx, :] = kc_ref[0, 0].astype(BF16)
            vall[0:n_ctx, :] = vc_ref[0, 0].astype(BF16)
        kall[n_ctx:n_ctx + n_own, :] = k.astype(BF16)
        vall[n_ctx:n_ctx + n_own, :] = v.astype(BF16)

    lv = lam_ref[...]
    lam = (jnp.exp(jnp.sum(lv[0:1] * lv[1:2], axis=-1, keepdims=True))
           - jnp.exp(jnp.sum(lv[2:3] * lv[3:4], axis=-1, keepdims=True)) + lam_init)

    lane = _iota((q_ref.shape[0], LANES), 1)
    lo = lane < DIFF_V
    scale = DIFF_QK ** -0.5
    for hp in range(n_pairs):
        cols = slice(hp * LANES, (hp + 1) * LANES)
        q = q_ref[:, cols]
        if use_rope:
            q = _rope(q, cq_ref[...], sq_ref[...])
        ks, vs = kall[:, cols], vall[:, cols]
        outs = []
        for h in range(2):
            pv, inv = [], []
            for m in range(2):
                j = 2 * h + m
                qm = jnp.where((lane >= DIFF_QK * j) & (lane < DIFF_QK * (j + 1)), q, 0.0)
                s = _bdot_nt(qm, ks)
                e = jnp.exp2((s - jnp.max(s, axis=-1, keepdims=True)) * (scale * LOG2_E))
                inv.append(1.0 / jnp.sum(e, axis=-1, keepdims=True))
                pv.append(_bdot(e, vs))
            outs.append(pv[0] * inv[0] - pv[1] * (lam * inv[1]))
        o = jnp.where(lo, outs[0], outs[1])
        ms = _half_sum(o * o, lo) * (1.0 / DIFF_V)
        o_ref[:, cols] = (o * lax.rsqrt(ms + HEAD_RMS_EPS)) * g_ref[...] * (1.0 - lam_init)


def _attention(p, seq, tq, n_pairs, lam_init, lams, g2s, li, ctx=None, rope=None, kv_out=None):
    emit_kv = kv_out is not None
    assert rope is None or n_pairs == 1
    t = p.shape[0]
    nb, nq = t // seq, seq // tq
    n_ctx = 0 if ctx is None else ctx[0].shape[2]
    width = n_pairs * LANES
    k_blk, v_blk = DIFF_QW // width, 2 * DIFF_QW // width
    in_specs = [
        pl.BlockSpec((tq, width), lambda b, hp, qi: (b * nq + qi, hp)),
        pl.BlockSpec((seq, width), lambda b, hp, qi: (b, k_blk + hp)),
        pl.BlockSpec((seq, width), lambda b, hp, qi: (b, v_blk + hp)),
    ]
    args = [p, p, p]
    if ctx is not None:
        in_specs += [pl.BlockSpec((1, 1, n_ctx, width), lambda b, hp, qi: (b, hp, 0, 0))] * 2
        args += list(ctx)
    if rope is not None:
        in_specs += [pl.BlockSpec((tq, LANES), lambda b, hp, qi: (qi, 0))] * 2
        in_specs += [pl.BlockSpec((seq, width), lambda b, hp, qi: (0, 0))] * 2
        args += [rope[0], rope[1], rope[0], rope[1]]
    in_specs += [pl.BlockSpec((None, 4, DIFF_QK), lambda b, hp, qi: (li, 0, 0)),
                 pl.BlockSpec((None, 1, LANES), lambda b, hp, qi: (li, 0, 0))]
    args += [lams, g2s]
    out_shape = [jax.ShapeDtypeStruct((t, DIFF_WIDTH), F32)]
    out_specs = [pl.BlockSpec((tq, width), lambda b, hp, qi: (b * nq + qi, hp))]
    aliases = {}
    if emit_kv:
        aliases = {len(args): 1, len(args) + 1: 2}
        in_specs += [pl.BlockSpec(memory_space=pl.ANY)] * 2
        args += list(kv_out)
        out_shape += [jax.ShapeDtypeStruct(a.shape, a.dtype) for a in kv_out]
        out_specs += [pl.BlockSpec((1, None, 2 * n_pairs, 2, seq, DIFF_QK), lambda b, hp, qi: (b, li, hp, 0, 0, 0)),
                      pl.BlockSpec((1, None, 2 * n_pairs, seq, DIFF_V), lambda b, hp, qi: (b, li, hp, 0, 0))]
    return pl.pallas_call(
        functools.partial(_attn_kernel, n_ctx, rope is not None, emit_kv, lam_init),
        out_shape=out_shape,
        grid=(nb, DIFF_HEADS // 2 // n_pairs, nq),
        in_specs=in_specs,
        out_specs=out_specs,
        scratch_shapes=[pltpu.VMEM((n_ctx + seq, width), BF16), pltpu.VMEM((n_ctx + seq, width), BF16)],
        input_output_aliases=aliases,
        compiler_params=_params("parallel", "parallel", "arbitrary"),
        name="diff_attn",
    )(*args)


def _rwkv_kernel(has_s0, emit_states, *refs):
    it = iter(refs)
    r_ref, k_ref, v_ref, wa_ref, gd_ref = (next(it) for _ in range(5))
    mu_r, mu_k, mu_v, mu_wa, mu_gd = (next(it) for _ in range(5))
    w0_ref, a0_ref, wa_up_ref, g_up_ref, vec_ref = (next(it) for _ in range(5))
    s0f_ref = s0b_ref = sf_ref = sb_ref = None
    if has_s0:
        s0f_ref, s0b_ref = next(it), next(it)
    if emit_states:
        next(it), next(it)
    o_ref = next(it)
    if emit_states:
        sf_ref, sb_ref = next(it), next(it)
    r_s, v_s, kk_s = next(it), next(it), next(it)
    lw_s, kd_s, b_s, y_s = next(it), next(it), next(it), next(it)
    st_s = next(it)

    seq = r_ref.shape[0]
    n_pairs = r_ref.shape[1] // LANES
    n_chunks = seq // CHUNK
    unroll = min(SCAN_UNROLL, n_chunks)

    def shifted(x, mu):
        prev, nxt = _shift_rows(x)
        return x + mu[0:1] * (prev - x) + mu[1:2] * (nxt - x)

    lane = _iota((seq, LANES), 1)
    lo = lane < RWKV_HS
    wa = shifted(wa_ref[...], mu_wa[...])
    gd_act = _sigmoid(shifted(gd_ref[...], mu_gd[...]))
    wd_in = jnp.where(lo, jnp.tanh(wa), 0.0)
    ad_in = jnp.where(lo, 0.0, wa)
    gates, bonuses, ln_gs, ln_bs = [], [], [], []
    for hp in range(n_pairs):
        cols = slice(hp * LANES, (hp + 1) * LANES)
        r = shifted(r_ref[:, cols], mu_r[:, cols])
        k = shifted(k_ref[:, cols], mu_k[:, cols])
        v = shifted(v_ref[:, cols], mu_v[:, cols])
        vec = vec_ref[:, cols]
        k_k, k_a, r_k, ln_g, ln_b = (vec[i:i + 1] for i in range(5))
        gates.append(_bdot(gd_act, g_up_ref[:, cols]))
        ln_gs.append(ln_g)
        ln_bs.append(ln_b)
        kk = k * k_k
        kk = kk * jnp.minimum(lax.rsqrt(_half_sum(kk * kk, lo)), 1e12)
        r_s[hp] = r
        v_s[hp] = v
        kk_s[hp] = kk
        bonus = jnp.zeros((seq, LANES), F32)
        for d in range(2):
            wpre = w0_ref[d:d + 1, cols] + _bdot(wd_in, wa_up_ref[d, :, cols])
            sp = jnp.maximum(-wpre, 0.0) + jnp.log(1.0 + jnp.exp(-jnp.abs(wpre)))
            lw_s[2 * hp + d] = -jnp.exp(-sp - 0.5)
            a = _sigmoid(a0_ref[d:d + 1, cols] + _bdot(ad_in, wa_up_ref[d, :, cols]))
            kd = k * (1.0 + (a - 1.0) * k_a)
            kd_s[2 * hp + d] = kd
            b_s[2 * hp + d] = kk * a
            bonus = bonus + _half_sum(r * kd * r_k, lo) * v
        bonuses.append(bonus)

    ci = _iota((CHUNK, CHUNK), 0)
    cj = _iota((CHUNK, CHUNK), 1)
    eye = jnp.where(ci == cj, 1.0, 0.0)
    blk8 = (ci // 8) == (cj // 8)
    merge_sizes = [8 * 2 ** n for n in range(int(math.log2(CHUNK // 8)))]
    merge_masks = [((ci // (2 * s)) == (cj // (2 * s))) & ((ci // s) != (cj // s)) for s in merge_sizes]
    before = (cj < ci, cj > ci)
    upto = (cj <= ci, cj >= ci)
    tri = tuple(jnp.where(m, 1.0, 0.0).astype(BF16) for m in upto)
    clane = _iota((CHUNK, LANES), 1)
    head_lanes = (clane < RWKV_HS, clane >= RWKV_HS)
    si = _iota((LANES, LANES), 0)
    sj = _iota((LANES, LANES), 1)
    same_head = (si < RWKV_HS) == (sj < RWKV_HS)
    diag = si == sj
    row_h0 = _iota((LANES, CHUNK), 0) < RWKV_HS

    def chunk_group(g):
        items = []
        for hp in range(n_pairs):
            items += [(2 * hp, g * unroll + u) for u in range(unroll)]
            items += [(2 * hp + 1, n_chunks - 1 - (g * unroll + u)) for u in range(unroll)]
        n_it = len(items)
        scans = [sc for sc, _ in items]
        dirs = [sc % 2 for sc in scans]
        rows = [pl.ds(pl.multiple_of(c * CHUNK, CHUNK), CHUNK) for _, c in items]
        r_c = [r_s[sc // 2, rw, :] for sc, rw in zip(scans, rows)]
        v_c = [v_s[sc // 2, rw, :] for sc, rw in zip(scans, rows)]
        kk_c = [kk_s[sc // 2, rw, :] for sc, rw in zip(scans, rows)]
        lw_c = [lw_s[sc, rw, :] for sc, rw in zip(scans, rows)]
        kd_c = [kd_s[sc, rw, :] for sc, rw in zip(scans, rows)]
        b_c = [b_s[sc, rw, :] for sc, rw in zip(scans, rows)]
        cum = [_dot_exact_lhs(tri[d], lw) for d, lw in zip(dirs, lw_c)]
        tot = [cm[CHUNK - 1:CHUNK, :] if d == 0 else cm[0:1, :] for d, cm in zip(dirs, cum)]
        e_neg = [jnp.exp(-cm) for cm in cum]
        al = [jnp.exp(cm - lw) * kk for cm, lw, kk in zip(cum, lw_c, kk_c)]
        be = [b * e for b, e in zip(b_c, e_neg)]
        ka = [kd * e for kd, e in zip(kd_c, e_neg)]
        rh = [r * jnp.exp(cm) for r, cm in zip(r_c, cum)]
        wc = [jnp.exp(tt) for tt in tot]
        combos = [(i, h) for i in range(n_it) for h in range(2)]
        x2 = [jnp.concatenate([a, r], axis=0) for a, r in zip(al, rh)]
        heads_of = lambda m_t: jnp.concatenate([jnp.where(row_h0, m_t, 0.0), jnp.where(row_h0, 0.0, m_t)], axis=1)
        xb2 = [_bdot(x, heads_of(b.T)) for x, b in zip(x2, be)]
        xk2 = [_bdot(x, heads_of(k_.T)) for x, k_ in zip(x2, ka)]
        xb = [xb2[i][:, h * CHUNK:(h + 1) * CHUNK] for i, h in combos]
        xk = [xk2[i][:, h * CHUNK:(h + 1) * CHUNK] for i, h in combos]
        n_mat = [jnp.where(before[dirs[i]], -x[0:CHUNK], 0.0) for x, (i, _) in zip(xb, combos)]
        g_mat = [jnp.where(before[dirs[i]], x[0:CHUNK], 0.0) for x, (i, _) in zip(xk, combos)]
        m2 = [jnp.where(upto[dirs[i]], x[CHUNK:], 0.0) for x, (i, _) in zip(xb, combos)]
        m1 = [jnp.where(upto[dirs[i]], x[CHUNK:], 0.0) for x, (i, _) in zip(xk, combos)]
        nd = [jnp.where(blk8, n, 0.0) for n in n_mat]
        t = [eye + x for x in nd]
        p = [_bdot(x, x) for x in nd]
        t = [a + _bdot(a, b) for a, b in zip(t, p)]
        p = [_bdot(x, x) for x in p]
        t = [a + _bdot(a, b) for a, b in zip(t, p)]
        for msk in merge_masks:
            q = [_bdot(jnp.where(msk, n, 0.0), a) for n, a in zip(n_mat, t)]
            t = [a + _bdot(a, b) for a, b in zip(t, q)]
        gv = [_bdot(g_, v_c[i]) for g_, (i, _) in zip(g_mat, combos)]
        z = [_bdot(t_, jnp.concatenate([al[i], g_], axis=1)) for t_, g_, (i, _) in zip(t, gv, combos)]
        yv_h = [_bdot(m, v_c[i]) for m, (i, _) in zip(m1, combos)]
        pick = lambda a, b: jnp.where(head_lanes[0], a, b)
        alp = [pick(z[2 * i][:, :LANES], z[2 * i + 1][:, :LANES]) for i in range(n_it)]
        uv = [pick(z[2 * i][:, LANES:], z[2 * i + 1][:, LANES:]) for i in range(n_it)]
        yv = [pick(yv_h[2 * i], yv_h[2 * i + 1]) for i in range(n_it)]
        be_t = [(b * w).T for b, w in zip(be, wc)]
        ka_t = [(k_ * w).T for k_, w in zip(ka, wc)]
        pz = [_bdot(bt, jnp.concatenate([a, u_], axis=1)) for bt, a, u_ in zip(be_t, alp, uv)]
        kv = [_bdot(kt, v_) for kt, v_ in zip(ka_t, v_c)]
        a_mat = [jnp.where(same_head, jnp.where(diag, w, 0.0) - pp[:, :LANES], 0.0) for w, pp in zip(wc, pz)]
        b_mat = [jnp.where(same_head, k_ - pp[:, LANES:], 0.0) for k_, pp in zip(kv, pz)]
        for i, sc in enumerate(scans):
            st = st_s[sc]
            xs = _bdot(jnp.concatenate([alp[i], rh[i]], axis=0), st)
            u = uv[i] + xs[0:CHUNK]
            m2u = pick(_bdot(m2[2 * i], u), _bdot(m2[2 * i + 1], u))
            y_s[sc, rows[i], :] = xs[CHUNK:] + yv[i] - m2u
            st_s[sc] = _bdot(a_mat[i], st) + b_mat[i]

    def body(g, carry):
        chunk_group(g)
        return carry

    for hp in range(n_pairs):
        if has_s0:
            st_s[2 * hp] = s0f_ref[0, hp]
            st_s[2 * hp + 1] = s0b_ref[0, hp]
        else:
            st_s[2 * hp] = jnp.zeros((LANES, LANES), F32)
            st_s[2 * hp + 1] = jnp.zeros((LANES, LANES), F32)
    lax.fori_loop(0, n_chunks // unroll, body, 0)
    for hp in range(n_pairs):
        if emit_states:
            for dst, sc in ((sf_ref, 2 * hp), (sb_ref, 2 * hp + 1)):
                s_vk = st_s[sc].T
                dst[0, 2 * hp] = s_vk[0:RWKV_HS, 0:RWKV_HS]
                dst[0, 2 * hp + 1] = s_vk[RWKV_HS:, RWKV_HS:]
        y = (y_s[2 * hp] + y_s[2 * hp + 1]) + bonuses[hp]
        mean = _half_sum(y, lo) * (1.0 / RWKV_HS)
        yc = y - mean
        var = _half_sum(yc * yc, lo) * (1.0 / RWKV_HS)
        o_ref[:, hp * LANES:(hp + 1) * LANES] = ((yc * lax.rsqrt(var + GN_EPS)) * ln_gs[hp] + ln_bs[hp]) * gates[hp]


def _rwkv(p, seq, n_pairs, li, mu, w0, a0, wa_up, g_up, vecs, s0=None, states_out=None):
    t = p.shape[0]
    nb = t // seq
    width = n_pairs * LANES
    n_grp = RWKV_WIDTH // width
    base = (2 * DIFF_QW + DIFF_WIDTH) // width
    tail = (2 * DIFF_QW + DIFF_WIDTH + 3 * RWKV_WIDTH) // LANES
    col = lambda blk: pl.BlockSpec((seq, width), lambda b, hp: (b, blk(hp)))
    mu_col = lambda blk: pl.BlockSpec((None, 2, width), lambda b, hp: (li, 0, blk(hp)))
    lane_blk = lambda k: pl.BlockSpec((seq, LANES), lambda b, hp: (b, tail + k))
    mu_lane_blk = lambda k: pl.BlockSpec((None, 2, LANES), lambda b, hp: (li, 0, 3 * RWKV_WIDTH // LANES + k))
    in_specs = [
        col(lambda hp: base + hp), col(lambda hp: base + n_grp + hp), col(lambda hp: base + 2 * n_grp + hp),
        lane_blk(0), lane_blk(1),
        mu_col(lambda hp: hp), mu_col(lambda hp: n_grp + hp), mu_col(lambda hp: 2 * n_grp + hp),
        mu_lane_blk(0), mu_lane_blk(1),
        pl.BlockSpec((None, 2, width), lambda b, hp: (li, 0, hp)),
        pl.BlockSpec((None, 2, width), lambda b, hp: (li, 0, hp)),
        pl.BlockSpec((None, 2, LANES, width), lambda b, hp: (li, 0, 0, hp)),
        pl.BlockSpec((None, LANES, width), lambda b, hp: (li, 0, hp)),
        pl.BlockSpec((None, 8, width), lambda b, hp: (li, 0, hp)),
    ]
    args = [p] * 5 + [mu] * 5 + [w0, a0, wa_up, g_up, vecs]
    if s0 is not None:
        in_specs += [pl.BlockSpec((1, n_pairs, LANES, LANES), lambda b, hp: (b, hp, 0, 0))] * 2
        args += list(s0)
    out_shape = [jax.ShapeDtypeStruct((t, RWKV_WIDTH), F32)]
    out_specs = [pl.BlockSpec((seq, width), lambda b, hp: (b, hp))]
    aliases = {}
    if states_out is not None:
        aliases = {len(args): 1, len(args) + 1: 2}
        in_specs += [pl.BlockSpec(memory_space=pl.ANY)] * 2
        args += list(states_out)
        out_shape += [jax.ShapeDtypeStruct(a.shape, a.dtype) for a in states_out]
        out_specs += [pl.BlockSpec((1, None, 2 * n_pairs, RWKV_HS, RWKV_HS), lambda b, hp: (b, li, hp, 0, 0))] * 2
    return pl.pallas_call(
        functools.partial(_rwkv_kernel, s0 is not None, states_out is not None),
        out_shape=out_shape,
        grid=(nb, n_grp),
        in_specs=in_specs,
        out_specs=out_specs,
        scratch_shapes=[pltpu.VMEM((n_pairs, seq, LANES), F32)] * 3 + [pltpu.VMEM((2 * n_pairs, seq, LANES), F32)] * 4
        + [pltpu.VMEM((2 * n_pairs, LANES, LANES), F32)],
        input_output_aliases=aliases,
        compiler_params=_params("parallel", "parallel"),
        name="rwkv",
    )(*args)


def _pair_states_in(s):
    nb = s.shape[0]
    st = jnp.swapaxes(s.astype(F32), -1, -2).reshape(nb, RWKV_HEADS // 2, 2, RWKV_HS, RWKV_HS)
    z = jnp.zeros_like(st[:, :, 0])
    top = jnp.concatenate([st[:, :, 0], z], axis=-1)
    bot = jnp.concatenate([z, st[:, :, 1]], axis=-1)
    return jnp.concatenate([top, bot], axis=-2)


def _dft_kernel(f_ref, ft_ref):
    n = f_ref.shape[1]
    shape = (2 * n, n)
    fr = _iota(shape, 0)
    tau = _iota(shape, 1)
    is_sin = fr >= n
    kf = jnp.where(is_sin, fr - n, fr)
    ang = ((kf * tau) & (2 * n - 1)).astype(F32) * (math.pi / n)
    val = jnp.where(is_sin, -jnp.sin(ang), jnp.cos(ang))
    nyq = jnp.where((tau & 1) == 0, 1.0, -1.0)
    val = jnp.where(is_sin & (kf == 0), nyq, val)
    f_ref[...] = val.astype(BF16)
    ft_ref[...] = val.T.astype(BF16)


def _dft_mats(n):
    return pl.pallas_call(
        _dft_kernel,
        out_shape=(jax.ShapeDtypeStruct((2 * n, n), BF16), jax.ShapeDtypeStruct((n, 2 * n), BF16)),
        name="dft_mats",
    )()


def _filter_kernel(seq, w0t_ref, w0c_ref, w0s_ref, b0_ref, w1_ref, b1_ref, w2_ref, b2_ref, fr_ref,
                   w3a_ref, w3b_ref, f_ref, o_ref, prev_scr):
    cb = f_ref.shape[1]
    n_blk = seq // cb
    e = pl.program_id(0)
    first_lag = (e - n_blk) * cb
    tap_row = jnp.abs(first_lag + _iota((1, cb), 1)).astype(F32)
    pos_row = tap_row * (1.0 / (seq - 1))
    band = _iota((BANDS, 1), 0).astype(F32)
    freq = 1e-4 + band * ((BANDS - 1 - 1e-4) / (BANDS - 1))
    arg = freq * (tap_row * (2.0 * math.pi / seq))
    fr = fr_ref[...]
    pre = (w0t_ref[...] * pos_row + _dot_f32(w0c_ref[...], jnp.cos(arg)) + _dot_f32(w0s_ref[...], -jnp.sin(arg))
           + b0_ref[...])
    hdn = jnp.sin(fr[:, 0:1] * pre)
    hdn = jnp.sin(fr[:, 1:2] * (_dot_f32(w1_ref[...], hdn) + b1_ref[...]))
    hdn = jnp.sin(fr[:, 2:3] * (_dot_f32(w2_ref[...], hdn) + b2_ref[...]))
    hdn = hdn.T
    tap = jnp.abs(first_lag + _iota((cb, 1), 0))
    pos = tap.astype(F32) * (1.0 / (seq - 1))
    chan = _iota((1, D_MODEL), 1).astype(F32)
    delta = jnp.abs(MIN_DECAY + chan * ((MAX_DECAY - MIN_DECAY) / (D_MODEL - 1)))
    window = jnp.where(tap < seq, jnp.exp(-pos * delta), 0.0)
    row = _iota((2 * cb, 1), 0)
    sign = jnp.where((row & 1) == 1, -1.0, 1.0)
    for o, w3_ref in enumerate((w3a_ref, w3b_ref)):
        cols = slice(o * D_MODEL, (o + 1) * D_MODEL)
        taps = (_dot_f32(hdn, w3_ref[...]) * window).astype(BF16)
        spec = jnp.dot(f_ref[...], taps, preferred_element_type=F32)

        @pl.when(e > 0)
        def _():
            o_ref[0, :, cols] = spec + sign * prev_scr[:, cols]

        prev_scr[:, cols] = spec - jnp.where(row <= cb, taps[0:1, :].astype(F32), 0.0)


def _hyena_filters(seq, li, fmat, small, w3):
    cb = fmat.shape[1]
    n_blk = seq // cb
    layer_slice = lambda a: pl.BlockSpec((None,) + a.shape[1:], lambda e: (li, 0, 0))
    w3_spec = lambda o: pl.BlockSpec((None, FILTER_WIDTH, D_MODEL),
                                     lambda e: (li, 0, 2 * o + jnp.where(e < n_blk, 1, 0)))
    return pl.pallas_call(
        functools.partial(_filter_kernel, seq),
        out_shape=jax.ShapeDtypeStruct((2 * n_blk - 1, 2 * cb, 2 * D_MODEL), F32),
        grid=(2 * n_blk,),
        in_specs=[layer_slice(a) for a in small] + [w3_spec(0), w3_spec(1),
                                                    pl.BlockSpec((2 * cb, cb), lambda e: (0, 0))],
        out_specs=pl.BlockSpec((1, 2 * cb, 2 * D_MODEL), lambda e: (jnp.maximum(e - 1, 0), 0, 0)),
        scratch_shapes=[pltpu.VMEM((2 * cb, 2 * D_MODEL), F32)],
        compiler_params=_params("arbitrary"),
        name="hyena_filters",
    )(*small, w3, w3, fmat)


def _short_conv(z, w, b):
    prev, nxt = _shift_rows(z)
    return prev * w[0:1] + z * w[1:2] + nxt * w[2:3] + b


def _conv_kernel(conv_v, n_blk, v_ref, x_ref, wv_ref, bv_ref, wx_ref, bx_ref, f_ref, ft_ref, kc_ref, bias_ref,
                 o_ref, uf_scr, yf_scr):
    cb = f_ref.shape[1]
    v = v_ref[...]
    if conv_v:
        v = _short_conv(v, wv_ref[...], bv_ref[...])
    fmat = f_ref[...]
    for j in range(n_blk):
        uf_scr[j] = jnp.dot(fmat, v[j * cb:(j + 1) * cb].astype(BF16), preferred_element_type=F32)

    def mix(r0, first):
        re, im = pl.ds(r0, MIX_ROWS), pl.ds(cb + r0, MIX_ROWS)
        real_pair = (_iota((MIX_ROWS, 1), 0) == 0) if first else None
        for i in range(n_blk):
            acc_r = acc_i = None
            for j in range(n_blk):
                d = i - j + n_blk - 1
                ur, ui = uf_scr[j, re, :], uf_scr[j, im, :]
                kr, ki = kc_ref[d, re, :], kc_ref[d, im, :]
                uiki = ui * ki
                if first:
                    t_r = ur * kr - jnp.where(real_pair, 0.0, uiki)
                    t_i = jnp.where(real_pair, uiki, ur * ki + ui * kr)
                else:
                    t_r = ur * kr - uiki
                    t_i = ur * ki + ui * kr
                acc_r = t_r if acc_r is None else acc_r + t_r
                acc_i = t_i if acc_i is None else acc_i + t_i
            scale = jnp.where(real_pair, 0.5 / cb, 1.0 / cb) if first else 1.0 / cb
            yf_scr[i, re, :] = acc_r * scale
            yf_scr[i, im, :] = acc_i * scale

    mix(0, True)

    def body(r, carry):
        mix(pl.multiple_of(r * MIX_ROWS, MIX_ROWS), False)
        return carry

    lax.fori_loop(1, cb // MIX_ROWS, body, 0)

    x = _short_conv(x_ref[...], wx_ref[...], bx_ref[...])
    ftm = ft_ref[...]
    bias = bias_ref[...]
    for i in range(n_blk):
        rows = slice(i * cb, (i + 1) * cb)
        y = jnp.dot(ftm, yf_scr[i].astype(BF16), preferred_element_type=F32)
        o_ref[rows, :] = x[rows] * (y + v[rows] * bias)


def _hyena_conv(v_arr, v_blk0, x_arr, x_blk0, conv_v, seq, li, order, cw, cwb, fmat, fmat_t, kc, biases, width):
    t = v_arr.shape[0]
    cb = fmat.shape[1]
    nb, ncb, n_blk = t // seq, D_MODEL // width, seq // cb
    wv_blk0 = v_blk0 if conv_v else 0
    kc_blk0 = order * ncb
    return pl.pallas_call(
        functools.partial(_conv_kernel, conv_v, n_blk),
        out_shape=jax.ShapeDtypeStruct((t, D_MODEL), F32),
        grid=(nb, ncb),
        in_specs=[
            pl.BlockSpec((seq, width), lambda b, c: (b, v_blk0 + c)),
            pl.BlockSpec((seq, width), lambda b, c: (b, x_blk0 + c)),
            pl.BlockSpec((None, 3, width), lambda b, c: (li, 0, wv_blk0 + c)),
            pl.BlockSpec((None, 1, width), lambda b, c: (li, 0, wv_blk0 + c)),
            pl.BlockSpec((None, 3, width), lambda b, c: (li, 0, x_blk0 + c)),
            pl.BlockSpec((None, 1, width), lambda b, c: (li, 0, x_blk0 + c)),
            pl.BlockSpec((2 * cb, cb), lambda b, c: (0, 0)),
            pl.BlockSpec((cb, 2 * cb), lambda b, c: (0, 0)),
            pl.BlockSpec((2 * n_blk - 1, 2 * cb, width), lambda b, c: (0, 0, kc_blk0 + c)),
            pl.BlockSpec((None, 1, width), lambda b, c: (li * 2 + order, 0, c)),
        ],
        out_specs=pl.BlockSpec((seq, width), lambda b, c: (b, c)),
        scratch_shapes=[pltpu.VMEM((n_blk, 2 * cb, width), F32), pltpu.VMEM((n_blk, 2 * cb, width), F32)],
        compiler_params=_params("parallel", "parallel"),
        name="hyena_conv",
    )(v_arr, x_arr, cw, cwb, cw, cwb, fmat, fmat_t, kc, biases)


def _axial_rope_tables(seq):
    n_rows = seq // GRID_W
    row = np.repeat(np.arange(n_rows, dtype=np.float64), GRID_W)
    col = np.tile(np.arange(GRID_W, dtype=np.float64), n_rows)
    inv = ROPE_BASE ** (-np.arange(ROPE_PAIRS, dtype=np.float64) / ROPE_PAIRS)
    ar = row[:, None] * inv[None]
    ac = col[:, None] * inv[None]
    ang = np.concatenate([ar, ar, ac, ac], axis=-1)
    reps = LANES // DIFF_QK
    return (jnp.asarray(np.tile(np.cos(ang), (1, reps)), F32), jnp.asarray(np.tile(np.sin(ang), (1, reps)), F32))


def _pair_lanes(a, seq_axis):
    nb = a.shape[0]
    if a.ndim == 5:
        a = a.reshape(nb, DIFF_HEADS // 2, 2, 2, a.shape[3], DIFF_QK).transpose(0, 1, 4, 2, 3, 5)
    else:
        a = a.reshape(nb, DIFF_HEADS // 2, 2, a.shape[2], DIFF_V).transpose(0, 1, 3, 2, 4)
    return a.reshape(nb, DIFF_HEADS // 2, a.shape[2], LANES)


def kernel(x_prompt, x_sample, cache_diff_k, cache_diff_v, state_rwkv_fwd, state_rwkv_bwd, c, c_ctx, ada_w, ada_b, norm1_g, norm2_g, ffn_w1, ffn_w3, ffn_w2, final_g, ab_w_in, ab_w_out, diff_lambda, diff_subln_g, rwkv_mu, rwkv_w0, rwkv_w_up, rwkv_a0, rwkv_a_up, rwkv_g_up, rwkv_k_k, rwkv_k_a, rwkv_r_k, rwkv_ln_g, rwkv_ln_b, hy_w_in, hy_b_in, hy_conv_w, hy_conv_b, hy_f_w0, hy_f_b0, hy_f_w1, hy_f_b1, hy_f_w2, hy_f_b2, hy_f_w3, hy_f_freq, hy_bias, hy_w_out, hy_b_out):
    n_ctx_seqs, ctx_len, _ = x_prompt.shape
    n_lat_seqs, lat_len, _ = x_sample.shape
    tm = 1024

    cond = jnp.zeros((N_MOD_ROWS, D_MODEL), F32).at[0].set(c_ctx).at[1:1 + n_lat_seqs].set(c)
    mods = _ada_all(cond, ada_w, ada_b).reshape(DEPTH * N_MOD_ROWS * 6, 1, D_MODEL)

    groups = [
        dict(x=x_prompt.reshape(-1, D_MODEL), seq=ctx_len, rows=(0, n_ctx_seqs * ctx_len), tq=ctx_len, pairs=4,
             scan_pairs=2, conv_width=1024),
        dict(x=x_sample.reshape(-1, D_MODEL), seq=lat_len, rows=(1, lat_len), tq=512, pairs=1, scan_pairs=1,
             conv_width=256),
    ]
    rope = _axial_rope_tables(lat_len)
    dft = {cb: _dft_mats(cb) for cb in {min(CONV_BLOCK_MAX, g["seq"]) for g in groups}}
    n_ab, n_c = ab_w_in.shape[0], hy_w_in.shape[0]

    gains1 = norm1_g.reshape(DEPTH, 1, D_MODEL)
    gains2 = norm2_g.reshape(DEPTH, 1, D_MODEL)
    g2s = jnp.tile(diff_subln_g, (1, 2)).reshape(n_ab, 1, LANES)
    wa_up = jnp.concatenate([rwkv_w_up, rwkv_a_up], axis=2)
    vec_rows = [rwkv_k_k, rwkv_k_a, rwkv_r_k.reshape(n_ab, RWKV_WIDTH), rwkv_ln_g, rwkv_ln_b]
    vecs = jnp.stack(vec_rows + [jnp.zeros_like(rwkv_k_k)] * (8 - len(vec_rows)), axis=1)
    w0_t = jnp.swapaxes(hy_f_w0, 1, 2)
    filt_small = [w0_t[:, :, 0:1], w0_t[:, :, 1:1 + BANDS], w0_t[:, :, 1 + BANDS:], hy_f_b0[:, :, None],
                  jnp.swapaxes(hy_f_w1, 1, 2), hy_f_b1[:, :, None], jnp.swapaxes(hy_f_w2, 1, 2), hy_f_b2[:, :, None],
                  jnp.swapaxes(hy_f_freq, 1, 2)]
    hy_b_in_s = hy_b_in.reshape(n_c, 1, -1)
    hy_b_out_s = hy_b_out.reshape(n_c, 1, D_MODEL)
    conv_b = hy_conv_b.reshape(n_c, 1, -1)
    conv_bias = hy_bias.reshape(n_c * 2, 1, D_MODEL)
    new_kv = (jnp.zeros((n_ctx_seqs, n_ab, DIFF_HEADS, 2, ctx_len, DIFF_QK), F32),
              jnp.zeros((n_ctx_seqs, n_ab, DIFF_HEADS, ctx_len, DIFF_V), F32))
    new_states = tuple(jnp.zeros((n_ctx_seqs, n_ab, RWKV_HEADS, RWKV_HS, RWKV_HS), F32) for _ in range(2))

    for l in range(DEPTH):
        i = l // 2
        for gi, g in enumerate(groups):
            x, seq, row = g["x"], g["seq"], g["rows"]
            if l % 2 == 0:
                p = _proj_in(x, gains1, mods, l, row, ab_w_in, i, None, tm, AB_IN // 2)
                if gi == 0:
                    att, *new_kv = _attention(p, seq, g["tq"], g["pairs"], _lambda_init(l), diff_lambda, g2s, i,
                                              kv_out=new_kv)
                    mix, *new_states = _rwkv(p, seq, g["scan_pairs"], i, rwkv_mu, rwkv_w0, rwkv_a0, wa_up, rwkv_g_up,
                                             vecs, states_out=new_states)
                else:
                    ctx = (_pair_lanes(cache_diff_k[:, i], 3), _pair_lanes(cache_diff_v[:, i], 2))
                    att, = _attention(p, seq, g["tq"], g["pairs"], _lambda_init(l), diff_lambda, g2s, i,
                                      ctx=ctx, rope=rope)
                    s0 = (_pair_states_in(state_rwkv_fwd[:, i]), _pair_states_in(state_rwkv_bwd[:, i]))
                    mix, = _rwkv(p, seq, g["scan_pairs"], i, rwkv_mu, rwkv_w0, rwkv_a0, wa_up, rwkv_g_up, vecs, s0=s0)
                x = _proj_out(x, mods, l, row, [att, mix], ab_w_out, i, None, tm)
            else:
                fmat, fmat_t = dft[min(CONV_BLOCK_MAX, seq)]
                z = _proj_in(x, gains1, mods, l, row, hy_w_in, i, hy_b_in_s, tm, 1536)
                kc = _hyena_filters(seq, i, fmat, filt_small, hy_f_w3)
                width = g["conv_width"]
                ncb = D_MODEL // width
                u = _hyena_conv(z, 2 * ncb, z, 0, True, seq, i, 0, hy_conv_w, conv_b, fmat, fmat_t, kc, conv_bias, width)
                u = _hyena_conv(u, 0, z, ncb, False, seq, i, 1, hy_conv_w, conv_b, fmat, fmat_t, kc, conv_bias, width)
                x = _proj_out(x, mods, l, row, [u], hy_w_out, i, hy_b_out_s, tm)
            g["x"] = _ffn(x, gains2, mods, l, row, ffn_w1, ffn_w3, ffn_w2, final_g, l == DEPTH - 1, 2 * tm, 256)

    y_prompt = groups[0]["x"].reshape(x_prompt.shape)
    y_sample = groups[1]["x"].reshape(x_sample.shape)
    return (y_prompt, y_sample, new_kv[0], new_kv[1], new_states[0], new_states[1])
```

```python
import functools
import math

import jax
import jax.numpy as jnp
import numpy as np
from jax import lax
from jax.experimental import pallas as pl
from jax.experimental.pallas import tpu as pltpu

F32 = jnp.float32
BF16 = jnp.bfloat16

D_MODEL = 1024
DEPTH = 4
GRID_W = 64
DIFF_HEADS = 8
DIFF_QK = 32
DIFF_V = 64
DIFF_QW = DIFF_HEADS * 2 * DIFF_QK
DIFF_WIDTH = DIFF_HEADS * DIFF_V
ROPE_PAIRS = DIFF_QK // 4
ROPE_BASE = 10000.0
RWKV_HEADS = 8
RWKV_HS = 64
RWKV_WIDTH = RWKV_HEADS * RWKV_HS
RWKV_IN = 3 * RWKV_WIDTH + 64 + 64 + 128
AB_IN = 2 * DIFF_QW + DIFF_WIDTH + RWKV_IN
BANDS = 16
FILTER_WIDTH = 64
MAX_DECAY = math.log(1e-2) / 0.3
MIN_DECAY = math.log(1e-2) / 1.5
D_FF = 2816
RMS_EPS = 1e-6
GN_EPS = 64e-5
HEAD_RMS_EPS = 1e-5
LOG2_E = math.log2(math.e)

LANES = 128
CHUNK = 128
SCAN_UNROLL = 4
CONV_BLOCK_MAX = 512
MIX_ROWS = 8
VMEM_LIMIT = 56 * 1024 * 1024

N_MOD_ROWS = 8


def _lambda_init(l):
    return 0.8 - 0.6 * math.exp(-0.3 * l)


def _params(*sem):
    return pltpu.CompilerParams(dimension_semantics=sem, vmem_limit_bytes=VMEM_LIMIT)


def _bdot(a, b):
    return jnp.dot(a.astype(BF16), b.astype(BF16), preferred_element_type=F32)


def _bdot_nt(a, b):
    return lax.dot_general(a.astype(BF16), b.astype(BF16), (((1,), (1,)), ((), ())),
                           preferred_element_type=F32)


def _split3(x):
    h1 = x.astype(BF16)
    r1 = x - h1.astype(F32)
    h2 = r1.astype(BF16)
    h3 = (r1 - h2.astype(F32)).astype(BF16)
    return h1, h2, h3


def _dot_f32(a, b):
    a1, a2, a3 = _split3(a)
    b1, b2, b3 = _split3(b)
    d = lambda x, y: jnp.dot(x, y, preferred_element_type=F32)
    return (d(a1, b1) + (d(a1, b2) + d(a2, b1))) + ((d(a1, b3) + d(a3, b1)) + d(a2, b2))


def _dot_exact_lhs(a_bf16, b):
    b1, b2, b3 = _split3(b)
    d = lambda y: jnp.dot(a_bf16, y, preferred_element_type=F32)
    return d(b1) + (d(b2) + d(b3))


def _sigmoid(x):
    return 0.5 * (jnp.tanh(0.5 * x) + 1.0)


def _iota(shape, dim):
    return lax.broadcasted_iota(jnp.int32, shape, dim)


def _shift_rows(x):
    n = x.shape[0]
    row = _iota(x.shape, 0)
    prev = jnp.where(row == 0, 0.0, pltpu.roll(x, 1, 0))
    nxt = jnp.where(row == n - 1, 0.0, pltpu.roll(x, n - 1, 0))
    return prev, nxt


def _half_sum(x, lo_mask):
    s_lo = jnp.sum(jnp.where(lo_mask, x, 0.0), axis=-1, keepdims=True)
    s_hi = jnp.sum(jnp.where(lo_mask, 0.0, x), axis=-1, keepdims=True)
    return jnp.where(lo_mask, s_lo, s_hi)


def _ada_kernel(c_ref, w_ref, b_ref, o_ref):
    c = c_ref[...]
    o_ref[0] = _bdot(c * _sigmoid(c), w_ref[0]) + b_ref[0]


def _ada_all(cond, ada_w, ada_b):
    tn = 1536
    n_out = 6 * D_MODEL
    return pl.pallas_call(
        _ada_kernel,
        out_shape=jax.ShapeDtypeStruct((DEPTH, N_MOD_ROWS, n_out), F32),
        grid=(DEPTH, n_out // tn),
        in_specs=[
            pl.BlockSpec((N_MOD_ROWS, D_MODEL), lambda l, j: (0, 0)),
            pl.BlockSpec((1, D_MODEL, tn), lambda l, j: (l, 0, j)),
            pl.BlockSpec((1, 1, tn), lambda l, j: (l, 0, j)),
        ],
        out_specs=pl.BlockSpec((1, N_MOD_ROWS, tn), lambda l, j: (l, 0, j)),
        compiler_params=_params("parallel", "parallel"),
        name="ada",
    )(cond, ada_w, ada_b.reshape(DEPTH, 1, n_out))


def _mod_spec(layer, chunk, cond_rows, tm, axis=0):
    first, span = cond_rows

    def index_map(*idx):
        return ((layer * N_MOD_ROWS + first + (idx[axis] * tm) // span) * 6 + chunk, 0, 0)
    return pl.BlockSpec((1, 1, D_MODEL), index_map)


def _normed(x, g, sc, sh):
    ms = jnp.mean(x * x, axis=-1, keepdims=True)
    return (x * lax.rsqrt(ms + RMS_EPS)) * g * (1.0 + sc) + sh


def _proj_in_kernel(has_bias, x_ref, g_ref, sc_ref, sh_ref, w_ref, *refs):
    o_ref = refs[-1]
    h = _normed(x_ref[...], g_ref[...], sc_ref[0], sh_ref[0])
    y = _bdot(h, w_ref[...])
    o_ref[...] = y + refs[0][...] if has_bias else y


def _proj_in(x, gains, mods, layer, cond_rows, ws, wi, bs, tm, tn):
    t, n_out = x.shape[0], ws.shape[2]
    in_specs = [
        pl.BlockSpec((tm, D_MODEL), lambda j, i: (i, 0)),
        pl.BlockSpec((None, 1, D_MODEL), lambda j, i: (layer, 0, 0)),
        _mod_spec(layer, 1, cond_rows, tm, axis=1),
        _mod_spec(layer, 0, cond_rows, tm, axis=1),
        pl.BlockSpec((None, D_MODEL, tn), lambda j, i: (wi, 0, j)),
    ]
    args = [x, gains, mods, mods, ws]
    if bs is not None:
        in_specs.append(pl.BlockSpec((None, 1, tn), lambda j, i: (wi, 0, j)))
        args.append(bs)
    return pl.pallas_call(
        functools.partial(_proj_in_kernel, bs is not None),
        out_shape=jax.ShapeDtypeStruct((t, n_out), F32),
        grid=(n_out // tn, t // tm),
        in_specs=in_specs,
        out_specs=pl.BlockSpec((tm, tn), lambda j, i: (i, j)),
        compiler_params=_params("parallel", "parallel"),
        name="proj_in",
    )(*args)


def _proj_out_kernel(n_u, has_bias, x_ref, gt_ref, *refs):
    u_refs, w_ref, o_ref = refs[:n_u], refs[n_u], refs[-1]
    acc = None
    off = 0
    for u_ref in u_refs:
        k = u_ref.shape[1]
        part = _bdot(u_ref[...], w_ref[off:off + k, :])
        acc = part if acc is None else acc + part
        off += k
    if has_bias:
        acc = acc + refs[n_u + 1][...]
    o_ref[...] = x_ref[...] + gt_ref[0] * acc


def _proj_out(x, mods, layer, cond_rows, us, ws, wi, bs, tm):
    t = x.shape[0]
    in_specs = [
        pl.BlockSpec((tm, D_MODEL), lambda i: (i, 0)),
        _mod_spec(layer, 2, cond_rows, tm),
        *[pl.BlockSpec((tm, u.shape[1]), lambda i: (i, 0)) for u in us],
        pl.BlockSpec((None,) + ws.shape[1:], lambda i: (wi, 0, 0)),
    ]
    args = [x, mods, *us, ws]
    if bs is not None:
        in_specs.append(pl.BlockSpec((None, 1, D_MODEL), lambda i: (wi, 0, 0)))
        args.append(bs)
    return pl.pallas_call(
        functools.partial(_proj_out_kernel, len(us), bs is not None),
        out_shape=jax.ShapeDtypeStruct((t, D_MODEL), F32),
        grid=(t // tm,),
        in_specs=in_specs,
        out_specs=pl.BlockSpec((tm, D_MODEL), lambda i: (i, 0)),
        compiler_params=_params("parallel"),
        name="proj_out",
    )(*args)


def _ffn_kernel(final, x_ref, g_ref, sc_ref, sh_ref, gt_ref, w1_ref, w3_ref, w2_ref, fg_ref, o_ref, h_scr):
    k = pl.program_id(1)

    @pl.when(k == 0)
    def _():
        h_scr[...] = _normed(x_ref[...], g_ref[...], sc_ref[0], sh_ref[0]).astype(BF16)
        o_ref[...] = jnp.zeros_like(o_ref)

    h = h_scr[...]
    a1 = jnp.dot(h, w1_ref[...].astype(BF16), preferred_element_type=F32)
    a3 = jnp.dot(h, w3_ref[...].astype(BF16), preferred_element_type=F32)
    o_ref[...] += _bdot((a1 * _sigmoid(a1)) * a3, w2_ref[...])

    @pl.when(k == pl.num_programs(1) - 1)
    def _():
        y = x_ref[...] + gt_ref[0] * o_ref[...]
        if final:
            ms = jnp.mean(y * y, axis=-1, keepdims=True)
            y = (y * lax.rsqrt(ms + RMS_EPS)) * fg_ref[...]
        o_ref[...] = y


def _ffn(x, gains, mods, layer, cond_rows, w1s, w3s, w2s, final_g, final, tm, tk):
    t = x.shape[0]
    return pl.pallas_call(
        functools.partial(_ffn_kernel, final),
        out_shape=jax.ShapeDtypeStruct((t, D_MODEL), F32),
        grid=(t // tm, D_FF // tk),
        in_specs=[
            pl.BlockSpec((tm, D_MODEL), lambda i, k: (i, 0)),
            pl.BlockSpec((None, 1, D_MODEL), lambda i, k: (layer, 0, 0)),
            _mod_spec(layer, 4, cond_rows, tm),
            _mod_spec(layer, 3, cond_rows, tm),
            _mod_spec(layer, 5, cond_rows, tm),
            pl.BlockSpec((None, D_MODEL, tk), lambda i, k: (layer, 0, k)),
            pl.BlockSpec((None, D_MODEL, tk), lambda i, k: (layer, 0, k)),
            pl.BlockSpec((None, tk, D_MODEL), lambda i, k: (layer, k, 0)),
            pl.BlockSpec((1, D_MODEL), lambda i, k: (0, 0)),
        ],
        out_specs=pl.BlockSpec((tm, D_MODEL), lambda i, k: (i, 0)),
        scratch_shapes=[pltpu.VMEM((tm, D_MODEL), BF16)],
        compiler_params=_params("parallel", "arbitrary"),
        name="ffn",
    )(x, gains, mods, mods, mods, w1s, w3s, w2s, final_g.reshape(1, D_MODEL))


def _rope(x, cos, sin):
    lane = _iota(x.shape, 1)
    rot = jnp.where((lane % 16) < 8, -pltpu.roll(x, LANES - 8, 1), pltpu.roll(x, 8, 1))
    return x * cos + rot * sin


def _attn_kernel(n_ctx, use_rope, emit_kv, layer, lam_init, *refs):
    it = iter(refs)
    q_ref, k_ref, v_ref = next(it), next(it), next(it)
    kc_ref = vc_ref = cq_ref = sq_ref = ck_ref = sk_ref = ko_ref = vo_ref = None
    if n_ctx:
        kc_ref, vc_ref = next(it), next(it)
    if use_rope:
        cq_ref, sq_ref, ck_ref, sk_ref = next(it), next(it), next(it), next(it)
    lam_ref, g_ref = next(it), next(it)
    if emit_kv == "update":
        next(it), next(it)
    o_ref = next(it)
    if emit_kv:
        ko_ref, vo_ref = next(it), next(it)
    kall, vall = next(it), next(it)
    n_own = k_ref.shape[0]
    n_pairs = q_ref.shape[1] // LANES

    @pl.when(pl.program_id(2) == 0)
    def _():
        k = k_ref[...]
        v = v_ref[...]
        if emit_kv:
            slabs = [()] if emit_kv == "update" else [(lj,) for lj in range(ko_ref.shape[1])]
            for slab in slabs:
                own = not slab or slab[0] == layer
                for h in range(2 * n_pairs):
                    for m in range(2):
                        c0 = (2 * h + m) * DIFF_QK
                        kh = k[:, c0:c0 + DIFF_QK]
                        ko_ref[(0,) + slab + (h, m)] = kh if own else jnp.zeros_like(kh)
                    vh = v[:, h * DIFF_V:(h + 1) * DIFF_V]
                    vo_ref[(0,) + slab + (h,)] = vh if own else jnp.zeros_like(vh)
        if use_rope:
            k = _rope(k, ck_ref[...], sk_ref[...])
        if n_ctx:
            kall[0:n_ctx, :] = kc_ref[0, 0].astype(BF16)
            vall[0:n_ctx, :] = vc_ref[0, 0].astype(BF16)
        kall[n_ctx:n_ctx + n_own, :] = k.astype(BF16)
        vall[n_ctx:n_ctx + n_own, :] = v.astype(BF16)

    lv = lam_ref[...]
    lam = (jnp.exp(jnp.sum(lv[0:1] * lv[1:2], axis=-1, keepdims=True))
           - jnp.exp(jnp.sum(lv[2:3] * lv[3:4], axis=-1, keepdims=True)) + lam_init)

    lane = _iota((q_ref.shape[0], LANES), 1)
    lo = lane < DIFF_V
    scale = DIFF_QK ** -0.5
    for hp in range(n_pairs):
        cols = slice(hp * LANES, (hp + 1) * LANES)
        q = q_ref[:, cols]
        if use_rope:
            q = _rope(q, cq_ref[...], sq_ref[...])
        ks, vs = kall[:, cols], vall[:, cols]
        outs = []
        for h in range(2):
            pv, inv = [], []
            for m in range(2):
                j = 2 * h + m
                qm = jnp.where((lane >= DIFF_QK * j) & (lane < DIFF_QK * (j + 1)), q, 0.0)
                s = _bdot_nt(qm, ks)
                e = jnp.exp2((s - jnp.max(s, axis=-1, keepdims=True)) * (scale * LOG2_E))
                inv.append(1.0 / jnp.sum(e, axis=-1, keepdims=True))
                pv.append(_bdot(e, vs))
            outs.append(pv[0] * inv[0] - pv[1] * (lam * inv[1]))
        o = jnp.where(lo, outs[0], outs[1])
        ms = _half_sum(o * o, lo) * (1.0 / DIFF_V)
        o_ref[:, cols] = (o * lax.rsqrt(ms + HEAD_RMS_EPS)) * g_ref[...] * (1.0 - lam_init)


def _attention(p, seq, tq, n_pairs, lam_init, lams, g2s, li, ctx=None, rope=None, kv_out=None):
    emit_kv = None if kv_out is None else ("create" if isinstance(kv_out[0], jax.ShapeDtypeStruct) else "update")
    assert rope is None or n_pairs == 1
    t = p.shape[0]
    nb, nq = t // seq, seq // tq
    n_ctx = 0 if ctx is None else ctx[0].shape[2]
    width = n_pairs * LANES
    k_blk, v_blk = DIFF_QW // width, 2 * DIFF_QW // width
    in_specs = [
        pl.BlockSpec((tq, width), lambda b, hp, qi: (b * nq + qi, hp)),
        pl.BlockSpec((seq, width), lambda b, hp, qi: (b, k_blk + hp)),
        pl.BlockSpec((seq, width), lambda b, hp, qi: (b, v_blk + hp)),
    ]
    args = [p, p, p]
    if ctx is not None:
        in_specs += [pl.BlockSpec((1, 1, n_ctx, width), lambda b, hp, qi: (b, hp, 0, 0))] * 2
        args += list(ctx)
    if rope is not None:
        in_specs += [pl.BlockSpec((tq, LANES), lambda b, hp, qi: (qi, 0))] * 2
        in_specs += [pl.BlockSpec((seq, width), lambda b, hp, qi: (0, 0))] * 2
        args += [rope[0], rope[1], rope[0], rope[1]]
    in_specs += [pl.BlockSpec((None, 4, DIFF_QK), lambda b, hp, qi: (li, 0, 0)),
                 pl.BlockSpec((None, 1, LANES), lambda b, hp, qi: (li, 0, 0))]
    args += [lams, g2s]
    out_shape = [jax.ShapeDtypeStruct((t, DIFF_WIDTH), F32)]
    out_specs = [pl.BlockSpec((tq, width), lambda b, hp, qi: (b * nq + qi, hp))]
    aliases = {}
    if emit_kv:
        out_shape += [jax.ShapeDtypeStruct(a.shape, a.dtype) for a in kv_out]
        if emit_kv == "update":
            aliases = {len(args): 1, len(args) + 1: 2}
            in_specs += [pl.BlockSpec(memory_space=pl.ANY)] * 2
            args += list(kv_out)
            slab, at = None, li
        else:
            slab, at = kv_out[0].shape[1], 0
        out_specs += [pl.BlockSpec((1, slab, 2 * n_pairs, 2, seq, DIFF_QK), lambda b, hp, qi: (b, at, hp, 0, 0, 0)),
                      pl.BlockSpec((1, slab, 2 * n_pairs, seq, DIFF_V), lambda b, hp, qi: (b, at, hp, 0, 0))]
    return pl.pallas_call(
        functools.partial(_attn_kernel, n_ctx, rope is not None, emit_kv, li, lam_init),
        out_shape=out_shape,
        grid=(nb, DIFF_HEADS // 2 // n_pairs, nq),
        in_specs=in_specs,
        out_specs=out_specs,
        scratch_shapes=[pltpu.VMEM((n_ctx + seq, width), BF16), pltpu.VMEM((n_ctx + seq, width), BF16)],
        input_output_aliases=aliases,
        compiler_params=_params("parallel", "parallel", "arbitrary"),
        name="diff_attn",
    )(*args)


def _rwkv_kernel(has_s0, emit_states, layer, *refs):
    it = iter(refs)
    r_ref, k_ref, v_ref, wa_ref, gd_ref = (next(it) for _ in range(5))
    mu_r, mu_k, mu_v, mu_wa, mu_gd = (next(it) for _ in range(5))
    w0_ref, a0_ref, wa_up_ref, g_up_ref, vec_ref = (next(it) for _ in range(5))
    s0f_ref = s0b_ref = sf_ref = sb_ref = None
    if has_s0:
        s0f_ref, s0b_ref = next(it), next(it)
    if emit_states == "update":
        next(it), next(it)
    o_ref = next(it)
    if emit_states:
        sf_ref, sb_ref = next(it), next(it)
    r_s, v_s, kk_s = next(it), next(it), next(it)
    lw_s, kd_s, b_s, y_s = next(it), next(it), next(it), next(it)
    st_s = next(it)

    seq = r_ref.shape[0]
    n_pairs = r_ref.shape[1] // LANES
    n_chunks = seq // CHUNK
    unroll = min(SCAN_UNROLL, n_chunks)

    def shifted(x, mu):
        prev, nxt = _shift_rows(x)
        return x + mu[0:1] * (prev - x) + mu[1:2] * (nxt - x)

    lane = _iota((seq, LANES), 1)
    lo = lane < RWKV_HS
    wa = shifted(wa_ref[...], mu_wa[...])
    gd_act = _sigmoid(shifted(gd_ref[...], mu_gd[...]))
    wd_in = jnp.where(lo, jnp.tanh(wa), 0.0)
    ad_in = jnp.where(lo, 0.0, wa)
    gates, bonuses, ln_gs, ln_bs = [], [], [], []
    for hp in range(n_pairs):
        cols = slice(hp * LANES, (hp + 1) * LANES)
        r = shifted(r_ref[:, cols], mu_r[:, cols])
        k = shifted(k_ref[:, cols], mu_k[:, cols])
        v = shifted(v_ref[:, cols], mu_v[:, cols])
        vec = vec_ref[:, cols]
        k_k, k_a, r_k, ln_g, ln_b = (vec[i:i + 1] for i in range(5))
        gates.append(_bdot(gd_act, g_up_ref[:, cols]))
        ln_gs.append(ln_g)
        ln_bs.append(ln_b)
        kk = k * k_k
        kk = kk * jnp.minimum(lax.rsqrt(_half_sum(kk * kk, lo)), 1e12)
        r_s[hp] = r
        v_s[hp] = v
        kk_s[hp] = kk
        bonus = jnp.zeros((seq, LANES), F32)
        for d in range(2):
            wpre = w0_ref[d:d + 1, cols] + _bdot(wd_in, wa_up_ref[d, :, cols])
            sp = jnp.maximum(-wpre, 0.0) + jnp.log(1.0 + jnp.exp(-jnp.abs(wpre)))
            lw_s[2 * hp + d] = -jnp.exp(-sp - 0.5)
            a = _sigmoid(a0_ref[d:d + 1, cols] + _bdot(ad_in, wa_up_ref[d, :, cols]))
            kd = k * (1.0 + (a - 1.0) * k_a)
            kd_s[2 * hp + d] = kd
            b_s[2 * hp + d] = kk * a
            bonus = bonus + _half_sum(r * kd * r_k, lo) * v
        bonuses.append(bonus)

    ci = _iota((CHUNK, CHUNK), 0)
    cj = _iota((CHUNK, CHUNK), 1)
    eye = jnp.where(ci == cj, 1.0, 0.0)
    blk8 = (ci // 8) == (cj // 8)
    merge_sizes = [8 * 2 ** n for n in range(int(math.log2(CHUNK // 8)))]
    merge_masks = [((ci // (2 * s)) == (cj // (2 * s))) & ((ci // s) != (cj // s)) for s in merge_sizes]
    before = (cj < ci, cj > ci)
    upto = (cj <= ci, cj >= ci)
    tri = tuple(jnp.where(m, 1.0, 0.0).astype(BF16) for m in upto)
    clane = _iota((CHUNK, LANES), 1)
    head_lanes = (clane < RWKV_HS, clane >= RWKV_HS)
    si = _iota((LANES, LANES), 0)
    sj = _iota((LANES, LANES), 1)
    same_head = (si < RWKV_HS) == (sj < RWKV_HS)
    diag = si == sj
    row_h0 = _iota((LANES, CHUNK), 0) < RWKV_HS

    def chunk_group(g):
        items = []
        for hp in range(n_pairs):
            items += [(2 * hp, g * unroll + u) for u in range(unroll)]
            items += [(2 * hp + 1, n_chunks - 1 - (g * unroll + u)) for u in range(unroll)]
        n_it = len(items)
        scans = [sc for sc, _ in items]
        dirs = [sc % 2 for sc in scans]
        rows = [pl.ds(pl.multiple_of(c * CHUNK, CHUNK), CHUNK) for _, c in items]
        r_c = [r_s[sc // 2, rw, :] for sc, rw in zip(scans, rows)]
        v_c = [v_s[sc // 2, rw, :] for sc, rw in zip(scans, rows)]
        kk_c = [kk_s[sc // 2, rw, :] for sc, rw in zip(scans, rows)]
        lw_c = [lw_s[sc, rw, :] for sc, rw in zip(scans, rows)]
        kd_c = [kd_s[sc, rw, :] for sc, rw in zip(scans, rows)]
        b_c = [b_s[sc, rw, :] for sc, rw in zip(scans, rows)]
        cum = [_dot_exact_lhs(tri[d], lw) for d, lw in zip(dirs, lw_c)]
        tot = [cm[CHUNK - 1:CHUNK, :] if d == 0 else cm[0:1, :] for d, cm in zip(dirs, cum)]
        e_neg = [jnp.exp(-cm) for cm in cum]
        al = [jnp.exp(cm - lw) * kk for cm, lw, kk in zip(cum, lw_c, kk_c)]
        be = [b * e for b, e in zip(b_c, e_neg)]
        ka = [kd * e for kd, e in zip(kd_c, e_neg)]
        rh = [r * jnp.exp(cm) for r, cm in zip(r_c, cum)]
        wc = [jnp.exp(tt) for tt in tot]
        combos = [(i, h) for i in range(n_it) for h in range(2)]
        x2 = [jnp.concatenate([a, r], axis=0) for a, r in zip(al, rh)]
        heads_of = lambda m_t: jnp.concatenate([jnp.where(row_h0, m_t, 0.0), jnp.where(row_h0, 0.0, m_t)], axis=1)
        xb2 = [_bdot(x, heads_of(b.T)) for x, b in zip(x2, be)]
        xk2 = [_bdot(x, heads_of(k_.T)) for x, k_ in zip(x2, ka)]
        xb = [xb2[i][:, h * CHUNK:(h + 1) * CHUNK] for i, h in combos]
        xk = [xk2[i][:, h * CHUNK:(h + 1) * CHUNK] for i, h in combos]
        n_mat = [jnp.where(before[dirs[i]], -x[0:CHUNK], 0.0) for x, (i, _) in zip(xb, combos)]
        g_mat = [jnp.where(before[dirs[i]], x[0:CHUNK], 0.0) for x, (i, _) in zip(xk, combos)]
        m2 = [jnp.where(upto[dirs[i]], x[CHUNK:], 0.0) for x, (i, _) in zip(xb, combos)]
        m1 = [jnp.where(upto[dirs[i]], x[CHUNK:], 0.0) for x, (i, _) in zip(xk, combos)]
        nd = [jnp.where(blk8, n, 0.0) for n in n_mat]
        t = [eye + x for x in nd]
        p = [_bdot(x, x) for x in nd]
        t = [a + _bdot(a, b) for a, b in zip(t, p)]
        p = [_bdot(x, x) for x in p]
        t = [a + _bdot(a, b) for a, b in zip(t, p)]
        for msk in merge_masks:
            q = [_bdot(jnp.where(msk, n, 0.0), a) for n, a in zip(n_mat, t)]
            t = [a + _bdot(a, b) for a, b in zip(t, q)]
        gv = [_bdot(g_, v_c[i]) for g_, (i, _) in zip(g_mat, combos)]
        z = [_bdot(t_, jnp.concatenate([al[i], g_], axis=1)) for t_, g_, (i, _) in zip(t, gv, combos)]
        yv_h = [_bdot(m, v_c[i]) for m, (i, _) in zip(m1, combos)]
        pick = lambda a, b: jnp.where(head_lanes[0], a, b)
        alp = [pick(z[2 * i][:, :LANES], z[2 * i + 1][:, :LANES]) for i in range(n_it)]
        uv = [pick(z[2 * i][:, LANES:], z[2 * i + 1][:, LANES:]) for i in range(n_it)]
        yv = [pick(yv_h[2 * i], yv_h[2 * i + 1]) for i in range(n_it)]
        be_t = [(b * w).T for b, w in zip(be, wc)]
        ka_t = [(k_ * w).T for k_, w in zip(ka, wc)]
        pz = [_bdot(bt, jnp.concatenate([a, u_], axis=1)) for bt, a, u_ in zip(be_t, alp, uv)]
        kv = [_bdot(kt, v_) for kt, v_ in zip(ka_t, v_c)]
        a_mat = [jnp.where(same_head, jnp.where(diag, w, 0.0) - pp[:, :LANES], 0.0) for w, pp in zip(wc, pz)]
        b_mat = [jnp.where(same_head, k_ - pp[:, LANES:], 0.0) for k_, pp in zip(kv, pz)]
        for i, sc in enumerate(scans):
            st = st_s[sc]
            xs = _bdot(jnp.concatenate([alp[i], rh[i]], axis=0), st)
            u = uv[i] + xs[0:CHUNK]
            m2u = pick(_bdot(m2[2 * i], u), _bdot(m2[2 * i + 1], u))
            y_s[sc, rows[i], :] = xs[CHUNK:] + yv[i] - m2u
            st_s[sc] = _bdot(a_mat[i], st) + b_mat[i]

    def body(g, carry):
        chunk_group(g)
        return carry

    for hp in range(n_pairs):
        if has_s0:
            st_s[2 * hp] = s0f_ref[0, hp]
            st_s[2 * hp + 1] = s0b_ref[0, hp]
        else:
            st_s[2 * hp] = jnp.zeros((LANES, LANES), F32)
            st_s[2 * hp + 1] = jnp.zeros((LANES, LANES), F32)
    lax.fori_loop(0, n_chunks // unroll, body, 0)
    for hp in range(n_pairs):
        if emit_states:
            slabs = [()] if emit_states == "update" else [(lj,) for lj in range(sf_ref.shape[1])]
            for dst, sc in ((sf_ref, 2 * hp), (sb_ref, 2 * hp + 1)):
                s_vk = st_s[sc].T
                for slab in slabs:
                    own = not slab or slab[0] == layer
                    for h, blk in enumerate((s_vk[0:RWKV_HS, 0:RWKV_HS], s_vk[RWKV_HS:, RWKV_HS:])):
                        dst[(0,) + slab + (2 * hp + h,)] = blk if own else jnp.zeros_like(blk)
        y = (y_s[2 * hp] + y_s[2 * hp + 1]) + bonuses[hp]
        mean = _half_sum(y, lo) * (1.0 / RWKV_HS)
        yc = y - mean
        var = _half_sum(yc * yc, lo) * (1.0 / RWKV_HS)
        o_ref[:, hp * LANES:(hp + 1) * LANES] = ((yc * lax.rsqrt(var + GN_EPS)) * ln_gs[hp] + ln_bs[hp]) * gates[hp]


def _rwkv(p, seq, n_pairs, li, mu, w0, a0, wa_up, g_up, vecs, s0=None, states_out=None):
    t = p.shape[0]
    nb = t // seq
    width = n_pairs * LANES
    n_grp = RWKV_WIDTH // width
    base = (2 * DIFF_QW + DIFF_WIDTH) // width
    tail = (2 * DIFF_QW + DIFF_WIDTH + 3 * RWKV_WIDTH) // LANES
    col = lambda blk: pl.BlockSpec((seq, width), lambda b, hp: (b, blk(hp)))
    mu_col = lambda blk: pl.BlockSpec((None, 2, width), lambda b, hp: (li, 0, blk(hp)))
    lane_blk = lambda k: pl.BlockSpec((seq, LANES), lambda b, hp: (b, tail + k))
    mu_lane_blk = lambda k: pl.BlockSpec((None, 2, LANES), lambda b, hp: (li, 0, 3 * RWKV_WIDTH // LANES + k))
    in_specs = [
        col(lambda hp: base + hp), col(lambda hp: base + n_grp + hp), col(lambda hp: base + 2 * n_grp + hp),
        lane_blk(0), lane_blk(1),
        mu_col(lambda hp: hp), mu_col(lambda hp: n_grp + hp), mu_col(lambda hp: 2 * n_grp + hp),
        mu_lane_blk(0), mu_lane_blk(1),
        pl.BlockSpec((None, 2, width), lambda b, hp: (li, 0, hp)),
        pl.BlockSpec((None, 2, width), lambda b, hp: (li, 0, hp)),
        pl.BlockSpec((None, 2, LANES, width), lambda b, hp: (li, 0, 0, hp)),
        pl.BlockSpec((None, LANES, width), lambda b, hp: (li, 0, hp)),
        pl.BlockSpec((None, 8, width), lambda b, hp: (li, 0, hp)),
    ]
    args = [p] * 5 + [mu] * 5 + [w0, a0, wa_up, g_up, vecs]
    if s0 is not None:
        in_specs += [pl.BlockSpec((1, n_pairs, LANES, LANES), lambda b, hp: (b, hp, 0, 0))] * 2
        args += list(s0)
    out_shape = [jax.ShapeDtypeStruct((t, RWKV_WIDTH), F32)]
    out_specs = [pl.BlockSpec((seq, width), lambda b, hp: (b, hp))]
    aliases = {}
    emit_states = None
    if states_out is not None:
        emit_states = "create" if isinstance(states_out[0], jax.ShapeDtypeStruct) else "update"
        out_shape += [jax.ShapeDtypeStruct(a.shape, a.dtype) for a in states_out]
        if emit_states == "update":
            aliases = {len(args): 1, len(args) + 1: 2}
            in_specs += [pl.BlockSpec(memory_space=pl.ANY)] * 2
            args += list(states_out)
            slab, at = None, li
        else:
            slab, at = states_out[0].shape[1], 0
        out_specs += [pl.BlockSpec((1, slab, 2 * n_pairs, RWKV_HS, RWKV_HS), lambda b, hp: (b, at, hp, 0, 0))] * 2
    return pl.pallas_call(
        functools.partial(_rwkv_kernel, s0 is not None, emit_states, li),
        out_shape=out_shape,
        grid=(nb, n_grp),
        in_specs=in_specs,
        out_specs=out_specs,
        scratch_shapes=[pltpu.VMEM((n_pairs, seq, LANES), F32)] * 3 + [pltpu.VMEM((2 * n_pairs, seq, LANES), F32)] * 4
        + [pltpu.VMEM((2 * n_pairs, LANES, LANES), F32)],
        input_output_aliases=aliases,
        compiler_params=_params("parallel", "parallel"),
        name="rwkv",
    )(*args)


def _pair_states_in(s):
    nb = s.shape[0]
    st = jnp.swapaxes(s.astype(F32), -1, -2).reshape(nb, RWKV_HEADS // 2, 2, RWKV_HS, RWKV_HS)
    z = jnp.zeros_like(st[:, :, 0])
    top = jnp.concatenate([st[:, :, 0], z], axis=-1)
    bot = jnp.concatenate([z, st[:, :, 1]], axis=-1)
    return jnp.concatenate([top, bot], axis=-2)


def _dft_kernel(f_ref, ft_ref):
    n = f_ref.shape[1]
    shape = (2 * n, n)
    fr = _iota(shape, 0)
    tau = _iota(shape, 1)
    is_sin = fr >= n
    kf = jnp.where(is_sin, fr - n, fr)
    ang = ((kf * tau) & (2 * n - 1)).astype(F32) * (math.pi / n)
    val = jnp.where(is_sin, -jnp.sin(ang), jnp.cos(ang))
    nyq = jnp.where((tau & 1) == 0, 1.0, -1.0)
    val = jnp.where(is_sin & (kf == 0), nyq, val)
    f_ref[...] = val.astype(BF16)
    ft_ref[...] = val.T.astype(BF16)


def _dft_mats(n):
    return pl.pallas_call(
        _dft_kernel,
        out_shape=(jax.ShapeDtypeStruct((2 * n, n), BF16), jax.ShapeDtypeStruct((n, 2 * n), BF16)),
        name="dft_mats",
    )()


def _filter_kernel(seq, w0t_ref, w0c_ref, w0s_ref, b0_ref, w1_ref, b1_ref, w2_ref, b2_ref, fr_ref,
                   w3a_ref, w3b_ref, f_ref, o_ref, prev_scr):
    cb = f_ref.shape[1]
    n_blk = seq // cb
    e = pl.program_id(0)
    first_lag = (e - n_blk) * cb
    tap_row = jnp.abs(first_lag + _iota((1, cb), 1)).astype(F32)
    pos_row = tap_row * (1.0 / (seq - 1))
    band = _iota((BANDS, 1), 0).astype(F32)
    freq = 1e-4 + band * ((BANDS - 1 - 1e-4) / (BANDS - 1))
    arg = freq * (tap_row * (2.0 * math.pi / seq))
    fr = fr_ref[...]
    pre = (w0t_ref[...] * pos_row + _dot_f32(w0c_ref[...], jnp.cos(arg)) + _dot_f32(w0s_ref[...], -jnp.sin(arg))
           + b0_ref[...])
    hdn = jnp.sin(fr[:, 0:1] * pre)
    hdn = jnp.sin(fr[:, 1:2] * (_dot_f32(w1_ref[...], hdn) + b1_ref[...]))
    hdn = jnp.sin(fr[:, 2:3] * (_dot_f32(w2_ref[...], hdn) + b2_ref[...]))
    hdn = hdn.T
    tap = jnp.abs(first_lag + _iota((cb, 1), 0))
    pos = tap.astype(F32) * (1.0 / (seq - 1))
    chan = _iota((1, D_MODEL), 1).astype(F32)
    delta = jnp.abs(MIN_DECAY + chan * ((MAX_DECAY - MIN_DECAY) / (D_MODEL - 1)))
    window = jnp.where(tap < seq, jnp.exp(-pos * delta), 0.0)
    row = _iota((2 * cb, 1), 0)
    sign = jnp.where((row & 1) == 1, -1.0, 1.0)
    for o, w3_ref in enumerate((w3a_ref, w3b_ref)):
        cols = slice(o * D_MODEL, (o + 1) * D_MODEL)
        taps = (_dot_f32(hdn, w3_ref[...]) * window).astype(BF16)
        spec = jnp.dot(f_ref[...], taps, preferred_element_type=F32)

        @pl.when(e > 0)
        def _():
            o_ref[0, :, cols] = spec + sign * prev_scr[:, cols]

        prev_scr[:, cols] = spec - jnp.where(row <= cb, taps[0:1, :].astype(F32), 0.0)


def _hyena_filters(seq, li, fmat, small, w3):
    cb = fmat.shape[1]
    n_blk = seq // cb
    layer_slice = lambda a: pl.BlockSpec((None,) + a.shape[1:], lambda e: (li, 0, 0))
    w3_spec = lambda o: pl.BlockSpec((None, FILTER_WIDTH, D_MODEL),
                                     lambda e: (li, 0, 2 * o + jnp.where(e < n_blk, 1, 0)))
    return pl.pallas_call(
        functools.partial(_filter_kernel, seq),
        out_shape=jax.ShapeDtypeStruct((2 * n_blk - 1, 2 * cb, 2 * D_MODEL), F32),
        grid=(2 * n_blk,),
        in_specs=[layer_slice(a) for a in small] + [w3_spec(0), w3_spec(1),
                                                    pl.BlockSpec((2 * cb, cb), lambda e: (0, 0))],
        out_specs=pl.BlockSpec((1, 2 * cb, 2 * D_MODEL), lambda e: (jnp.maximum(e - 1, 0), 0, 0)),
        scratch_shapes=[pltpu.VMEM((2 * cb, 2 * D_MODEL), F32)],
        compiler_params=_params("arbitrary"),
        name="hyena_filters",
    )(*small, w3, w3, fmat)


def _short_conv(z, w, b):
    prev, nxt = _shift_rows(z)
    return prev * w[0:1] + z * w[1:2] + nxt * w[2:3] + b


def _conv_kernel(conv_v, n_blk, v_ref, x_ref, wv_ref, bv_ref, wx_ref, bx_ref, f_ref, ft_ref, kc_ref, bias_ref,
                 o_ref, uf_scr, yf_scr):
    cb = f_ref.shape[1]
    v = v_ref[...]
    if conv_v:
        v = _short_conv(v, wv_ref[...], bv_ref[...])
    fmat = f_ref[...]
    for j in range(n_blk):
        uf_scr[j] = jnp.dot(fmat, v[j * cb:(j + 1) * cb].astype(BF16), preferred_element_type=F32)

    def mix(r0, first):
        re, im = pl.ds(r0, MIX_ROWS), pl.ds(cb + r0, MIX_ROWS)
        real_pair = (_iota((MIX_ROWS, 1), 0) == 0) if first else None
        for i in range(n_blk):
            acc_r = acc_i = None
            for j in range(n_blk):
                d = i - j + n_blk - 1
                ur, ui = uf_scr[j, re, :], uf_scr[j, im, :]
                kr, ki = kc_ref[d, re, :], kc_ref[d, im, :]
                uiki = ui * ki
                if first:
                    t_r = ur * kr - jnp.where(real_pair, 0.0, uiki)
                    t_i = jnp.where(real_pair, uiki, ur * ki + ui * kr)
                else:
                    t_r = ur * kr - uiki
                    t_i = ur * ki + ui * kr
                acc_r = t_r if acc_r is None else acc_r + t_r
                acc_i = t_i if acc_i is None else acc_i + t_i
            scale = jnp.where(real_pair, 0.5 / cb, 1.0 / cb) if first else 1.0 / cb
            yf_scr[i, re, :] = acc_r * scale
            yf_scr[i, im, :] = acc_i * scale

    mix(0, True)

    def body(r, carry):
        mix(pl.multiple_of(r * MIX_ROWS, MIX_ROWS), False)
        return carry

    lax.fori_loop(1, cb // MIX_ROWS, body, 0)

    x = _short_conv(x_ref[...], wx_ref[...], bx_ref[...])
    ftm = ft_ref[...]
    bias = bias_ref[...]
    for i in range(n_blk):
        rows = slice(i * cb, (i + 1) * cb)
        y = jnp.dot(ftm, yf_scr[i].astype(BF16), preferred_element_type=F32)
        o_ref[rows, :] = x[rows] * (y + v[rows] * bias)


def _hyena_conv(v_arr, v_blk0, x_arr, x_blk0, conv_v, seq, li, order, cw, cwb, fmat, fmat_t, kc, biases, width):
    t = v_arr.shape[0]
    cb = fmat.shape[1]
    nb, ncb, n_blk = t // seq, D_MODEL // width, seq // cb
    wv_blk0 = v_blk0 if conv_v else 0
    kc_blk0 = order * ncb
    return pl.pallas_call(
        functools.partial(_conv_kernel, conv_v, n_blk),
        out_shape=jax.ShapeDtypeStruct((t, D_MODEL), F32),
        grid=(nb, ncb),
        in_specs=[
            pl.BlockSpec((seq, width), lambda b, c: (b, v_blk0 + c)),
            pl.BlockSpec((seq, width), lambda b, c: (b, x_blk0 + c)),
            pl.BlockSpec((None, 3, width), lambda b, c: (li, 0, wv_blk0 + c)),
            pl.BlockSpec((None, 1, width), lambda b, c: (li, 0, wv_blk0 + c)),
            pl.BlockSpec((None, 3, width), lambda b, c: (li, 0, x_blk0 + c)),
            pl.BlockSpec((None, 1, width), lambda b, c: (li, 0, x_blk0 + c)),
            pl.BlockSpec((2 * cb, cb), lambda b, c: (0, 0)),
            pl.BlockSpec((cb, 2 * cb), lambda b, c: (0, 0)),
            pl.BlockSpec((2 * n_blk - 1, 2 * cb, width), lambda b, c: (0, 0, kc_blk0 + c)),
            pl.BlockSpec((None, 1, width), lambda b, c: (li * 2 + order, 0, c)),
        ],
        out_specs=pl.BlockSpec((seq, width), lambda b, c: (b, c)),
        scratch_shapes=[pltpu.VMEM((n_blk, 2 * cb, width), F32), pltpu.VMEM((n_blk, 2 * cb, width), F32)],
        compiler_params=_params("parallel", "parallel"),
        name="hyena_conv",
    )(v_arr, x_arr, cw, cwb, cw, cwb, fmat, fmat_t, kc, biases)


def _axial_rope_tables(seq):
    n_rows = seq // GRID_W
    row = np.repeat(np.arange(n_rows, dtype=np.float64), GRID_W)
    col = np.tile(np.arange(GRID_W, dtype=np.float64), n_rows)
    inv = ROPE_BASE ** (-np.arange(ROPE_PAIRS, dtype=np.float64) / ROPE_PAIRS)
    ar = row[:, None] * inv[None]
    ac = col[:, None] * inv[None]
    ang = np.concatenate([ar, ar, ac, ac], axis=-1)
    reps = LANES // DIFF_QK
    return (jnp.asarray(np.tile(np.cos(ang), (1, reps)), F32), jnp.asarray(np.tile(np.sin(ang), (1, reps)), F32))


def _pair_lanes(a, seq_axis):
    nb = a.shape[0]
    if a.ndim == 5:
        a = a.reshape(nb, DIFF_HEADS // 2, 2, 2, a.shape[3], DIFF_QK).transpose(0, 1, 4, 2, 3, 5)
    else:
        a = a.reshape(nb, DIFF_HEADS // 2, 2, a.shape[2], DIFF_V).transpose(0, 1, 3, 2, 4)
    return a.reshape(nb, DIFF_HEADS // 2, a.shape[2], LANES)


def kernel(x_prompt, x_sample, cache_diff_k, cache_diff_v, state_rwkv_fwd, state_rwkv_bwd, c, c_ctx, ada_w, ada_b, norm1_g, norm2_g, ffn_w1, ffn_w3, ffn_w2, final_g, ab_w_in, ab_w_out, diff_lambda, diff_subln_g, rwkv_mu, rwkv_w0, rwkv_w_up, rwkv_a0, rwkv_a_up, rwkv_g_up, rwkv_k_k, rwkv_k_a, rwkv_r_k, rwkv_ln_g, rwkv_ln_b, hy_w_in, hy_b_in, hy_conv_w, hy_conv_b, hy_f_w0, hy_f_b0, hy_f_w1, hy_f_b1, hy_f_w2, hy_f_b2, hy_f_w3, hy_f_freq, hy_bias, hy_w_out, hy_b_out):
    n_ctx_seqs, ctx_len, _ = x_prompt.shape
    n_lat_seqs, lat_len, _ = x_sample.shape
    tm = 1024

    cond = jnp.zeros((N_MOD_ROWS, D_MODEL), F32).at[0].set(c_ctx).at[1:1 + n_lat_seqs].set(c)
    mods = _ada_all(cond, ada_w, ada_b).reshape(DEPTH * N_MOD_ROWS * 6, 1, D_MODEL)

    groups = [
        dict(x=x_prompt.reshape(-1, D_MODEL), seq=ctx_len, rows=(0, n_ctx_seqs * ctx_len), tq=ctx_len, pairs=4,
             scan_pairs=2, conv_width=1024),
        dict(x=x_sample.reshape(-1, D_MODEL), seq=lat_len, rows=(1, lat_len), tq=512, pairs=1, scan_pairs=1,
             conv_width=256),
    ]
    rope = _axial_rope_tables(lat_len)
    dft = {cb: _dft_mats(cb) for cb in {min(CONV_BLOCK_MAX, g["seq"]) for g in groups}}
    n_ab, n_c = ab_w_in.shape[0], hy_w_in.shape[0]

    gains1 = norm1_g.reshape(DEPTH, 1, D_MODEL)
    gains2 = norm2_g.reshape(DEPTH, 1, D_MODEL)
    g2s = jnp.tile(diff_subln_g, (1, 2)).reshape(n_ab, 1, LANES)
    wa_up = jnp.concatenate([rwkv_w_up, rwkv_a_up], axis=2)
    vec_rows = [rwkv_k_k, rwkv_k_a, rwkv_r_k.reshape(n_ab, RWKV_WIDTH), rwkv_ln_g, rwkv_ln_b]
    vecs = jnp.stack(vec_rows + [jnp.zeros_like(rwkv_k_k)] * (8 - len(vec_rows)), axis=1)
    w0_t = jnp.swapaxes(hy_f_w0, 1, 2)
    filt_small = [w0_t[:, :, 0:1], w0_t[:, :, 1:1 + BANDS], w0_t[:, :, 1 + BANDS:], hy_f_b0[:, :, None],
                  jnp.swapaxes(hy_f_w1, 1, 2), hy_f_b1[:, :, None], jnp.swapaxes(hy_f_w2, 1, 2), hy_f_b2[:, :, None],
                  jnp.swapaxes(hy_f_freq, 1, 2)]
    hy_b_in_s = hy_b_in.reshape(n_c, 1, -1)
    hy_b_out_s = hy_b_out.reshape(n_c, 1, D_MODEL)
    conv_b = hy_conv_b.reshape(n_c, 1, -1)
    conv_bias = hy_bias.reshape(n_c * 2, 1, D_MODEL)
    new_kv = (jax.ShapeDtypeStruct((n_ctx_seqs, n_ab, DIFF_HEADS, 2, ctx_len, DIFF_QK), F32),
              jax.ShapeDtypeStruct((n_ctx_seqs, n_ab, DIFF_HEADS, ctx_len, DIFF_V), F32))
    new_states = (jax.ShapeDtypeStruct((n_ctx_seqs, n_ab, RWKV_HEADS, RWKV_HS, RWKV_HS), F32),) * 2

    for l in range(DEPTH):
        i = l // 2
        for gi, g in enumerate(groups):
            x, seq, row = g["x"], g["seq"], g["rows"]
            if l % 2 == 0:
                p = _proj_in(x, gains1, mods, l, row, ab_w_in, i, None, tm, AB_IN // 2)
                if gi == 0:
                    att, *new_kv = _attention(p, seq, g["tq"], g["pairs"], _lambda_init(l), diff_lambda, g2s, i,
                                              kv_out=new_kv)
                    mix, *new_states = _rwkv(p, seq, g["scan_pairs"], i, rwkv_mu, rwkv_w0, rwkv_a0, wa_up, rwkv_g_up,
                                             vecs, states_out=new_states)
                else:
                    ctx = (_pair_lanes(cache_diff_k[:, i], 3), _pair_lanes(cache_diff_v[:, i], 2))
                    att, = _attention(p, seq, g["tq"], g["pairs"], _lambda_init(l), diff_lambda, g2s, i,
                                      ctx=ctx, rope=rope)
                    s0 = (_pair_states_in(state_rwkv_fwd[:, i]), _pair_states_in(state_rwkv_bwd[:, i]))
                    mix, = _rwkv(p, seq, g["scan_pairs"], i, rwkv_mu, rwkv_w0, rwkv_a0, wa_up, rwkv_g_up, vecs, s0=s0)
                x = _proj_out(x, mods, l, row, [att, mix], ab_w_out, i, None, tm)
            else:
                fmat, fmat_t = dft[min(CONV_BLOCK_MAX, seq)]
                z = _proj_in(x, gains1, mods, l, row, hy_w_in, i, hy_b_in_s, tm, 1536)
                kc = _hyena_filters(seq, i, fmat, filt_small, hy_f_w3)
                width = g["conv_width"]
                ncb = D_MODEL // width
                u = _hyena_conv(z, 2 * ncb, z, 0, True, seq, i, 0, hy_conv_w, conv_b, fmat, fmat_t, kc, conv_bias, width)
                u = _hyena_conv(u, 0, z, ncb, False, seq, i, 1, hy_conv_w, conv_b, fmat, fmat_t, kc, conv_bias, width)
                x = _proj_out(x, mods, l, row, [u], hy_w_out, i, hy_b_out_s, tm)
            g["x"] = _ffn(x, gains2, mods, l, row, ffn_w1, ffn_w3, ffn_w2, final_g, l == DEPTH - 1, 2 * tm, 256)

    y_prompt = groups[0]["x"].reshape(x_prompt.shape)
    y_sample = groups[1]["x"].reshape(x_sample.shape)
    return (y_prompt, y_sample, new_kv[0], new_kv[1], new_states[0], new_states[1])
```

```python
import functools
import math

import jax
import jax.numpy as jnp
import numpy as np
from jax import lax
from jax.experimental import pallas as pl
from jax.experimental.pallas import tpu as pltpu

F32 = jnp.float32
BF16 = jnp.bfloat16

D_MODEL = 1024
DEPTH = 4
GRID_W = 64
DIFF_HEADS = 8
DIFF_QK = 32
DIFF_V = 64
DIFF_QW = DIFF_HEADS * 2 * DIFF_QK
DIFF_WIDTH = DIFF_HEADS * DIFF_V
ROPE_PAIRS = DIFF_QK // 4
ROPE_BASE = 10000.0
RWKV_HEADS = 8
RWKV_HS = 64
RWKV_WIDTH = RWKV_HEADS * RWKV_HS
RWKV_IN = 3 * RWKV_WIDTH + 64 + 64 + 128
AB_IN = 2 * DIFF_QW + DIFF_WIDTH + RWKV_IN
BANDS = 16
FILTER_WIDTH = 64
MAX_DECAY = math.log(1e-2) / 0.3
MIN_DECAY = math.log(1e-2) / 1.5
D_FF = 2816
RMS_EPS = 1e-6
GN_EPS = 64e-5
HEAD_RMS_EPS = 1e-5
LOG2_E = math.log2(math.e)

LANES = 128
CHUNK = 128
SCAN_UNROLL = 4
CONV_BLOCK_MAX = 512
MIX_ROWS = 8
VMEM_LIMIT = 56 * 1024 * 1024

N_MOD_ROWS = 8


def _lambda_init(l):
    return 0.8 - 0.6 * math.exp(-0.3 * l)


def _params(*sem):
    return pltpu.CompilerParams(dimension_semantics=sem, vmem_limit_bytes=VMEM_LIMIT)


def _bdot(a, b):
    return jnp.dot(a.astype(BF16), b.astype(BF16), preferred_element_type=F32)


def _bdot_nt(a, b):
    return lax.dot_general(a.astype(BF16), b.astype(BF16), (((1,), (1,)), ((), ())),
                           preferred_element_type=F32)


def _split3(x):
    h1 = x.astype(BF16)
    r1 = x - h1.astype(F32)
    h2 = r1.astype(BF16)
    h3 = (r1 - h2.astype(F32)).astype(BF16)
    return h1, h2, h3


def _dot_f32(a, b):
    a1, a2, a3 = _split3(a)
    b1, b2, b3 = _split3(b)
    d = lambda x, y: jnp.dot(x, y, preferred_element_type=F32)
    return (d(a1, b1) + (d(a1, b2) + d(a2, b1))) + ((d(a1, b3) + d(a3, b1)) + d(a2, b2))


def _dot_exact_lhs(a_bf16, b):
    b1, b2, b3 = _split3(b)
    d = lambda y: jnp.dot(a_bf16, y, preferred_element_type=F32)
    return d(b1) + (d(b2) + d(b3))


def _sigmoid(x):
    return 0.5 * (jnp.tanh(0.5 * x) + 1.0)


def _iota(shape, dim):
    return lax.broadcasted_iota(jnp.int32, shape, dim)


def _shift_rows(x):
    n = x.shape[0]
    row = _iota(x.shape, 0)
    prev = jnp.where(row == 0, 0.0, pltpu.roll(x, 1, 0))
    nxt = jnp.where(row == n - 1, 0.0, pltpu.roll(x, n - 1, 0))
    return prev, nxt


def _half_sum(x, lo_mask):
    s_lo = jnp.sum(jnp.where(lo_mask, x, 0.0), axis=-1, keepdims=True)
    s_hi = jnp.sum(jnp.where(lo_mask, 0.0, x), axis=-1, keepdims=True)
    return jnp.where(lo_mask, s_lo, s_hi)


def _ada_kernel(c_ref, w_ref, b_ref, o_ref):
    c = c_ref[...]
    o_ref[0] = _bdot(c * _sigmoid(c), w_ref[0]) + b_ref[0]


def _ada_all(cond, ada_w, ada_b):
    tn = 1536
    n_out = 6 * D_MODEL
    return pl.pallas_call(
        _ada_kernel,
        out_shape=jax.ShapeDtypeStruct((DEPTH, N_MOD_ROWS, n_out), F32),
        grid=(DEPTH, n_out // tn),
        in_specs=[
            pl.BlockSpec((N_MOD_ROWS, D_MODEL), lambda l, j: (0, 0)),
            pl.BlockSpec((1, D_MODEL, tn), lambda l, j: (l, 0, j)),
            pl.BlockSpec((1, 1, tn), lambda l, j: (l, 0, j)),
        ],
        out_specs=pl.BlockSpec((1, N_MOD_ROWS, tn), lambda l, j: (l, 0, j)),
        compiler_params=_params("parallel", "parallel"),
        name="ada",
    )(cond, ada_w, ada_b.reshape(DEPTH, 1, n_out))


def _mod_spec(layer, chunk, cond_rows, tm, axis=0):
    first, span = cond_rows

    def index_map(*idx):
        return ((layer * N_MOD_ROWS + first + (idx[axis] * tm) // span) * 6 + chunk, 0, 0)
    return pl.BlockSpec((1, 1, D_MODEL), index_map)


def _normed(x, g, sc, sh):
    ms = jnp.mean(x * x, axis=-1, keepdims=True)
    return (x * lax.rsqrt(ms + RMS_EPS)) * g * (1.0 + sc) + sh


def _proj_in_kernel(has_bias, x_ref, g_ref, sc_ref, sh_ref, w_ref, *refs):
    o_ref = refs[-1]
    h = _normed(x_ref[...], g_ref[...], sc_ref[0], sh_ref[0])
    y = _bdot(h, w_ref[...])
    o_ref[...] = y + refs[0][...] if has_bias else y


def _proj_in(x, gains, mods, layer, cond_rows, ws, wi, bs, tm, tn):
    t, n_out = x.shape[0], ws.shape[2]
    in_specs = [
        pl.BlockSpec((tm, D_MODEL), lambda j, i: (i, 0)),
        pl.BlockSpec((None, 1, D_MODEL), lambda j, i: (layer, 0, 0)),
        _mod_spec(layer, 1, cond_rows, tm, axis=1),
        _mod_spec(layer, 0, cond_rows, tm, axis=1),
        pl.BlockSpec((None, D_MODEL, tn), lambda j, i: (wi, 0, j)),
    ]
    args = [x, gains, mods, mods, ws]
    if bs is not None:
        in_specs.append(pl.BlockSpec((None, 1, tn), lambda j, i: (wi, 0, j)))
        args.append(bs)
    return pl.pallas_call(
        functools.partial(_proj_in_kernel, bs is not None),
        out_shape=jax.ShapeDtypeStruct((t, n_out), F32),
        grid=(n_out // tn, t // tm),
        in_specs=in_specs,
        out_specs=pl.BlockSpec((tm, tn), lambda j, i: (i, j)),
        compiler_params=_params("parallel", "parallel"),
        name="proj_in",
    )(*args)


def _proj_out_kernel(n_u, has_bias, x_ref, gt_ref, *refs):
    u_refs, w_ref, o_ref = refs[:n_u], refs[n_u], refs[-1]
    acc = None
    off = 0
    for u_ref in u_refs:
        k = u_ref.shape[1]
        part = _bdot(u_ref[...], w_ref[off:off + k, :])
        acc = part if acc is None else acc + part
        off += k
    if has_bias:
        acc = acc + refs[n_u + 1][...]
    o_ref[...] = x_ref[...] + gt_ref[0] * acc


def _proj_out(x, mods, layer, cond_rows, us, ws, wi, bs, tm):
    t = x.shape[0]
    in_specs = [
        pl.BlockSpec((tm, D_MODEL), lambda i: (i, 0)),
        _mod_spec(layer, 2, cond_rows, tm),
        *[pl.BlockSpec((tm, u.shape[1]), lambda i: (i, 0)) for u in us],
        pl.BlockSpec((None,) + ws.shape[1:], lambda i: (wi, 0, 0)),
    ]
    args = [x, mods, *us, ws]
    if bs is not None:
        in_specs.append(pl.BlockSpec((None, 1, D_MODEL), lambda i: (wi, 0, 0)))
        args.append(bs)
    return pl.pallas_call(
        functools.partial(_proj_out_kernel, len(us), bs is not None),
        out_shape=jax.ShapeDtypeStruct((t, D_MODEL), F32),
        grid=(t // tm,),
        in_specs=in_specs,
        out_specs=pl.BlockSpec((tm, D_MODEL), lambda i: (i, 0)),
        compiler_params=_params("parallel"),
        name="proj_out",
    )(*args)


def _ffn_kernel(final, x_ref, g_ref, sc_ref, sh_ref, gt_ref, w1_ref, w3_ref, w2_ref, fg_ref, o_ref, h_scr):
    k = pl.program_id(1)

    @pl.when(k == 0)
    def _():
        h_scr[...] = _normed(x_ref[...], g_ref[...], sc_ref[0], sh_ref[0]).astype(BF16)
        o_ref[...] = jnp.zeros_like(o_ref)

    h = h_scr[...]
    a1 = jnp.dot(h, w1_ref[...].astype(BF16), preferred_element_type=F32)
    a3 = jnp.dot(h, w3_ref[...].astype(BF16), preferred_element_type=F32)
    o_ref[...] += _bdot((a1 * _sigmoid(a1)) * a3, w2_ref[...])

    @pl.when(k == pl.num_programs(1) - 1)
    def _():
        y = x_ref[...] + gt_ref[0] * o_ref[...]
        if final:
            ms = jnp.mean(y * y, axis=-1, keepdims=True)
            y = (y * lax.rsqrt(ms + RMS_EPS)) * fg_ref[...]
        o_ref[...] = y


def _ffn(x, gains, mods, layer, cond_rows, w1s, w3s, w2s, final_g, final, tm, tk):
    t = x.shape[0]
    return pl.pallas_call(
        functools.partial(_ffn_kernel, final),
        out_shape=jax.ShapeDtypeStruct((t, D_MODEL), F32),
        grid=(t // tm, D_FF // tk),
        in_specs=[
            pl.BlockSpec((tm, D_MODEL), lambda i, k: (i, 0)),
            pl.BlockSpec((None, 1, D_MODEL), lambda i, k: (layer, 0, 0)),
            _mod_spec(layer, 4, cond_rows, tm),
            _mod_spec(layer, 3, cond_rows, tm),
            _mod_spec(layer, 5, cond_rows, tm),
            pl.BlockSpec((None, D_MODEL, tk), lambda i, k: (layer, 0, k)),
            pl.BlockSpec((None, D_MODEL, tk), lambda i, k: (layer, 0, k)),
            pl.BlockSpec((None, tk, D_MODEL), lambda i, k: (layer, k, 0)),
            pl.BlockSpec((1, D_MODEL), lambda i, k: (0, 0)),
        ],
        out_specs=pl.BlockSpec((tm, D_MODEL), lambda i, k: (i, 0)),
        scratch_shapes=[pltpu.VMEM((tm, D_MODEL), BF16)],
        compiler_params=_params("parallel", "arbitrary"),
        name="ffn",
    )(x, gains, mods, mods, mods, w1s, w3s, w2s, final_g.reshape(1, D_MODEL))


def _rope(x, cos, sin):
    lane = _iota(x.shape, 1)
    rot = jnp.where((lane % 16) < 8, -pltpu.roll(x, LANES - 8, 1), pltpu.roll(x, 8, 1))
    return x * cos + rot * sin


def _attn_kernel(n_ctx, use_rope, emit_kv, layer, lam_init, *refs):
    it = iter(refs)
    q_ref, k_ref, v_ref = next(it), next(it), next(it)
    kc_ref = vc_ref = cq_ref = sq_ref = ck_ref = sk_ref = ko_ref = vo_ref = None
    if n_ctx:
        kc_ref, vc_ref = next(it), next(it)
    if use_rope:
        cq_ref, sq_ref, ck_ref, sk_ref = next(it), next(it), next(it), next(it)
    lam_ref, g_ref = next(it), next(it)
    if emit_kv == "update":
        next(it), next(it)
    o_ref = next(it)
    if emit_kv:
        ko_ref, vo_ref = next(it), next(it)
    kall, vall = next(it), next(it)
    n_own = k_ref.shape[0]
    n_pairs = q_ref.shape[1] // LANES

    @pl.when(pl.program_id(2) == 0)
    def _():
        k = k_ref[...]
        v = v_ref[...]
        if emit_kv:
            slabs = [()] if emit_kv == "update" else [(lj,) for lj in range(ko_ref.shape[1])]
            for slab in slabs:
                own = not slab or slab[0] == layer
                for h in range(2 * n_pairs):
                    for m in range(2):
                        c0 = (2 * h + m) * DIFF_QK
                        kh = k[:, c0:c0 + DIFF_QK]
                        ko_ref[(0,) + slab + (h, m)] = kh if own else jnp.zeros_like(kh)
                    vh = v[:, h * DIFF_V:(h + 1) * DIFF_V]
                    vo_ref[(0,) + slab + (h,)] = vh if own else jnp.zeros_like(vh)
        if use_rope:
            k = _rope(k, ck_ref[...], sk_ref[...])
        if n_ctx:
            kall[0:n_ctx, :] = kc_ref[0, 0].astype(BF16)
            vall[0:n_ctx, :] = vc_ref[0, 0].astype(BF16)
        kall[n_ctx:n_ctx + n_own, :] = k.astype(BF16)
        vall[n_ctx:n_ctx + n_own, :] = v.astype(BF16)

    lv = lam_ref[...]
    lam = (jnp.exp(jnp.sum(lv[0:1] * lv[1:2], axis=-1, keepdims=True))
           - jnp.exp(jnp.sum(lv[2:3] * lv[3:4], axis=-1, keepdims=True)) + lam_init)

    lane = _iota((q_ref.shape[0], LANES), 1)
    lo = lane < DIFF_V
    scale = DIFF_QK ** -0.5
    for hp in range(n_pairs):
        cols = slice(hp * LANES, (hp + 1) * LANES)
        q = q_ref[:, cols]
        if use_rope:
            q = _rope(q, cq_ref[...], sq_ref[...])
        ks, vs = kall[:, cols], vall[:, cols]
        outs = []
        for h in range(2):
            pv, inv = [], []
            for m in range(2):
                j = 2 * h + m
                qm = jnp.where((lane >= DIFF_QK * j) & (lane < DIFF_QK * (j + 1)), q, 0.0)
                s = _bdot_nt(qm, ks)
                e = jnp.exp2((s - jnp.max(s, axis=-1, keepdims=True)) * (scale * LOG2_E))
                inv.append(1.0 / jnp.sum(e, axis=-1, keepdims=True))
                pv.append(_bdot(e, vs))
            outs.append(pv[0] * inv[0] - pv[1] * (lam * inv[1]))
        o = jnp.where(lo, outs[0], outs[1])
        ms = _half_sum(o * o, lo) * (1.0 / DIFF_V)
        o_ref[:, cols] = (o * lax.rsqrt(ms + HEAD_RMS_EPS)) * g_ref[...] * (1.0 - lam_init)


def _attention(p, seq, tq, n_pairs, lam_init, lams, g2s, li, ctx=None, rope=None, kv_out=None):
    emit_kv = None if kv_out is None else ("create" if isinstance(kv_out[0], jax.ShapeDtypeStruct) else "update")
    assert rope is None or n_pairs == 1
    t = p.shape[0]
    nb, nq = t // seq, seq // tq
    n_ctx = 0 if ctx is None else ctx[0].shape[2]
    width = n_pairs * LANES
    k_blk, v_blk = DIFF_QW // width, 2 * DIFF_QW // width
    in_specs = [
        pl.BlockSpec((tq, width), lambda b, hp, qi: (b * nq + qi, hp)),
        pl.BlockSpec((seq, width), lambda b, hp, qi: (b, k_blk + hp)),
        pl.BlockSpec((seq, width), lambda b, hp, qi: (b, v_blk + hp)),
    ]
    args = [p, p, p]
    if ctx is not None:
        in_specs += [pl.BlockSpec((1, 1, n_ctx, width), lambda b, hp, qi: (b, hp, 0, 0))] * 2
        args += list(ctx)
    if rope is not None:
        in_specs += [pl.BlockSpec((tq, LANES), lambda b, hp, qi: (qi, 0))] * 2
        in_specs += [pl.BlockSpec((seq, width), lambda b, hp, qi: (0, 0))] * 2
        args += [rope[0], rope[1], rope[0], rope[1]]
    in_specs += [pl.BlockSpec((None, 4, DIFF_QK), lambda b, hp, qi: (li, 0, 0)),
                 pl.BlockSpec((None, 1, LANES), lambda b, hp, qi: (li, 0, 0))]
    args += [lams, g2s]
    out_shape = [jax.ShapeDtypeStruct((t, DIFF_WIDTH), F32)]
    out_specs = [pl.BlockSpec((tq, width), lambda b, hp, qi: (b * nq + qi, hp))]
    aliases = {}
    if emit_kv:
        out_shape += [jax.ShapeDtypeStruct(a.shape, a.dtype) for a in kv_out]
        if emit_kv == "update":
            aliases = {len(args): 1, len(args) + 1: 2}
            in_specs += [pl.BlockSpec(memory_space=pl.ANY)] * 2
            args += list(kv_out)
            slab, at = None, li
        else:
            slab, at = kv_out[0].shape[1], 0
        out_specs += [pl.BlockSpec((1, slab, 2 * n_pairs, 2, seq, DIFF_QK), lambda b, hp, qi: (b, at, hp, 0, 0, 0)),
                      pl.BlockSpec((1, slab, 2 * n_pairs, seq, DIFF_V), lambda b, hp, qi: (b, at, hp, 0, 0))]
    return pl.pallas_call(
        functools.partial(_attn_kernel, n_ctx, rope is not None, emit_kv, li, lam_init),
        out_shape=out_shape,
        grid=(nb, DIFF_HEADS // 2 // n_pairs, nq),
        in_specs=in_specs,
        out_specs=out_specs,
        scratch_shapes=[pltpu.VMEM((n_ctx + seq, width), BF16), pltpu.VMEM((n_ctx + seq, width), BF16)],
        input_output_aliases=aliases,
        compiler_params=_params("parallel", "parallel", "arbitrary"),
        name="diff_attn",
    )(*args)


def _rwkv_kernel(has_s0, emit_states, layer, *refs):
    it = iter(refs)
    r_ref, k_ref, v_ref, wa_ref, gd_ref = (next(it) for _ in range(5))
    mu_r, mu_k, mu_v, mu_wa, mu_gd = (next(it) for _ in range(5))
    w0_ref, a0_ref, wa_up_ref, g_up_ref, vec_ref = (next(it) for _ in range(5))
    s0f_ref = s0b_ref = sf_ref = sb_ref = None
    if has_s0:
        s0f_ref, s0b_ref = next(it), next(it)
    if emit_states == "update":
        next(it), next(it)
    o_ref = next(it)
    if emit_states:
        sf_ref, sb_ref = next(it), next(it)
    r_s, v_s, kk_s = next(it), next(it), next(it)
    lw_s, kd_s, b_s, y_s = next(it), next(it), next(it), next(it)
    st_s = next(it)

    seq = r_ref.shape[0]
    n_pairs = r_ref.shape[1] // LANES
    n_chunks = seq // CHUNK
    unroll = min(SCAN_UNROLL, n_chunks)

    def shifted(x, mu):
        prev, nxt = _shift_rows(x)
        return x + mu[0:1] * (prev - x) + mu[1:2] * (nxt - x)

    lane = _iota((seq, LANES), 1)
    lo = lane < RWKV_HS
    wa = shifted(wa_ref[...], mu_wa[...])
    gd_act = _sigmoid(shifted(gd_ref[...], mu_gd[...]))
    wd_in = jnp.where(lo, jnp.tanh(wa), 0.0)
    ad_in = jnp.where(lo, 0.0, wa)
    gates, bonuses, ln_gs, ln_bs = [], [], [], []
    for hp in range(n_pairs):
        cols = slice(hp * LANES, (hp + 1) * LANES)
        r = shifted(r_ref[:, cols], mu_r[:, cols])
        k = shifted(k_ref[:, cols], mu_k[:, cols])
        v = shifted(v_ref[:, cols], mu_v[:, cols])
        vec = vec_ref[:, cols]
        k_k, k_a, r_k, ln_g, ln_b = (vec[i:i + 1] for i in range(5))
        gates.append(_bdot(gd_act, g_up_ref[:, cols]))
        ln_gs.append(ln_g)
        ln_bs.append(ln_b)
        kk = k * k_k
        kk = kk * jnp.minimum(lax.rsqrt(_half_sum(kk * kk, lo)), 1e12)
        r_s[hp] = r
        v_s[hp] = v
        kk_s[hp] = kk
        bonus = jnp.zeros((seq, LANES), F32)
        for d in range(2):
            wpre = w0_ref[d:d + 1, cols] + _bdot(wd_in, wa_up_ref[d, :, cols])
            sp = jnp.maximum(-wpre, 0.0) + jnp.log(1.0 + jnp.exp(-jnp.abs(wpre)))
            lw_s[2 * hp + d] = -jnp.exp(-sp - 0.5)
            a = _sigmoid(a0_ref[d:d + 1, cols] + _bdot(ad_in, wa_up_ref[d, :, cols]))
            kd = k * (1.0 + (a - 1.0) * k_a)
            kd_s[2 * hp + d] = kd
            b_s[2 * hp + d] = kk * a
            bonus = bonus + _half_sum(r * kd * r_k, lo) * v
        bonuses.append(bonus)

    ci = _iota((CHUNK, CHUNK), 0)
    cj = _iota((CHUNK, CHUNK), 1)
    eye = jnp.where(ci == cj, 1.0, 0.0)
    blk8 = (ci // 8) == (cj // 8)
    merge_sizes = [8 * 2 ** n for n in range(int(math.log2(CHUNK // 8)))]
    merge_masks = [((ci // (2 * s)) == (cj // (2 * s))) & ((ci // s) != (cj // s)) for s in merge_sizes]
    before = (cj < ci, cj > ci)
    upto = (cj <= ci, cj >= ci)
    tri = tuple(jnp.where(m, 1.0, 0.0).astype(BF16) for m in upto)
    clane = _iota((CHUNK, LANES), 1)
    head_lanes = (clane < RWKV_HS, clane >= RWKV_HS)
    si = _iota((LANES, LANES), 0)
    sj = _iota((LANES, LANES), 1)
    same_head = (si < RWKV_HS) == (sj < RWKV_HS)
    diag = si == sj
    row_h0 = _iota((LANES, CHUNK), 0) < RWKV_HS

    def chunk_group(g):
        items = []
        for hp in range(n_pairs):
            items += [(2 * hp, g * unroll + u) for u in range(unroll)]
            items += [(2 * hp + 1, n_chunks - 1 - (g * unroll + u)) for u in range(unroll)]
        n_it = len(items)
        scans = [sc for sc, _ in items]
        dirs = [sc % 2 for sc in scans]
        rows = [pl.ds(pl.multiple_of(c * CHUNK, CHUNK), CHUNK) for _, c in items]
        r_c = [r_s[sc // 2, rw, :] for sc, rw in zip(scans, rows)]
        v_c = [v_s[sc // 2, rw, :] for sc, rw in zip(scans, rows)]
        kk_c = [kk_s[sc // 2, rw, :] for sc, rw in zip(scans, rows)]
        lw_c = [lw_s[sc, rw, :] for sc, rw in zip(scans, rows)]
        kd_c = [kd_s[sc, rw, :] for sc, rw in zip(scans, rows)]
        b_c = [b_s[sc, rw, :] for sc, rw in zip(scans, rows)]
        cum = [_dot_exact_lhs(tri[d], lw) for d, lw in zip(dirs, lw_c)]
        tot = [cm[CHUNK - 1:CHUNK, :] if d == 0 else cm[0:1, :] for d, cm in zip(dirs, cum)]
        e_neg = [jnp.exp(-cm) for cm in cum]
        al = [jnp.exp(cm - lw) * kk for cm, lw, kk in zip(cum, lw_c, kk_c)]
        be = [b * e for b, e in zip(b_c, e_neg)]
        ka = [kd * e for kd, e in zip(kd_c, e_neg)]
        rh = [r * jnp.exp(cm) for r, cm in zip(r_c, cum)]
        wc = [jnp.exp(tt) for tt in tot]
        combos = [(i, h) for i in range(n_it) for h in range(2)]
        x2 = [jnp.concatenate([a, r], axis=0) for a, r in zip(al, rh)]
        heads_of = lambda m_t: jnp.concatenate([jnp.where(row_h0, m_t, 0.0), jnp.where(row_h0, 0.0, m_t)], axis=1)
        xb2 = [_bdot(x, heads_of(b.T)) for x, b in zip(x2, be)]
        xk2 = [_bdot(x, heads_of(k_.T)) for x, k_ in zip(x2, ka)]
        xb = [xb2[i][:, h * CHUNK:(h + 1) * CHUNK] for i, h in combos]
        xk = [xk2[i][:, h * CHUNK:(h + 1) * CHUNK] for i, h in combos]
        n_mat = [jnp.where(before[dirs[i]], -x[0:CHUNK], 0.0) for x, (i, _) in zip(xb, combos)]
        g_mat = [jnp.where(before[dirs[i]], x[0:CHUNK], 0.0) for x, (i, _) in zip(xk, combos)]
        m2 = [jnp.where(upto[dirs[i]], x[CHUNK:], 0.0) for x, (i, _) in zip(xb, combos)]
        m1 = [jnp.where(upto[dirs[i]], x[CHUNK:], 0.0) for x, (i, _) in zip(xk, combos)]
        nd = [jnp.where(blk8, n, 0.0) for n in n_mat]
        t = [eye + x for x in nd]
        p = [_bdot(x, x) for x in nd]
        t = [a + _bdot(a, b) for a, b in zip(t, p)]
        p = [_bdot(x, x) for x in p]
        t = [a + _bdot(a, b) for a, b in zip(t, p)]
        for msk in merge_masks:
            q = [_bdot(jnp.where(msk, n, 0.0), a) for n, a in zip(n_mat, t)]
            t = [a + _bdot(a, b) for a, b in zip(t, q)]
        gv = [_bdot(g_, v_c[i]) for g_, (i, _) in zip(g_mat, combos)]
        z = [_bdot(t_, jnp.concatenate([al[i], g_], axis=1)) for t_, g_, (i, _) in zip(t, gv, combos)]
        yv_h = [_bdot(m, v_c[i]) for m, (i, _) in zip(m1, combos)]
        pick = lambda a, b: jnp.where(head_lanes[0], a, b)
        alp = [pick(z[2 * i][:, :LANES], z[2 * i + 1][:, :LANES]) for i in range(n_it)]
        uv = [pick(z[2 * i][:, LANES:], z[2 * i + 1][:, LANES:]) for i in range(n_it)]
        yv = [pick(yv_h[2 * i], yv_h[2 * i + 1]) for i in range(n_it)]
        be_t = [(b * w).T for b, w in zip(be, wc)]
        ka_t = [(k_ * w).T for k_, w in zip(ka, wc)]
        pz = [_bdot(bt, jnp.concatenate([a, u_], axis=1)) for bt, a, u_ in zip(be_t, alp, uv)]
        kv = [_bdot(kt, v_) for kt, v_ in zip(ka_t, v_c)]
        a_mat = [jnp.where(same_head, jnp.where(diag, w, 0.0) - pp[:, :LANES], 0.0) for w, pp in zip(wc, pz)]
        b_mat = [jnp.where(same_head, k_ - pp[:, LANES:], 0.0) for k_, pp in zip(kv, pz)]
        for i, sc in enumerate(scans):
            st = st_s[sc]
            xs = _bdot(jnp.concatenate([alp[i], rh[i]], axis=0), st)
            u = uv[i] + xs[0:CHUNK]
            m2u = pick(_bdot(m2[2 * i], u), _bdot(m2[2 * i + 1], u))
            y_s[sc, rows[i], :] = xs[CHUNK:] + yv[i] - m2u
            st_s[sc] = _bdot(a_mat[i], st) + b_mat[i]

    def body(g, carry):
        chunk_group(g)
        return carry

    for hp in range(n_pairs):
        if has_s0:
            st_s[2 * hp] = s0f_ref[0, hp]
            st_s[2 * hp + 1] = s0b_ref[0, hp]
        else:
            st_s[2 * hp] = jnp.zeros((LANES, LANES), F32)
            st_s[2 * hp + 1] = jnp.zeros((LANES, LANES), F32)
    lax.fori_loop(0, n_chunks // unroll, body, 0)
    for hp in range(n_pairs):
        if emit_states:
            slabs = [()] if emit_states == "update" else [(lj,) for lj in range(sf_ref.shape[1])]
            for dst, sc in ((sf_ref, 2 * hp), (sb_ref, 2 * hp + 1)):
                s_vk = st_s[sc].T
                for slab in slabs:
                    own = not slab or slab[0] == layer
                    for h, blk in enumerate((s_vk[0:RWKV_HS, 0:RWKV_HS], s_vk[RWKV_HS:, RWKV_HS:])):
                        dst[(0,) + slab + (2 * hp + h,)] = blk if own else jnp.zeros_like(blk)
        y = (y_s[2 * hp] + y_s[2 * hp + 1]) + bonuses[hp]
        mean = _half_sum(y, lo) * (1.0 / RWKV_HS)
        yc = y - mean
        var = _half_sum(yc * yc, lo) * (1.0 / RWKV_HS)
        o_ref[:, hp * LANES:(hp + 1) * LANES] = ((yc * lax.rsqrt(var + GN_EPS)) * ln_gs[hp] + ln_bs[hp]) * gates[hp]


def _rwkv(p, seq, n_pairs, li, mu, w0, a0, wa_up, g_up, vecs, s0=None, states_out=None):
    t = p.shape[0]
    nb = t // seq
    width = n_pairs * LANES
    n_grp = RWKV_WIDTH // width
    base = (2 * DIFF_QW + DIFF_WIDTH) // width
    tail = (2 * DIFF_QW + DIFF_WIDTH + 3 * RWKV_WIDTH) // LANES
    col = lambda blk: pl.BlockSpec((seq, width), lambda b, hp: (b, blk(hp)))
    mu_col = lambda blk: pl.BlockSpec((None, 2, width), lambda b, hp: (li, 0, blk(hp)))
    lane_blk = lambda k: pl.BlockSpec((seq, LANES), lambda b, hp: (b, tail + k))
    mu_lane_blk = lambda k: pl.BlockSpec((None, 2, LANES), lambda b, hp: (li, 0, 3 * RWKV_WIDTH // LANES + k))
    in_specs = [
        col(lambda hp: base + hp), col(lambda hp: base + n_grp + hp), col(lambda hp: base + 2 * n_grp + hp),
        lane_blk(0), lane_blk(1),
        mu_col(lambda hp: hp), mu_col(lambda hp: n_grp + hp), mu_col(lambda hp: 2 * n_grp + hp),
        mu_lane_blk(0), mu_lane_blk(1),
        pl.BlockSpec((None, 2, width), lambda b, hp: (li, 0, hp)),
        pl.BlockSpec((None, 2, width), lambda b, hp: (li, 0, hp)),
        pl.BlockSpec((None, 2, LANES, width), lambda b, hp: (li, 0, 0, hp)),
        pl.BlockSpec((None, LANES, width), lambda b, hp: (li, 0, hp)),
        pl.BlockSpec((None, 8, width), lambda b, hp: (li, 0, hp)),
    ]
    args = [p] * 5 + [mu] * 5 + [w0, a0, wa_up, g_up, vecs]
    if s0 is not None:
        in_specs += [pl.BlockSpec((1, n_pairs, LANES, LANES), lambda b, hp: (b, hp, 0, 0))] * 2
        args += list(s0)
    out_shape = [jax.ShapeDtypeStruct((t, RWKV_WIDTH), F32)]
    out_specs = [pl.BlockSpec((seq, width), lambda b, hp: (b, hp))]
    aliases = {}
    emit_states = None
    if states_out is not None:
        emit_states = "create" if isinstance(states_out[0], jax.ShapeDtypeStruct) else "update"
        out_shape += [jax.ShapeDtypeStruct(a.shape, a.dtype) for a in states_out]
        if emit_states == "update":
            aliases = {len(args): 1, len(args) + 1: 2}
            in_specs += [pl.BlockSpec(memory_space=pl.ANY)] * 2
            args += list(states_out)
            slab, at = None, li
        else:
            slab, at = states_out[0].shape[1], 0
        out_specs += [pl.BlockSpec((1, slab, 2 * n_pairs, RWKV_HS, RWKV_HS), lambda b, hp: (b, at, hp, 0, 0))] * 2
    return pl.pallas_call(
        functools.partial(_rwkv_kernel, s0 is not None, emit_states, li),
        out_shape=out_shape,
        grid=(nb, n_grp),
        in_specs=in_specs,
        out_specs=out_specs,
        scratch_shapes=[pltpu.VMEM((n_pairs, seq, LANES), F32)] * 3 + [pltpu.VMEM((2 * n_pairs, seq, LANES), F32)] * 4
        + [pltpu.VMEM((2 * n_pairs, LANES, LANES), F32)],
        input_output_aliases=aliases,
        compiler_params=_params("parallel", "parallel"),
        name="rwkv",
    )(*args)


def _pair_states_in(s):
    nb = s.shape[0]
    st = jnp.swapaxes(s.astype(F32), -1, -2).reshape(nb, RWKV_HEADS // 2, 2, RWKV_HS, RWKV_HS)
    z = jnp.zeros_like(st[:, :, 0])
    top = jnp.concatenate([st[:, :, 0], z], axis=-1)
    bot = jnp.concatenate([z, st[:, :, 1]], axis=-1)
    return jnp.concatenate([top, bot], axis=-2)


def _dft_kernel(f_ref, ft_ref):
    n = f_ref.shape[1]
    shape = (2 * n, n)
    fr = _iota(shape, 0)
    tau = _iota(shape, 1)
    is_sin = fr >= n
    kf = jnp.where(is_sin, fr - n, fr)
    ang = ((kf * tau) & (2 * n - 1)).astype(F32) * (math.pi / n)
    val = jnp.where(is_sin, -jnp.sin(ang), jnp.cos(ang))
    nyq = jnp.where((tau & 1) == 0, 1.0, -1.0)
    val = jnp.where(is_sin & (kf == 0), nyq, val)
    f_ref[...] = val.astype(BF16)
    ft_ref[...] = val.T.astype(BF16)


def _dft_mats(n):
    return pl.pallas_call(
        _dft_kernel,
        out_shape=(jax.ShapeDtypeStruct((2 * n, n), BF16), jax.ShapeDtypeStruct((n, 2 * n), BF16)),
        name="dft_mats",
    )()


def _filter_kernel(seq, w0t_ref, w0c_ref, w0s_ref, b0_ref, w1_ref, b1_ref, w2_ref, b2_ref, fr_ref,
                   w3a_ref, w3b_ref, f_ref, o_ref, prev_scr):
    cb = f_ref.shape[1]
    n_blk = seq // cb
    e = pl.program_id(0)
    first_lag = (e - n_blk) * cb
    tap_row = jnp.abs(first_lag + _iota((1, cb), 1)).astype(F32)
    pos_row = tap_row * (1.0 / (seq - 1))
    band = _iota((BANDS, 1), 0).astype(F32)
    freq = 1e-4 + band * ((BANDS - 1 - 1e-4) / (BANDS - 1))
    arg = freq * (tap_row * (2.0 * math.pi / seq))
    fr = fr_ref[...]
    pre = (w0t_ref[...] * pos_row + _dot_f32(w0c_ref[...], jnp.cos(arg)) + _dot_f32(w0s_ref[...], -jnp.sin(arg))
           + b0_ref[...])
    hdn = jnp.sin(fr[:, 0:1] * pre)
    hdn = jnp.sin(fr[:, 1:2] * (_dot_f32(w1_ref[...], hdn) + b1_ref[...]))
    hdn = jnp.sin(fr[:, 2:3] * (_dot_f32(w2_ref[...], hdn) + b2_ref[...]))
    hdn = hdn.T
    tap = jnp.abs(first_lag + _iota((cb, 1), 0))
    pos = tap.astype(F32) * (1.0 / (seq - 1))
    chan = _iota((1, D_MODEL), 1).astype(F32)
    delta = jnp.abs(MIN_DECAY + chan * ((MAX_DECAY - MIN_DECAY) / (D_MODEL - 1)))
    window = jnp.where(tap < seq, jnp.exp(-pos * delta), 0.0)
    row = _iota((2 * cb, 1), 0)
    sign = jnp.where((row & 1) == 1, -1.0, 1.0)
    for o, w3_ref in enumerate((w3a_ref, w3b_ref)):
        cols = slice(o * D_MODEL, (o + 1) * D_MODEL)
        taps = (_dot_f32(hdn, w3_ref[...]) * window).astype(BF16)
        spec = jnp.dot(f_ref[...], taps, preferred_element_type=F32)

        @pl.when(e > 0)
        def _():
            o_ref[0, :, cols] = spec + sign * prev_scr[:, cols]

        prev_scr[:, cols] = spec - jnp.where(row <= cb, taps[0:1, :].astype(F32), 0.0)


def _hyena_filters(seq, li, fmat, small, w3):
    cb = fmat.shape[1]
    n_blk = seq // cb
    layer_slice = lambda a: pl.BlockSpec((None,) + a.shape[1:], lambda e: (li, 0, 0))
    w3_spec = lambda o: pl.BlockSpec((None, FILTER_WIDTH, D_MODEL),
                                     lambda e: (li, 0, 2 * o + jnp.where(e < n_blk, 1, 0)))
    return pl.pallas_call(
        functools.partial(_filter_kernel, seq),
        out_shape=jax.ShapeDtypeStruct((2 * n_blk - 1, 2 * cb, 2 * D_MODEL), F32),
        grid=(2 * n_blk,),
        in_specs=[layer_slice(a) for a in small] + [w3_spec(0), w3_spec(1),
                                                    pl.BlockSpec((2 * cb, cb), lambda e: (0, 0))],
        out_specs=pl.BlockSpec((1, 2 * cb, 2 * D_MODEL), lambda e: (jnp.maximum(e - 1, 0), 0, 0)),
        scratch_shapes=[pltpu.VMEM((2 * cb, 2 * D_MODEL), F32)],
        compiler_params=_params("arbitrary"),
        name="hyena_filters",
    )(*small, w3, w3, fmat)


def _short_conv(z, w, b):
    prev, nxt = _shift_rows(z)
    return prev * w[0:1] + z * w[1:2] + nxt * w[2:3] + b


def _conv_kernel(conv_v, n_blk, v_ref, x_ref, wv_ref, bv_ref, wx_ref, bx_ref, f_ref, ft_ref, kc_ref, bias_ref,
                 o_ref, uf_scr, yf_scr):
    cb = f_ref.shape[1]
    v = v_ref[...]
    if conv_v:
        v = _short_conv(v, wv_ref[...], bv_ref[...])
    fmat = f_ref[...]
    for j in range(n_blk):
        uf_scr[j] = jnp.dot(fmat, v[j * cb:(j + 1) * cb].astype(BF16), preferred_element_type=F32)

    def mix(r0, first):
        re, im = pl.ds(r0, MIX_ROWS), pl.ds(cb + r0, MIX_ROWS)
        real_pair = (_iota((MIX_ROWS, 1), 0) == 0) if first else None
        for i in range(n_blk):
            acc_r = acc_i = None
            for j in range(n_blk):
                d = i - j + n_blk - 1
                ur, ui = uf_scr[j, re, :], uf_scr[j, im, :]
                kr, ki = kc_ref[d, re, :], kc_ref[d, im, :]
                uiki = ui * ki
                if first:
                    t_r = ur * kr - jnp.where(real_pair, 0.0, uiki)
                    t_i = jnp.where(real_pair, uiki, ur * ki + ui * kr)
                else:
                    t_r = ur * kr - uiki
                    t_i = ur * ki + ui * kr
                acc_r = t_r if acc_r is None else acc_r + t_r
                acc_i = t_i if acc_i is None else acc_i + t_i
            scale = jnp.where(real_pair, 0.5 / cb, 1.0 / cb) if first else 1.0 / cb
            yf_scr[i, re, :] = acc_r * scale
            yf_scr[i, im, :] = acc_i * scale

    mix(0, True)

    def body(r, carry):
        mix(pl.multiple_of(r * MIX_ROWS, MIX_ROWS), False)
        return carry

    lax.fori_loop(1, cb // MIX_ROWS, body, 0)

    x = _short_conv(x_ref[...], wx_ref[...], bx_ref[...])
    ftm = ft_ref[...]
    bias = bias_ref[...]
    for i in range(n_blk):
        rows = slice(i * cb, (i + 1) * cb)
        y = jnp.dot(ftm, yf_scr[i].astype(BF16), preferred_element_type=F32)
        o_ref[rows, :] = x[rows] * (y + v[rows] * bias)


def _hyena_conv(v_arr, v_blk0, x_arr, x_blk0, conv_v, seq, li, order, cw, cwb, fmat, fmat_t, kc, biases, width):
    t = v_arr.shape[0]
    cb = fmat.shape[1]
    nb, ncb, n_blk = t // seq, D_MODEL // width, seq // cb
    wv_blk0 = v_blk0 if conv_v else 0
    kc_blk0 = order * ncb
    return pl.pallas_call(
        functools.partial(_conv_kernel, conv_v, n_blk),
        out_shape=jax.ShapeDtypeStruct((t, D_MODEL), F32),
        grid=(nb, ncb),
        in_specs=[
            pl.BlockSpec((seq, width), lambda b, c: (b, v_blk0 + c)),
            pl.BlockSpec((seq, width), lambda b, c: (b, x_blk0 + c)),
            pl.BlockSpec((None, 3, width), lambda b, c: (li, 0, wv_blk0 + c)),
            pl.BlockSpec((None, 1, width), lambda b, c: (li, 0, wv_blk0 + c)),
            pl.BlockSpec((None, 3, width), lambda b, c: (li, 0, x_blk0 + c)),
            pl.BlockSpec((None, 1, width), lambda b, c: (li, 0, x_blk0 + c)),
            pl.BlockSpec((2 * cb, cb), lambda b, c: (0, 0)),
            pl.BlockSpec((cb, 2 * cb), lambda b, c: (0, 0)),
            pl.BlockSpec((2 * n_blk - 1, 2 * cb, width), lambda b, c: (0, 0, kc_blk0 + c)),
            pl.BlockSpec((None, 1, width), lambda b, c: (li * 2 + order, 0, c)),
        ],
        out_specs=pl.BlockSpec((seq, width), lambda b, c: (b, c)),
        scratch_shapes=[pltpu.VMEM((n_blk, 2 * cb, width), F32), pltpu.VMEM((n_blk, 2 * cb, width), F32)],
        compiler_params=_params("parallel", "parallel"),
        name="hyena_conv",
    )(v_arr, x_arr, cw, cwb, cw, cwb, fmat, fmat_t, kc, biases)


def _axial_rope_tables(seq):
    n_rows = seq // GRID_W
    row = np.repeat(np.arange(n_rows, dtype=np.float64), GRID_W)
    col = np.tile(np.arange(GRID_W, dtype=np.float64), n_rows)
    inv = ROPE_BASE ** (-np.arange(ROPE_PAIRS, dtype=np.float64) / ROPE_PAIRS)
    ar = row[:, None] * inv[None]
    ac = col[:, None] * inv[None]
    ang = np.concatenate([ar, ar, ac, ac], axis=-1)
    reps = LANES // DIFF_QK
    return (jnp.asarray(np.tile(np.cos(ang), (1, reps)), F32), jnp.asarray(np.tile(np.sin(ang), (1, reps)), F32))


def _pair_lanes(a, seq_axis):
    nb = a.shape[0]
    if a.ndim == 5:
        a = a.reshape(nb, DIFF_HEADS // 2, 2, 2, a.shape[3], DIFF_QK).transpose(0, 1, 4, 2, 3, 5)
    else:
        a = a.reshape(nb, DIFF_HEADS // 2, 2, a.shape[2], DIFF_V).transpose(0, 1, 3, 2, 4)
    return a.reshape(nb, DIFF_HEADS // 2, a.shape[2], LANES)


def kernel(x_prompt, x_sample, cache_diff_k, cache_diff_v, state_rwkv_fwd, state_rwkv_bwd, c, c_ctx, ada_w, ada_b, norm1_g, norm2_g, ffn_w1, ffn_w3, ffn_w2, final_g, ab_w_in, ab_w_out, diff_lambda, diff_subln_g, rwkv_mu, rwkv_w0, rwkv_w_up, rwkv_a0, rwkv_a_up, rwkv_g_up, rwkv_k_k, rwkv_k_a, rwkv_r_k, rwkv_ln_g, rwkv_ln_b, hy_w_in, hy_b_in, hy_conv_w, hy_conv_b, hy_f_w0, hy_f_b0, hy_f_w1, hy_f_b1, hy_f_w2, hy_f_b2, hy_f_w3, hy_f_freq, hy_bias, hy_w_out, hy_b_out):
    n_ctx_seqs, ctx_len, _ = x_prompt.shape
    n_lat_seqs, lat_len, _ = x_sample.shape
    tm = 1024

    cond = jnp.zeros((N_MOD_ROWS, D_MODEL), F32).at[0].set(c_ctx).at[1:1 + n_lat_seqs].set(c)
    mods = _ada_all(cond, ada_w, ada_b).reshape(DEPTH * N_MOD_ROWS * 6, 1, D_MODEL)

    groups = [
        dict(x=x_prompt.reshape(-1, D_MODEL), seq=ctx_len, rows=(0, n_ctx_seqs * ctx_len), tq=ctx_len, pairs=4,
             scan_pairs=2, conv_width=1024),
        dict(x=x_sample.reshape(-1, D_MODEL), seq=lat_len, rows=(1, lat_len), tq=1024, pairs=1, scan_pairs=1,
             conv_width=256),
    ]
    rope = _axial_rope_tables(lat_len)
    dft = {cb: _dft_mats(cb) for cb in {min(CONV_BLOCK_MAX, g["seq"]) for g in groups}}
    n_ab, n_c = ab_w_in.shape[0], hy_w_in.shape[0]

    gains1 = norm1_g.reshape(DEPTH, 1, D_MODEL)
    gains2 = norm2_g.reshape(DEPTH, 1, D_MODEL)
    g2s = jnp.tile(diff_subln_g, (1, 2)).reshape(n_ab, 1, LANES)
    wa_up = jnp.concatenate([rwkv_w_up, rwkv_a_up], axis=2)
    vec_rows = [rwkv_k_k, rwkv_k_a, rwkv_r_k.reshape(n_ab, RWKV_WIDTH), rwkv_ln_g, rwkv_ln_b]
    vecs = jnp.stack(vec_rows + [jnp.zeros_like(rwkv_k_k)] * (8 - len(vec_rows)), axis=1)
    w0_t = jnp.swapaxes(hy_f_w0, 1, 2)
    filt_small = [w0_t[:, :, 0:1], w0_t[:, :, 1:1 + BANDS], w0_t[:, :, 1 + BANDS:], hy_f_b0[:, :, None],
                  jnp.swapaxes(hy_f_w1, 1, 2), hy_f_b1[:, :, None], jnp.swapaxes(hy_f_w2, 1, 2), hy_f_b2[:, :, None],
                  jnp.swapaxes(hy_f_freq, 1, 2)]
    hy_b_in_s = hy_b_in.reshape(n_c, 1, -1)
    hy_b_out_s = hy_b_out.reshape(n_c, 1, D_MODEL)
    conv_b = hy_conv_b.reshape(n_c, 1, -1)
    conv_bias = hy_bias.reshape(n_c * 2, 1, D_MODEL)
    new_kv = (jax.ShapeDtypeStruct((n_ctx_seqs, n_ab, DIFF_HEADS, 2, ctx_len, DIFF_QK), F32),
              jax.ShapeDtypeStruct((n_ctx_seqs, n_ab, DIFF_HEADS, ctx_len, DIFF_V), F32))
    new_states = (jax.ShapeDtypeStruct((n_ctx_seqs, n_ab, RWKV_HEADS, RWKV_HS, RWKV_HS), F32),) * 2

    for l in range(DEPTH):
        i = l // 2
        for gi, g in enumerate(groups):
            x, seq, row = g["x"], g["seq"], g["rows"]
            if l % 2 == 0:
                p = _proj_in(x, gains1, mods, l, row, ab_w_in, i, None, tm, AB_IN // 2)
                if gi == 0:
                    att, *new_kv = _attention(p, seq, g["tq"], g["pairs"], _lambda_init(l), diff_lambda, g2s, i,
                                              kv_out=new_kv)
                    mix, *new_states = _rwkv(p, seq, g["scan_pairs"], i, rwkv_mu, rwkv_w0, rwkv_a0, wa_up, rwkv_g_up,
                                             vecs, states_out=new_states)
                else:
                    ctx = (_pair_lanes(cache_diff_k[:, i], 3), _pair_lanes(cache_diff_v[:, i], 2))
                    att, = _attention(p, seq, g["tq"], g["pairs"], _lambda_init(l), diff_lambda, g2s, i,
                                      ctx=ctx, rope=rope)
                    s0 = (_pair_states_in(state_rwkv_fwd[:, i]), _pair_states_in(state_rwkv_bwd[:, i]))
                    mix, = _rwkv(p, seq, g["scan_pairs"], i, rwkv_mu, rwkv_w0, rwkv_a0, wa_up, rwkv_g_up, vecs, s0=s0)
                x = _proj_out(x, mods, l, row, [att, mix], ab_w_out, i, None, tm)
            else:
                fmat, fmat_t = dft[min(CONV_BLOCK_MAX, seq)]
                z = _proj_in(x, gains1, mods, l, row, hy_w_in, i, hy_b_in_s, tm, 1536)
                kc = _hyena_filters(seq, i, fmat, filt_small, hy_f_w3)
                width = g["conv_width"]
                ncb = D_MODEL // width
                u = _hyena_conv(z, 2 * ncb, z, 0, True, seq, i, 0, hy_conv_w, conv_b, fmat, fmat_t, kc, conv_bias, width)
                u = _hyena_conv(u, 0, z, ncb, False, seq, i, 1, hy_conv_w, conv_b, fmat, fmat_t, kc, conv_bias, width)
                x = _proj_out(x, mods, l, row, [u], hy_w_out, i, hy_b_out_s, tm)
            g["x"] = _ffn(x, gains2, mods, l, row, ffn_w1, ffn_w3, ffn_w2, final_g, l == DEPTH - 1, 2 * tm, 256)

    y_prompt = groups[0]["x"].reshape(x_prompt.shape)
    y_sample = groups[1]["x"].reshape(x_sample.shape)
    return (y_prompt, y_sample, new_kv[0], new_kv[1], new_states[0], new_states[1])
```

```python
import functools
import math

import jax
import jax.numpy as jnp
import numpy as np
from jax import lax
from jax.experimental import pallas as pl
from jax.experimental.pallas import tpu as pltpu

F32 = jnp.float32
BF16 = jnp.bfloat16

D_MODEL = 1024
DEPTH = 4
GRID_W = 64
DIFF_HEADS = 8
DIFF_QK = 32
DIFF_V = 64
DIFF_QW = DIFF_HEADS * 2 * DIFF_QK
DIFF_WIDTH = DIFF_HEADS * DIFF_V
ROPE_PAIRS = DIFF_QK // 4
ROPE_BASE = 10000.0
RWKV_HEADS = 8
RWKV_HS = 64
RWKV_WIDTH = RWKV_HEADS * RWKV_HS
RWKV_IN = 3 * RWKV_WIDTH + 64 + 64 + 128
AB_IN = 2 * DIFF_QW + DIFF_WIDTH + RWKV_IN
BANDS = 16
FILTER_WIDTH = 64
MAX_DECAY = math.log(1e-2) / 0.3
MIN_DECAY = math.log(1e-2) / 1.5
D_FF = 2816
RMS_EPS = 1e-6
GN_EPS = 64e-5
HEAD_RMS_EPS = 1e-5
LOG2_E = math.log2(math.e)

LANES = 128
CHUNK = 128
SCAN_UNROLL = 4
CONV_BLOCK_MAX = 512
MIX_ROWS = 8
VMEM_LIMIT = 56 * 1024 * 1024

N_MOD_ROWS = 8


def _lambda_init(l):
    return 0.8 - 0.6 * math.exp(-0.3 * l)


def _params(*sem):
    return pltpu.CompilerParams(dimension_semantics=sem, vmem_limit_bytes=VMEM_LIMIT)


def _bdot(a, b):
    return jnp.dot(a.astype(BF16), b.astype(BF16), preferred_element_type=F32)


def _bdot_nt(a, b):
    return lax.dot_general(a.astype(BF16), b.astype(BF16), (((1,), (1,)), ((), ())),
                           preferred_element_type=F32)


def _split3(x):
    h1 = x.astype(BF16)
    r1 = x - h1.astype(F32)
    h2 = r1.astype(BF16)
    h3 = (r1 - h2.astype(F32)).astype(BF16)
    return h1, h2, h3


def _dot_f32(a, b):
    a1, a2, a3 = _split3(a)
    b1, b2, b3 = _split3(b)
    d = lambda x, y: jnp.dot(x, y, preferred_element_type=F32)
    return (d(a1, b1) + (d(a1, b2) + d(a2, b1))) + ((d(a1, b3) + d(a3, b1)) + d(a2, b2))


def _dot_exact_lhs(a_bf16, b):
    b1, b2, b3 = _split3(b)
    d = lambda y: jnp.dot(a_bf16, y, preferred_element_type=F32)
    return d(b1) + (d(b2) + d(b3))


def _sigmoid(x):
    return 0.5 * (jnp.tanh(0.5 * x) + 1.0)


def _iota(shape, dim):
    return lax.broadcasted_iota(jnp.int32, shape, dim)


def _shift_rows(x):
    n = x.shape[0]
    row = _iota(x.shape, 0)
    prev = jnp.where(row == 0, 0.0, pltpu.roll(x, 1, 0))
    nxt = jnp.where(row == n - 1, 0.0, pltpu.roll(x, n - 1, 0))
    return prev, nxt


def _half_sum(x, lo_mask):
    s_lo = jnp.sum(jnp.where(lo_mask, x, 0.0), axis=-1, keepdims=True)
    s_hi = jnp.sum(jnp.where(lo_mask, 0.0, x), axis=-1, keepdims=True)
    return jnp.where(lo_mask, s_lo, s_hi)


def _ada_kernel(c_ref, w_ref, b_ref, o_ref):
    c = c_ref[...]
    o_ref[0] = _bdot(c * _sigmoid(c), w_ref[0]) + b_ref[0]


def _ada_all(cond, ada_w, ada_b):
    tn = 1536
    n_out = 6 * D_MODEL
    return pl.pallas_call(
        _ada_kernel,
        out_shape=jax.ShapeDtypeStruct((DEPTH, N_MOD_ROWS, n_out), F32),
        grid=(DEPTH, n_out // tn),
        in_specs=[
            pl.BlockSpec((N_MOD_ROWS, D_MODEL), lambda l, j: (0, 0)),
            pl.BlockSpec((1, D_MODEL, tn), lambda l, j: (l, 0, j)),
            pl.BlockSpec((1, 1, tn), lambda l, j: (l, 0, j)),
        ],
        out_specs=pl.BlockSpec((1, N_MOD_ROWS, tn), lambda l, j: (l, 0, j)),
        compiler_params=_params("parallel", "parallel"),
        name="ada",
    )(cond, ada_w, ada_b.reshape(DEPTH, 1, n_out))


def _mod_spec(layer, chunk, cond_rows, tm, axis=0):
    first, span = cond_rows

    def index_map(*idx):
        return ((layer * N_MOD_ROWS + first + (idx[axis] * tm) // span) * 6 + chunk, 0, 0)
    return pl.BlockSpec((1, 1, D_MODEL), index_map)


def _normed(x, g, sc, sh):
    ms = jnp.mean(x * x, axis=-1, keepdims=True)
    return (x * lax.rsqrt(ms + RMS_EPS)) * g * (1.0 + sc) + sh


def _proj_in_kernel(has_bias, x_ref, g_ref, sc_ref, sh_ref, w_ref, *refs):
    o_ref, wb_scr = refs[-2:]

    @pl.when(pl.program_id(0) == 0)
    def _():
        wb_scr[...] = w_ref[...].astype(BF16)

    h = _normed(x_ref[...], g_ref[...], sc_ref[0], sh_ref[0])
    y = jnp.dot(h.astype(BF16), wb_scr[...], preferred_element_type=F32)
    o_ref[...] = y + refs[0][...] if has_bias else y


def _proj_in(x, gains, mods, layer, cond_rows, ws, wi, bs, tm):
    t, n_out = x.shape[0], ws.shape[2]
    in_specs = [
        pl.BlockSpec((tm, D_MODEL), lambda i: (i, 0)),
        pl.BlockSpec((None, 1, D_MODEL), lambda i: (layer, 0, 0)),
        _mod_spec(layer, 1, cond_rows, tm),
        _mod_spec(layer, 0, cond_rows, tm),
        pl.BlockSpec((None, D_MODEL, n_out), lambda i: (wi, 0, 0), pipeline_mode=pl.Buffered(1)),
    ]
    args = [x, gains, mods, mods, ws]
    if bs is not None:
        in_specs.append(pl.BlockSpec((None, 1, n_out), lambda i: (wi, 0, 0)))
        args.append(bs)
    return pl.pallas_call(
        functools.partial(_proj_in_kernel, bs is not None),
        out_shape=jax.ShapeDtypeStruct((t, n_out), F32),
        grid=(t // tm,),
        in_specs=in_specs,
        out_specs=pl.BlockSpec((tm, n_out), lambda i: (i, 0)),
        scratch_shapes=[pltpu.VMEM((D_MODEL, n_out), BF16)],
        compiler_params=_params("arbitrary"),
        name="proj_in",
    )(*args)


def _proj_out_kernel(n_u, has_bias, x_ref, gt_ref, *refs):
    u_refs, w_ref, o_ref = refs[:n_u], refs[n_u], refs[-1]
    acc = None
    off = 0
    for u_ref in u_refs:
        k = u_ref.shape[1]
        part = _bdot(u_ref[...], w_ref[off:off + k, :])
        acc = part if acc is None else acc + part
        off += k
    if has_bias:
        acc = acc + refs[n_u + 1][...]
    o_ref[...] = x_ref[...] + gt_ref[0] * acc


def _proj_out(x, mods, layer, cond_rows, us, ws, wi, bs, tm):
    t = x.shape[0]
    in_specs = [
        pl.BlockSpec((tm, D_MODEL), lambda i: (i, 0)),
        _mod_spec(layer, 2, cond_rows, tm),
        *[pl.BlockSpec((tm, u.shape[1]), lambda i: (i, 0)) for u in us],
        pl.BlockSpec((None,) + ws.shape[1:], lambda i: (wi, 0, 0)),
    ]
    args = [x, mods, *us, ws]
    if bs is not None:
        in_specs.append(pl.BlockSpec((None, 1, D_MODEL), lambda i: (wi, 0, 0)))
        args.append(bs)
    return pl.pallas_call(
        functools.partial(_proj_out_kernel, len(us), bs is not None),
        out_shape=jax.ShapeDtypeStruct((t, D_MODEL), F32),
        grid=(t // tm,),
        in_specs=in_specs,
        out_specs=pl.BlockSpec((tm, D_MODEL), lambda i: (i, 0)),
        compiler_params=_params("parallel"),
        name="proj_out",
    )(*args)


def _ffn_kernel(final, x_ref, g_ref, sc_ref, sh_ref, gt_ref, w1_ref, w3_ref, w2_ref, fg_ref, o_ref, h_scr):
    k = pl.program_id(1)

    @pl.when(k == 0)
    def _():
        h_scr[...] = _normed(x_ref[...], g_ref[...], sc_ref[0], sh_ref[0]).astype(BF16)
        o_ref[...] = jnp.zeros_like(o_ref)

    h = h_scr[...]
    a1 = jnp.dot(h, w1_ref[...].astype(BF16), preferred_element_type=F32)
    a3 = jnp.dot(h, w3_ref[...].astype(BF16), preferred_element_type=F32)
    o_ref[...] += _bdot((a1 * _sigmoid(a1)) * a3, w2_ref[...])

    @pl.when(k == pl.num_programs(1) - 1)
    def _():
        y = x_ref[...] + gt_ref[0] * o_ref[...]
        if final:
            ms = jnp.mean(y * y, axis=-1, keepdims=True)
            y = (y * lax.rsqrt(ms + RMS_EPS)) * fg_ref[...]
        o_ref[...] = y


def _ffn(x, gains, mods, layer, cond_rows, w1s, w3s, w2s, final_g, final, tm, tk):
    t = x.shape[0]
    return pl.pallas_call(
        functools.partial(_ffn_kernel, final),
        out_shape=jax.ShapeDtypeStruct((t, D_MODEL), F32),
        grid=(t // tm, D_FF // tk),
        in_specs=[
            pl.BlockSpec((tm, D_MODEL), lambda i, k: (i, 0)),
            pl.BlockSpec((None, 1, D_MODEL), lambda i, k: (layer, 0, 0)),
            _mod_spec(layer, 4, cond_rows, tm),
            _mod_spec(layer, 3, cond_rows, tm),
            _mod_spec(layer, 5, cond_rows, tm),
            pl.BlockSpec((None, D_MODEL, tk), lambda i, k: (layer, 0, k)),
            pl.BlockSpec((None, D_MODEL, tk), lambda i, k: (layer, 0, k)),
            pl.BlockSpec((None, tk, D_MODEL), lambda i, k: (layer, k, 0)),
            pl.BlockSpec((1, D_MODEL), lambda i, k: (0, 0)),
        ],
        out_specs=pl.BlockSpec((tm, D_MODEL), lambda i, k: (i, 0)),
        scratch_shapes=[pltpu.VMEM((tm, D_MODEL), BF16)],
        compiler_params=_params("parallel", "arbitrary"),
        name="ffn",
    )(x, gains, mods, mods, mods, w1s, w3s, w2s, final_g.reshape(1, D_MODEL))


def _rope(x, cos, sin):
    lane = _iota(x.shape, 1)
    rot = jnp.where((lane % 16) < 8, -pltpu.roll(x, LANES - 8, 1), pltpu.roll(x, 8, 1))
    return x * cos + rot * sin


def _attn_kernel(n_ctx, use_rope, emit_kv, layer, lam_init, *refs):
    it = iter(refs)
    q_ref, k_ref, v_ref = next(it), next(it), next(it)
    kc_ref = vc_ref = cq_ref = sq_ref = ck_ref = sk_ref = ko_ref = vo_ref = None
    if n_ctx:
        kc_ref, vc_ref = next(it), next(it)
    if use_rope:
        cq_ref, sq_ref, ck_ref, sk_ref = next(it), next(it), next(it), next(it)
    lam_ref, g_ref = next(it), next(it)
    if emit_kv == "update":
        next(it), next(it)
    o_ref = next(it)
    if emit_kv:
        ko_ref, vo_ref = next(it), next(it)
    kall, vall = next(it), next(it)
    n_own = k_ref.shape[0]
    n_pairs = q_ref.shape[1] // LANES

    @pl.when(pl.program_id(2) == 0)
    def _():
        k = k_ref[...]
        v = v_ref[...]
        if emit_kv:
            slabs = [()] if emit_kv == "update" else [(lj,) for lj in range(ko_ref.shape[1])]
            for slab in slabs:
                own = not slab or slab[0] == layer
                for h in range(2 * n_pairs):
                    for m in range(2):
                        c0 = (2 * h + m) * DIFF_QK
                        kh = k[:, c0:c0 + DIFF_QK]
                        ko_ref[(0,) + slab + (h, m)] = kh if own else jnp.zeros_like(kh)
                    vh = v[:, h * DIFF_V:(h + 1) * DIFF_V]
                    vo_ref[(0,) + slab + (h,)] = vh if own else jnp.zeros_like(vh)
        if use_rope:
            k = _rope(k, ck_ref[...], sk_ref[...])
        if n_ctx:
            kall[0:n_ctx, :] = kc_ref[0, 0].astype(BF16)
            vall[0:n_ctx, :] = vc_ref[0, 0].astype(BF16)
        kall[n_ctx:n_ctx + n_own, :] = k.astype(BF16)
        vall[n_ctx:n_ctx + n_own, :] = v.astype(BF16)

    lv = lam_ref[...]
    lam = (jnp.exp(jnp.sum(lv[0:1] * lv[1:2], axis=-1, keepdims=True))
           - jnp.exp(jnp.sum(lv[2:3] * lv[3:4], axis=-1, keepdims=True)) + lam_init)

    lane = _iota((q_ref.shape[0], LANES), 1)
    lo = lane < DIFF_V
    scale = DIFF_QK ** -0.5
    for hp in range(n_pairs):
        cols = slice(hp * LANES, (hp + 1) * LANES)
        q = q_ref[:, cols]
        if use_rope:
            q = _rope(q, cq_ref[...], sq_ref[...])
        ks, vs = kall[:, cols], vall[:, cols]
        outs = []
        for h in range(2):
            pv, inv = [], []
            for m in range(2):
                j = 2 * h + m
                qm = jnp.where((lane >= DIFF_QK * j) & (lane < DIFF_QK * (j + 1)), q, 0.0)
                s = _bdot_nt(qm, ks)
                e = jnp.exp2((s - jnp.max(s, axis=-1, keepdims=True)) * (scale * LOG2_E))
                inv.append(1.0 / jnp.sum(e, axis=-1, keepdims=True))
                pv.append(_bdot(e, vs))
            outs.append(pv[0] * inv[0] - pv[1] * (lam * inv[1]))
        o = jnp.where(lo, outs[0], outs[1])
        ms = _half_sum(o * o, lo) * (1.0 / DIFF_V)
        o_ref[:, cols] = (o * lax.rsqrt(ms + HEAD_RMS_EPS)) * g_ref[...] * (1.0 - lam_init)


def _attention(p, seq, tq, n_pairs, lam_init, lams, g2s, li, ctx=None, rope=None, kv_out=None):
    emit_kv = None if kv_out is None else ("create" if isinstance(kv_out[0], jax.ShapeDtypeStruct) else "update")
    assert rope is None or n_pairs == 1
    t = p.shape[0]
    nb, nq = t // seq, seq // tq
    n_ctx = 0 if ctx is None else ctx[0].shape[2]
    width = n_pairs * LANES
    k_blk, v_blk = DIFF_QW // width, 2 * DIFF_QW // width
    in_specs = [
        pl.BlockSpec((tq, width), lambda b, hp, qi: (b * nq + qi, hp)),
        pl.BlockSpec((seq, width), lambda b, hp, qi: (b, k_blk + hp)),
        pl.BlockSpec((seq, width), lambda b, hp, qi: (b, v_blk + hp)),
    ]
    args = [p, p, p]
    if ctx is not None:
        in_specs += [pl.BlockSpec((1, 1, n_ctx, width), lambda b, hp, qi: (b, hp, 0, 0))] * 2
        args += list(ctx)
    if rope is not None:
        in_specs += [pl.BlockSpec((tq, LANES), lambda b, hp, qi: (qi, 0))] * 2
        in_specs += [pl.BlockSpec((seq, width), lambda b, hp, qi: (0, 0))] * 2
        args += [rope[0], rope[1], rope[0], rope[1]]
    in_specs += [pl.BlockSpec((None, 4, DIFF_QK), lambda b, hp, qi: (li, 0, 0)),
                 pl.BlockSpec((None, 1, LANES), lambda b, hp, qi: (li, 0, 0))]
    args += [lams, g2s]
    out_shape = [jax.ShapeDtypeStruct((t, DIFF_WIDTH), F32)]
    out_specs = [pl.BlockSpec((tq, width), lambda b, hp, qi: (b * nq + qi, hp))]
    aliases = {}
    if emit_kv:
        out_shape += [jax.ShapeDtypeStruct(a.shape, a.dtype) for a in kv_out]
        if emit_kv == "update":
            aliases = {len(args): 1, len(args) + 1: 2}
            in_specs += [pl.BlockSpec(memory_space=pl.ANY)] * 2
            args += list(kv_out)
            slab, at = None, li
        else:
            slab, at = kv_out[0].shape[1], 0
        out_specs += [pl.BlockSpec((1, slab, 2 * n_pairs, 2, seq, DIFF_QK), lambda b, hp, qi: (b, at, hp, 0, 0, 0)),
                      pl.BlockSpec((1, slab, 2 * n_pairs, seq, DIFF_V), lambda b, hp, qi: (b, at, hp, 0, 0))]
    return pl.pallas_call(
        functools.partial(_attn_kernel, n_ctx, rope is not None, emit_kv, li, lam_init),
        out_shape=out_shape,
        grid=(nb, DIFF_HEADS // 2 // n_pairs, nq),
        in_specs=in_specs,
        out_specs=out_specs,
        scratch_shapes=[pltpu.VMEM((n_ctx + seq, width), BF16), pltpu.VMEM((n_ctx + seq, width), BF16)],
        input_output_aliases=aliases,
        compiler_params=_params("parallel", "parallel", "arbitrary"),
        name="diff_attn",
    )(*args)


def _rwkv_kernel(has_s0, emit_states, layer, *refs):
    it = iter(refs)
    r_ref, k_ref, v_ref, wa_ref, gd_ref = (next(it) for _ in range(5))
    mu_r, mu_k, mu_v, mu_wa, mu_gd = (next(it) for _ in range(5))
    w0_ref, a0_ref, wa_up_ref, g_up_ref, vec_ref = (next(it) for _ in range(5))
    s0f_ref = s0b_ref = sf_ref = sb_ref = None
    if has_s0:
        s0f_ref, s0b_ref = next(it), next(it)
    if emit_states == "update":
        next(it), next(it)
    o_ref = next(it)
    if emit_states:
        sf_ref, sb_ref = next(it), next(it)
    r_s, v_s, kk_s = next(it), next(it), next(it)
    lw_s, kd_s, b_s, y_s = next(it), next(it), next(it), next(it)
    st_s = next(it)

    seq = r_ref.shape[0]
    n_pairs = r_ref.shape[1] // LANES
    n_chunks = seq // CHUNK
    unroll = min(SCAN_UNROLL, n_chunks)

    def shifted(x, mu):
        prev, nxt = _shift_rows(x)
        return x + mu[0:1] * (prev - x) + mu[1:2] * (nxt - x)

    lane = _iota((seq, LANES), 1)
    lo = lane < RWKV_HS
    wa = shifted(wa_ref[...], mu_wa[...])
    gd_act = _sigmoid(shifted(gd_ref[...], mu_gd[...]))
    wd_in = jnp.where(lo, jnp.tanh(wa), 0.0)
    ad_in = jnp.where(lo, 0.0, wa)
    gates, bonuses, ln_gs, ln_bs = [], [], [], []
    for hp in range(n_pairs):
        cols = slice(hp * LANES, (hp + 1) * LANES)
        r = shifted(r_ref[:, cols], mu_r[:, cols])
        k = shifted(k_ref[:, cols], mu_k[:, cols])
        v = shifted(v_ref[:, cols], mu_v[:, cols])
        vec = vec_ref[:, cols]
        k_k, k_a, r_k, ln_g, ln_b = (vec[i:i + 1] for i in range(5))
        gates.append(_bdot(gd_act, g_up_ref[:, cols]))
        ln_gs.append(ln_g)
        ln_bs.append(ln_b)
        kk = k * k_k
        kk = kk * jnp.minimum(lax.rsqrt(_half_sum(kk * kk, lo)), 1e12)
        r_s[hp] = r
        v_s[hp] = v
        kk_s[hp] = kk
        bonus = jnp.zeros((seq, LANES), F32)
        for d in range(2):
            wpre = w0_ref[d:d + 1, cols] + _bdot(wd_in, wa_up_ref[d, :, cols])
            sp = jnp.maximum(-wpre, 0.0) + jnp.log(1.0 + jnp.exp(-jnp.abs(wpre)))
            lw_s[2 * hp + d] = -jnp.exp(-sp - 0.5)
            a = _sigmoid(a0_ref[d:d + 1, cols] + _bdot(ad_in, wa_up_ref[d, :, cols]))
            kd = k * (1.0 + (a - 1.0) * k_a)
            kd_s[2 * hp + d] = kd
            b_s[2 * hp + d] = kk * a
            bonus = bonus + _half_sum(r * kd * r_k, lo) * v
        bonuses.append(bonus)

    ci = _iota((CHUNK, CHUNK), 0)
    cj = _iota((CHUNK, CHUNK), 1)
    eye = jnp.where(ci == cj, 1.0, 0.0)
    blk8 = (ci // 8) == (cj // 8)
    merge_sizes = [8 * 2 ** n for n in range(int(math.log2(CHUNK // 8)))]
    merge_masks = [((ci // (2 * s)) == (cj // (2 * s))) & ((ci // s) != (cj // s)) for s in merge_sizes]
    before = (cj < ci, cj > ci)
    upto = (cj <= ci, cj >= ci)
    tri = tuple(jnp.where(m, 1.0, 0.0).astype(BF16) for m in upto)
    clane = _iota((CHUNK, LANES), 1)
    head_lanes = (clane < RWKV_HS, clane >= RWKV_HS)
    si = _iota((LANES, LANES), 0)
    sj = _iota((LANES, LANES), 1)
    same_head = (si < RWKV_HS) == (sj < RWKV_HS)
    diag = si == sj
    row_h0 = _iota((LANES, CHUNK), 0) < RWKV_HS

    def chunk_group(g):
        items = []
        for hp in range(n_pairs):
            items += [(2 * hp, g * unroll + u) for u in range(unroll)]
            items += [(2 * hp + 1, n_chunks - 1 - (g * unroll + u)) for u in range(unroll)]
        n_it = len(items)
        scans = [sc for sc, _ in items]
        dirs = [sc % 2 for sc in scans]
        rows = [pl.ds(pl.multiple_of(c * CHUNK, CHUNK), CHUNK) for _, c in items]
        r_c = [r_s[sc // 2, rw, :] for sc, rw in zip(scans, rows)]
        v_c = [v_s[sc // 2, rw, :] for sc, rw in zip(scans, rows)]
        kk_c = [kk_s[sc // 2, rw, :] for sc, rw in zip(scans, rows)]
        lw_c = [lw_s[sc, rw, :] for sc, rw in zip(scans, rows)]
        kd_c = [kd_s[sc, rw, :] for sc, rw in zip(scans, rows)]
        b_c = [b_s[sc, rw, :] for sc, rw in zip(scans, rows)]
        cum = [_dot_exact_lhs(tri[d], lw) for d, lw in zip(dirs, lw_c)]
        tot = [cm[CHUNK - 1:CHUNK, :] if d == 0 else cm[0:1, :] for d, cm in zip(dirs, cum)]
        e_neg = [jnp.exp(-cm) for cm in cum]
        al = [jnp.exp(cm - lw) * kk for cm, lw, kk in zip(cum, lw_c, kk_c)]
        be = [b * e for b, e in zip(b_c, e_neg)]
        ka = [kd * e for kd, e in zip(kd_c, e_neg)]
        rh = [r * jnp.exp(cm) for r, cm in zip(r_c, cum)]
        wc = [jnp.exp(tt) for tt in tot]
        combos = [(i, h) for i in range(n_it) for h in range(2)]
        x2 = [jnp.concatenate([a, r], axis=0) for a, r in zip(al, rh)]
        heads_of = lambda m_t: jnp.concatenate([jnp.where(row_h0, m_t, 0.0), jnp.where(row_h0, 0.0, m_t)], axis=1)
        xb2 = [_bdot(x, heads_of(b.T)) for x, b in zip(x2, be)]
        xk2 = [_bdot(x, heads_of(k_.T)) for x, k_ in zip(x2, ka)]
        xb = [xb2[i][:, h * CHUNK:(h + 1) * CHUNK] for i, h in combos]
        xk = [xk2[i][:, h * CHUNK:(h + 1) * CHUNK] for i, h in combos]
        n_mat = [jnp.where(before[dirs[i]], -x[0:CHUNK], 0.0) for x, (i, _) in zip(xb, combos)]
        g_mat = [jnp.where(before[dirs[i]], x[0:CHUNK], 0.0) for x, (i, _) in zip(xk, combos)]
        m2 = [jnp.where(upto[dirs[i]], x[CHUNK:], 0.0) for x, (i, _) in zip(xb, combos)]
        m1 = [jnp.where(upto[dirs[i]], x[CHUNK:], 0.0) for x, (i, _) in zip(xk, combos)]
        nd = [jnp.where(blk8, n, 0.0) for n in n_mat]
        t = [eye + x for x in nd]
        p = [_bdot(x, x) for x in nd]
        t = [a + _bdot(a, b) for a, b in zip(t, p)]
        p = [_bdot(x, x) for x in p]
        t = [a + _bdot(a, b) for a, b in zip(t, p)]
        for msk in merge_masks:
            q = [_bdot(jnp.where(msk, n, 0.0), a) for n, a in zip(n_mat, t)]
            t = [a + _bdot(a, b) for a, b in zip(t, q)]
        gv = [_bdot(g_, v_c[i]) for g_, (i, _) in zip(g_mat, combos)]
        z = [_bdot(t_, jnp.concatenate([al[i], g_], axis=1)) for t_, g_, (i, _) in zip(t, gv, combos)]
        yv_h = [_bdot(m, v_c[i]) for m, (i, _) in zip(m1, combos)]
        pick = lambda a, b: jnp.where(head_lanes[0], a, b)
        alp = [pick(z[2 * i][:, :LANES], z[2 * i + 1][:, :LANES]) for i in range(n_it)]
        uv = [pick(z[2 * i][:, LANES:], z[2 * i + 1][:, LANES:]) for i in range(n_it)]
        yv = [pick(yv_h[2 * i], yv_h[2 * i + 1]) for i in range(n_it)]
        be_t = [(b * w).T for b, w in zip(be, wc)]
        ka_t = [(k_ * w).T for k_, w in zip(ka, wc)]
        pz = [_bdot(bt, jnp.concatenate([a, u_], axis=1)) for bt, a, u_ in zip(be_t, alp, uv)]
        kv = [_bdot(kt, v_) for kt, v_ in zip(ka_t, v_c)]
        a_mat = [jnp.where(same_head, jnp.where(diag, w, 0.0) - pp[:, :LANES], 0.0) for w, pp in zip(wc, pz)]
        b_mat = [jnp.where(same_head, k_ - pp[:, LANES:], 0.0) for k_, pp in zip(kv, pz)]
        for i, sc in enumerate(scans):
            st = st_s[sc]
            xs = _bdot(jnp.concatenate([alp[i], rh[i]], axis=0), st)
            u = uv[i] + xs[0:CHUNK]
            m2u = pick(_bdot(m2[2 * i], u), _bdot(m2[2 * i + 1], u))
            y_s[sc, rows[i], :] = xs[CHUNK:] + yv[i] - m2u
            st_s[sc] = _bdot(a_mat[i], st) + b_mat[i]

    def body(g, carry):
        chunk_group(g)
        return carry

    for hp in range(n_pairs):
        if has_s0:
            st_s[2 * hp] = s0f_ref[0, hp]
            st_s[2 * hp + 1] = s0b_ref[0, hp]
        else:
            st_s[2 * hp] = jnp.zeros((LANES, LANES), F32)
            st_s[2 * hp + 1] = jnp.zeros((LANES, LANES), F32)
    lax.fori_loop(0, n_chunks // unroll, body, 0)
    for hp in range(n_pairs):
        if emit_states:
            slabs = [()] if emit_states == "update" else [(lj,) for lj in range(sf_ref.shape[1])]
            for dst, sc in ((sf_ref, 2 * hp), (sb_ref, 2 * hp + 1)):
                s_vk = st_s[sc].T
                for slab in slabs:
                    own = not slab or slab[0] == layer
                    for h, blk in enumerate((s_vk[0:RWKV_HS, 0:RWKV_HS], s_vk[RWKV_HS:, RWKV_HS:])):
                        dst[(0,) + slab + (2 * hp + h,)] = blk if own else jnp.zeros_like(blk)
        y = (y_s[2 * hp] + y_s[2 * hp + 1]) + bonuses[hp]
        mean = _half_sum(y, lo) * (1.0 / RWKV_HS)
        yc = y - mean
        var = _half_sum(yc * yc, lo) * (1.0 / RWKV_HS)
        o_ref[:, hp * LANES:(hp + 1) * LANES] = ((yc * lax.rsqrt(var + GN_EPS)) * ln_gs[hp] + ln_bs[hp]) * gates[hp]


def _rwkv(p, seq, n_pairs, li, mu, w0, a0, wa_up, g_up, vecs, s0=None, states_out=None):
    t = p.shape[0]
    nb = t // seq
    width = n_pairs * LANES
    n_grp = RWKV_WIDTH // width
    base = (2 * DIFF_QW + DIFF_WIDTH) // width
    tail = (2 * DIFF_QW + DIFF_WIDTH + 3 * RWKV_WIDTH) // LANES
    col = lambda blk: pl.BlockSpec((seq, width), lambda b, hp: (b, blk(hp)))
    mu_col = lambda blk: pl.BlockSpec((None, 2, width), lambda b, hp: (li, 0, blk(hp)))
    lane_blk = lambda k: pl.BlockSpec((seq, LANES), lambda b, hp: (b, tail + k))
    mu_lane_blk = lambda k: pl.BlockSpec((None, 2, LANES), lambda b, hp: (li, 0, 3 * RWKV_WIDTH // LANES + k))
    in_specs = [
        col(lambda hp: base + hp), col(lambda hp: base + n_grp + hp), col(lambda hp: base + 2 * n_grp + hp),
        lane_blk(0), lane_blk(1),
        mu_col(lambda hp: hp), mu_col(lambda hp: n_grp + hp), mu_col(lambda hp: 2 * n_grp + hp),
        mu_lane_blk(0), mu_lane_blk(1),
        pl.BlockSpec((None, 2, width), lambda b, hp: (li, 0, hp)),
        pl.BlockSpec((None, 2, width), lambda b, hp: (li, 0, hp)),
        pl.BlockSpec((None, 2, LANES, width), lambda b, hp: (li, 0, 0, hp)),
        pl.BlockSpec((None, LANES, width), lambda b, hp: (li, 0, hp)),
        pl.BlockSpec((None, 8, width), lambda b, hp: (li, 0, hp)),
    ]
    args = [p] * 5 + [mu] * 5 + [w0, a0, wa_up, g_up, vecs]
    if s0 is not None:
        in_specs += [pl.BlockSpec((1, n_pairs, LANES, LANES), lambda b, hp: (b, hp, 0, 0))] * 2
        args += list(s0)
    out_shape = [jax.ShapeDtypeStruct((t, RWKV_WIDTH), F32)]
    out_specs = [pl.BlockSpec((seq, width), lambda b, hp: (b, hp))]
    aliases = {}
    emit_states = None
    if states_out is not None:
        emit_states = "create" if isinstance(states_out[0], jax.ShapeDtypeStruct) else "update"
        out_shape += [jax.ShapeDtypeStruct(a.shape, a.dtype) for a in states_out]
        if emit_states == "update":
            aliases = {len(args): 1, len(args) + 1: 2}
            in_specs += [pl.BlockSpec(memory_space=pl.ANY)] * 2
            args += list(states_out)
            slab, at = None, li
        else:
            slab, at = states_out[0].shape[1], 0
        out_specs += [pl.BlockSpec((1, slab, 2 * n_pairs, RWKV_HS, RWKV_HS), lambda b, hp: (b, at, hp, 0, 0))] * 2
    return pl.pallas_call(
        functools.partial(_rwkv_kernel, s0 is not None, emit_states, li),
        out_shape=out_shape,
        grid=(nb, n_grp),
        in_specs=in_specs,
        out_specs=out_specs,
        scratch_shapes=[pltpu.VMEM((n_pairs, seq, LANES), F32)] * 3 + [pltpu.VMEM((2 * n_pairs, seq, LANES), F32)] * 4
        + [pltpu.VMEM((2 * n_pairs, LANES, LANES), F32)],
        input_output_aliases=aliases,
        compiler_params=_params("parallel", "parallel"),
        name="rwkv",
    )(*args)


def _pair_states_in(s):
    nb = s.shape[0]
    st = jnp.swapaxes(s.astype(F32), -1, -2).reshape(nb, RWKV_HEADS // 2, 2, RWKV_HS, RWKV_HS)
    z = jnp.zeros_like(st[:, :, 0])
    top = jnp.concatenate([st[:, :, 0], z], axis=-1)
    bot = jnp.concatenate([z, st[:, :, 1]], axis=-1)
    return jnp.concatenate([top, bot], axis=-2)


def _dft_kernel(f_ref, ft_ref):
    n = f_ref.shape[1]
    shape = (2 * n, n)
    fr = _iota(shape, 0)
    tau = _iota(shape, 1)
    is_sin = fr >= n
    kf = jnp.where(is_sin, fr - n, fr)
    ang = ((kf * tau) & (2 * n - 1)).astype(F32) * (math.pi / n)
    val = jnp.where(is_sin, -jnp.sin(ang), jnp.cos(ang))
    nyq = jnp.where((tau & 1) == 0, 1.0, -1.0)
    val = jnp.where(is_sin & (kf == 0), nyq, val)
    f_ref[...] = val.astype(BF16)
    ft_ref[...] = val.T.astype(BF16)


def _dft_mats(n):
    return pl.pallas_call(
        _dft_kernel,
        out_shape=(jax.ShapeDtypeStruct((2 * n, n), BF16), jax.ShapeDtypeStruct((n, 2 * n), BF16)),
        name="dft_mats",
    )()


def _filter_kernel(seq, w0t_ref, w0c_ref, w0s_ref, b0_ref, w1_ref, b1_ref, w2_ref, b2_ref, fr_ref,
                   w3a_ref, w3b_ref, f_ref, o_ref, prev_scr):
    cb = f_ref.shape[1]
    n_blk = seq // cb
    e = pl.program_id(0)
    first_lag = (e - n_blk) * cb
    tap_row = jnp.abs(first_lag + _iota((1, cb), 1)).astype(F32)
    pos_row = tap_row * (1.0 / (seq - 1))
    band = _iota((BANDS, 1), 0).astype(F32)
    freq = 1e-4 + band * ((BANDS - 1 - 1e-4) / (BANDS - 1))
    arg = freq * (tap_row * (2.0 * math.pi / seq))
    fr = fr_ref[...]
    pre = (w0t_ref[...] * pos_row + _dot_f32(w0c_ref[...], jnp.cos(arg)) + _dot_f32(w0s_ref[...], -jnp.sin(arg))
           + b0_ref[...])
    hdn = jnp.sin(fr[:, 0:1] * pre)
    hdn = jnp.sin(fr[:, 1:2] * (_dot_f32(w1_ref[...], hdn) + b1_ref[...]))
    hdn = jnp.sin(fr[:, 2:3] * (_dot_f32(w2_ref[...], hdn) + b2_ref[...]))
    hdn = hdn.T
    tap = jnp.abs(first_lag + _iota((cb, 1), 0))
    pos = tap.astype(F32) * (1.0 / (seq - 1))
    chan = _iota((1, D_MODEL), 1).astype(F32)
    delta = jnp.abs(MIN_DECAY + chan * ((MAX_DECAY - MIN_DECAY) / (D_MODEL - 1)))
    window = jnp.where(tap < seq, jnp.exp(-pos * delta), 0.0)
    row = _iota((2 * cb, 1), 0)
    sign = jnp.where((row & 1) == 1, -1.0, 1.0)
    for o, w3_ref in enumerate((w3a_ref, w3b_ref)):
        cols = slice(o * D_MODEL, (o + 1) * D_MODEL)
        taps = (_dot_f32(hdn, w3_ref[...]) * window).astype(BF16)
        spec = jnp.dot(f_ref[...], taps, preferred_element_type=F32)

        @pl.when(e > 0)
        def _():
            o_ref[0, :, cols] = spec + sign * prev_scr[:, cols]

        prev_scr[:, cols] = spec - jnp.where(row <= cb, taps[0:1, :].astype(F32), 0.0)


def _hyena_filters(seq, li, fmat, small, w3):
    cb = fmat.shape[1]
    n_blk = seq // cb
    layer_slice = lambda a: pl.BlockSpec((None,) + a.shape[1:], lambda e: (li, 0, 0))
    w3_spec = lambda o: pl.BlockSpec((None, FILTER_WIDTH, D_MODEL),
                                     lambda e: (li, 0, 2 * o + jnp.where(e < n_blk, 1, 0)))
    return pl.pallas_call(
        functools.partial(_filter_kernel, seq),
        out_shape=jax.ShapeDtypeStruct((2 * n_blk - 1, 2 * cb, 2 * D_MODEL), F32),
        grid=(2 * n_blk,),
        in_specs=[layer_slice(a) for a in small] + [w3_spec(0), w3_spec(1),
                                                    pl.BlockSpec((2 * cb, cb), lambda e: (0, 0))],
        out_specs=pl.BlockSpec((1, 2 * cb, 2 * D_MODEL), lambda e: (jnp.maximum(e - 1, 0), 0, 0)),
        scratch_shapes=[pltpu.VMEM((2 * cb, 2 * D_MODEL), F32)],
        compiler_params=_params("arbitrary"),
        name="hyena_filters",
    )(*small, w3, w3, fmat)


def _short_conv(z, w, b):
    prev, nxt = _shift_rows(z)
    return prev * w[0:1] + z * w[1:2] + nxt * w[2:3] + b


def _conv_kernel(conv_v, n_blk, v_ref, x_ref, wv_ref, bv_ref, wx_ref, bx_ref, f_ref, ft_ref, kc_ref, bias_ref,
                 o_ref, uf_scr, yf_scr):
    cb = f_ref.shape[1]
    v = v_ref[...]
    if conv_v:
        v = _short_conv(v, wv_ref[...], bv_ref[...])
    fmat = f_ref[...]
    for j in range(n_blk):
        uf_scr[j] = jnp.dot(fmat, v[j * cb:(j + 1) * cb].astype(BF16), preferred_element_type=F32)

    def mix(r0, first):
        re, im = pl.ds(r0, MIX_ROWS), pl.ds(cb + r0, MIX_ROWS)
        real_pair = (_iota((MIX_ROWS, 1), 0) == 0) if first else None
        for i in range(n_blk):
            acc_r = acc_i = None
            for j in range(n_blk):
                d = i - j + n_blk - 1
                ur, ui = uf_scr[j, re, :], uf_scr[j, im, :]
                kr, ki = kc_ref[d, re, :], kc_ref[d, im, :]
                uiki = ui * ki
                if first:
                    t_r = ur * kr - jnp.where(real_pair, 0.0, uiki)
                    t_i = jnp.where(real_pair, uiki, ur * ki + ui * kr)
                else:
                    t_r = ur * kr - uiki
                    t_i = ur * ki + ui * kr
                acc_r = t_r if acc_r is None else acc_r + t_r
                acc_i = t_i if acc_i is None else acc_i + t_i
            scale = jnp.where(real_pair, 0.5 / cb, 1.0 / cb) if first else 1.0 / cb
            yf_scr[i, re, :] = acc_r * scale
            yf_scr[i, im, :] = acc_i * scale

    mix(0, True)

    def body(r, carry):
        mix(pl.multiple_of(r * MIX_ROWS, MIX_ROWS), False)
        return carry

    lax.fori_loop(1, cb // MIX_ROWS, body, 0)

    x = _short_conv(x_ref[...], wx_ref[...], bx_ref[...])
    ftm = ft_ref[...]
    bias = bias_ref[...]
    for i in range(n_blk):
        rows = slice(i * cb, (i + 1) * cb)
        y = jnp.dot(ftm, yf_scr[i].astype(BF16), preferred_element_type=F32)
        o_ref[rows, :] = x[rows] * (y + v[rows] * bias)


def _hyena_conv(v_arr, v_blk0, x_arr, x_blk0, conv_v, seq, li, order, cw, cwb, fmat, fmat_t, kc, biases, width):
    t = v_arr.shape[0]
    cb = fmat.shape[1]
    nb, ncb, n_blk = t // seq, D_MODEL // width, seq // cb
    wv_blk0 = v_blk0 if conv_v else 0
    kc_blk0 = order * ncb
    return pl.pallas_call(
        functools.partial(_conv_kernel, conv_v, n_blk),
        out_shape=jax.ShapeDtypeStruct((t, D_MODEL), F32),
        grid=(nb, ncb),
        in_specs=[
            pl.BlockSpec((seq, width), lambda b, c: (b, v_blk0 + c)),
            pl.BlockSpec((seq, width), lambda b, c: (b, x_blk0 + c)),
            pl.BlockSpec((None, 3, width), lambda b, c: (li, 0, wv_blk0 + c)),
            pl.BlockSpec((None, 1, width), lambda b, c: (li, 0, wv_blk0 + c)),
            pl.BlockSpec((None, 3, width), lambda b, c: (li, 0, x_blk0 + c)),
            pl.BlockSpec((None, 1, width), lambda b, c: (li, 0, x_blk0 + c)),
            pl.BlockSpec((2 * cb, cb), lambda b, c: (0, 0)),
            pl.BlockSpec((cb, 2 * cb), lambda b, c: (0, 0)),
            pl.BlockSpec((2 * n_blk - 1, 2 * cb, width), lambda b, c: (0, 0, kc_blk0 + c)),
            pl.BlockSpec((None, 1, width), lambda b, c: (li * 2 + order, 0, c)),
        ],
        out_specs=pl.BlockSpec((seq, width), lambda b, c: (b, c)),
        scratch_shapes=[pltpu.VMEM((n_blk, 2 * cb, width), F32), pltpu.VMEM((n_blk, 2 * cb, width), F32)],
        compiler_params=_params("parallel", "parallel"),
        name="hyena_conv",
    )(v_arr, x_arr, cw, cwb, cw, cwb, fmat, fmat_t, kc, biases)


def _axial_rope_tables(seq):
    n_rows = seq // GRID_W
    row = np.repeat(np.arange(n_rows, dtype=np.float64), GRID_W)
    col = np.tile(np.arange(GRID_W, dtype=np.float64), n_rows)
    inv = ROPE_BASE ** (-np.arange(ROPE_PAIRS, dtype=np.float64) / ROPE_PAIRS)
    ar = row[:, None] * inv[None]
    ac = col[:, None] * inv[None]
    ang = np.concatenate([ar, ar, ac, ac], axis=-1)
    reps = LANES // DIFF_QK
    return (jnp.asarray(np.tile(np.cos(ang), (1, reps)), F32), jnp.asarray(np.tile(np.sin(ang), (1, reps)), F32))


def _pair_lanes(a, seq_axis):
    nb = a.shape[0]
    if a.ndim == 5:
        a = a.reshape(nb, DIFF_HEADS // 2, 2, 2, a.shape[3], DIFF_QK).transpose(0, 1, 4, 2, 3, 5)
    else:
        a = a.reshape(nb, DIFF_HEADS // 2, 2, a.shape[2], DIFF_V).transpose(0, 1, 3, 2, 4)
    return a.reshape(nb, DIFF_HEADS // 2, a.shape[2], LANES)


def kernel(x_prompt, x_sample, cache_diff_k, cache_diff_v, state_rwkv_fwd, state_rwkv_bwd, c, c_ctx, ada_w, ada_b, norm1_g, norm2_g, ffn_w1, ffn_w3, ffn_w2, final_g, ab_w_in, ab_w_out, diff_lambda, diff_subln_g, rwkv_mu, rwkv_w0, rwkv_w_up, rwkv_a0, rwkv_a_up, rwkv_g_up, rwkv_k_k, rwkv_k_a, rwkv_r_k, rwkv_ln_g, rwkv_ln_b, hy_w_in, hy_b_in, hy_conv_w, hy_conv_b, hy_f_w0, hy_f_b0, hy_f_w1, hy_f_b1, hy_f_w2, hy_f_b2, hy_f_w3, hy_f_freq, hy_bias, hy_w_out, hy_b_out):
    n_ctx_seqs, ctx_len, _ = x_prompt.shape
    n_lat_seqs, lat_len, _ = x_sample.shape
    tm = 1024

    cond = jnp.zeros((N_MOD_ROWS, D_MODEL), F32).at[0].set(c_ctx).at[1:1 + n_lat_seqs].set(c)
    mods = _ada_all(cond, ada_w, ada_b).reshape(DEPTH * N_MOD_ROWS * 6, 1, D_MODEL)

    groups = [
        dict(x=x_prompt.reshape(-1, D_MODEL), seq=ctx_len, rows=(0, n_ctx_seqs * ctx_len), tq=ctx_len, pairs=4,
             scan_pairs=2, conv_width=1024),
        dict(x=x_sample.reshape(-1, D_MODEL), seq=lat_len, rows=(1, lat_len), tq=1024, pairs=1, scan_pairs=1,
             conv_width=256),
    ]
    rope = _axial_rope_tables(lat_len)
    dft = {cb: _dft_mats(cb) for cb in {min(CONV_BLOCK_MAX, g["seq"]) for g in groups}}
    n_ab, n_c = ab_w_in.shape[0], hy_w_in.shape[0]

    gains1 = norm1_g.reshape(DEPTH, 1, D_MODEL)
    gains2 = norm2_g.reshape(DEPTH, 1, D_MODEL)
    g2s = jnp.tile(diff_subln_g, (1, 2)).reshape(n_ab, 1, LANES)
    wa_up = jnp.concatenate([rwkv_w_up, rwkv_a_up], axis=2)
    vec_rows = [rwkv_k_k, rwkv_k_a, rwkv_r_k.reshape(n_ab, RWKV_WIDTH), rwkv_ln_g, rwkv_ln_b]
    vecs = jnp.stack(vec_rows + [jnp.zeros_like(rwkv_k_k)] * (8 - len(vec_rows)), axis=1)
    w0_t = jnp.swapaxes(hy_f_w0, 1, 2)
    filt_small = [w0_t[:, :, 0:1], w0_t[:, :, 1:1 + BANDS], w0_t[:, :, 1 + BANDS:], hy_f_b0[:, :, None],
                  jnp.swapaxes(hy_f_w1, 1, 2), hy_f_b1[:, :, None], jnp.swapaxes(hy_f_w2, 1, 2), hy_f_b2[:, :, None],
                  jnp.swapaxes(hy_f_freq, 1, 2)]
    hy_b_in_s = hy_b_in.reshape(n_c, 1, -1)
    hy_b_out_s = hy_b_out.reshape(n_c, 1, D_MODEL)
    conv_b = hy_conv_b.reshape(n_c, 1, -1)
    conv_bias = hy_bias.reshape(n_c * 2, 1, D_MODEL)
    new_kv = (jax.ShapeDtypeStruct((n_ctx_seqs, n_ab, DIFF_HEADS, 2, ctx_len, DIFF_QK), F32),
              jax.ShapeDtypeStruct((n_ctx_seqs, n_ab, DIFF_HEADS, ctx_len, DIFF_V), F32))
    new_states = (jax.ShapeDtypeStruct((n_ctx_seqs, n_ab, RWKV_HEADS, RWKV_HS, RWKV_HS), F32),) * 2

    for l in range(DEPTH):
        i = l // 2
        for gi, g in enumerate(groups):
            x, seq, row = g["x"], g["seq"], g["rows"]
            if l % 2 == 0:
                p = _proj_in(x, gains1, mods, l, row, ab_w_in, i, None, tm // 2)
                if gi == 0:
                    att, *new_kv = _attention(p, seq, g["tq"], g["pairs"], _lambda_init(l), diff_lambda, g2s, i,
                                              kv_out=new_kv)
                    mix, *new_states = _rwkv(p, seq, g["scan_pairs"], i, rwkv_mu, rwkv_w0, rwkv_a0, wa_up, rwkv_g_up,
                                             vecs, states_out=new_states)
                else:
                    ctx = (_pair_lanes(cache_diff_k[:, i], 3), _pair_lanes(cache_diff_v[:, i], 2))
                    att, = _attention(p, seq, g["tq"], g["pairs"], _lambda_init(l), diff_lambda, g2s, i,
                                      ctx=ctx, rope=rope)
                    s0 = (_pair_states_in(state_rwkv_fwd[:, i]), _pair_states_in(state_rwkv_bwd[:, i]))
                    mix, = _rwkv(p, seq, g["scan_pairs"], i, rwkv_mu, rwkv_w0, rwkv_a0, wa_up, rwkv_g_up, vecs, s0=s0)
                x = _proj_out(x, mods, l, row, [att, mix], ab_w_out, i, None, tm)
            else:
                fmat, fmat_t = dft[min(CONV_BLOCK_MAX, seq)]
                z = _proj_in(x, gains1, mods, l, row, hy_w_in, i, hy_b_in_s, tm // 2)
                kc = _hyena_filters(seq, i, fmat, filt_small, hy_f_w3)
                width = g["conv_width"]
                ncb = D_MODEL // width
                u = _hyena_conv(z, 2 * ncb, z, 0, True, seq, i, 0, hy_conv_w, conv_b, fmat, fmat_t, kc, conv_bias, width)
                u = _hyena_conv(u, 0, z, ncb, False, seq, i, 1, hy_conv_w, conv_b, fmat, fmat_t, kc, conv_bias, width)
                x = _proj_out(x, mods, l, row, [u], hy_w_out, i, hy_b_out_s, tm)
            g["x"] = _ffn(x, gains2, mods, l, row, ffn_w1, ffn_w3, ffn_w2, final_g, l == DEPTH - 1, 2 * tm, 256)

    y_prompt = groups[0]["x"].reshape(x_prompt.shape)
    y_sample = groups[1]["x"].reshape(x_sample.shape)
    return (y_prompt, y_sample, new_kv[0], new_kv[1], new_states[0], new_states[1])
```

```python
import functools
import math

import jax
import jax.numpy as jnp
import numpy as np
from jax import lax
from jax.experimental import pallas as pl
from jax.experimental.pallas import tpu as pltpu

F32 = jnp.float32
BF16 = jnp.bfloat16

D_MODEL = 1024
DEPTH = 4
GRID_W = 64
DIFF_HEADS = 8
DIFF_QK = 32
DIFF_V = 64
DIFF_QW = DIFF_HEADS * 2 * DIFF_QK
DIFF_WIDTH = DIFF_HEADS * DIFF_V
ROPE_PAIRS = DIFF_QK // 4
ROPE_BASE = 10000.0
RWKV_HEADS = 8
RWKV_HS = 64
RWKV_WIDTH = RWKV_HEADS * RWKV_HS
RWKV_IN = 3 * RWKV_WIDTH + 64 + 64 + 128
AB_IN = 2 * DIFF_QW + DIFF_WIDTH + RWKV_IN
BANDS = 16
FILTER_WIDTH = 64
MAX_DECAY = math.log(1e-2) / 0.3
MIN_DECAY = math.log(1e-2) / 1.5
D_FF = 2816
RMS_EPS = 1e-6
GN_EPS = 64e-5
HEAD_RMS_EPS = 1e-5
LOG2_E = math.log2(math.e)

LANES = 128
CHUNK = 128
SCAN_UNROLL = 4
CONV_BLOCK_MAX = 512
MIX_ROWS = 16
VMEM_LIMIT = 56 * 1024 * 1024

N_MOD_ROWS = 8


def _lambda_init(l):
    return 0.8 - 0.6 * math.exp(-0.3 * l)


def _params(*sem):
    return pltpu.CompilerParams(dimension_semantics=sem, vmem_limit_bytes=VMEM_LIMIT)


def _bdot(a, b):
    return jnp.dot(a.astype(BF16), b.astype(BF16), preferred_element_type=F32)


def _bdot_nt(a, b):
    return lax.dot_general(a.astype(BF16), b.astype(BF16), (((1,), (1,)), ((), ())),
                           preferred_element_type=F32)


def _split3(x):
    h1 = x.astype(BF16)
    r1 = x - h1.astype(F32)
    h2 = r1.astype(BF16)
    h3 = (r1 - h2.astype(F32)).astype(BF16)
    return h1, h2, h3


def _dot_f32(a, b):
    a1, a2, a3 = _split3(a)
    b1, b2, b3 = _split3(b)
    d = lambda x, y: jnp.dot(x, y, preferred_element_type=F32)
    return (d(a1, b1) + (d(a1, b2) + d(a2, b1))) + ((d(a1, b3) + d(a3, b1)) + d(a2, b2))


def _dot_exact_lhs(a_bf16, b):
    b1, b2, b3 = _split3(b)
    d = lambda y: jnp.dot(a_bf16, y, preferred_element_type=F32)
    return d(b1) + (d(b2) + d(b3))


def _sigmoid(x):
    return 0.5 * (jnp.tanh(0.5 * x) + 1.0)


def _iota(shape, dim):
    return lax.broadcasted_iota(jnp.int32, shape, dim)


def _shift_rows(x):
    n = x.shape[0]
    row = _iota(x.shape, 0)
    prev = jnp.where(row == 0, 0.0, pltpu.roll(x, 1, 0))
    nxt = jnp.where(row == n - 1, 0.0, pltpu.roll(x, n - 1, 0))
    return prev, nxt


def _half_sum(x, lo_mask):
    s_lo = jnp.sum(jnp.where(lo_mask, x, 0.0), axis=-1, keepdims=True)
    s_hi = jnp.sum(jnp.where(lo_mask, 0.0, x), axis=-1, keepdims=True)
    return jnp.where(lo_mask, s_lo, s_hi)


def _ada_kernel(c_ref, w_ref, b_ref, o_ref):
    c = c_ref[...]
    o_ref[0] = _bdot(c * _sigmoid(c), w_ref[0]) + b_ref[0]


def _ada_all(cond, ada_w, ada_b):
    tn = 1536
    n_out = 6 * D_MODEL
    return pl.pallas_call(
        _ada_kernel,
        out_shape=jax.ShapeDtypeStruct((DEPTH, N_MOD_ROWS, n_out), F32),
        grid=(DEPTH, n_out // tn),
        in_specs=[
            pl.BlockSpec((N_MOD_ROWS, D_MODEL), lambda l, j: (0, 0)),
            pl.BlockSpec((1, D_MODEL, tn), lambda l, j: (l, 0, j)),
            pl.BlockSpec((1, 1, tn), lambda l, j: (l, 0, j)),
        ],
        out_specs=pl.BlockSpec((1, N_MOD_ROWS, tn), lambda l, j: (l, 0, j)),
        compiler_params=_params("parallel", "parallel"),
        name="ada",
    )(cond, ada_w, ada_b.reshape(DEPTH, 1, n_out))


def _mod_spec(layer, chunk, cond_rows, tm, axis=0):
    first, span = cond_rows

    def index_map(*idx):
        return ((layer * N_MOD_ROWS + first + (idx[axis] * tm) // span) * 6 + chunk, 0, 0)
    return pl.BlockSpec((1, 1, D_MODEL), index_map)


def _normed(x, g, sc, sh):
    ms = jnp.mean(x * x, axis=-1, keepdims=True)
    return (x * lax.rsqrt(ms + RMS_EPS)) * g * (1.0 + sc) + sh


def _proj_in_kernel(has_bias, x_ref, g_ref, sc_ref, sh_ref, w_ref, *refs):
    o_ref, wb_scr = refs[-2:]

    @pl.when(pl.program_id(0) == 0)
    def _():
        wb_scr[...] = w_ref[...].astype(BF16)

    h = _normed(x_ref[...], g_ref[...], sc_ref[0], sh_ref[0])
    y = jnp.dot(h.astype(BF16), wb_scr[...], preferred_element_type=F32)
    o_ref[...] = y + refs[0][...] if has_bias else y


def _proj_in(x, gains, mods, layer, cond_rows, ws, wi, bs, tm):
    t, n_out = x.shape[0], ws.shape[2]
    in_specs = [
        pl.BlockSpec((tm, D_MODEL), lambda i: (i, 0)),
        pl.BlockSpec((None, 1, D_MODEL), lambda i: (layer, 0, 0)),
        _mod_spec(layer, 1, cond_rows, tm),
        _mod_spec(layer, 0, cond_rows, tm),
        pl.BlockSpec((None, D_MODEL, n_out), lambda i: (wi, 0, 0), pipeline_mode=pl.Buffered(1)),
    ]
    args = [x, gains, mods, mods, ws]
    if bs is not None:
        in_specs.append(pl.BlockSpec((None, 1, n_out), lambda i: (wi, 0, 0)))
        args.append(bs)
    return pl.pallas_call(
        functools.partial(_proj_in_kernel, bs is not None),
        out_shape=jax.ShapeDtypeStruct((t, n_out), F32),
        grid=(t // tm,),
        in_specs=in_specs,
        out_specs=pl.BlockSpec((tm, n_out), lambda i: (i, 0)),
        scratch_shapes=[pltpu.VMEM((D_MODEL, n_out), BF16)],
        compiler_params=_params("arbitrary"),
        name="proj_in",
    )(*args)


def _proj_out_kernel(n_u, has_bias, x_ref, gt_ref, *refs):
    u_refs, w_ref, o_ref = refs[:n_u], refs[n_u], refs[-1]
    acc = None
    off = 0
    for u_ref in u_refs:
        k = u_ref.shape[1]
        part = _bdot(u_ref[...], w_ref[off:off + k, :])
        acc = part if acc is None else acc + part
        off += k
    if has_bias:
        acc = acc + refs[n_u + 1][...]
    o_ref[...] = x_ref[...] + gt_ref[0] * acc


def _proj_out(x, mods, layer, cond_rows, us, ws, wi, bs, tm):
    t = x.shape[0]
    in_specs = [
        pl.BlockSpec((tm, D_MODEL), lambda i: (i, 0)),
        _mod_spec(layer, 2, cond_rows, tm),
        *[pl.BlockSpec((tm, u.shape[1]), lambda i: (i, 0)) for u in us],
        pl.BlockSpec((None,) + ws.shape[1:], lambda i: (wi, 0, 0)),
    ]
    args = [x, mods, *us, ws]
    if bs is not None:
        in_specs.append(pl.BlockSpec((None, 1, D_MODEL), lambda i: (wi, 0, 0)))
        args.append(bs)
    return pl.pallas_call(
        functools.partial(_proj_out_kernel, len(us), bs is not None),
        out_shape=jax.ShapeDtypeStruct((t, D_MODEL), F32),
        grid=(t // tm,),
        in_specs=in_specs,
        out_specs=pl.BlockSpec((tm, D_MODEL), lambda i: (i, 0)),
        compiler_params=_params("parallel"),
        name="proj_out",
    )(*args)


def _ffn_kernel(final, x_ref, g_ref, sc_ref, sh_ref, gt_ref, w1_ref, w3_ref, w2_ref, fg_ref, o_ref, h_scr):
    k = pl.program_id(1)

    @pl.when(k == 0)
    def _():
        h_scr[...] = _normed(x_ref[...], g_ref[...], sc_ref[0], sh_ref[0]).astype(BF16)
        o_ref[...] = jnp.zeros_like(o_ref)

    h = h_scr[...]
    a1 = jnp.dot(h, w1_ref[...].astype(BF16), preferred_element_type=F32)
    a3 = jnp.dot(h, w3_ref[...].astype(BF16), preferred_element_type=F32)
    o_ref[...] += _bdot((a1 * _sigmoid(a1)) * a3, w2_ref[...])

    @pl.when(k == pl.num_programs(1) - 1)
    def _():
        y = x_ref[...] + gt_ref[0] * o_ref[...]
        if final:
            ms = jnp.mean(y * y, axis=-1, keepdims=True)
            y = (y * lax.rsqrt(ms + RMS_EPS)) * fg_ref[...]
        o_ref[...] = y


def _ffn(x, gains, mods, layer, cond_rows, w1s, w3s, w2s, final_g, final, tm, tk):
    t = x.shape[0]
    return pl.pallas_call(
        functools.partial(_ffn_kernel, final),
        out_shape=jax.ShapeDtypeStruct((t, D_MODEL), F32),
        grid=(t // tm, D_FF // tk),
        in_specs=[
            pl.BlockSpec((tm, D_MODEL), lambda i, k: (i, 0)),
            pl.BlockSpec((None, 1, D_MODEL), lambda i, k: (layer, 0, 0)),
            _mod_spec(layer, 4, cond_rows, tm),
            _mod_spec(layer, 3, cond_rows, tm),
            _mod_spec(layer, 5, cond_rows, tm),
            pl.BlockSpec((None, D_MODEL, tk), lambda i, k: (layer, 0, k)),
            pl.BlockSpec((None, D_MODEL, tk), lambda i, k: (layer, 0, k)),
            pl.BlockSpec((None, tk, D_MODEL), lambda i, k: (layer, k, 0)),
            pl.BlockSpec((1, D_MODEL), lambda i, k: (0, 0)),
        ],
        out_specs=pl.BlockSpec((tm, D_MODEL), lambda i, k: (i, 0)),
        scratch_shapes=[pltpu.VMEM((tm, D_MODEL), BF16)],
        compiler_params=_params("parallel", "arbitrary"),
        name="ffn",
    )(x, gains, mods, mods, mods, w1s, w3s, w2s, final_g.reshape(1, D_MODEL))


def _rope(x, cos, sin):
    lane = _iota(x.shape, 1)
    rot = jnp.where((lane % 16) < 8, -pltpu.roll(x, LANES - 8, 1), pltpu.roll(x, 8, 1))
    return x * cos + rot * sin


def _attn_kernel(n_ctx, use_rope, emit_kv, layer, lam_init, *refs):
    it = iter(refs)
    q_ref, k_ref, v_ref = next(it), next(it), next(it)
    kc_ref = vc_ref = cq_ref = sq_ref = ck_ref = sk_ref = ko_ref = vo_ref = None
    if n_ctx:
        kc_ref, vc_ref = next(it), next(it)
    if use_rope:
        cq_ref, sq_ref, ck_ref, sk_ref = next(it), next(it), next(it), next(it)
    lam_ref, g_ref = next(it), next(it)
    if emit_kv == "update":
        next(it), next(it)
    o_ref = next(it)
    if emit_kv:
        ko_ref, vo_ref = next(it), next(it)
    kall, vall = next(it), next(it)
    n_own = k_ref.shape[0]
    n_pairs = q_ref.shape[1] // LANES

    @pl.when(pl.program_id(2) == 0)
    def _():
        k = k_ref[...]
        v = v_ref[...]
        if emit_kv:
            slabs = [()] if emit_kv == "update" else [(lj,) for lj in range(ko_ref.shape[1])]
            for slab in slabs:
                own = not slab or slab[0] == layer
                for h in range(2 * n_pairs):
                    for m in range(2):
                        c0 = (2 * h + m) * DIFF_QK
                        kh = k[:, c0:c0 + DIFF_QK]
                        ko_ref[(0,) + slab + (h, m)] = kh if own else jnp.zeros_like(kh)
                    vh = v[:, h * DIFF_V:(h + 1) * DIFF_V]
                    vo_ref[(0,) + slab + (h,)] = vh if own else jnp.zeros_like(vh)
        if use_rope:
            k = _rope(k, ck_ref[...], sk_ref[...])
        if n_ctx:
            kall[0:n_ctx, :] = kc_ref[0, 0].astype(BF16)
            vall[0:n_ctx, :] = vc_ref[0, 0].astype(BF16)
        kall[n_ctx:n_ctx + n_own, :] = k.astype(BF16)
        vall[n_ctx:n_ctx + n_own, :] = v.astype(BF16)

    lv = lam_ref[...]
    lam = (jnp.exp(jnp.sum(lv[0:1] * lv[1:2], axis=-1, keepdims=True))
           - jnp.exp(jnp.sum(lv[2:3] * lv[3:4], axis=-1, keepdims=True)) + lam_init)

    lane = _iota((q_ref.shape[0], LANES), 1)
    lo = lane < DIFF_V
    scale = DIFF_QK ** -0.5
    for hp in range(n_pairs):
        cols = slice(hp * LANES, (hp + 1) * LANES)
        q = q_ref[:, cols]
        if use_rope:
            q = _rope(q, cq_ref[...], sq_ref[...])
        ks, vs = kall[:, cols], vall[:, cols]
        outs = []
        for h in range(2):
            pv, inv = [], []
            for m in range(2):
                j = 2 * h + m
                qm = jnp.where((lane >= DIFF_QK * j) & (lane < DIFF_QK * (j + 1)), q, 0.0)
                s = _bdot_nt(qm, ks)
                e = jnp.exp2((s - jnp.max(s, axis=-1, keepdims=True)) * (scale * LOG2_E))
                inv.append(1.0 / jnp.sum(e, axis=-1, keepdims=True))
                pv.append(_bdot(e, vs))
            outs.append(pv[0] * inv[0] - pv[1] * (lam * inv[1]))
        o = jnp.where(lo, outs[0], outs[1])
        ms = _half_sum(o * o, lo) * (1.0 / DIFF_V)
        o_ref[:, cols] = (o * lax.rsqrt(ms + HEAD_RMS_EPS)) * g_ref[...] * (1.0 - lam_init)


def _attention(p, seq, tq, n_pairs, lam_init, lams, g2s, li, ctx=None, rope=None, kv_out=None):
    emit_kv = None if kv_out is None else ("create" if isinstance(kv_out[0], jax.ShapeDtypeStruct) else "update")
    assert rope is None or n_pairs == 1
    t = p.shape[0]
    nb, nq = t // seq, seq // tq
    n_ctx = 0 if ctx is None else ctx[0].shape[2]
    width = n_pairs * LANES
    k_blk, v_blk = DIFF_QW // width, 2 * DIFF_QW // width
    in_specs = [
        pl.BlockSpec((tq, width), lambda b, hp, qi: (b * nq + qi, hp)),
        pl.BlockSpec((seq, width), lambda b, hp, qi: (b, k_blk + hp)),
        pl.BlockSpec((seq, width), lambda b, hp, qi: (b, v_blk + hp)),
    ]
    args = [p, p, p]
    if ctx is not None:
        in_specs += [pl.BlockSpec((1, 1, n_ctx, width), lambda b, hp, qi: (b, hp, 0, 0))] * 2
        args += list(ctx)
    if rope is not None:
        in_specs += [pl.BlockSpec((tq, LANES), lambda b, hp, qi: (qi, 0))] * 2
        in_specs += [pl.BlockSpec((seq, width), lambda b, hp, qi: (0, 0))] * 2
        args += [rope[0], rope[1], rope[0], rope[1]]
    in_specs += [pl.BlockSpec((None, 4, DIFF_QK), lambda b, hp, qi: (li, 0, 0)),
                 pl.BlockSpec((None, 1, LANES), lambda b, hp, qi: (li, 0, 0))]
    args += [lams, g2s]
    out_shape = [jax.ShapeDtypeStruct((t, DIFF_WIDTH), F32)]
    out_specs = [pl.BlockSpec((tq, width), lambda b, hp, qi: (b * nq + qi, hp))]
    aliases = {}
    if emit_kv:
        out_shape += [jax.ShapeDtypeStruct(a.shape, a.dtype) for a in kv_out]
        if emit_kv == "update":
            aliases = {len(args): 1, len(args) + 1: 2}
            in_specs += [pl.BlockSpec(memory_space=pl.ANY)] * 2
            args += list(kv_out)
            slab, at = None, li
        else:
            slab, at = kv_out[0].shape[1], 0
        out_specs += [pl.BlockSpec((1, slab, 2 * n_pairs, 2, seq, DIFF_QK), lambda b, hp, qi: (b, at, hp, 0, 0, 0)),
                      pl.BlockSpec((1, slab, 2 * n_pairs, seq, DIFF_V), lambda b, hp, qi: (b, at, hp, 0, 0))]
    return pl.pallas_call(
        functools.partial(_attn_kernel, n_ctx, rope is not None, emit_kv, li, lam_init),
        out_shape=out_shape,
        grid=(nb, DIFF_HEADS // 2 // n_pairs, nq),
        in_specs=in_specs,
        out_specs=out_specs,
        scratch_shapes=[pltpu.VMEM((n_ctx + seq, width), BF16), pltpu.VMEM((n_ctx + seq, width), BF16)],
        input_output_aliases=aliases,
        compiler_params=_params("parallel", "parallel", "arbitrary"),
        name="diff_attn",
    )(*args)


def _rwkv_kernel(has_s0, emit_states, layer, *refs):
    it = iter(refs)
    r_ref, k_ref, v_ref, wa_ref, gd_ref = (next(it) for _ in range(5))
    mu_r, mu_k, mu_v, mu_wa, mu_gd = (next(it) for _ in range(5))
    w0_ref, a0_ref, wa_up_ref, g_up_ref, vec_ref = (next(it) for _ in range(5))
    s0f_ref = s0b_ref = sf_ref = sb_ref = None
    if has_s0:
        s0f_ref, s0b_ref = next(it), next(it)
    if emit_states == "update":
        next(it), next(it)
    o_ref = next(it)
    if emit_states:
        sf_ref, sb_ref = next(it), next(it)
    r_s, v_s, kk_s = next(it), next(it), next(it)
    lw_s, kd_s, b_s, y_s = next(it), next(it), next(it), next(it)
    st_s = next(it)

    seq = r_ref.shape[0]
    n_pairs = r_ref.shape[1] // LANES
    n_chunks = seq // CHUNK
    unroll = min(SCAN_UNROLL, n_chunks)

    def shifted(x, mu):
        prev, nxt = _shift_rows(x)
        return x + mu[0:1] * (prev - x) + mu[1:2] * (nxt - x)

    lane = _iota((seq, LANES), 1)
    lo = lane < RWKV_HS
    wa = shifted(wa_ref[...], mu_wa[...])
    gd_act = _sigmoid(shifted(gd_ref[...], mu_gd[...]))
    wd_in = jnp.where(lo, jnp.tanh(wa), 0.0)
    ad_in = jnp.where(lo, 0.0, wa)
    gates, bonuses, ln_gs, ln_bs = [], [], [], []
    for hp in range(n_pairs):
        cols = slice(hp * LANES, (hp + 1) * LANES)
        r = shifted(r_ref[:, cols], mu_r[:, cols])
        k = shifted(k_ref[:, cols], mu_k[:, cols])
        v = shifted(v_ref[:, cols], mu_v[:, cols])
        vec = vec_ref[:, cols]
        k_k, k_a, r_k, ln_g, ln_b = (vec[i:i + 1] for i in range(5))
        gates.append(_bdot(gd_act, g_up_ref[:, cols]))
        ln_gs.append(ln_g)
        ln_bs.append(ln_b)
        kk = k * k_k
        kk = kk * jnp.minimum(lax.rsqrt(_half_sum(kk * kk, lo)), 1e12)
        r_s[hp] = r
        v_s[hp] = v
        kk_s[hp] = kk
        bonus = jnp.zeros((seq, LANES), F32)
        for d in range(2):
            wpre = w0_ref[d:d + 1, cols] + _bdot(wd_in, wa_up_ref[d, :, cols])
            sp = jnp.maximum(-wpre, 0.0) + jnp.log(1.0 + jnp.exp(-jnp.abs(wpre)))
            lw_s[2 * hp + d] = -jnp.exp(-sp - 0.5)
            a = _sigmoid(a0_ref[d:d + 1, cols] + _bdot(ad_in, wa_up_ref[d, :, cols]))
            kd = k * (1.0 + (a - 1.0) * k_a)
            kd_s[2 * hp + d] = kd
            b_s[2 * hp + d] = kk * a
            bonus = bonus + _half_sum(r * kd * r_k, lo) * v
        bonuses.append(bonus)

    ci = _iota((CHUNK, CHUNK), 0)
    cj = _iota((CHUNK, CHUNK), 1)
    eye = jnp.where(ci == cj, 1.0, 0.0)
    blk8 = (ci // 8) == (cj // 8)
    merge_sizes = [8 * 2 ** n for n in range(int(math.log2(CHUNK // 8)))]
    merge_masks = [((ci // (2 * s)) == (cj // (2 * s))) & ((ci // s) != (cj // s)) for s in merge_sizes]
    before = (cj < ci, cj > ci)
    upto = (cj <= ci, cj >= ci)
    tri = tuple(jnp.where(m, 1.0, 0.0).astype(BF16) for m in upto)
    clane = _iota((CHUNK, LANES), 1)
    head_lanes = (clane < RWKV_HS, clane >= RWKV_HS)
    si = _iota((LANES, LANES), 0)
    sj = _iota((LANES, LANES), 1)
    same_head = (si < RWKV_HS) == (sj < RWKV_HS)
    diag = si == sj
    row_h0 = _iota((LANES, CHUNK), 0) < RWKV_HS

    def chunk_group(g):
        items = []
        for hp in range(n_pairs):
            items += [(2 * hp, g * unroll + u) for u in range(unroll)]
            items += [(2 * hp + 1, n_chunks - 1 - (g * unroll + u)) for u in range(unroll)]
        n_it = len(items)
        scans = [sc for sc, _ in items]
        dirs = [sc % 2 for sc in scans]
        rows = [pl.ds(pl.multiple_of(c * CHUNK, CHUNK), CHUNK) for _, c in items]
        r_c = [r_s[sc // 2, rw, :] for sc, rw in zip(scans, rows)]
        v_c = [v_s[sc // 2, rw, :] for sc, rw in zip(scans, rows)]
        kk_c = [kk_s[sc // 2, rw, :] for sc, rw in zip(scans, rows)]
        lw_c = [lw_s[sc, rw, :] for sc, rw in zip(scans, rows)]
        kd_c = [kd_s[sc, rw, :] for sc, rw in zip(scans, rows)]
        b_c = [b_s[sc, rw, :] for sc, rw in zip(scans, rows)]
        cum = [_dot_exact_lhs(tri[d], lw) for d, lw in zip(dirs, lw_c)]
        tot = [cm[CHUNK - 1:CHUNK, :] if d == 0 else cm[0:1, :] for d, cm in zip(dirs, cum)]
        e_neg = [jnp.exp(-cm) for cm in cum]
        al = [jnp.exp(cm - lw) * kk for cm, lw, kk in zip(cum, lw_c, kk_c)]
        be = [b * e for b, e in zip(b_c, e_neg)]
        ka = [kd * e for kd, e in zip(kd_c, e_neg)]
        rh = [r * jnp.exp(cm) for r, cm in zip(r_c, cum)]
        wc = [jnp.exp(tt) for tt in tot]
        combos = [(i, h) for i in range(n_it) for h in range(2)]
        x2 = [jnp.concatenate([a, r], axis=0) for a, r in zip(al, rh)]
        heads_of = lambda m_t: jnp.concatenate([jnp.where(row_h0, m_t, 0.0), jnp.where(row_h0, 0.0, m_t)], axis=1)
        xb2 = [_bdot(x, heads_of(b.T)) for x, b in zip(x2, be)]
        xk2 = [_bdot(x, heads_of(k_.T)) for x, k_ in zip(x2, ka)]
        xb = [xb2[i][:, h * CHUNK:(h + 1) * CHUNK] for i, h in combos]
        xk = [xk2[i][:, h * CHUNK:(h + 1) * CHUNK] for i, h in combos]
        n_mat = [jnp.where(before[dirs[i]], -x[0:CHUNK], 0.0) for x, (i, _) in zip(xb, combos)]
        g_mat = [jnp.where(before[dirs[i]], x[0:CHUNK], 0.0) for x, (i, _) in zip(xk, combos)]
        m2 = [jnp.where(upto[dirs[i]], x[CHUNK:], 0.0) for x, (i, _) in zip(xb, combos)]
        m1 = [jnp.where(upto[dirs[i]], x[CHUNK:], 0.0) for x, (i, _) in zip(xk, combos)]
        nd = [jnp.where(blk8, n, 0.0) for n in n_mat]
        t = [eye + x for x in nd]
        p = [_bdot(x, x) for x in nd]
        t = [a + _bdot(a, b) for a, b in zip(t, p)]
        p = [_bdot(x, x) for x in p]
        t = [a + _bdot(a, b) for a, b in zip(t, p)]
        for msk in merge_masks:
            q = [_bdot(jnp.where(msk, n, 0.0), a) for n, a in zip(n_mat, t)]
            t = [a + _bdot(a, b) for a, b in zip(t, q)]
        gv = [_bdot(g_, v_c[i]) for g_, (i, _) in zip(g_mat, combos)]
        z = [_bdot(t_, jnp.concatenate([al[i], g_], axis=1)) for t_, g_, (i, _) in zip(t, gv, combos)]
        yv_h = [_bdot(m, v_c[i]) for m, (i, _) in zip(m1, combos)]
        pick = lambda a, b: jnp.where(head_lanes[0], a, b)
        alp = [pick(z[2 * i][:, :LANES], z[2 * i + 1][:, :LANES]) for i in range(n_it)]
        uv = [pick(z[2 * i][:, LANES:], z[2 * i + 1][:, LANES:]) for i in range(n_it)]
        yv = [pick(yv_h[2 * i], yv_h[2 * i + 1]) for i in range(n_it)]
        be_t = [(b * w).T for b, w in zip(be, wc)]
        ka_t = [(k_ * w).T for k_, w in zip(ka, wc)]
        pz = [_bdot(bt, jnp.concatenate([a, u_], axis=1)) for bt, a, u_ in zip(be_t, alp, uv)]
        kv = [_bdot(kt, v_) for kt, v_ in zip(ka_t, v_c)]
        a_mat = [jnp.where(same_head, jnp.where(diag, w, 0.0) - pp[:, :LANES], 0.0) for w, pp in zip(wc, pz)]
        b_mat = [jnp.where(same_head, k_ - pp[:, LANES:], 0.0) for k_, pp in zip(kv, pz)]
        for i, sc in enumerate(scans):
            st = st_s[sc]
            xs = _bdot(jnp.concatenate([alp[i], rh[i]], axis=0), st)
            u = uv[i] + xs[0:CHUNK]
            m2u = pick(_bdot(m2[2 * i], u), _bdot(m2[2 * i + 1], u))
            y_s[sc, rows[i], :] = xs[CHUNK:] + yv[i] - m2u
            st_s[sc] = _bdot(a_mat[i], st) + b_mat[i]

    def body(g, carry):
        chunk_group(g)
        return carry

    for hp in range(n_pairs):
        if has_s0:
            st_s[2 * hp] = s0f_ref[0, hp]
            st_s[2 * hp + 1] = s0b_ref[0, hp]
        else:
            st_s[2 * hp] = jnp.zeros((LANES, LANES), F32)
            st_s[2 * hp + 1] = jnp.zeros((LANES, LANES), F32)
    lax.fori_loop(0, n_chunks // unroll, body, 0)
    for hp in range(n_pairs):
        if emit_states:
            slabs = [()] if emit_states == "update" else [(lj,) for lj in range(sf_ref.shape[1])]
            for dst, sc in ((sf_ref, 2 * hp), (sb_ref, 2 * hp + 1)):
                s_vk = st_s[sc].T
                for slab in slabs:
                    own = not slab or slab[0] == layer
                    for h, blk in enumerate((s_vk[0:RWKV_HS, 0:RWKV_HS], s_vk[RWKV_HS:, RWKV_HS:])):
                        dst[(0,) + slab + (2 * hp + h,)] = blk if own else jnp.zeros_like(blk)
        y = (y_s[2 * hp] + y_s[2 * hp + 1]) + bonuses[hp]
        mean = _half_sum(y, lo) * (1.0 / RWKV_HS)
        yc = y - mean
        var = _half_sum(yc * yc, lo) * (1.0 / RWKV_HS)
        o_ref[:, hp * LANES:(hp + 1) * LANES] = ((yc * lax.rsqrt(var + GN_EPS)) * ln_gs[hp] + ln_bs[hp]) * gates[hp]


def _rwkv(p, seq, n_pairs, li, mu, w0, a0, wa_up, g_up, vecs, s0=None, states_out=None):
    t = p.shape[0]
    nb = t // seq
    width = n_pairs * LANES
    n_grp = RWKV_WIDTH // width
    base = (2 * DIFF_QW + DIFF_WIDTH) // width
    tail = (2 * DIFF_QW + DIFF_WIDTH + 3 * RWKV_WIDTH) // LANES
    col = lambda blk: pl.BlockSpec((seq, width), lambda b, hp: (b, blk(hp)))
    mu_col = lambda blk: pl.BlockSpec((None, 2, width), lambda b, hp: (li, 0, blk(hp)))
    lane_blk = lambda k: pl.BlockSpec((seq, LANES), lambda b, hp: (b, tail + k))
    mu_lane_blk = lambda k: pl.BlockSpec((None, 2, LANES), lambda b, hp: (li, 0, 3 * RWKV_WIDTH // LANES + k))
    in_specs = [
        col(lambda hp: base + hp), col(lambda hp: base + n_grp + hp), col(lambda hp: base + 2 * n_grp + hp),
        lane_blk(0), lane_blk(1),
        mu_col(lambda hp: hp), mu_col(lambda hp: n_grp + hp), mu_col(lambda hp: 2 * n_grp + hp),
        mu_lane_blk(0), mu_lane_blk(1),
        pl.BlockSpec((None, 2, width), lambda b, hp: (li, 0, hp)),
        pl.BlockSpec((None, 2, width), lambda b, hp: (li, 0, hp)),
        pl.BlockSpec((None, 2, LANES, width), lambda b, hp: (li, 0, 0, hp)),
        pl.BlockSpec((None, LANES, width), lambda b, hp: (li, 0, hp)),
        pl.BlockSpec((None, 8, width), lambda b, hp: (li, 0, hp)),
    ]
    args = [p] * 5 + [mu] * 5 + [w0, a0, wa_up, g_up, vecs]
    if s0 is not None:
        in_specs += [pl.BlockSpec((1, n_pairs, LANES, LANES), lambda b, hp: (b, hp, 0, 0))] * 2
        args += list(s0)
    out_shape = [jax.ShapeDtypeStruct((t, RWKV_WIDTH), F32)]
    out_specs = [pl.BlockSpec((seq, width), lambda b, hp: (b, hp))]
    aliases = {}
    emit_states = None
    if states_out is not None:
        emit_states = "create" if isinstance(states_out[0], jax.ShapeDtypeStruct) else "update"
        out_shape += [jax.ShapeDtypeStruct(a.shape, a.dtype) for a in states_out]
        if emit_states == "update":
            aliases = {len(args): 1, len(args) + 1: 2}
            in_specs += [pl.BlockSpec(memory_space=pl.ANY)] * 2
            args += list(states_out)
            slab, at = None, li
        else:
            slab, at = states_out[0].shape[1], 0
        out_specs += [pl.BlockSpec((1, slab, 2 * n_pairs, RWKV_HS, RWKV_HS), lambda b, hp: (b, at, hp, 0, 0))] * 2
    return pl.pallas_call(
        functools.partial(_rwkv_kernel, s0 is not None, emit_states, li),
        out_shape=out_shape,
        grid=(nb, n_grp),
        in_specs=in_specs,
        out_specs=out_specs,
        scratch_shapes=[pltpu.VMEM((n_pairs, seq, LANES), F32)] * 3 + [pltpu.VMEM((2 * n_pairs, seq, LANES), F32)] * 4
        + [pltpu.VMEM((2 * n_pairs, LANES, LANES), F32)],
        input_output_aliases=aliases,
        compiler_params=_params("parallel", "parallel"),
        name="rwkv",
    )(*args)


def _pair_states_in(s):
    nb = s.shape[0]
    st = jnp.swapaxes(s.astype(F32), -1, -2).reshape(nb, RWKV_HEADS // 2, 2, RWKV_HS, RWKV_HS)
    z = jnp.zeros_like(st[:, :, 0])
    top = jnp.concatenate([st[:, :, 0], z], axis=-1)
    bot = jnp.concatenate([z, st[:, :, 1]], axis=-1)
    return jnp.concatenate([top, bot], axis=-2)


def _dft_kernel(f_ref, ft_ref):
    n = f_ref.shape[1]
    shape = (2 * n, n)
    fr = _iota(shape, 0)
    tau = _iota(shape, 1)
    is_sin = fr >= n
    kf = jnp.where(is_sin, fr - n, fr)
    ang = ((kf * tau) & (2 * n - 1)).astype(F32) * (math.pi / n)
    val = jnp.where(is_sin, -jnp.sin(ang), jnp.cos(ang))
    nyq = jnp.where((tau & 1) == 0, 1.0, -1.0)
    val = jnp.where(is_sin & (kf == 0), nyq, val)
    f_ref[...] = val.astype(BF16)
    ft_ref[...] = val.T.astype(BF16)


def _dft_mats(n):
    return pl.pallas_call(
        _dft_kernel,
        out_shape=(jax.ShapeDtypeStruct((2 * n, n), BF16), jax.ShapeDtypeStruct((n, 2 * n), BF16)),
        name="dft_mats",
    )()


def _filter_kernel(seq, w0t_ref, w0c_ref, w0s_ref, b0_ref, w1_ref, b1_ref, w2_ref, b2_ref, fr_ref,
                   w3a_ref, w3b_ref, f_ref, o_ref, prev_scr):
    cb = f_ref.shape[1]
    n_blk = seq // cb
    e = pl.program_id(0)
    first_lag = (e - n_blk) * cb
    tap_row = jnp.abs(first_lag + _iota((1, cb), 1)).astype(F32)
    pos_row = tap_row * (1.0 / (seq - 1))
    band = _iota((BANDS, 1), 0).astype(F32)
    freq = 1e-4 + band * ((BANDS - 1 - 1e-4) / (BANDS - 1))
    arg = freq * (tap_row * (2.0 * math.pi / seq))
    fr = fr_ref[...]
    pre = (w0t_ref[...] * pos_row + _dot_f32(w0c_ref[...], jnp.cos(arg)) + _dot_f32(w0s_ref[...], -jnp.sin(arg))
           + b0_ref[...])
    hdn = jnp.sin(fr[:, 0:1] * pre)
    hdn = jnp.sin(fr[:, 1:2] * (_dot_f32(w1_ref[...], hdn) + b1_ref[...]))
    hdn = jnp.sin(fr[:, 2:3] * (_dot_f32(w2_ref[...], hdn) + b2_ref[...]))
    hdn = hdn.T
    tap = jnp.abs(first_lag + _iota((cb, 1), 0))
    pos = tap.astype(F32) * (1.0 / (seq - 1))
    chan = _iota((1, D_MODEL), 1).astype(F32)
    delta = jnp.abs(MIN_DECAY + chan * ((MAX_DECAY - MIN_DECAY) / (D_MODEL - 1)))
    window = jnp.where(tap < seq, jnp.exp(-pos * delta), 0.0)
    row = _iota((2 * cb, 1), 0)
    sign = jnp.where((row & 1) == 1, -1.0, 1.0)
    for o, w3_ref in enumerate((w3a_ref, w3b_ref)):
        cols = slice(o * D_MODEL, (o + 1) * D_MODEL)
        taps = (_dot_f32(hdn, w3_ref[...]) * window).astype(BF16)
        spec = jnp.dot(f_ref[...], taps, preferred_element_type=F32)

        @pl.when(e > 0)
        def _():
            o_ref[0, :, cols] = spec + sign * prev_scr[:, cols]

        prev_scr[:, cols] = spec - jnp.where(row <= cb, taps[0:1, :].astype(F32), 0.0)


def _hyena_filters(seq, li, fmat, small, w3):
    cb = fmat.shape[1]
    n_blk = seq // cb
    layer_slice = lambda a: pl.BlockSpec((None,) + a.shape[1:], lambda e: (li, 0, 0))
    w3_spec = lambda o: pl.BlockSpec((None, FILTER_WIDTH, D_MODEL),
                                     lambda e: (li, 0, 2 * o + jnp.where(e < n_blk, 1, 0)))
    return pl.pallas_call(
        functools.partial(_filter_kernel, seq),
        out_shape=jax.ShapeDtypeStruct((2 * n_blk - 1, 2 * cb, 2 * D_MODEL), F32),
        grid=(2 * n_blk,),
        in_specs=[layer_slice(a) for a in small] + [w3_spec(0), w3_spec(1),
                                                    pl.BlockSpec((2 * cb, cb), lambda e: (0, 0))],
        out_specs=pl.BlockSpec((1, 2 * cb, 2 * D_MODEL), lambda e: (jnp.maximum(e - 1, 0), 0, 0)),
        scratch_shapes=[pltpu.VMEM((2 * cb, 2 * D_MODEL), F32)],
        compiler_params=_params("arbitrary"),
        name="hyena_filters",
    )(*small, w3, w3, fmat)


def _short_conv(z, w, b):
    prev, nxt = _shift_rows(z)
    return prev * w[0:1] + z * w[1:2] + nxt * w[2:3] + b


def _conv_kernel(conv_v, n_blk, v_ref, x_ref, wv_ref, bv_ref, wx_ref, bx_ref, f_ref, ft_ref, kc_ref, bias_ref,
                 o_ref, uf_scr, yf_scr):
    cb = f_ref.shape[1]
    v = v_ref[...]
    if conv_v:
        v = _short_conv(v, wv_ref[...], bv_ref[...])
    fmat = f_ref[...]
    for j in range(n_blk):
        uf_scr[j] = jnp.dot(fmat, v[j * cb:(j + 1) * cb].astype(BF16), preferred_element_type=F32)

    def mix(r0, first):
        re, im = pl.ds(r0, MIX_ROWS), pl.ds(cb + r0, MIX_ROWS)
        real_pair = (_iota((MIX_ROWS, 1), 0) == 0) if first else None
        for i in range(n_blk):
            acc_r = acc_i = None
            for j in range(n_blk):
                d = i - j + n_blk - 1
                ur, ui = uf_scr[j, re, :], uf_scr[j, im, :]
                kr, ki = kc_ref[d, re, :], kc_ref[d, im, :]
                uiki = ui * ki
                if first:
                    t_r = ur * kr - jnp.where(real_pair, 0.0, uiki)
                    t_i = jnp.where(real_pair, uiki, ur * ki + ui * kr)
                else:
                    t_r = ur * kr - uiki
                    t_i = ur * ki + ui * kr
                acc_r = t_r if acc_r is None else acc_r + t_r
                acc_i = t_i if acc_i is None else acc_i + t_i
            scale = jnp.where(real_pair, 0.5 / cb, 1.0 / cb) if first else 1.0 / cb
            yf_scr[i, re, :] = acc_r * scale
            yf_scr[i, im, :] = acc_i * scale

    mix(0, True)

    def body(r, carry):
        mix(pl.multiple_of(r * MIX_ROWS, MIX_ROWS), False)
        return carry

    lax.fori_loop(1, cb // MIX_ROWS, body, 0)

    x = _short_conv(x_ref[...], wx_ref[...], bx_ref[...])
    ftm = ft_ref[...]
    bias = bias_ref[...]
    for i in range(n_blk):
        rows = slice(i * cb, (i + 1) * cb)
        y = jnp.dot(ftm, yf_scr[i].astype(BF16), preferred_element_type=F32)
        o_ref[rows, :] = x[rows] * (y + v[rows] * bias)


def _hyena_conv(v_arr, v_blk0, x_arr, x_blk0, conv_v, seq, li, order, cw, cwb, fmat, fmat_t, kc, biases, width):
    t = v_arr.shape[0]
    cb = fmat.shape[1]
    nb, ncb, n_blk = t // seq, D_MODEL // width, seq // cb
    wv_blk0 = v_blk0 if conv_v else 0
    kc_blk0 = order * ncb
    return pl.pallas_call(
        functools.partial(_conv_kernel, conv_v, n_blk),
        out_shape=jax.ShapeDtypeStruct((t, D_MODEL), F32),
        grid=(nb, ncb),
        in_specs=[
            pl.BlockSpec((seq, width), lambda b, c: (b, v_blk0 + c)),
            pl.BlockSpec((seq, width), lambda b, c: (b, x_blk0 + c)),
            pl.BlockSpec((None, 3, width), lambda b, c: (li, 0, wv_blk0 + c)),
            pl.BlockSpec((None, 1, width), lambda b, c: (li, 0, wv_blk0 + c)),
            pl.BlockSpec((None, 3, width), lambda b, c: (li, 0, x_blk0 + c)),
            pl.BlockSpec((None, 1, width), lambda b, c: (li, 0, x_blk0 + c)),
            pl.BlockSpec((2 * cb, cb), lambda b, c: (0, 0)),
            pl.BlockSpec((cb, 2 * cb), lambda b, c: (0, 0)),
            pl.BlockSpec((2 * n_blk - 1, 2 * cb, width), lambda b, c: (0, 0, kc_blk0 + c)),
            pl.BlockSpec((None, 1, width), lambda b, c: (li * 2 + order, 0, c)),
        ],
        out_specs=pl.BlockSpec((seq, width), lambda b, c: (b, c)),
        scratch_shapes=[pltpu.VMEM((n_blk, 2 * cb, width), F32), pltpu.VMEM((n_blk, 2 * cb, width), F32)],
        compiler_params=_params("parallel", "parallel"),
        name="hyena_conv",
    )(v_arr, x_arr, cw, cwb, cw, cwb, fmat, fmat_t, kc, biases)


def _axial_rope_tables(seq):
    n_rows = seq // GRID_W
    row = np.repeat(np.arange(n_rows, dtype=np.float64), GRID_W)
    col = np.tile(np.arange(GRID_W, dtype=np.float64), n_rows)
    inv = ROPE_BASE ** (-np.arange(ROPE_PAIRS, dtype=np.float64) / ROPE_PAIRS)
    ar = row[:, None] * inv[None]
    ac = col[:, None] * inv[None]
    ang = np.concatenate([ar, ar, ac, ac], axis=-1)
    reps = LANES // DIFF_QK
    return (jnp.asarray(np.tile(np.cos(ang), (1, reps)), F32), jnp.asarray(np.tile(np.sin(ang), (1, reps)), F32))


def _pair_lanes(a, seq_axis):
    nb = a.shape[0]
    if a.ndim == 5:
        a = a.reshape(nb, DIFF_HEADS // 2, 2, 2, a.shape[3], DIFF_QK).transpose(0, 1, 4, 2, 3, 5)
    else:
        a = a.reshape(nb, DIFF_HEADS // 2, 2, a.shape[2], DIFF_V).transpose(0, 1, 3, 2, 4)
    return a.reshape(nb, DIFF_HEADS // 2, a.shape[2], LANES)


def kernel(x_prompt, x_sample, cache_diff_k, cache_diff_v, state_rwkv_fwd, state_rwkv_bwd, c, c_ctx, ada_w, ada_b, norm1_g, norm2_g, ffn_w1, ffn_w3, ffn_w2, final_g, ab_w_in, ab_w_out, diff_lambda, diff_subln_g, rwkv_mu, rwkv_w0, rwkv_w_up, rwkv_a0, rwkv_a_up, rwkv_g_up, rwkv_k_k, rwkv_k_a, rwkv_r_k, rwkv_ln_g, rwkv_ln_b, hy_w_in, hy_b_in, hy_conv_w, hy_conv_b, hy_f_w0, hy_f_b0, hy_f_w1, hy_f_b1, hy_f_w2, hy_f_b2, hy_f_w3, hy_f_freq, hy_bias, hy_w_out, hy_b_out):
    n_ctx_seqs, ctx_len, _ = x_prompt.shape
    n_lat_seqs, lat_len, _ = x_sample.shape
    tm = 1024

    cond = jnp.zeros((N_MOD_ROWS, D_MODEL), F32).at[0].set(c_ctx).at[1:1 + n_lat_seqs].set(c)
    mods = _ada_all(cond, ada_w, ada_b).reshape(DEPTH * N_MOD_ROWS * 6, 1, D_MODEL)

    groups = [
        dict(x=x_prompt.reshape(-1, D_MODEL), seq=ctx_len, rows=(0, n_ctx_seqs * ctx_len), tq=ctx_len, pairs=4,
             scan_pairs=4, conv_width=1024),
        dict(x=x_sample.reshape(-1, D_MODEL), seq=lat_len, rows=(1, lat_len), tq=1024, pairs=1, scan_pairs=1,
             conv_width=256),
    ]
    rope = _axial_rope_tables(lat_len)
    dft = {cb: _dft_mats(cb) for cb in {min(CONV_BLOCK_MAX, g["seq"]) for g in groups}}
    n_ab, n_c = ab_w_in.shape[0], hy_w_in.shape[0]

    gains1 = norm1_g.reshape(DEPTH, 1, D_MODEL)
    gains2 = norm2_g.reshape(DEPTH, 1, D_MODEL)
    g2s = jnp.tile(diff_subln_g, (1, 2)).reshape(n_ab, 1, LANES)
    wa_up = jnp.concatenate([rwkv_w_up, rwkv_a_up], axis=2)
    vec_rows = [rwkv_k_k, rwkv_k_a, rwkv_r_k.reshape(n_ab, RWKV_WIDTH), rwkv_ln_g, rwkv_ln_b]
    vecs = jnp.stack(vec_rows + [jnp.zeros_like(rwkv_k_k)] * (8 - len(vec_rows)), axis=1)
    w0_t = jnp.swapaxes(hy_f_w0, 1, 2)
    filt_small = [w0_t[:, :, 0:1], w0_t[:, :, 1:1 + BANDS], w0_t[:, :, 1 + BANDS:], hy_f_b0[:, :, None],
                  jnp.swapaxes(hy_f_w1, 1, 2), hy_f_b1[:, :, None], jnp.swapaxes(hy_f_w2, 1, 2), hy_f_b2[:, :, None],
                  jnp.swapaxes(hy_f_freq, 1, 2)]
    hy_b_in_s = hy_b_in.reshape(n_c, 1, -1)
    hy_b_out_s = hy_b_out.reshape(n_c, 1, D_MODEL)
    conv_b = hy_conv_b.reshape(n_c, 1, -1)
    conv_bias = hy_bias.reshape(n_c * 2, 1, D_MODEL)
    new_kv = (jax.ShapeDtypeStruct((n_ctx_seqs, n_ab, DIFF_HEADS, 2, ctx_len, DIFF_QK), F32),
              jax.ShapeDtypeStruct((n_ctx_seqs, n_ab, DIFF_HEADS, ctx_len, DIFF_V), F32))
    new_states = (jax.ShapeDtypeStruct((n_ctx_seqs, n_ab, RWKV_HEADS, RWKV_HS, RWKV_HS), F32),) * 2

    for l in range(DEPTH):
        i = l // 2
        for gi, g in enumerate(groups):
            x, seq, row = g["x"], g["seq"], g["rows"]
            if l % 2 == 0:
                p = _proj_in(x, gains1, mods, l, row, ab_w_in, i, None, tm // 2)
                if gi == 0:
                    att, *new_kv = _attention(p, seq, g["tq"], g["pairs"], _lambda_init(l), diff_lambda, g2s, i,
                                              kv_out=new_kv)
                    mix, *new_states = _rwkv(p, seq, g["scan_pairs"], i, rwkv_mu, rwkv_w0, rwkv_a0, wa_up, rwkv_g_up,
                                             vecs, states_out=new_states)
                else:
                    ctx = (_pair_lanes(cache_diff_k[:, i], 3), _pair_lanes(cache_diff_v[:, i], 2))
                    att, = _attention(p, seq, g["tq"], g["pairs"], _lambda_init(l), diff_lambda, g2s, i,
                                      ctx=ctx, rope=rope)
                    s0 = (_pair_states_in(state_rwkv_fwd[:, i]), _pair_states_in(state_rwkv_bwd[:, i]))
                    mix, = _rwkv(p, seq, g["scan_pairs"], i, rwkv_mu, rwkv_w0, rwkv_a0, wa_up, rwkv_g_up, vecs, s0=s0)
                x = _proj_out(x, mods, l, row, [att, mix], ab_w_out, i, None, tm)
            else:
                fmat, fmat_t = dft[min(CONV_BLOCK_MAX, seq)]
                z = _proj_in(x, gains1, mods, l, row, hy_w_in, i, hy_b_in_s, tm // 2)
                kc = _hyena_filters(seq, i, fmat, filt_small, hy_f_w3)
                width = g["conv_width"]
                ncb = D_MODEL // width
                u = _hyena_conv(z, 2 * ncb, z, 0, True, seq, i, 0, hy_conv_w, conv_b, fmat, fmat_t, kc, conv_bias, width)
                u = _hyena_conv(u, 0, z, ncb, False, seq, i, 1, hy_conv_w, conv_b, fmat, fmat_t, kc, conv_bias, width)
                x = _proj_out(x, mods, l, row, [u], hy_w_out, i, hy_b_out_s, tm)
            g["x"] = _ffn(x, gains2, mods, l, row, ffn_w1, ffn_w3, ffn_w2, final_g, l == DEPTH - 1, 2 * tm, 256)

    y_prompt = groups[0]["x"].reshape(x_prompt.shape)
    y_sample = groups[1]["x"].reshape(x_sample.shape)
    return (y_prompt, y_sample, new_kv[0], new_kv[1], new_states[0], new_states[1])
```
